```python
import jax, jax.numpy as jnp
from jax import lax
import numpy as np

D_MODEL = 1024
BATCH = 8
SEQ = 2048
DEPTH = 1
DEC_BATCH = 128
DEC_SEQ = 4
PAST_LEN = 16384
PAGE_SIZE = 128

D_MIX = D_MODEL
D_CONV = D_MIX // 2
D_POOL = D_MIX - D_CONV
CONV_HEADS = 8
CONV_W = 3
POOL_WINDOWS = (2, 4, 8, 16)
N_POOL_GROUPS = len(POOL_WINDOWS)
POOL_GROUP = D_POOL // N_POOL_GROUPS
POOL_BUF = max(POOL_WINDOWS) - 1
D_IN = 3 * D_CONV + D_POOL
N_EXPERTS = 32
TOP_K = 4
D_FF = D_MODEL
SWIGLU_LIMIT = 7.0
SWIGLU_ALPHA = 1.702
EPS = 1e-5

kernel_name = "hymba_conv_pool_moe_step"


def rmsnorm(x, g):
    xf = x.astype(jnp.float32)
    y = xf * lax.rsqrt(jnp.mean(xf * xf, axis=-1, keepdims=True) + EPS)
    return (y * g.astype(jnp.float32)).astype(x.dtype)


def conv_mixer(pc, buf, conv_w):
    b_gate = pc[..., :D_CONV]
    c_gate = pc[..., D_CONV:2 * D_CONV]
    h = pc[..., 2 * D_CONV:]
    u = c_gate * h
    s = u.shape[1]
    up = jnp.concatenate([buf.astype(u.dtype), u], axis=1)
    y = up[:, 0:s] * conv_w[0]
    for k in range(1, CONV_W):
        y = y + up[:, k:k + s] * conv_w[k]
    return b_gate * y, up[:, -(CONV_W - 1):]


def pool_mixer(v, buf, start_pos, pool_w, pool_scale):
    s = v.shape[1]
    vp = jnp.concatenate([buf.astype(v.dtype), v], axis=1)
    vf = vp.astype(jnp.float32)
    cs = jnp.concatenate([jnp.zeros_like(vf[:, :1]), jnp.cumsum(vf, axis=1)], axis=1)
    hi = cs[:, POOL_BUF + 1:POOL_BUF + 1 + s]
    pos = (start_pos + jnp.arange(s, dtype=jnp.int32)).astype(jnp.float32)
    vfx = v.astype(jnp.float32)
    outs = []
    for g, w in enumerate(POOL_WINDOWS):
        sl = slice(g * POOL_GROUP, (g + 1) * POOL_GROUP)
        lo = cs[:, POOL_BUF + 1 - w:POOL_BUF + 1 - w + s, sl]
        cnt = jnp.minimum(jnp.float32(w), pos + 1.0)[None, :, None]
        outs.append((hi[..., sl] - lo) / cnt - vfx[..., sl])
    p = jnp.stack(outs, axis=2).astype(v.dtype)
    z = jnp.einsum('bsgc,gcd->bsgd', p, pool_w).reshape(v.shape)
    return z * pool_scale, vp[:, -POOL_BUF:]


def moe(x, router_w, router_b, w1, b1, w2, b2):
    shp = x.shape
    xt = x.reshape(-1, shp[-1])
    logits = xt.astype(jnp.float32) @ router_w.astype(jnp.float32) + router_b.astype(jnp.float32)
    vals, idx = lax.top_k(logits, TOP_K)
    gates = jax.nn.softmax(vals, axis=-1)
    combine = jnp.sum(jax.nn.one_hot(idx, N_EXPERTS, dtype=jnp.float32) * gates[..., None], axis=1)
    combine = combine.astype(x.dtype)
    acc = jnp.zeros_like(xt)
    for e in range(N_EXPERTS):
        gu = xt @ w1[e] + b1[e]
        gate = jnp.minimum(gu[:, :D_FF], SWIGLU_LIMIT)
        lin = jnp.clip(gu[:, D_FF:], -SWIGLU_LIMIT, SWIGLU_LIMIT)
        glu = gate * jax.nn.sigmoid(SWIGLU_ALPHA * gate)
        out = (glu * (lin + 1.0)) @ w2[e] + b2[e]
        acc = acc + combine[:, e:e + 1] * out
    return acc.reshape(shp)


def layer(x, conv_buf, pool_buf, start_pos, l, norm_mix_g, w_in, conv_w, pool_w, pool_scale,
          w_out, norm_ffn_g, router_w, router_b, exp_w1, exp_b1, exp_w2, exp_b2):
    xn = rmsnorm(x, norm_mix_g[l])
    proj = jnp.einsum('bsd,de->bse', xn, w_in[l])
    zc, new_conv = conv_mixer(proj[..., :3 * D_CONV], conv_buf, conv_w[l])
    zp, new_pool = pool_mixer(proj[..., 3 * D_CONV:], pool_buf, start_pos, pool_w[l], pool_scale[l])
    z = jnp.concatenate([zc, zp], axis=-1)
    x = x + jnp.einsum('bse,ed->bsd', z, w_out[l])
    x = x + moe(rmsnorm(x, norm_ffn_g[l]), router_w[l], router_b[l],
                exp_w1[l], exp_b1[l], exp_w2[l], exp_b2[l])
    return x, new_conv, new_pool


def setup_inputs(seed: int = 0) -> dict:
    key = jax.random.key(seed)
    ks = jax.random.split(key, 20)
    f = jnp.float32
    n = lambda k, s, sc: jax.random.normal(k, s, f) * sc
    return {
        "x_prompt": n(ks[0], (BATCH, SEQ, D_MODEL), 1.0),
        "x_sample": n(ks[1], (DEC_BATCH, DEC_SEQ, D_MODEL), 1.0),
        "state_conv": n(ks[2], (DEPTH, DEC_BATCH, CONV_W - 1, D_CONV), 1.0),
        "state_pool": n(ks[3], (DEPTH, DEC_BATCH, POOL_BUF, D_POOL), 1.0),
        "norm_mix_g": 1.0 + n(ks[4], (DEPTH, D_MODEL), 0.01),
        "w_in": n(ks[5], (DEPTH, D_MODEL, D_IN), D_MODEL ** -0.5),
        "conv_w": n(ks[6], (DEPTH, CONV_W, D_CONV), CONV_W ** -0.5),
        "pool_w": n(ks[7], (DEPTH, N_POOL_GROUPS, POOL_GROUP, POOL_GROUP), POOL_GROUP ** -0.5),
        "pool_scale": 1.0 + n(ks[8], (DEPTH, D_POOL), 0.1),
        "w_out": n(ks[9], (DEPTH, D_MIX, D_MODEL), D_MIX ** -0.5),
        "norm_ffn_g": 1.0 + n(ks[10], (DEPTH, D_MODEL), 0.01),
        "router_w": n(ks[11], (DEPTH, D_MODEL, N_EXPERTS), D_MODEL ** -0.5),
        "router_b": n(ks[12], (DEPTH, N_EXPERTS), 0.01),
        "exp_w1": n(ks[13], (DEPTH, N_EXPERTS, D_MODEL, 2 * D_FF), D_MODEL ** -0.5),
        "exp_b1": n(ks[14], (DEPTH, N_EXPERTS, 2 * D_FF), 0.01),
        "exp_w2": n(ks[15], (DEPTH, N_EXPERTS, D_FF, D_MODEL), D_FF ** -0.5),
        "exp_b2": n(ks[16], (DEPTH, N_EXPERTS, D_MODEL), 0.01),
        "final_norm_g": 1.0 + n(ks[17], (D_MODEL,), 0.01),
    }


def reference(x_prompt, x_sample, state_conv, state_pool, norm_mix_g, w_in, conv_w, pool_w,
              pool_scale, w_out, norm_ffn_g, router_w, router_b, exp_w1, exp_b1, exp_w2,
              exp_b2, final_norm_g):
    params = (norm_mix_g, w_in, conv_w, pool_w, pool_scale, w_out, norm_ffn_g,
              router_w, router_b, exp_w1, exp_b1, exp_w2, exp_b2)
    hp = x_prompt
    hs = x_sample
    cp_list, pp_list, cs_list, ps_list = [], [], [], []
    for l in range(DEPTH):
        zero_conv = jnp.zeros((x_prompt.shape[0], CONV_W - 1, D_CONV), x_prompt.dtype)
        zero_pool = jnp.zeros((x_prompt.shape[0], POOL_BUF, D_POOL), x_prompt.dtype)
        hp, ncp, npp = layer(hp, zero_conv, zero_pool, 0, l, *params)
        hs, ncs, nps = layer(hs, state_conv[l], state_pool[l], PAST_LEN, l, *params)
        cp_list.append(ncp); pp_list.append(npp)
        cs_list.append(ncs); ps_list.append(nps)
    y_prompt = rmsnorm(hp, final_norm_g)
    y_sample = rmsnorm(hs, final_norm_g)
    new_conv_prompt = jnp.stack(cp_list, axis=0)
    new_pool_prompt = jnp.stack(pp_list, axis=0)
    new_conv_sample = jnp.stack(cs_list, axis=0)
    new_pool_sample = jnp.stack(ps_list, axis=0)
    return (y_prompt, y_sample, new_conv_prompt, new_pool_prompt, new_conv_sample, new_pool_sample)
```

```python
import functools

import jax
import jax.numpy as jnp
from jax import lax
from jax.experimental import pallas as pl
from jax.experimental.pallas import tpu as pltpu

D_MODEL = 1024
D_CONV = 512
D_POOL = 512
D_IN = 3 * D_CONV + D_POOL
CONV_W = 3
POOL_WINDOWS = (2, 4, 8, 16)
POOL_GROUP = 128
POOL_BUF = 15
N_EXPERTS = 32
TOP_K = 4
D_FF = 1024
SWIGLU_LIMIT = 7.0
SWIGLU_ALPHA = 1.702
EPS = 1e-5
PAST_LEN = 16384

BATCH, SEQ = 8, 2048
DEC_BATCH, DEC_SEQ = 128, 4
N_PROMPT = BATCH * SEQ
N_SAMPLE = DEC_BATCH * DEC_SEQ
N_TOK = N_PROMPT + N_SAMPLE

TS = 512
SEQ_TILES = SEQ // TS
N_PROMPT_TILES = N_PROMPT // TS
N_TILES = N_PROMPT_TILES + 1
HIST = 16
TM = 256
N_ROWS = (N_TOK * TOP_K + N_EXPERTS * (TM - 1)) // TM * TM
N_ROW_TILES = N_ROWS // TM
D_PACK = D_MODEL // 2

VMEM_LIMIT = 56 * 1024 * 1024

f32 = jnp.float32
bf16 = jnp.bfloat16
i32 = jnp.int32
u32 = jnp.uint32


def _rms(x, g):
    return x * lax.rsqrt(jnp.mean(x * x, axis=-1, keepdims=True) + EPS) * g


def _pack_bf16_pairs(xb):
    bits = lax.bitcast_convert_type(xb.astype(f32), u32)
    return (bits[:, :D_PACK] >> 16) | (bits[:, D_PACK:] & jnp.uint32(0xFFFF0000))


def _unpack_bf16_pairs(p):
    lo = lax.bitcast_convert_type(p << 16, f32)
    hi = lax.bitcast_convert_type(p & jnp.uint32(0xFFFF0000), f32)
    return jnp.concatenate([lo, hi], axis=1).astype(bf16)


def _pool_project(p, pw_ref, ps_ref):
    outs = []
    for g in range(len(POOL_WINDOWS)):
        sl = slice(g * POOL_GROUP, (g + 1) * POOL_GROUP)
        outs.append(jnp.dot(p[:, sl].astype(bf16), pw_ref[g], preferred_element_type=f32))
    return jnp.concatenate(outs, axis=1) * ps_ref[...]


def _post_mix(x, z, wout_ref, g2_ref, rwh_ref, rwl_ref, rb_ref, run_ref,
              x1_ref, xp_ref, idx_ref, rank_ref, gate_ref):
    n = x.shape[0]
    x1 = x + jnp.dot(z.astype(bf16), wout_ref[...], preferred_element_type=f32)
    x1_ref[...] = x1
    xn = _rms(x1, g2_ref[...])
    xb = xn.astype(bf16)
    xp_ref[...] = _pack_bf16_pairs(xb)

    xlo = (xn - xb.astype(f32)).astype(bf16)
    nt = (((1,), (1,)), ((), ()))
    logits = (lax.dot_general(rwh_ref[...], xb, nt, preferred_element_type=f32)
              + lax.dot_general(rwl_ref[...], xb, nt, preferred_element_type=f32)
              + lax.dot_general(rwh_ref[...], xlo, nt, preferred_element_type=f32)
              + rb_ref[...])

    e_iota = lax.broadcasted_iota(i32, (N_EXPERTS, n), 0)
    work = logits
    hots, vals, idxs = [], [], []
    for _ in range(TOP_K):
        m = jnp.max(work, axis=0, keepdims=True)
        ik = jnp.min(jnp.where(work == m, e_iota, N_EXPERTS), axis=0, keepdims=True)
        hot = e_iota == ik
        work = jnp.where(hot, -jnp.inf, work)
        hots.append(hot); vals.append(m); idxs.append(ik)
    exps = [jnp.exp(v - vals[0]) for v in vals]
    den = exps[0] + exps[1] + exps[2] + exps[3]
    gate_ref[...] = jnp.concatenate([e / den for e in exps], axis=0)
    idx_ref[...] = jnp.concatenate(idxs, axis=0)

    multi = jnp.where(hots[0] | hots[1] | hots[2] | hots[3], 1.0, 0.0)
    r_io = lax.broadcasted_iota(i32, (n, n), 0)
    c_io = lax.broadcasted_iota(i32, (n, n), 1)
    tri = jnp.where(r_io < c_io, 1.0, 0.0).astype(bf16)
    base = jnp.dot(multi.astype(bf16), tri, preferred_element_type=f32) + run_ref[...]
    ranks = [jnp.sum(jnp.where(h, base, 0.0), axis=0, keepdims=True) for h in hots]
    rank_ref[...] = jnp.concatenate(ranks, axis=0).astype(i32)
    run_ref[...] = run_ref[...] + jnp.sum(multi, axis=1, keepdims=True)


def _mixer_kernel(xp_in_ref, xs_in_ref, sc_ref, sp_ref, g1_ref, win_ref, cw_ref, pw_ref, ps_ref,
                  wout_ref, g2_ref, rwh_ref, rwl_ref, rb_ref,
                  x1_ref, xpk_ref, idx_ref, rank_ref, gate_ref, cnt_ref,
                  npc_ref, npp_ref, nsc_ref, nsv_ref,
                  ubuf, vbuf, run_ref):
    i = pl.program_id(0)
    post = functools.partial(_post_mix, wout_ref=wout_ref, g2_ref=g2_ref, rwh_ref=rwh_ref,
                             rwl_ref=rwl_ref, rb_ref=rb_ref, run_ref=run_ref, x1_ref=x1_ref,
                             xp_ref=xpk_ref, idx_ref=idx_ref, rank_ref=rank_ref, gate_ref=gate_ref)

    @pl.when(i == 0)
    def _():
        run_ref[...] = jnp.zeros_like(run_ref)

    @pl.when(i < N_PROMPT_TILES)
    def _():
        s = i % SEQ_TILES

        @pl.when(s == 0)
        def _():
            ubuf[0:HIST, :] = jnp.zeros((HIST, D_CONV), f32)
            vbuf[0:HIST, :] = jnp.zeros((HIST, D_POOL), f32)

        x = xp_in_ref[...]
        xn = _rms(x, g1_ref[...]).astype(bf16)
        proj = jnp.dot(xn, win_ref[...], preferred_element_type=f32)
        bg = proj[:, :D_CONV]
        u = proj[:, D_CONV:2 * D_CONV] * proj[:, 2 * D_CONV:3 * D_CONV]
        v = proj[:, 3 * D_CONV:]
        ubuf[HIST:HIST + TS, :] = u
        vbuf[HIST:HIST + TS, :] = v

        y = (ubuf[HIST - 2:HIST - 2 + TS, :] * cw_ref[0:1, :]
             + ubuf[HIST - 1:HIST - 1 + TS, :] * cw_ref[1:2, :]
             + u * cw_ref[2:3, :])
        zc = bg * y

        pos = s * TS + lax.broadcasted_iota(i32, (TS, 1), 0)
        ps = []
        for g, w in enumerate(POOL_WINDOWS):
            sl = slice(g * POOL_GROUP, (g + 1) * POOL_GROUP)
            acc = vbuf[:, sl]
            for step in range(g + 1):
                acc = acc + pltpu.roll(acc, 1 << step, axis=0)
            cnt = jnp.minimum(w, pos + 1).astype(f32)
            ps.append(acc[HIST:, :] * (1.0 / cnt) - v[:, sl])
        zp = _pool_project(jnp.concatenate(ps, axis=1), pw_ref, ps_ref)

        ubuf[HIST - 8:HIST, :] = ubuf[TS + HIST - 8:TS + HIST, :]
        vbuf[0:HIST, :] = vbuf[TS:TS + HIST, :]

        @pl.when(s == SEQ_TILES - 1)
        def _():
            npc_ref[0] = ubuf[HIST - 8:HIST, :]
            npp_ref[0] = vbuf[0:HIST, :]

        post(x, jnp.concatenate([zc, zp], axis=1))

    @pl.when(i == N_PROMPT_TILES)
    def _():
        x = xs_in_ref[...]
        xn = _rms(x, g1_ref[...]).astype(bf16)
        proj = jnp.dot(xn, win_ref[...], preferred_element_type=f32)
        nb = DEC_BATCH
        rows = lambda a, t: a[t * nb:(t + 1) * nb]
        bg = proj[:, :D_CONV]
        u = proj[:, D_CONV:2 * D_CONV] * proj[:, 2 * D_CONV:3 * D_CONV]
        v = proj[:, 3 * D_CONV:]
        up = [sc_ref[j] for j in range(CONV_W - 1)] + [rows(u, t) for t in range(DEC_SEQ)]
        vp = [sp_ref[j] for j in range(POOL_BUF)] + [rows(v, t) for t in range(DEC_SEQ)]
        zc, ps = [], []
        for t in range(DEC_SEQ):
            y = up[t] * cw_ref[0:1, :]
            for k in range(1, CONV_W):
                y = y + up[t + k] * cw_ref[k:k + 1, :]
            zc.append(rows(bg, t) * y)
            pg = []
            for g, w in enumerate(POOL_WINDOWS):
                sl = slice(g * POOL_GROUP, (g + 1) * POOL_GROUP)
                lo = t + POOL_BUF - w + 1
                acc = vp[lo][:, sl]
                for j in range(lo + 1, t + POOL_BUF + 1):
                    acc = acc + vp[j][:, sl]
                cnt = float(min(w, PAST_LEN + t + 1))
                pg.append(acc * (1.0 / cnt) - vp[t + POOL_BUF][:, sl])
            ps.append(jnp.concatenate(pg, axis=1))
        zp = _pool_project(jnp.concatenate(ps, axis=0), pw_ref, ps_ref)
        for j in range(CONV_W - 1):
            nsc_ref[j] = up[DEC_SEQ + j]
        for t in range(DEC_SEQ):
            nsv_ref[t] = rows(v, t)
        post(x, jnp.concatenate([jnp.concatenate(zc, axis=0), zp], axis=1))

    cnt_ref[...] = run_ref[...]


def _mixer(xp, xs, sc_t, sp_t, g1, win, cw, pw, ps, wout, g2, rwh, rwl, rb):
    const = lambda shape: pl.BlockSpec(shape, lambda i: (0,) * len(shape),
                                       pipeline_mode=pl.Buffered(1))
    ptile = lambda i: jnp.minimum(i, N_PROMPT_TILES - 1)
    pbatch = lambda i: jnp.minimum(i, N_PROMPT_TILES - 1) // SEQ_TILES
    out_shape = (
        jax.ShapeDtypeStruct((N_TOK, D_MODEL), f32),
        jax.ShapeDtypeStruct((N_TOK, D_PACK), u32),
        jax.ShapeDtypeStruct((TOP_K, N_TOK), i32),
        jax.ShapeDtypeStruct((TOP_K, N_TOK), i32),
        jax.ShapeDtypeStruct((TOP_K, N_TOK), f32),
        jax.ShapeDtypeStruct((N_EXPERTS, 1), f32),
        jax.ShapeDtypeStruct((BATCH, 8, D_CONV), f32),
        jax.ShapeDtypeStruct((BATCH, HIST, D_POOL), f32),
        jax.ShapeDtypeStruct((CONV_W - 1, DEC_BATCH, D_CONV), f32),
        jax.ShapeDtypeStruct((DEC_SEQ, DEC_BATCH, D_POOL), f32),
    )
    out_specs = (
        pl.BlockSpec((TS, D_MODEL), lambda i: (i, 0)),
        pl.BlockSpec((TS, D_PACK), lambda i: (i, 0)),
        pl.BlockSpec((TOP_K, TS), lambda i: (0, i)),
        pl.BlockSpec((TOP_K, TS), lambda i: (0, i)),
        pl.BlockSpec((TOP_K, TS), lambda i: (0, i)),
        pl.BlockSpec((N_EXPERTS, 1), lambda i: (0, 0)),
        pl.BlockSpec((1, 8, D_CONV), lambda i: (pbatch(i), 0, 0)),
        pl.BlockSpec((1, HIST, D_POOL), lambda i: (pbatch(i), 0, 0)),
        pl.BlockSpec((CONV_W - 1, DEC_BATCH, D_CONV), lambda i: (0, 0, 0)),
        pl.BlockSpec((DEC_SEQ, DEC_BATCH, D_POOL), lambda i: (0, 0, 0)),
    )
    in_specs = [
        pl.BlockSpec((TS, D_MODEL), lambda i: (ptile(i), 0)),
        const((N_SAMPLE, D_MODEL)),
        const((CONV_W - 1, DEC_BATCH, D_CONV)),
        const((POOL_BUF, DEC_BATCH, D_POOL)),
        const((1, D_MODEL)),
        const((D_MODEL, D_IN)),
        const((CONV_W, D_CONV)),
        const((len(POOL_WINDOWS), POOL_GROUP, POOL_GROUP)),
        const((1, D_POOL)),
        const((D_MODEL, D_MODEL)),
        const((1, D_MODEL)),
        const((N_EXPERTS, D_MODEL)),
        const((N_EXPERTS, D_MODEL)),
        const((N_EXPERTS, 1)),
    ]
    return pl.pallas_call(
        _mixer_kernel,
        grid=(N_TILES,),
        in_specs=in_specs,
        out_specs=out_specs,
        out_shape=out_shape,
        scratch_shapes=[
            pltpu.VMEM((TS + HIST, D_CONV), f32),
            pltpu.VMEM((TS + HIST, D_POOL), f32),
            pltpu.VMEM((N_EXPERTS, 1), f32),
        ],
        compiler_params=pltpu.CompilerParams(
            dimension_semantics=("arbitrary",), vmem_limit_bytes=VMEM_LIMIT),
        name="mixer",
    )(xp, xs, sc_t, sp_t, g1, win, cw, pw, ps, wout, g2, rwh, rwl, rb)


def _dispatch_kernel(pstart_ref, plen_ref, nu_ref, pos_ref, x_ref, xs_hbm, zrow, sem):
    i = pl.program_id(0)

    @pl.when(i == 0)
    def _():
        zrow[...] = jnp.zeros_like(zrow)

        def fill_tile(j, c):
            cp = pltpu.make_async_copy(zrow, xs_hbm.at[pl.ds(pl.multiple_of(j * TM, TM), TM)],
                                       sem.at[1])
            cp.start()
            cp.wait()
            return c
        lax.fori_loop(nu_ref[0], N_ROW_TILES, fill_tile, 0)
        for e in range(N_EXPERTS):
            n = plen_ref[e]

            def fill(j, c):
                pltpu.make_async_copy(zrow.at[pl.ds(0, 1)],
                                      xs_hbm.at[pl.ds(pstart_ref[e] + j, 1)], sem.at[1]).start()
                return c
            lax.fori_loop(0, n, fill, 0)

            def drain(j, c):
                pltpu.make_async_copy(zrow.at[pl.ds(0, 1)],
                                      xs_hbm.at[pl.ds(0, 1)], sem.at[1]).wait()
                return c
            lax.fori_loop(0, n, drain, 0)

    def issue(t, c):
        for k in range(TOP_K):
            pltpu.make_async_copy(x_ref.at[pl.ds(t, 1)],
                                  xs_hbm.at[pl.ds(pos_ref[k, t], 1)], sem.at[0]).start()
        return c
    lax.fori_loop(0, TS, issue, 0)
    for k in range(TOP_K):
        pltpu.make_async_copy(x_ref, xs_hbm.at[pl.ds(0, TS)], sem.at[0]).wait()


def _dispatch(pad_start, pad_len, n_used, pos_t, xpk):
    grid_spec = pltpu.PrefetchScalarGridSpec(
        num_scalar_prefetch=3,
        grid=(N_TILES,),
        in_specs=[
            pl.BlockSpec((TOP_K, TS), lambda i, a, b, c: (0, i), memory_space=pltpu.SMEM),
            pl.BlockSpec((TS, D_PACK), lambda i, a, b, c: (i, 0)),
        ],
        out_specs=pl.BlockSpec(memory_space=pl.ANY),
        scratch_shapes=[pltpu.VMEM((TM, D_PACK), u32), pltpu.SemaphoreType.DMA((2,))],
    )
    return pl.pallas_call(
        _dispatch_kernel,
        grid_spec=grid_spec,
        out_shape=jax.ShapeDtypeStruct((N_ROWS, D_PACK), u32),
        compiler_params=pltpu.CompilerParams(dimension_semantics=("arbitrary",)),
        name="dispatch",
    )(pad_start, pad_len, n_used, pos_t, xpk)


def _expert_kernel(te_ref, nu_ref, xs_ref, w1_ref, b1_ref, w2_ref, b2_ref, ys_ref, w1b, w2b):
    i = pl.program_id(0)
    live = i < nu_ref[0]
    new_expert = (i == 0) | (te_ref[i] != te_ref[jnp.maximum(i - 1, 0)])

    @pl.when(live & new_expert)
    def _():
        w1b[...] = w1_ref[0].astype(bf16)
        w2b[...] = w2_ref[0].astype(bf16)

    @pl.when(live)
    def _():
        x = _unpack_bf16_pairs(xs_ref[...])
        gu = jnp.dot(x, w1b[...], preferred_element_type=f32) + b1_ref[0]
        gate = jnp.minimum(gu[:, :D_FF], SWIGLU_LIMIT)
        lin = jnp.clip(gu[:, D_FF:], -SWIGLU_LIMIT, SWIGLU_LIMIT)
        glu = gate * jax.nn.sigmoid(SWIGLU_ALPHA * gate)
        h = (glu * (lin + 1.0)).astype(bf16)
        ys_ref[...] = jnp.dot(h, w2b[...], preferred_element_type=f32) + b2_ref[0]

    @pl.when(jnp.logical_not(live))
    def _():
        ys_ref[...] = jnp.zeros_like(ys_ref)


def _experts(tile_expert, n_used, xs, w1, b1, w2, b2):
    row = lambda i, te, nu: (jnp.minimum(i, nu[0] - 1), 0)
    wsel = lambda i, te, nu: (te[i], 0, 0)
    grid_spec = pltpu.PrefetchScalarGridSpec(
        num_scalar_prefetch=2,
        grid=(N_ROW_TILES,),
        in_specs=[
            pl.BlockSpec((TM, D_PACK), row),
            pl.BlockSpec((1, D_MODEL, 2 * D_FF), wsel),
            pl.BlockSpec((1, 1, 2 * D_FF), wsel),
            pl.BlockSpec((1, D_FF, D_MODEL), wsel),
            pl.BlockSpec((1, 1, D_MODEL), wsel),
        ],
        out_specs=pl.BlockSpec((TM, D_MODEL), lambda i, te, nu: (i, 0)),
        scratch_shapes=[pltpu.VMEM((D_MODEL, 2 * D_FF), bf16), pltpu.VMEM((D_FF, D_MODEL), bf16)],
    )
    return pl.pallas_call(
        _expert_kernel,
        grid_spec=grid_spec,
        out_shape=jax.ShapeDtypeStruct((N_ROWS, D_MODEL), f32),
        compiler_params=pltpu.CompilerParams(
            dimension_semantics=("arbitrary",), vmem_limit_bytes=VMEM_LIMIT),
        name="experts",
    )(tile_expert, n_used, xs, w1, b1, w2, b2)


def _combine_kernel(pos_ref, x1_ref, g_ref, gf_ref, ys_hbm, yp_ref, ysm_ref, buf, sem):
    i = pl.program_id(0)

    def issue(t, c):
        for k in range(TOP_K):
            pltpu.make_async_copy(ys_hbm.at[pl.ds(pos_ref[k, t], 1)],
                                  buf.at[k, pl.ds(t, 1)], sem.at[k]).start()
        return c
    lax.fori_loop(0, TS, issue, 0)
    for k in range(TOP_K):
        pltpu.make_async_copy(ys_hbm.at[pl.ds(0, TS)], buf.at[k], sem.at[k]).wait()

    g = g_ref[...]
    y = x1_ref[...]
    for k in range(TOP_K):
        y = y + g[:, k:k + 1] * buf[k]
    out = _rms(y, gf_ref[...])

    @pl.when(i < N_PROMPT_TILES)
    def _():
        yp_ref[...] = out

    @pl.when(i == N_PROMPT_TILES)
    def _():
        ysm_ref[...] = out


def _combine(pos_t, x1, gates_tm, gf, ys):
    ptile = lambda i: (jnp.minimum(i, N_PROMPT_TILES - 1), 0)
    return pl.pallas_call(
        _combine_kernel,
        grid=(N_TILES,),
        in_specs=[
            pl.BlockSpec((TOP_K, TS), lambda i: (0, i), memory_space=pltpu.SMEM),
            pl.BlockSpec((TS, D_MODEL), lambda i: (i, 0)),
            pl.BlockSpec((TS, TOP_K), lambda i: (i, 0)),
            pl.BlockSpec((1, D_MODEL), lambda i: (0, 0)),
            pl.BlockSpec(memory_space=pl.ANY),
        ],
        out_specs=(
            pl.BlockSpec((TS, D_MODEL), ptile),
            pl.BlockSpec((N_SAMPLE, D_MODEL), lambda i: (0, 0)),
        ),
        out_shape=(
            jax.ShapeDtypeStruct((N_PROMPT, D_MODEL), f32),
            jax.ShapeDtypeStruct((N_SAMPLE, D_MODEL), f32),
        ),
        scratch_shapes=[pltpu.VMEM((TOP_K, TS, D_MODEL), f32), pltpu.SemaphoreType.DMA((TOP_K,))],
        compiler_params=pltpu.CompilerParams(
            dimension_semantics=("arbitrary",), vmem_limit_bytes=VMEM_LIMIT),
        name="combine",
    )(pos_t, x1, gates_tm, gf, ys)


def kernel(x_prompt, x_sample, state_conv, state_pool, norm_mix_g, w_in, conv_w, pool_w, pool_scale,
           w_out, norm_ffn_g, router_w, router_b, exp_w1, exp_b1, exp_w2, exp_b2, final_norm_g):
    l = 0
    xp = x_prompt.reshape(N_PROMPT, D_MODEL)
    xs = jnp.transpose(x_sample, (1, 0, 2)).reshape(N_SAMPLE, D_MODEL)
    sc_t = jnp.transpose(state_conv[l], (1, 0, 2))
    sp_t = jnp.transpose(state_pool[l], (1, 0, 2))
    rw_t = router_w[l].T
    rwh = rw_t.astype(bf16)
    rwl = (rw_t - rwh.astype(f32)).astype(bf16)

    (x1, xpk, idx_t, rank_t, gate_t, cnt, npc, npp, nsc, nsv) = _mixer(
        xp, xs, sc_t, sp_t, norm_mix_g[l][None, :], w_in[l].astype(bf16), conv_w[l],
        pool_w[l].astype(bf16), pool_scale[l][None, :], w_out[l].astype(bf16),
        norm_ffn_g[l][None, :], rwh, rwl, router_b[l][:, None])

    counts = cnt[:, 0].astype(i32)
    padded = (counts + (TM - 1)) // TM * TM
    gend = jnp.cumsum(padded)
    gstart = gend - padded
    pos_t = jnp.take(gstart, idx_t) + rank_t
    n_used = gend[-1] // TM
    tile_row = jnp.arange(N_ROW_TILES, dtype=i32) * TM
    tile_expert = jnp.sum((tile_row[:, None] >= gend[None, :]).astype(i32), axis=1)
    tile_expert = jnp.minimum(tile_expert, jnp.take(tile_expert, n_used - 1))

    xs_sorted = _dispatch(gstart + counts, padded - counts, n_used[None], pos_t, xpk)
    ys = _experts(tile_expert, n_used[None], xs_sorted, exp_w1[l], exp_b1[l][:, None, :],
                  exp_w2[l], exp_b2[l][:, None, :])
    y_p, y_s = _combine(pos_t, x1, gate_t.T, final_norm_g[None, :], ys)

    y_prompt = y_p.reshape(BATCH, SEQ, D_MODEL)
    y_sample = jnp.transpose(y_s.reshape(DEC_SEQ, DEC_BATCH, D_MODEL), (1, 0, 2))
    new_conv_prompt = npc[None, :, 8 - (CONV_W - 1):, :]
    new_pool_prompt = npp[None, :, HIST - POOL_BUF:, :]
    new_conv_sample = jnp.transpose(nsc, (1, 0, 2))[None]
    new_pool_sample = jnp.concatenate(
        [state_pool[l][:, DEC_SEQ:, :], jnp.transpose(nsv, (1, 0, 2))], axis=1)[None]
    return (y_prompt, y_sample, new_conv_prompt, new_pool_prompt, new_conv_sample, new_pool_sample)
```

```python
import functools

import jax
import jax.numpy as jnp
from jax import lax
from jax.experimental import pallas as pl
from jax.experimental.pallas import tpu as pltpu

D_MODEL = 1024
D_CONV = 512
D_POOL = 512
D_IN = 3 * D_CONV + D_POOL
CONV_W = 3
POOL_WINDOWS = (2, 4, 8, 16)
POOL_GROUP = 128
POOL_BUF = 15
N_EXPERTS = 32
TOP_K = 4
D_FF = 1024
SWIGLU_LIMIT = 7.0
SWIGLU_ALPHA = 1.702
EPS = 1e-5
PAST_LEN = 16384

BATCH, SEQ = 8, 2048
DEC_BATCH, DEC_SEQ = 128, 4
N_PROMPT = BATCH * SEQ
N_SAMPLE = DEC_BATCH * DEC_SEQ
N_TOK = N_PROMPT + N_SAMPLE

TS = 512
SEQ_TILES = SEQ // TS
N_PROMPT_TILES = N_PROMPT // TS
N_TILES = N_PROMPT_TILES + 1
HIST = 16
TM = 256
N_ROWS = (N_TOK * TOP_K + N_EXPERTS * (TM - 1)) // TM * TM
N_ROW_TILES = N_ROWS // TM
D_PACK = D_MODEL // 2

VMEM_LIMIT = 56 * 1024 * 1024

f32 = jnp.float32
bf16 = jnp.bfloat16
i32 = jnp.int32
u32 = jnp.uint32


def _rms(x, g):
    return x * lax.rsqrt(jnp.mean(x * x, axis=-1, keepdims=True) + EPS) * g


def _pack_bf16_pairs(xb):
    bits = lax.bitcast_convert_type(xb.astype(f32), u32)
    return (bits[:, :D_PACK] >> 16) | (bits[:, D_PACK:] & jnp.uint32(0xFFFF0000))


def _unpack_bf16_pairs(p):
    lo = lax.bitcast_convert_type(p << 16, f32)
    hi = lax.bitcast_convert_type(p & jnp.uint32(0xFFFF0000), f32)
    return jnp.concatenate([lo, hi], axis=1).astype(bf16)


def _pool_project(p, pw_ref, ps_ref):
    outs = []
    for g in range(len(POOL_WINDOWS)):
        sl = slice(g * POOL_GROUP, (g + 1) * POOL_GROUP)
        outs.append(jnp.dot(p[:, sl].astype(bf16), pw_ref[g], preferred_element_type=f32))
    return jnp.concatenate(outs, axis=1) * ps_ref[...]


def _post_mix(x, z, wout_ref, g2_ref, rwh_ref, rwl_ref, rb_ref, run_ref,
              x1_ref, xp_ref, idx_ref, rank_ref, gate_ref):
    n = x.shape[0]
    x1 = x + jnp.dot(z.astype(bf16), wout_ref[...], preferred_element_type=f32)
    x1_ref[...] = x1
    xn = _rms(x1, g2_ref[...])
    xb = xn.astype(bf16)
    xp_ref[...] = _pack_bf16_pairs(xb)

    xlo = (xn - xb.astype(f32)).astype(bf16)
    nt = (((1,), (1,)), ((), ()))
    logits = (lax.dot_general(rwh_ref[...], xb, nt, preferred_element_type=f32)
              + lax.dot_general(rwl_ref[...], xb, nt, preferred_element_type=f32)
              + lax.dot_general(rwh_ref[...], xlo, nt, preferred_element_type=f32)
              + rb_ref[...])

    e_iota = lax.broadcasted_iota(i32, (N_EXPERTS, n), 0)
    work = logits
    hots, vals, idxs = [], [], []
    for _ in range(TOP_K):
        m = jnp.max(work, axis=0, keepdims=True)
        ik = jnp.min(jnp.where(work == m, e_iota, N_EXPERTS), axis=0, keepdims=True)
        hot = e_iota == ik
        work = jnp.where(hot, -jnp.inf, work)
        hots.append(hot); vals.append(m); idxs.append(ik)
    exps = [jnp.exp(v - vals[0]) for v in vals]
    den = exps[0] + exps[1] + exps[2] + exps[3]
    gate_ref[...] = jnp.concatenate([e / den for e in exps], axis=0)
    idx_ref[...] = jnp.concatenate(idxs, axis=0)

    multi = jnp.where(hots[0] | hots[1] | hots[2] | hots[3], 1.0, 0.0)
    r_io = lax.broadcasted_iota(i32, (n, n), 0)
    c_io = lax.broadcasted_iota(i32, (n, n), 1)
    tri = jnp.where(r_io < c_io, 1.0, 0.0).astype(bf16)
    base = jnp.dot(multi.astype(bf16), tri, preferred_element_type=f32) + run_ref[...]
    ranks = [jnp.sum(jnp.where(h, base, 0.0), axis=0, keepdims=True) for h in hots]
    rank_ref[...] = jnp.concatenate(ranks, axis=0).astype(i32)
    run_ref[...] = run_ref[...] + jnp.sum(multi, axis=1, keepdims=True)


def _mixer_kernel(xp_in_ref, xs_in_ref, sc_ref, sp_ref, g1_ref, win_ref, cw_ref, pw_ref, ps_ref,
                  wout_ref, g2_ref, rwh_ref, rwl_ref, rb_ref,
                  x1_ref, xpk_ref, idx_ref, rank_ref, gate_ref, cnt_ref,
                  npc_ref, npp_ref, nsc_ref, nsv_ref,
                  ubuf, vbuf, run_ref):
    i = pl.program_id(0)
    post = functools.partial(_post_mix, wout_ref=wout_ref, g2_ref=g2_ref, rwh_ref=rwh_ref,
                             rwl_ref=rwl_ref, rb_ref=rb_ref, run_ref=run_ref, x1_ref=x1_ref,
                             xp_ref=xpk_ref, idx_ref=idx_ref, rank_ref=rank_ref, gate_ref=gate_ref)

    @pl.when(i == 0)
    def _():
        run_ref[...] = jnp.zeros_like(run_ref)

    @pl.when(i < N_PROMPT_TILES)
    def _():
        s = i % SEQ_TILES

        @pl.when(s == 0)
        def _():
            ubuf[0:HIST, :] = jnp.zeros((HIST, D_CONV), f32)
            vbuf[0:HIST, :] = jnp.zeros((HIST, D_POOL), f32)

        x = xp_in_ref[...]
        xn = _rms(x, g1_ref[...]).astype(bf16)
        proj = jnp.dot(xn, win_ref[...], preferred_element_type=f32)
        bg = proj[:, :D_CONV]
        u = proj[:, D_CONV:2 * D_CONV] * proj[:, 2 * D_CONV:3 * D_CONV]
        v = proj[:, 3 * D_CONV:]
        ubuf[HIST:HIST + TS, :] = u
        vbuf[HIST:HIST + TS, :] = v

        y = (ubuf[HIST - 2:HIST - 2 + TS, :] * cw_ref[0:1, :]
             + ubuf[HIST - 1:HIST - 1 + TS, :] * cw_ref[1:2, :]
             + u * cw_ref[2:3, :])
        zc = bg * y

        pos = s * TS + lax.broadcasted_iota(i32, (TS, 1), 0)
        ps = []
        for g, w in enumerate(POOL_WINDOWS):
            sl = slice(g * POOL_GROUP, (g + 1) * POOL_GROUP)
            acc = vbuf[:, sl]
            for step in range(g + 1):
                acc = acc + pltpu.roll(acc, 1 << step, axis=0)
            cnt = jnp.minimum(w, pos + 1).astype(f32)
            ps.append(acc[HIST:, :] * (1.0 / cnt) - v[:, sl])
        zp = _pool_project(jnp.concatenate(ps, axis=1), pw_ref, ps_ref)

        ubuf[HIST - 8:HIST, :] = ubuf[TS + HIST - 8:TS + HIST, :]
        vbuf[0:HIST, :] = vbuf[TS:TS + HIST, :]

        @pl.when(s == SEQ_TILES - 1)
        def _():
            npc_ref[0] = ubuf[HIST - 8:HIST, :]
            npp_ref[0] = vbuf[0:HIST, :]

        post(x, jnp.concatenate([zc, zp], axis=1))

    @pl.when(i == N_PROMPT_TILES)
    def _():
        x = xs_in_ref[...]
        xn = _rms(x, g1_ref[...]).astype(bf16)
        proj = jnp.dot(xn, win_ref[...], preferred_element_type=f32)
        nb = DEC_BATCH
        rows = lambda a, t: a[t * nb:(t + 1) * nb]
        bg = proj[:, :D_CONV]
        u = proj[:, D_CONV:2 * D_CONV] * proj[:, 2 * D_CONV:3 * D_CONV]
        v = proj[:, 3 * D_CONV:]
        up = [sc_ref[j] for j in range(CONV_W - 1)] + [rows(u, t) for t in range(DEC_SEQ)]
        vp = [sp_ref[j] for j in range(POOL_BUF)] + [rows(v, t) for t in range(DEC_SEQ)]
        zc, ps = [], []
        for t in range(DEC_SEQ):
            y = up[t] * cw_ref[0:1, :]
            for k in range(1, CONV_W):
                y = y + up[t + k] * cw_ref[k:k + 1, :]
            zc.append(rows(bg, t) * y)
            pg = []
            for g, w in enumerate(POOL_WINDOWS):
                sl = slice(g * POOL_GROUP, (g + 1) * POOL_GROUP)
                lo = t + POOL_BUF - w + 1
                acc = vp[lo][:, sl]
                for j in range(lo + 1, t + POOL_BUF + 1):
                    acc = acc + vp[j][:, sl]
                cnt = float(min(w, PAST_LEN + t + 1))
                pg.append(acc * (1.0 / cnt) - vp[t + POOL_BUF][:, sl])
            ps.append(jnp.concatenate(pg, axis=1))
        zp = _pool_project(jnp.concatenate(ps, axis=0), pw_ref, ps_ref)
        for j in range(CONV_W - 1):
            nsc_ref[j] = up[DEC_SEQ + j]
        for t in range(DEC_SEQ):
            nsv_ref[t] = rows(v, t)
        post(x, jnp.concatenate([jnp.concatenate(zc, axis=0), zp], axis=1))

    cnt_ref[...] = run_ref[...]


def _mixer(xp, xs, sc_t, sp_t, g1, win, cw, pw, ps, wout, g2, rwh, rwl, rb):
    const = lambda shape: pl.BlockSpec(shape, lambda i: (0,) * len(shape),
                                       pipeline_mode=pl.Buffered(1))
    ptile = lambda i: jnp.minimum(i, N_PROMPT_TILES - 1)
    pbatch = lambda i: jnp.minimum(i, N_PROMPT_TILES - 1) // SEQ_TILES
    out_shape = (
        jax.ShapeDtypeStruct((N_TOK, D_MODEL), f32),
        jax.ShapeDtypeStruct((N_TOK, D_PACK), u32),
        jax.ShapeDtypeStruct((TOP_K, N_TOK), i32),
        jax.ShapeDtypeStruct((TOP_K, N_TOK), i32),
        jax.ShapeDtypeStruct((TOP_K, N_TOK), f32),
        jax.ShapeDtypeStruct((N_EXPERTS, 1), f32),
        jax.ShapeDtypeStruct((BATCH, 8, D_CONV), f32),
        jax.ShapeDtypeStruct((BATCH, HIST, D_POOL), f32),
        jax.ShapeDtypeStruct((CONV_W - 1, DEC_BATCH, D_CONV), f32),
        jax.ShapeDtypeStruct((DEC_SEQ, DEC_BATCH, D_POOL), f32),
    )
    out_specs = (
        pl.BlockSpec((TS, D_MODEL), lambda i: (i, 0)),
        pl.BlockSpec((TS, D_PACK), lambda i: (i, 0)),
        pl.BlockSpec((TOP_K, TS), lambda i: (0, i)),
        pl.BlockSpec((TOP_K, TS), lambda i: (0, i)),
        pl.BlockSpec((TOP_K, TS), lambda i: (0, i)),
        pl.BlockSpec((N_EXPERTS, 1), lambda i: (0, 0)),
        pl.BlockSpec((1, 8, D_CONV), lambda i: (pbatch(i), 0, 0)),
        pl.BlockSpec((1, HIST, D_POOL), lambda i: (pbatch(i), 0, 0)),
        pl.BlockSpec((CONV_W - 1, DEC_BATCH, D_CONV), lambda i: (0, 0, 0)),
        pl.BlockSpec((DEC_SEQ, DEC_BATCH, D_POOL), lambda i: (0, 0, 0)),
    )
    in_specs = [
        pl.BlockSpec((TS, D_MODEL), lambda i: (ptile(i), 0)),
        const((N_SAMPLE, D_MODEL)),
        const((CONV_W - 1, DEC_BATCH, D_CONV)),
        const((POOL_BUF, DEC_BATCH, D_POOL)),
        const((1, D_MODEL)),
        const((D_MODEL, D_IN)),
        const((CONV_W, D_CONV)),
        const((len(POOL_WINDOWS), POOL_GROUP, POOL_GROUP)),
        const((1, D_POOL)),
        const((D_MODEL, D_MODEL)),
        const((1, D_MODEL)),
        const((N_EXPERTS, D_MODEL)),
        const((N_EXPERTS, D_MODEL)),
        const((N_EXPERTS, 1)),
    ]
    return pl.pallas_call(
        _mixer_kernel,
        grid=(N_TILES,),
        in_specs=in_specs,
        out_specs=out_specs,
        out_shape=out_shape,
        scratch_shapes=[
            pltpu.VMEM((TS + HIST, D_CONV), f32),
            pltpu.VMEM((TS + HIST, D_POOL), f32),
            pltpu.VMEM((N_EXPERTS, 1), f32),
        ],
        compiler_params=pltpu.CompilerParams(
            dimension_semantics=("arbitrary",), vmem_limit_bytes=VMEM_LIMIT),
        name="mixer",
    )(xp, xs, sc_t, sp_t, g1, win, cw, pw, ps, wout, g2, rwh, rwl, rb)


def _dispatch_kernel(pstart_ref, plen_ref, nu_ref, pos_ref, x_ref, xs_hbm, zrow, sem):
    i = pl.program_id(0)

    @pl.when(i == 0)
    def _():
        zrow[...] = jnp.zeros_like(zrow)

        def fill_tile(j, c):
            cp = pltpu.make_async_copy(zrow, xs_hbm.at[pl.ds(pl.multiple_of(j * TM, TM), TM)],
                                       sem.at[1])
            cp.start()
            cp.wait()
            return c
        lax.fori_loop(nu_ref[0], N_ROW_TILES, fill_tile, 0)
        for e in range(N_EXPERTS):
            n = plen_ref[e]

            def fill(j, c):
                pltpu.make_async_copy(zrow.at[pl.ds(0, 1)],
                                      xs_hbm.at[pl.ds(pstart_ref[e] + j, 1)], sem.at[1]).start()
                return c
            lax.fori_loop(0, n, fill, 0)

            def drain(j, c):
                pltpu.make_async_copy(zrow.at[pl.ds(0, 1)],
                                      xs_hbm.at[pl.ds(0, 1)], sem.at[1]).wait()
                return c
            lax.fori_loop(0, n, drain, 0)

    def issue(t, c):
        for k in range(TOP_K):
            pltpu.make_async_copy(x_ref.at[pl.ds(t, 1)],
                                  xs_hbm.at[pl.ds(pos_ref[k, t], 1)], sem.at[0]).start()
        return c
    lax.fori_loop(0, TS, issue, 0)
    for k in range(TOP_K):
        pltpu.make_async_copy(x_ref, xs_hbm.at[pl.ds(0, TS)], sem.at[0]).wait()


def _dispatch(pad_start, pad_len, n_used, pos_t, xpk):
    grid_spec = pltpu.PrefetchScalarGridSpec(
        num_scalar_prefetch=3,
        grid=(N_TILES,),
        in_specs=[
            pl.BlockSpec((TOP_K, TS), lambda i, a, b, c: (0, i), memory_space=pltpu.SMEM),
            pl.BlockSpec((TS, D_PACK), lambda i, a, b, c: (i, 0)),
        ],
        out_specs=pl.BlockSpec(memory_space=pl.ANY),
        scratch_shapes=[pltpu.VMEM((TM, D_PACK), u32), pltpu.SemaphoreType.DMA((2,))],
    )
    return pl.pallas_call(
        _dispatch_kernel,
        grid_spec=grid_spec,
        out_shape=jax.ShapeDtypeStruct((N_ROWS, D_PACK), u32),
        compiler_params=pltpu.CompilerParams(dimension_semantics=("arbitrary",)),
        name="dispatch",
    )(pad_start, pad_len, n_used, pos_t, xpk)


def _expert_kernel(te_ref, nu_ref, xs_ref, w1_ref, b1_ref, w2_ref, b2_ref, ys_ref, w1b, w2b):
    i = pl.program_id(0)
    live = i < nu_ref[0]
    new_expert = (i == 0) | (te_ref[i] != te_ref[jnp.maximum(i - 1, 0)])

    @pl.when(live & new_expert)
    def _():
        w1b[...] = w1_ref[0].astype(bf16)
        w2b[...] = w2_ref[0].astype(bf16)

    @pl.when(live)
    def _():
        x = _unpack_bf16_pairs(xs_ref[...])
        gu = jnp.dot(x, w1b[...], preferred_element_type=f32) + b1_ref[0]
        gate = jnp.minimum(gu[:, :D_FF], SWIGLU_LIMIT)
        lin = jnp.clip(gu[:, D_FF:], -SWIGLU_LIMIT, SWIGLU_LIMIT)
        glu = gate * jax.nn.sigmoid(SWIGLU_ALPHA * gate)
        h = (glu * (lin + 1.0)).astype(bf16)
        ys_ref[...] = jnp.dot(h, w2b[...], preferred_element_type=f32) + b2_ref[0]

    @pl.when(jnp.logical_not(live))
    def _():
        ys_ref[...] = jnp.zeros_like(ys_ref)


def _experts(tile_expert, n_used, xs, w1, b1, w2, b2):
    row = lambda i, te, nu: (jnp.minimum(i, nu[0] - 1), 0)
    wsel = lambda i, te, nu: (te[i], 0, 0)
    grid_spec = pltpu.PrefetchScalarGridSpec(
        num_scalar_prefetch=2,
        grid=(N_ROW_TILES,),
        in_specs=[
            pl.BlockSpec((TM, D_PACK), row),
            pl.BlockSpec((1, D_MODEL, 2 * D_FF), wsel),
            pl.BlockSpec((1, 1, 2 * D_FF), wsel),
            pl.BlockSpec((1, D_FF, D_MODEL), wsel),
            pl.BlockSpec((1, 1, D_MODEL), wsel),
        ],
        out_specs=pl.BlockSpec((TM, D_MODEL), lambda i, te, nu: (i, 0)),
        scratch_shapes=[pltpu.VMEM((D_MODEL, 2 * D_FF), bf16), pltpu.VMEM((D_FF, D_MODEL), bf16)],
    )
    return pl.pallas_call(
        _expert_kernel,
        grid_spec=grid_spec,
        out_shape=jax.ShapeDtypeStruct((N_ROWS, D_MODEL), f32),
        compiler_params=pltpu.CompilerParams(
            dimension_semantics=("arbitrary",), vmem_limit_bytes=VMEM_LIMIT),
        name="experts",
    )(tile_expert, n_used, xs, w1, b1, w2, b2)


def _combine_kernel(pos_ref, x1_ref, g_ref, gf_ref, ys_hbm, yp_ref, ysm_ref, buf, sem):
    i = pl.program_id(0)

    def issue(t, c):
        for k in range(TOP_K):
            pltpu.make_async_copy(ys_hbm.at[pl.ds(pos_ref[k, t], 1)],
                                  buf.at[k, pl.ds(t, 1)], sem.at[k]).start()
        return c
    lax.fori_loop(0, TS, issue, 0)
    for k in range(TOP_K):
        pltpu.make_async_copy(ys_hbm.at[pl.ds(0, TS)], buf.at[k], sem.at[k]).wait()

    g = g_ref[...]
    y = x1_ref[...]
    for k in range(TOP_K):
        y = y + g[:, k:k + 1] * buf[k]
    out = _rms(y, gf_ref[...])

    @pl.when(i < N_PROMPT_TILES)
    def _():
        yp_ref[...] = out

    @pl.when(i == N_PROMPT_TILES)
    def _():
        ysm_ref[...] = out


def _combine(pos_t, x1, gates_tm, gf, ys):
    ptile = lambda i: (jnp.minimum(i, N_PROMPT_TILES - 1), 0)
    return pl.pallas_call(
        _combine_kernel,
        grid=(N_TILES,),
        in_specs=[
            pl.BlockSpec((TOP_K, TS), lambda i: (0, i), memory_space=pltpu.SMEM),
            pl.BlockSpec((TS, D_MODEL), lambda i: (i, 0)),
            pl.BlockSpec((TS, TOP_K), lambda i: (i, 0)),
            pl.BlockSpec((1, D_MODEL), lambda i: (0, 0)),
            pl.BlockSpec(memory_space=pl.ANY),
        ],
        out_specs=(
            pl.BlockSpec((TS, D_MODEL), ptile),
            pl.BlockSpec((N_SAMPLE, D_MODEL), lambda i: (0, 0)),
        ),
        out_shape=(
            jax.ShapeDtypeStruct((N_PROMPT, D_MODEL), f32),
            jax.ShapeDtypeStruct((N_SAMPLE, D_MODEL), f32),
        ),
        scratch_shapes=[pltpu.VMEM((TOP_K, TS, D_MODEL), f32), pltpu.SemaphoreType.DMA((TOP_K,))],
        compiler_params=pltpu.CompilerParams(
            dimension_semantics=("arbitrary",), vmem_limit_bytes=VMEM_LIMIT),
        name="combine",
    )(pos_t, x1, gates_tm, gf, ys)


def kernel(x_prompt, x_sample, state_conv, state_pool, norm_mix_g, w_in, conv_w, pool_w, pool_scale,
           w_out, norm_ffn_g, router_w, router_b, exp_w1, exp_b1, exp_w2, exp_b2, final_norm_g):
    l = 0
    xp = x_prompt.reshape(N_PROMPT, D_MODEL)
    xs = jnp.transpose(x_sample, (1, 0, 2)).reshape(N_SAMPLE, D_MODEL)
    sc_t = jnp.transpose(state_conv[l], (1, 0, 2))
    sp_t = jnp.transpose(state_pool[l], (1, 0, 2))
    rw_t = router_w[l].T
    rwh = rw_t.astype(bf16)
    rwl = (rw_t - rwh.astype(f32)).astype(bf16)

    (x1, xpk, idx_t, rank_t, gate_t, cnt, npc, npp, nsc, nsv) = _mixer(
        xp, xs, sc_t, sp_t, norm_mix_g[l][None, :], w_in[l].astype(bf16), conv_w[l],
        pool_w[l].astype(bf16), pool_scale[l][None, :], w_out[l].astype(bf16),
        norm_ffn_g[l][None, :], rwh, rwl, router_b[l][:, None])

    counts = cnt[:, 0].astype(i32)
    padded = (counts + (TM - 1)) // TM * TM
    gend = jnp.cumsum(padded)
    gstart = gend - padded
    e_ids = jnp.arange(N_EXPERTS, dtype=i32)[:, None, None]
    pos_t = jnp.sum(jnp.where(idx_t[None] == e_ids, gstart[:, None, None], 0), axis=0) + rank_t
    n_used = gend[-1] // TM
    tile_row = jnp.arange(N_ROW_TILES, dtype=i32) * TM
    tile_expert = jnp.sum((tile_row[:, None] >= gend[None, :]).astype(i32), axis=1)
    tile_expert = jnp.minimum(tile_expert, jnp.take(tile_expert, n_used - 1))

    xs_sorted = _dispatch(gstart + counts, padded - counts, n_used[None], pos_t, xpk)
    ys = _experts(tile_expert, n_used[None], xs_sorted, exp_w1[l], exp_b1[l][:, None, :],
                  exp_w2[l], exp_b2[l][:, None, :])
    y_p, y_s = _combine(pos_t, x1, gate_t.T, final_norm_g[None, :], ys)

    y_prompt = y_p.reshape(BATCH, SEQ, D_MODEL)
    y_sample = jnp.transpose(y_s.reshape(DEC_SEQ, DEC_BATCH, D_MODEL), (1, 0, 2))
    new_conv_prompt = npc[None, :, 8 - (CONV_W - 1):, :]
    new_pool_prompt = npp[None, :, HIST - POOL_BUF:, :]
    new_conv_sample = jnp.transpose(nsc, (1, 0, 2))[None]
    new_pool_sample = jnp.concatenate(
        [state_pool[l][:, DEC_SEQ:, :], jnp.transpose(nsv, (1, 0, 2))], axis=1)[None]
    return (y_prompt, y_sample, new_conv_prompt, new_pool_prompt, new_conv_sample, new_pool_sample)
```

```python
import functools

import jax
import jax.numpy as jnp
from jax import lax
from jax.experimental import pallas as pl
from jax.experimental.pallas import tpu as pltpu

D_MODEL = 1024
D_CONV = 512
D_POOL = 512
D_IN = 3 * D_CONV + D_POOL
CONV_W = 3
POOL_WINDOWS = (2, 4, 8, 16)
POOL_GROUP = 128
POOL_BUF = 15
N_EXPERTS = 32
TOP_K = 4
D_FF = 1024
SWIGLU_LIMIT = 7.0
SWIGLU_ALPHA = 1.702
EPS = 1e-5
PAST_LEN = 16384

BATCH, SEQ = 8, 2048
DEC_BATCH, DEC_SEQ = 128, 4
N_PROMPT = BATCH * SEQ
N_SAMPLE = DEC_BATCH * DEC_SEQ
N_TOK = N_PROMPT + N_SAMPLE

TS = 512
SEQ_TILES = SEQ // TS
N_PROMPT_TILES = N_PROMPT // TS
N_TILES = N_PROMPT_TILES + 1
HIST = 16
TM = 256
PIECE = 16
S_BLK = 256
S_MAX = (TS * TOP_K + N_EXPERTS * (PIECE - 1) + S_BLK - 1) // S_BLK * S_BLK
N_ROW_TILES = (N_TOK * TOP_K + N_TILES * N_EXPERTS * (PIECE - 1) + N_EXPERTS * (TM - PIECE)) // TM
N_ROWS = N_ROW_TILES * TM

VMEM_LIMIT = 56 * 1024 * 1024

f32 = jnp.float32
bf16 = jnp.bfloat16
i32 = jnp.int32


def _rms(x, g):
    return x * lax.rsqrt(jnp.mean(x * x, axis=-1, keepdims=True) + EPS) * g


def _pool_project(p, pw_ref, ps_ref):
    outs = []
    for g in range(len(POOL_WINDOWS)):
        sl = slice(g * POOL_GROUP, (g + 1) * POOL_GROUP)
        outs.append(jnp.dot(p[:, sl].astype(bf16), pw_ref[g], preferred_element_type=f32))
    return jnp.concatenate(outs, axis=1) * ps_ref[...]


def _post_mix(x, z, wout_ref, g2_ref, rwh_ref, rwl_ref, rb_ref,
              x1_ref, xn_ref, idx_ref, rank_ref, gate_ref, cnt_ref):
    n = x.shape[0]
    x1 = x + jnp.dot(z.astype(bf16), wout_ref[...], preferred_element_type=f32)
    x1_ref[...] = x1
    xn = _rms(x1, g2_ref[...])
    xb = xn.astype(bf16)
    xn_ref[...] = xb

    xlo = (xn - xb.astype(f32)).astype(bf16)
    nt = (((1,), (1,)), ((), ()))
    logits = (lax.dot_general(rwh_ref[...], xb, nt, preferred_element_type=f32)
              + lax.dot_general(rwl_ref[...], xb, nt, preferred_element_type=f32)
              + lax.dot_general(rwh_ref[...], xlo, nt, preferred_element_type=f32)
              + rb_ref[...])

    e_iota = lax.broadcasted_iota(i32, (N_EXPERTS, n), 0)
    work = logits
    hots, vals, idxs = [], [], []
    for _ in range(TOP_K):
        m = jnp.max(work, axis=0, keepdims=True)
        ik = jnp.min(jnp.where(work == m, e_iota, N_EXPERTS), axis=0, keepdims=True)
        hot = e_iota == ik
        work = jnp.where(hot, -jnp.inf, work)
        hots.append(hot); vals.append(m); idxs.append(ik)
    exps = [jnp.exp(v - vals[0]) for v in vals]
    den = exps[0] + exps[1] + exps[2] + exps[3]
    gate_ref[...] = jnp.concatenate([e / den for e in exps], axis=0)
    idx_ref[...] = jnp.concatenate(idxs, axis=0)

    multi = jnp.where(hots[0] | hots[1] | hots[2] | hots[3], 1.0, 0.0)
    r_io = lax.broadcasted_iota(i32, (n, n), 0)
    c_io = lax.broadcasted_iota(i32, (n, n), 1)
    tri = jnp.where(r_io < c_io, 1.0, 0.0).astype(bf16)
    before = jnp.dot(multi.astype(bf16), tri, preferred_element_type=f32)
    ranks = [jnp.sum(jnp.where(h, before, 0.0), axis=0, keepdims=True) for h in hots]
    rank_ref[...] = jnp.concatenate(ranks, axis=0).astype(i32)
    cnt_ref[...] = jnp.sum(multi, axis=1, keepdims=True)


def _mixer_kernel(xp_in_ref, xs_in_ref, sc_ref, sp_ref, g1_ref, win_ref, cw_ref, pw_ref, ps_ref,
                  wout_ref, g2_ref, rwh_ref, rwl_ref, rb_ref,
                  x1_ref, xn_ref, idx_ref, rank_ref, gate_ref, cnt_ref,
                  npc_ref, npp_ref, nsc_ref, nsv_ref,
                  ubuf, vbuf):
    i = pl.program_id(0)
    post = functools.partial(_post_mix, wout_ref=wout_ref, g2_ref=g2_ref, rwh_ref=rwh_ref,
                             rwl_ref=rwl_ref, rb_ref=rb_ref, x1_ref=x1_ref, xn_ref=xn_ref,
                             idx_ref=idx_ref, rank_ref=rank_ref, gate_ref=gate_ref,
                             cnt_ref=cnt_ref)

    @pl.when(i < N_PROMPT_TILES)
    def _():
        s = i % SEQ_TILES

        @pl.when(s == 0)
        def _():
            ubuf[0:HIST, :] = jnp.zeros((HIST, D_CONV), f32)
            vbuf[0:HIST, :] = jnp.zeros((HIST, D_POOL), f32)

        x = xp_in_ref[...]
        xn = _rms(x, g1_ref[...]).astype(bf16)
        proj = jnp.dot(xn, win_ref[...], preferred_element_type=f32)
        bg = proj[:, :D_CONV]
        u = proj[:, D_CONV:2 * D_CONV] * proj[:, 2 * D_CONV:3 * D_CONV]
        v = proj[:, 3 * D_CONV:]
        ubuf[HIST:HIST + TS, :] = u
        vbuf[HIST:HIST + TS, :] = v

        y = (ubuf[HIST - 2:HIST - 2 + TS, :] * cw_ref[0:1, :]
             + ubuf[HIST - 1:HIST - 1 + TS, :] * cw_ref[1:2, :]
             + u * cw_ref[2:3, :])
        zc = bg * y

        pos = s * TS + lax.broadcasted_iota(i32, (TS, 1), 0)
        ps = []
        for g, w in enumerate(POOL_WINDOWS):
            sl = slice(g * POOL_GROUP, (g + 1) * POOL_GROUP)
            acc = vbuf[:, sl]
            for step in range(g + 1):
                acc = acc + pltpu.roll(acc, 1 << step, axis=0)
            cnt = jnp.minimum(w, pos + 1).astype(f32)
            ps.append(acc[HIST:, :] * (1.0 / cnt) - v[:, sl])
        zp = _pool_project(jnp.concatenate(ps, axis=1), pw_ref, ps_ref)

        ubuf[HIST - 8:HIST, :] = ubuf[TS + HIST - 8:TS + HIST, :]
        vbuf[0:HIST, :] = vbuf[TS:TS + HIST, :]

        @pl.when(s == SEQ_TILES - 1)
        def _():
            npc_ref[0] = ubuf[HIST - 8:HIST, :]
            npp_ref[0] = vbuf[0:HIST, :]

        post(x, jnp.concatenate([zc, zp], axis=1))

    @pl.when(i == N_PROMPT_TILES)
    def _():
        x = xs_in_ref[...]
        xn = _rms(x, g1_ref[...]).astype(bf16)
        proj = jnp.dot(xn, win_ref[...], preferred_element_type=f32)
        nb = DEC_BATCH
        rows = lambda a, t: a[t * nb:(t + 1) * nb]
        bg = proj[:, :D_CONV]
        u = proj[:, D_CONV:2 * D_CONV] * proj[:, 2 * D_CONV:3 * D_CONV]
        v = proj[:, 3 * D_CONV:]
        up = [sc_ref[j] for j in range(CONV_W - 1)] + [rows(u, t) for t in range(DEC_SEQ)]
        vp = [sp_ref[j] for j in range(POOL_BUF)] + [rows(v, t) for t in range(DEC_SEQ)]
        zc, ps = [], []
        for t in range(DEC_SEQ):
            y = up[t] * cw_ref[0:1, :]
            for k in range(1, CONV_W):
                y = y + up[t + k] * cw_ref[k:k + 1, :]
            zc.append(rows(bg, t) * y)
            pg = []
            for g, w in enumerate(POOL_WINDOWS):
                sl = slice(g * POOL_GROUP, (g + 1) * POOL_GROUP)
                lo = t + POOL_BUF - w + 1
                acc = vp[lo][:, sl]
                for j in range(lo + 1, t + POOL_BUF + 1):
                    acc = acc + vp[j][:, sl]
                cnt = float(min(w, PAST_LEN + t + 1))
                pg.append(acc * (1.0 / cnt) - vp[t + POOL_BUF][:, sl])
            ps.append(jnp.concatenate(pg, axis=1))
        zp = _pool_project(jnp.concatenate(ps, axis=0), pw_ref, ps_ref)
        for j in range(CONV_W - 1):
            nsc_ref[j] = up[DEC_SEQ + j]
        for t in range(DEC_SEQ):
            nsv_ref[t] = rows(v, t)
        post(x, jnp.concatenate([jnp.concatenate(zc, axis=0), zp], axis=1))


def _mixer(xp, xs, sc_t, sp_t, g1, win, cw, pw, ps, wout, g2, rwh, rwl, rb):
    const = lambda shape: pl.BlockSpec(shape, lambda i: (0,) * len(shape),
                                       pipeline_mode=pl.Buffered(1))
    ptile = lambda i: jnp.minimum(i, N_PROMPT_TILES - 1)
    pbatch = lambda i: jnp.minimum(i, N_PROMPT_TILES - 1) // SEQ_TILES
    out_shape = (
        jax.ShapeDtypeStruct((N_TOK, D_MODEL), f32),
        jax.ShapeDtypeStruct((N_TOK, D_MODEL), bf16),
        jax.ShapeDtypeStruct((TOP_K, N_TOK), i32),
        jax.ShapeDtypeStruct((TOP_K, N_TOK), i32),
        jax.ShapeDtypeStruct((TOP_K, N_TOK), f32),
        jax.ShapeDtypeStruct((N_TILES * N_EXPERTS, 1), f32),
        jax.ShapeDtypeStruct((BATCH, 8, D_CONV), f32),
        jax.ShapeDtypeStruct((BATCH, HIST, D_POOL), f32),
        jax.ShapeDtypeStruct((CONV_W - 1, DEC_BATCH, D_CONV), f32),
        jax.ShapeDtypeStruct((DEC_SEQ, DEC_BATCH, D_POOL), f32),
    )
    out_specs = (
        pl.BlockSpec((TS, D_MODEL), lambda i: (i, 0)),
        pl.BlockSpec((TS, D_MODEL), lambda i: (i, 0)),
        pl.BlockSpec((TOP_K, TS), lambda i: (0, i)),
        pl.BlockSpec((TOP_K, TS), lambda i: (0, i)),
        pl.BlockSpec((TOP_K, TS), lambda i: (0, i)),
        pl.BlockSpec((N_EXPERTS, 1), lambda i: (i, 0)),
        pl.BlockSpec((1, 8, D_CONV), lambda i: (pbatch(i), 0, 0)),
        pl.BlockSpec((1, HIST, D_POOL), lambda i: (pbatch(i), 0, 0)),
        pl.BlockSpec((CONV_W - 1, DEC_BATCH, D_CONV), lambda i: (0, 0, 0)),
        pl.BlockSpec((DEC_SEQ, DEC_BATCH, D_POOL), lambda i: (0, 0, 0)),
    )
    in_specs = [
        pl.BlockSpec((TS, D_MODEL), lambda i: (ptile(i), 0)),
        const((N_SAMPLE, D_MODEL)),
        const((CONV_W - 1, DEC_BATCH, D_CONV)),
        const((POOL_BUF, DEC_BATCH, D_POOL)),
        const((1, D_MODEL)),
        const((D_MODEL, D_IN)),
        const((CONV_W, D_CONV)),
        const((len(POOL_WINDOWS), POOL_GROUP, POOL_GROUP)),
        const((1, D_POOL)),
        const((D_MODEL, D_MODEL)),
        const((1, D_MODEL)),
        const((N_EXPERTS, D_MODEL)),
        const((N_EXPERTS, D_MODEL)),
        const((N_EXPERTS, 1)),
    ]
    return pl.pallas_call(
        _mixer_kernel,
        grid=(N_TILES,),
        in_specs=in_specs,
        out_specs=out_specs,
        out_shape=out_shape,
        scratch_shapes=[
            pltpu.VMEM((TS + HIST, D_CONV), f32),
            pltpu.VMEM((TS + HIST, D_POOL), f32),
        ],
        compiler_params=pltpu.CompilerParams(
            dimension_semantics=("arbitrary",), vmem_limit_bytes=VMEM_LIMIT),
        name="mixer",
    )(xp, xs, sc_t, sp_t, g1, win, cw, pw, ps, wout, g2, rwh, rwl, rb)


def _piece_copies(src_of, dst_of, sem, lo_ref, np_ref, cs_ref, tile):
    def copy(e, p):
        lo = pl.multiple_of(lo_ref[tile * N_EXPERTS + e] + p * PIECE, PIECE)
        cs = pl.multiple_of(cs_ref[tile * N_EXPERTS + e] + p * PIECE, PIECE)
        return pltpu.make_async_copy(src_of(lo, cs), dst_of(lo, cs), sem)

    def start_expert(e, c):
        def start(p, c2):
            copy(e, p).start()
            return c2
        return lax.fori_loop(0, np_ref[tile * N_EXPERTS + e], start, c)

    def wait_expert(e, c):
        def wait(p, c2):
            copy(e, p).wait()
            return c2
        return lax.fori_loop(0, np_ref[tile * N_EXPERTS + e], wait, c)

    return (lambda: lax.fori_loop(0, N_EXPERTS, start_expert, 0),
            lambda: lax.fori_loop(0, N_EXPERTS, wait_expert, 0))


def _dispatch_kernel(lo_ref, np_ref, cs_ref, ts_ref, tp_ref, nu_ref,
                     slot_ref, x_ref, xs_hbm, srt, zblk, sem):
    i = pl.program_id(0)

    @pl.when(i == 0)
    def _():
        zblk[...] = jnp.zeros_like(zblk)

        def fill_expert(e, c):
            def dst(p):
                row = pl.multiple_of(ts_ref[e] + p * PIECE, PIECE)
                return xs_hbm.at[pl.ds(row, PIECE)]

            def start(p, c2):
                pltpu.make_async_copy(zblk.at[pl.ds(0, PIECE)], dst(p), sem.at[1]).start()
                return c2

            def wait(p, c2):
                pltpu.make_async_copy(zblk.at[pl.ds(0, PIECE)], dst(p), sem.at[1]).wait()
                return c2
            lax.fori_loop(0, tp_ref[e], start, 0)
            return lax.fori_loop(0, tp_ref[e], wait, c)
        lax.fori_loop(0, N_EXPERTS, fill_expert, 0)

        def fill_tile(j, c):
            cp = pltpu.make_async_copy(zblk, xs_hbm.at[pl.ds(pl.multiple_of(j * TM, TM), TM)],
                                       sem.at[1])
            cp.start()
            cp.wait()
            return c
        lax.fori_loop(nu_ref[0], N_ROW_TILES, fill_tile, 0)

    slot = slot_ref[...]
    x = x_ref[...]

    def sort_block(b, c):
        base = pl.multiple_of(b * S_BLK, S_BLK)
        s_io = base + lax.broadcasted_iota(i32, (S_BLK, TS), 0)
        hit = s_io == slot[0:1, :]
        for k in range(1, TOP_K):
            hit = hit | (s_io == slot[k:k + 1, :])
        sel = jnp.where(hit, 1.0, 0.0).astype(bf16)
        srt[pl.ds(base, S_BLK), :] = jnp.dot(sel, x, preferred_element_type=f32).astype(bf16)
        return c
    lax.fori_loop(0, S_MAX // S_BLK, sort_block, 0)

    start, wait = _piece_copies(lambda lo, cs: srt.at[pl.ds(lo, PIECE)],
                                lambda lo, cs: xs_hbm.at[pl.ds(cs, PIECE)],
                                sem.at[0], lo_ref, np_ref, cs_ref, i)
    start()
    wait()


def _dispatch(lo, npieces, cs, tail_start, tail_pieces, n_used, slot_t, xn):
    grid_spec = pltpu.PrefetchScalarGridSpec(
        num_scalar_prefetch=6,
        grid=(N_TILES,),
        in_specs=[
            pl.BlockSpec((TOP_K, TS), lambda i, *_: (0, i)),
            pl.BlockSpec((TS, D_MODEL), lambda i, *_: (i, 0)),
        ],
        out_specs=pl.BlockSpec(memory_space=pl.ANY),
        scratch_shapes=[pltpu.VMEM((S_MAX, D_MODEL), bf16), pltpu.VMEM((TM, D_MODEL), bf16),
                        pltpu.SemaphoreType.DMA((2,))],
    )
    return pl.pallas_call(
        _dispatch_kernel,
        grid_spec=grid_spec,
        out_shape=jax.ShapeDtypeStruct((N_ROWS, D_MODEL), bf16),
        compiler_params=pltpu.CompilerParams(
            dimension_semantics=("arbitrary",), vmem_limit_bytes=VMEM_LIMIT),
        name="dispatch",
    )(lo, npieces, cs, tail_start, tail_pieces, n_used, slot_t, xn)


def _expert_kernel(te_ref, nu_ref, xs_ref, w1_ref, b1_ref, w2_ref, b2_ref, ys_ref, w1b, w2b):
    i = pl.program_id(0)
    live = i < nu_ref[0]
    new_expert = (i == 0) | (te_ref[i] != te_ref[jnp.maximum(i - 1, 0)])

    @pl.when(live & new_expert)
    def _():
        w1b[...] = w1_ref[0].astype(bf16)
        w2b[...] = w2_ref[0].astype(bf16)

    @pl.when(live)
    def _():
        gu = jnp.dot(xs_ref[...], w1b[...], preferred_element_type=f32) + b1_ref[0]
        gate = jnp.minimum(gu[:, :D_FF], SWIGLU_LIMIT)
        lin = jnp.clip(gu[:, D_FF:], -SWIGLU_LIMIT, SWIGLU_LIMIT)
        glu = gate * jax.nn.sigmoid(SWIGLU_ALPHA * gate)
        h = (glu * (lin + 1.0)).astype(bf16)
        out = jnp.dot(h, w2b[...], preferred_element_type=f32) + b2_ref[0]
        ys_ref[...] = out.astype(bf16)

    @pl.when(jnp.logical_not(live))
    def _():
        ys_ref[...] = jnp.zeros_like(ys_ref)


def _experts(tile_expert, n_used, xs, w1, b1, w2, b2):
    row = lambda i, te, nu: (jnp.minimum(i, nu[0] - 1), 0)
    wsel = lambda i, te, nu: (te[i], 0, 0)
    grid_spec = pltpu.PrefetchScalarGridSpec(
        num_scalar_prefetch=2,
        grid=(N_ROW_TILES,),
        in_specs=[
            pl.BlockSpec((TM, D_MODEL), row),
            pl.BlockSpec((1, D_MODEL, 2 * D_FF), wsel),
            pl.BlockSpec((1, 1, 2 * D_FF), wsel),
            pl.BlockSpec((1, D_FF, D_MODEL), wsel),
            pl.BlockSpec((1, 1, D_MODEL), wsel),
        ],
        out_specs=pl.BlockSpec((TM, D_MODEL), lambda i, te, nu: (i, 0)),
        scratch_shapes=[pltpu.VMEM((D_MODEL, 2 * D_FF), bf16), pltpu.VMEM((D_FF, D_MODEL), bf16)],
    )
    return pl.pallas_call(
        _expert_kernel,
        grid_spec=grid_spec,
        out_shape=jax.ShapeDtypeStruct((N_ROWS, D_MODEL), bf16),
        compiler_params=pltpu.CompilerParams(
            dimension_semantics=("arbitrary",), vmem_limit_bytes=VMEM_LIMIT),
        name="experts",
    )(tile_expert, n_used, xs, w1, b1, w2, b2)


def _combine_kernel(lo_ref, np_ref, cs_ref, slot_ref, g_ref, x1_ref, gf_ref, ys_hbm,
                    yp_ref, ysm_ref, buf, whi, wlo, sem):
    i = pl.program_id(0)

    @pl.when(i == 0)
    def _():
        buf[...] = jnp.zeros_like(buf)

    start, wait = _piece_copies(lambda lo, cs: ys_hbm.at[pl.ds(cs, PIECE)],
                                lambda lo, cs: buf.at[pl.ds(lo, PIECE)],
                                sem.at[0], lo_ref, np_ref, cs_ref, i)
    start()

    slot = slot_ref[...]
    g = g_ref[...]
    for b in range(S_MAX // S_BLK):
        s_io = b * S_BLK + lax.broadcasted_iota(i32, (TS, S_BLK), 1)
        w = jnp.where(s_io == slot[:, 0:1], g[:, 0:1], 0.0)
        for k in range(1, TOP_K):
            w = w + jnp.where(s_io == slot[:, k:k + 1], g[:, k:k + 1], 0.0)
        hi = w.astype(bf16)
        whi[:, b * S_BLK:(b + 1) * S_BLK] = hi
        wlo[:, b * S_BLK:(b + 1) * S_BLK] = (w - hi.astype(f32)).astype(bf16)

    wait()
    rows = buf[...]
    y = (x1_ref[...] + jnp.dot(whi[...], rows, preferred_element_type=f32)
         + jnp.dot(wlo[...], rows, preferred_element_type=f32))
    out = _rms(y, gf_ref[...])

    @pl.when(i < N_PROMPT_TILES)
    def _():
        yp_ref[...] = out

    @pl.when(i == N_PROMPT_TILES)
    def _():
        ysm_ref[...] = out


def _combine(lo, npieces, cs, slot_tm, gates_tm, x1, gf, ys):
    ptile = lambda i, *_: (jnp.minimum(i, N_PROMPT_TILES - 1), 0)
    grid_spec = pltpu.PrefetchScalarGridSpec(
        num_scalar_prefetch=3,
        grid=(N_TILES,),
        in_specs=[
            pl.BlockSpec((TS, TOP_K), lambda i, *_: (i, 0)),
            pl.BlockSpec((TS, TOP_K), lambda i, *_: (i, 0)),
            pl.BlockSpec((TS, D_MODEL), lambda i, *_: (i, 0)),
            pl.BlockSpec((1, D_MODEL), lambda i, *_: (0, 0)),
            pl.BlockSpec(memory_space=pl.ANY),
        ],
        out_specs=(
            pl.BlockSpec((TS, D_MODEL), ptile),
            pl.BlockSpec((N_SAMPLE, D_MODEL), lambda i, *_: (0, 0)),
        ),
        scratch_shapes=[pltpu.VMEM((S_MAX, D_MODEL), bf16), pltpu.VMEM((TS, S_MAX), bf16),
                        pltpu.VMEM((TS, S_MAX), bf16), pltpu.SemaphoreType.DMA((1,))],
    )
    return pl.pallas_call(
        _combine_kernel,
        grid_spec=grid_spec,
        out_shape=(
            jax.ShapeDtypeStruct((N_PROMPT, D_MODEL), f32),
            jax.ShapeDtypeStruct((N_SAMPLE, D_MODEL), f32),
        ),
        compiler_params=pltpu.CompilerParams(
            dimension_semantics=("arbitrary",), vmem_limit_bytes=VMEM_LIMIT),
        name="combine",
    )(lo, npieces, cs, slot_tm, gates_tm, x1, gf, ys)


def _routing_tables(cnt, idx_t, rank_t):
    n = cnt.reshape(N_TILES, N_EXPERTS).astype(i32)
    c = (n + (PIECE - 1)) // PIECE * PIECE
    lo = jnp.cumsum(c, axis=1) - c
    group = jnp.sum(c, axis=0)
    padded = (group + (TM - 1)) // TM * TM
    gend = jnp.cumsum(padded)
    gstart = gend - padded
    cs = gstart[None, :] + jnp.cumsum(c, axis=0) - c
    n_used = gend[-1] // TM
    tile_row = jnp.arange(N_ROW_TILES, dtype=i32) * TM
    tile_expert = jnp.sum((tile_row[:, None] >= gend[None, :]).astype(i32), axis=1)
    tile_expert = jnp.minimum(tile_expert, jnp.take(tile_expert, n_used - 1))
    lo_tok = jnp.repeat(lo.T, TS, axis=1)
    e_ids = jnp.arange(N_EXPERTS, dtype=i32)[:, None, None]
    slot_t = jnp.sum(jnp.where(idx_t[None] == e_ids, lo_tok[:, None, :], 0), axis=0) + rank_t
    return dict(lo=lo.reshape(-1), npieces=(c // PIECE).reshape(-1), cs=cs.reshape(-1),
                tail_start=gstart + group, tail_pieces=(padded - group) // PIECE,
                n_used=n_used[None], tile_expert=tile_expert, slot_t=slot_t)


def kernel(x_prompt, x_sample, state_conv, state_pool, norm_mix_g, w_in, conv_w, pool_w, pool_scale,
           w_out, norm_ffn_g, router_w, router_b, exp_w1, exp_b1, exp_w2, exp_b2, final_norm_g):
    l = 0
    xp = x_prompt.reshape(N_PROMPT, D_MODEL)
    xs = jnp.transpose(x_sample, (1, 0, 2)).reshape(N_SAMPLE, D_MODEL)
    sc_t = jnp.transpose(state_conv[l], (1, 0, 2))
    sp_t = jnp.transpose(state_pool[l], (1, 0, 2))
    rw_t = router_w[l].T
    rwh = rw_t.astype(bf16)
    rwl = (rw_t - rwh.astype(f32)).astype(bf16)

    (x1, xn, idx_t, rank_t, gate_t, cnt, npc, npp, nsc, nsv) = _mixer(
        xp, xs, sc_t, sp_t, norm_mix_g[l][None, :], w_in[l].astype(bf16), conv_w[l],
        pool_w[l].astype(bf16), pool_scale[l][None, :], w_out[l].astype(bf16),
        norm_ffn_g[l][None, :], rwh, rwl, router_b[l][:, None])

    t = _routing_tables(cnt, idx_t, rank_t)
    xs_sorted = _dispatch(t["lo"], t["npieces"], t["cs"], t["tail_start"], t["tail_pieces"],
                          t["n_used"], t["slot_t"], xn)
    ys = _experts(t["tile_expert"], t["n_used"], xs_sorted, exp_w1[l], exp_b1[l][:, None, :],
                  exp_w2[l], exp_b2[l][:, None, :])
    y_p, y_s = _combine(t["lo"], t["npieces"], t["cs"], t["slot_t"].T, gate_t.T, x1,
                        final_norm_g[None, :], ys)

    y_prompt = y_p.reshape(BATCH, SEQ, D_MODEL)
    y_sample = jnp.transpose(y_s.reshape(DEC_SEQ, DEC_BATCH, D_MODEL), (1, 0, 2))
    new_conv_prompt = npc[None, :, 8 - (CONV_W - 1):, :]
    new_pool_prompt = npp[None, :, HIST - POOL_BUF:, :]
    new_conv_sample = jnp.transpose(nsc, (1, 0, 2))[None]
    new_pool_sample = jnp.concatenate(
        [state_pool[l][:, DEC_SEQ:, :], jnp.transpose(nsv, (1, 0, 2))], axis=1)[None]
    return (y_prompt, y_sample, new_conv_prompt, new_pool_prompt, new_conv_sample, new_pool_sample)
```

```python
import functools

import jax
import jax.numpy as jnp
from jax import lax
from jax.experimental import pallas as pl
from jax.experimental.pallas import tpu as pltpu

D_MODEL = 1024
D_CONV = 512
D_POOL = 512
D_IN = 3 * D_CONV + D_POOL
CONV_W = 3
POOL_WINDOWS = (2, 4, 8, 16)
POOL_GROUP = 128
POOL_BUF = 15
N_EXPERTS = 32
TOP_K = 4
D_FF = 1024
SWIGLU_LIMIT = 7.0
SWIGLU_ALPHA = 1.702
EPS = 1e-5
PAST_LEN = 16384

BATCH, SEQ = 8, 2048
DEC_BATCH, DEC_SEQ = 128, 4
N_PROMPT = BATCH * SEQ
N_SAMPLE = DEC_BATCH * DEC_SEQ
N_TOK = N_PROMPT + N_SAMPLE

TS = 512
SEQ_TILES = SEQ // TS
N_PROMPT_TILES = N_PROMPT // TS
N_TILES = N_PROMPT_TILES + 1
HIST = 16
TM = 256
PIECE = 16
S_BLK = 256
S_MAX = (TS * TOP_K + N_EXPERTS * (PIECE - 1) + S_BLK - 1) // S_BLK * S_BLK
N_ROW_TILES = (N_TOK * TOP_K + N_TILES * N_EXPERTS * (PIECE - 1) + N_EXPERTS * (TM - PIECE)) // TM
N_ROWS = N_ROW_TILES * TM

VMEM_LIMIT = 56 * 1024 * 1024

f32 = jnp.float32
bf16 = jnp.bfloat16
i32 = jnp.int32


def _rms(x, g):
    return x * lax.rsqrt(jnp.mean(x * x, axis=-1, keepdims=True) + EPS) * g


def _pool_project(p, pw_ref, ps_ref):
    outs = []
    for g in range(len(POOL_WINDOWS)):
        sl = slice(g * POOL_GROUP, (g + 1) * POOL_GROUP)
        outs.append(jnp.dot(p[:, sl].astype(bf16), pw_ref[g], preferred_element_type=f32))
    return jnp.concatenate(outs, axis=1) * ps_ref[...]


def _post_mix(x, z, wout_ref, g2_ref, rwh_ref, rwl_ref, rb_ref,
              x1_ref, xn_ref, idx_ref, rank_ref, gate_ref, cnt_ref):
    n = x.shape[0]
    x1 = x + jnp.dot(z.astype(bf16), wout_ref[...], preferred_element_type=f32)
    x1_ref[...] = x1
    xn = _rms(x1, g2_ref[...])
    xb = xn.astype(bf16)
    xn_ref[...] = xb

    xlo = (xn - xb.astype(f32)).astype(bf16)
    nt = (((1,), (1,)), ((), ()))
    logits = (lax.dot_general(rwh_ref[...], xb, nt, preferred_element_type=f32)
              + lax.dot_general(rwl_ref[...], xb, nt, preferred_element_type=f32)
              + lax.dot_general(rwh_ref[...], xlo, nt, preferred_element_type=f32)
              + rb_ref[...])

    e_iota = lax.broadcasted_iota(i32, (N_EXPERTS, n), 0)
    work = logits
    hots, vals, idxs = [], [], []
    for _ in range(TOP_K):
        m = jnp.max(work, axis=0, keepdims=True)
        ik = jnp.min(jnp.where(work == m, e_iota, N_EXPERTS), axis=0, keepdims=True)
        hot = e_iota == ik
        work = jnp.where(hot, -jnp.inf, work)
        hots.append(hot); vals.append(m); idxs.append(ik)
    exps = [jnp.exp(v - vals[0]) for v in vals]
    den = exps[0] + exps[1] + exps[2] + exps[3]
    gate_ref[...] = jnp.concatenate([e / den for e in exps], axis=0)
    idx_ref[...] = jnp.concatenate(idxs, axis=0)

    multi = jnp.where(hots[0] | hots[1] | hots[2] | hots[3], 1.0, 0.0)
    r_io = lax.broadcasted_iota(i32, (n, n), 0)
    c_io = lax.broadcasted_iota(i32, (n, n), 1)
    tri = jnp.where(r_io < c_io, 1.0, 0.0).astype(bf16)
    before = jnp.dot(multi.astype(bf16), tri, preferred_element_type=f32)
    ranks = [jnp.sum(jnp.where(h, before, 0.0), axis=0, keepdims=True) for h in hots]
    rank_ref[...] = jnp.concatenate(ranks, axis=0).astype(i32)
    cnt_ref[...] = jnp.sum(multi, axis=1, keepdims=True)


def _mixer_kernel(xp_in_ref, xs_in_ref, sc_ref, sp_ref, g1_ref, win_ref, cw_ref, pw_ref, ps_ref,
                  wout_ref, g2_ref, rwh_ref, rwl_ref, rb_ref,
                  x1_ref, xn_ref, idx_ref, rank_ref, gate_ref, cnt_ref,
                  npc_ref, npp_ref, nsc_ref, nsv_ref,
                  ubuf, vbuf):
    i = pl.program_id(0)
    post = functools.partial(_post_mix, wout_ref=wout_ref, g2_ref=g2_ref, rwh_ref=rwh_ref,
                             rwl_ref=rwl_ref, rb_ref=rb_ref, x1_ref=x1_ref, xn_ref=xn_ref,
                             idx_ref=idx_ref, rank_ref=rank_ref, gate_ref=gate_ref,
                             cnt_ref=cnt_ref)

    @pl.when(i < N_PROMPT_TILES)
    def _():
        s = i % SEQ_TILES

        @pl.when(s == 0)
        def _():
            ubuf[0:HIST, :] = jnp.zeros((HIST, D_CONV), f32)
            vbuf[0:HIST, :] = jnp.zeros((HIST, D_POOL), f32)

        x = xp_in_ref[...]
        xn = _rms(x, g1_ref[...]).astype(bf16)
        proj = jnp.dot(xn, win_ref[...], preferred_element_type=f32)
        bg = proj[:, :D_CONV]
        u = proj[:, D_CONV:2 * D_CONV] * proj[:, 2 * D_CONV:3 * D_CONV]
        v = proj[:, 3 * D_CONV:]
        ubuf[HIST:HIST + TS, :] = u
        vbuf[HIST:HIST + TS, :] = v

        y = (ubuf[HIST - 2:HIST - 2 + TS, :] * cw_ref[0:1, :]
             + ubuf[HIST - 1:HIST - 1 + TS, :] * cw_ref[1:2, :]
             + u * cw_ref[2:3, :])
        zc = bg * y

        pos = s * TS + lax.broadcasted_iota(i32, (TS, 1), 0)
        ps = []
        for g, w in enumerate(POOL_WINDOWS):
            sl = slice(g * POOL_GROUP, (g + 1) * POOL_GROUP)
            acc = vbuf[:, sl]
            for step in range(g + 1):
                acc = acc + pltpu.roll(acc, 1 << step, axis=0)
            cnt = jnp.minimum(w, pos + 1).astype(f32)
            ps.append(acc[HIST:, :] * (1.0 / cnt) - v[:, sl])
        zp = _pool_project(jnp.concatenate(ps, axis=1), pw_ref, ps_ref)

        ubuf[HIST - 8:HIST, :] = ubuf[TS + HIST - 8:TS + HIST, :]
        vbuf[0:HIST, :] = vbuf[TS:TS + HIST, :]

        @pl.when(s == SEQ_TILES - 1)
        def _():
            npc_ref[0] = ubuf[HIST - 8:HIST, :]
            npp_ref[0] = vbuf[0:HIST, :]

        post(x, jnp.concatenate([zc, zp], axis=1))

    @pl.when(i == N_PROMPT_TILES)
    def _():
        x = xs_in_ref[...]
        xn = _rms(x, g1_ref[...]).astype(bf16)
        proj = jnp.dot(xn, win_ref[...], preferred_element_type=f32)
        nb = DEC_BATCH
        rows = lambda a, t: a[t * nb:(t + 1) * nb]
        bg = proj[:, :D_CONV]
        u = proj[:, D_CONV:2 * D_CONV] * proj[:, 2 * D_CONV:3 * D_CONV]
        v = proj[:, 3 * D_CONV:]
        up = [sc_ref[j] for j in range(CONV_W - 1)] + [rows(u, t) for t in range(DEC_SEQ)]
        vp = [sp_ref[j] for j in range(POOL_BUF)] + [rows(v, t) for t in range(DEC_SEQ)]
        zc, ps = [], []
        for t in range(DEC_SEQ):
            y = up[t] * cw_ref[0:1, :]
            for k in range(1, CONV_W):
                y = y + up[t + k] * cw_ref[k:k + 1, :]
            zc.append(rows(bg, t) * y)
            pg = []
            for g, w in enumerate(POOL_WINDOWS):
                sl = slice(g * POOL_GROUP, (g + 1) * POOL_GROUP)
                lo = t + POOL_BUF - w + 1
                acc = vp[lo][:, sl]
                for j in range(lo + 1, t + POOL_BUF + 1):
                    acc = acc + vp[j][:, sl]
                cnt = float(min(w, PAST_LEN + t + 1))
                pg.append(acc * (1.0 / cnt) - vp[t + POOL_BUF][:, sl])
            ps.append(jnp.concatenate(pg, axis=1))
        zp = _pool_project(jnp.concatenate(ps, axis=0), pw_ref, ps_ref)
        for j in range(CONV_W - 1):
            nsc_ref[j] = up[DEC_SEQ + j]
        for t in range(DEC_SEQ):
            nsv_ref[t] = rows(v, t)
        post(x, jnp.concatenate([jnp.concatenate(zc, axis=0), zp], axis=1))


def _mixer(xp, xs, sc_t, sp_t, g1, win, cw, pw, ps, wout, g2, rwh, rwl, rb):
    const = lambda shape: pl.BlockSpec(shape, lambda i: (0,) * len(shape),
                                       pipeline_mode=pl.Buffered(1))
    ptile = lambda i: jnp.minimum(i, N_PROMPT_TILES - 1)
    pbatch = lambda i: jnp.minimum(i, N_PROMPT_TILES - 1) // SEQ_TILES
    out_shape = (
        jax.ShapeDtypeStruct((N_TOK, D_MODEL), f32),
        jax.ShapeDtypeStruct((N_TOK, D_MODEL), bf16),
        jax.ShapeDtypeStruct((TOP_K, N_TOK), i32),
        jax.ShapeDtypeStruct((TOP_K, N_TOK), i32),
        jax.ShapeDtypeStruct((TOP_K, N_TOK), f32),
        jax.ShapeDtypeStruct((N_TILES * N_EXPERTS, 1), f32),
        jax.ShapeDtypeStruct((BATCH, 8, D_CONV), f32),
        jax.ShapeDtypeStruct((BATCH, HIST, D_POOL), f32),
        jax.ShapeDtypeStruct((CONV_W - 1, DEC_BATCH, D_CONV), f32),
        jax.ShapeDtypeStruct((DEC_SEQ, DEC_BATCH, D_POOL), f32),
    )
    out_specs = (
        pl.BlockSpec((TS, D_MODEL), lambda i: (i, 0)),
        pl.BlockSpec((TS, D_MODEL), lambda i: (i, 0)),
        pl.BlockSpec((TOP_K, TS), lambda i: (0, i)),
        pl.BlockSpec((TOP_K, TS), lambda i: (0, i)),
        pl.BlockSpec((TOP_K, TS), lambda i: (0, i)),
        pl.BlockSpec((N_EXPERTS, 1), lambda i: (i, 0)),
        pl.BlockSpec((1, 8, D_CONV), lambda i: (pbatch(i), 0, 0)),
        pl.BlockSpec((1, HIST, D_POOL), lambda i: (pbatch(i), 0, 0)),
        pl.BlockSpec((CONV_W - 1, DEC_BATCH, D_CONV), lambda i: (0, 0, 0)),
        pl.BlockSpec((DEC_SEQ, DEC_BATCH, D_POOL), lambda i: (0, 0, 0)),
    )
    in_specs = [
        pl.BlockSpec((TS, D_MODEL), lambda i: (ptile(i), 0)),
        const((N_SAMPLE, D_MODEL)),
        const((CONV_W - 1, DEC_BATCH, D_CONV)),
        const((POOL_BUF, DEC_BATCH, D_POOL)),
        const((1, D_MODEL)),
        const((D_MODEL, D_IN)),
        const((CONV_W, D_CONV)),
        const((len(POOL_WINDOWS), POOL_GROUP, POOL_GROUP)),
        const((1, D_POOL)),
        const((D_MODEL, D_MODEL)),
        const((1, D_MODEL)),
        const((N_EXPERTS, D_MODEL)),
        const((N_EXPERTS, D_MODEL)),
        const((N_EXPERTS, 1)),
    ]
    return pl.pallas_call(
        _mixer_kernel,
        grid=(N_TILES,),
        in_specs=in_specs,
        out_specs=out_specs,
        out_shape=out_shape,
        scratch_shapes=[
            pltpu.VMEM((TS + HIST, D_CONV), f32),
            pltpu.VMEM((TS + HIST, D_POOL), f32),
        ],
        compiler_params=pltpu.CompilerParams(
            dimension_semantics=("arbitrary",), vmem_limit_bytes=VMEM_LIMIT),
        name="mixer",
    )(xp, xs, sc_t, sp_t, g1, win, cw, pw, ps, wout, g2, rwh, rwl, rb)


def _start_pieces(src_of, dst_of, sem, lo_ref, np_ref, cs_ref, tile):
    def start_expert(e, c):
        lo0 = lo_ref[tile * N_EXPERTS + e]
        cs0 = cs_ref[tile * N_EXPERTS + e]

        def start(p, c2):
            lo = pl.multiple_of(lo0 + p * PIECE, PIECE)
            cs = pl.multiple_of(cs0 + p * PIECE, PIECE)
            pltpu.make_async_copy(src_of(lo, cs), dst_of(lo, cs), sem).start()
            return c2
        return lax.fori_loop(0, np_ref[tile * N_EXPERTS + e], start, c)
    lax.fori_loop(0, N_EXPERTS, start_expert, 0)


def _wait_rows(src, dst, sem, rows):
    n = PIECE
    while n <= S_MAX:
        @pl.when((rows & n) != 0)
        def _(n=n):
            pltpu.make_async_copy(src.at[pl.ds(0, n)], dst.at[pl.ds(0, n)], sem).wait()
        n *= 2


def _dispatch_kernel(lo_ref, np_ref, cs_ref, tot_ref, ts_ref, tp_ref, nu_ref,
                     slot_ref, x_ref, xs_hbm, srt, zblk, sem):
    i = pl.program_id(0)
    par = i % 2
    cur = srt.at[par]

    @pl.when(i == 0)
    def _():
        zblk[...] = jnp.zeros_like(zblk)

        def fill_expert(e, c):
            def dst(p):
                row = pl.multiple_of(ts_ref[e] + p * PIECE, PIECE)
                return xs_hbm.at[pl.ds(row, PIECE)]

            def start(p, c2):
                pltpu.make_async_copy(zblk.at[pl.ds(0, PIECE)], dst(p), sem.at[2]).start()
                return c2

            def wait(p, c2):
                pltpu.make_async_copy(zblk.at[pl.ds(0, PIECE)], dst(p), sem.at[2]).wait()
                return c2
            lax.fori_loop(0, tp_ref[e], start, 0)
            return lax.fori_loop(0, tp_ref[e], wait, c)
        lax.fori_loop(0, N_EXPERTS, fill_expert, 0)

        def fill_tile(j, c):
            cp = pltpu.make_async_copy(zblk, xs_hbm.at[pl.ds(pl.multiple_of(j * TM, TM), TM)],
                                       sem.at[2])
            cp.start()
            cp.wait()
            return c
        lax.fori_loop(nu_ref[0], N_ROW_TILES, fill_tile, 0)

    slot = slot_ref[...]
    x = x_ref[...]

    def sort_block(b, c):
        base = pl.multiple_of(b * S_BLK, S_BLK)
        s_io = base + lax.broadcasted_iota(i32, (S_BLK, TS), 0)
        hit = s_io == slot[0:1, :]
        for k in range(1, TOP_K):
            hit = hit | (s_io == slot[k:k + 1, :])
        sel = jnp.where(hit, 1.0, 0.0).astype(bf16)
        cur[pl.ds(base, S_BLK), :] = jnp.dot(sel, x, preferred_element_type=f32).astype(bf16)
        return c
    lax.fori_loop(0, (tot_ref[i] + (S_BLK - 1)) // S_BLK, sort_block, 0)

    _start_pieces(lambda lo, cs: cur.at[pl.ds(lo, PIECE)],
                  lambda lo, cs: xs_hbm.at[pl.ds(cs, PIECE)],
                  sem.at[par], lo_ref, np_ref, cs_ref, i)

    @pl.when(i > 0)
    def _():
        _wait_rows(srt.at[1 - par], xs_hbm, sem.at[1 - par], tot_ref[i - 1])

    @pl.when(i == N_TILES - 1)
    def _():
        _wait_rows(cur, xs_hbm, sem.at[par], tot_ref[i])


def _dispatch(lo, npieces, cs, tot, tail_start, tail_pieces, n_used, slot_t, xn):
    grid_spec = pltpu.PrefetchScalarGridSpec(
        num_scalar_prefetch=7,
        grid=(N_TILES,),
        in_specs=[
            pl.BlockSpec((TOP_K, TS), lambda i, *_: (0, i)),
            pl.BlockSpec((TS, D_MODEL), lambda i, *_: (i, 0)),
        ],
        out_specs=pl.BlockSpec(memory_space=pl.ANY),
        scratch_shapes=[pltpu.VMEM((2, S_MAX, D_MODEL), bf16), pltpu.VMEM((TM, D_MODEL), bf16),
                        pltpu.SemaphoreType.DMA((3,))],
    )
    return pl.pallas_call(
        _dispatch_kernel,
        grid_spec=grid_spec,
        out_shape=jax.ShapeDtypeStruct((N_ROWS, D_MODEL), bf16),
        compiler_params=pltpu.CompilerParams(
            dimension_semantics=("arbitrary",), vmem_limit_bytes=VMEM_LIMIT),
        name="dispatch",
    )(lo, npieces, cs, tot, tail_start, tail_pieces, n_used, slot_t, xn)


def _expert_kernel(te_ref, nu_ref, first_ref, ord_ref, nxt_ref,
                   xs_ref, b1_ref, b2_ref, w1_hbm, w2_hbm, ys_ref, w1f, w2f, w1b, w2b, sem):
    i = pl.program_id(0)
    live = i < nu_ref[0]

    def weight_copies(e, slot):
        return (pltpu.make_async_copy(w1_hbm.at[e], w1f.at[slot], sem.at[slot]),
                pltpu.make_async_copy(w2_hbm.at[e], w2f.at[slot], sem.at[2 + slot]))

    @pl.when(live & (first_ref[i] == 1))
    def _():
        slot = ord_ref[i] % 2

        @pl.when(i == 0)
        def _():
            for cp in weight_copies(te_ref[0], 0):
                cp.start()
        for cp in weight_copies(te_ref[i], slot):
            cp.wait()

        @pl.when(nxt_ref[i] >= 0)
        def _():
            for cp in weight_copies(nxt_ref[i], 1 - slot):
                cp.start()
        w1b[...] = w1f[slot].astype(bf16)
        w2b[...] = w2f[slot].astype(bf16)

    @pl.when(live)
    def _():
        gu = jnp.dot(xs_ref[...], w1b[...], preferred_element_type=f32) + b1_ref[0]
        gate = jnp.minimum(gu[:, :D_FF], SWIGLU_LIMIT)
        lin = jnp.clip(gu[:, D_FF:], -SWIGLU_LIMIT, SWIGLU_LIMIT)
        glu = gate * jax.nn.sigmoid(SWIGLU_ALPHA * gate)
        h = (glu * (lin + 1.0)).astype(bf16)
        out = jnp.dot(h, w2b[...], preferred_element_type=f32) + b2_ref[0]
        ys_ref[...] = out.astype(bf16)

    @pl.when(jnp.logical_not(live))
    def _():
        ys_ref[...] = jnp.zeros_like(ys_ref)


def _experts(tile_expert, n_used, first, order, nxt, xs, w1, b1, w2, b2):
    row = lambda i, te, nu, *_: (jnp.minimum(i, nu[0] - 1), 0)
    wsel = lambda i, te, *_: (te[i], 0, 0)
    grid_spec = pltpu.PrefetchScalarGridSpec(
        num_scalar_prefetch=5,
        grid=(N_ROW_TILES,),
        in_specs=[
            pl.BlockSpec((TM, D_MODEL), row),
            pl.BlockSpec((1, 1, 2 * D_FF), wsel),
            pl.BlockSpec((1, 1, D_MODEL), wsel),
            pl.BlockSpec(memory_space=pl.ANY),
            pl.BlockSpec(memory_space=pl.ANY),
        ],
        out_specs=pl.BlockSpec((TM, D_MODEL), lambda i, *_: (i, 0)),
        scratch_shapes=[pltpu.VMEM((2, D_MODEL, 2 * D_FF), f32), pltpu.VMEM((2, D_FF, D_MODEL), f32),
                        pltpu.VMEM((D_MODEL, 2 * D_FF), bf16), pltpu.VMEM((D_FF, D_MODEL), bf16),
                        pltpu.SemaphoreType.DMA((4,))],
    )
    return pl.pallas_call(
        _expert_kernel,
        grid_spec=grid_spec,
        out_shape=jax.ShapeDtypeStruct((N_ROWS, D_MODEL), bf16),
        compiler_params=pltpu.CompilerParams(
            dimension_semantics=("arbitrary",), vmem_limit_bytes=VMEM_LIMIT),
        name="experts",
    )(tile_expert, n_used, first, order, nxt, xs, b1, b2, w1, w2)


def _combine_kernel(lo_ref, np_ref, cs_ref, tot_ref, slot_ref, g_ref, x1_ref, gf_ref, ys_hbm,
                    yp_ref, ysm_ref, buf, whi, wlo, sem):
    i = pl.program_id(0)
    par = i % 2

    def fetch(tile, b):
        _start_pieces(lambda lo, cs: ys_hbm.at[pl.ds(cs, PIECE)],
                      lambda lo, cs: buf.at[b, pl.ds(lo, PIECE)],
                      sem.at[b], lo_ref, np_ref, cs_ref, tile)

    @pl.when(i == 0)
    def _():
        buf[...] = jnp.zeros_like(buf)
        fetch(0, 0)

    @pl.when(i + 1 < N_TILES)
    def _():
        fetch(i + 1, 1 - par)

    slot = slot_ref[...]
    g = g_ref[...]
    for b in range(S_MAX // S_BLK):
        s_io = b * S_BLK + lax.broadcasted_iota(i32, (TS, S_BLK), 1)
        w = jnp.where(s_io == slot[:, 0:1], g[:, 0:1], 0.0)
        for k in range(1, TOP_K):
            w = w + jnp.where(s_io == slot[:, k:k + 1], g[:, k:k + 1], 0.0)
        hi = w.astype(bf16)
        whi[:, b * S_BLK:(b + 1) * S_BLK] = hi
        wlo[:, b * S_BLK:(b + 1) * S_BLK] = (w - hi.astype(f32)).astype(bf16)

    _wait_rows(ys_hbm, buf.at[par], sem.at[par], tot_ref[i])
    rows = buf[par]
    y = (x1_ref[...] + jnp.dot(whi[...], rows, preferred_element_type=f32)
         + jnp.dot(wlo[...], rows, preferred_element_type=f32))
    out = _rms(y, gf_ref[...])

    @pl.when(i < N_PROMPT_TILES)
    def _():
        yp_ref[...] = out

    @pl.when(i == N_PROMPT_TILES)
    def _():
        ysm_ref[...] = out


def _combine(lo, npieces, cs, tot, slot_tm, gates_tm, x1, gf, ys):
    ptile = lambda i, *_: (jnp.minimum(i, N_PROMPT_TILES - 1), 0)
    grid_spec = pltpu.PrefetchScalarGridSpec(
        num_scalar_prefetch=4,
        grid=(N_TILES,),
        in_specs=[
            pl.BlockSpec((TS, TOP_K), lambda i, *_: (i, 0)),
            pl.BlockSpec((TS, TOP_K), lambda i, *_: (i, 0)),
            pl.BlockSpec((TS, D_MODEL), lambda i, *_: (i, 0)),
            pl.BlockSpec((1, D_MODEL), lambda i, *_: (0, 0)),
            pl.BlockSpec(memory_space=pl.ANY),
        ],
        out_specs=(
            pl.BlockSpec((TS, D_MODEL), ptile),
            pl.BlockSpec((N_SAMPLE, D_MODEL), lambda i, *_: (0, 0)),
        ),
        scratch_shapes=[pltpu.VMEM((2, S_MAX, D_MODEL), bf16), pltpu.VMEM((TS, S_MAX), bf16),
                        pltpu.VMEM((TS, S_MAX), bf16), pltpu.SemaphoreType.DMA((2,))],
    )
    return pl.pallas_call(
        _combine_kernel,
        grid_spec=grid_spec,
        out_shape=(
            jax.ShapeDtypeStruct((N_PROMPT, D_MODEL), f32),
            jax.ShapeDtypeStruct((N_SAMPLE, D_MODEL), f32),
        ),
        compiler_params=pltpu.CompilerParams(
            dimension_semantics=("arbitrary",), vmem_limit_bytes=VMEM_LIMIT),
        name="combine",
    )(lo, npieces, cs, tot, slot_tm, gates_tm, x1, gf, ys)


def _routing_tables(cnt, idx_t, rank_t):
    n = cnt.reshape(N_TILES, N_EXPERTS).astype(i32)
    c = (n + (PIECE - 1)) // PIECE * PIECE
    lo = jnp.cumsum(c, axis=1) - c
    group = jnp.sum(c, axis=0)
    padded = (group + (TM - 1)) // TM * TM
    gend = jnp.cumsum(padded)
    gstart = gend - padded
    cs = gstart[None, :] + jnp.cumsum(c, axis=0) - c
    n_used = gend[-1] // TM
    tile_row = jnp.arange(N_ROW_TILES, dtype=i32) * TM
    tile_expert = jnp.sum((tile_row[:, None] >= gend[None, :]).astype(i32), axis=1)
    tile_expert = jnp.minimum(tile_expert, jnp.take(tile_expert, n_used - 1))
    tile_id = jnp.arange(N_ROW_TILES, dtype=i32)
    first = (tile_id < n_used) & ((tile_id == 0) | (tile_expert != jnp.roll(tile_expert, 1)))
    order = jnp.maximum(jnp.cumsum(first.astype(i32)) - 1, 0)
    e_ar = jnp.arange(N_EXPERTS, dtype=i32)
    later = (e_ar[None, :] > e_ar[:, None]) & (padded[None, :] > 0)
    nxt_e = jnp.min(jnp.where(later, e_ar[None, :], N_EXPERTS), axis=1)
    nxt_e = jnp.where(nxt_e == N_EXPERTS, -1, nxt_e)
    nxt = jnp.sum(jnp.where(tile_expert[:, None] == e_ar[None, :], nxt_e[None, :], 0), axis=1)
    lo_tok = jnp.repeat(lo.T, TS, axis=1)
    e_ids = jnp.arange(N_EXPERTS, dtype=i32)[:, None, None]
    slot_t = jnp.sum(jnp.where(idx_t[None] == e_ids, lo_tok[:, None, :], 0), axis=0) + rank_t
    return dict(lo=lo.reshape(-1), npieces=(c // PIECE).reshape(-1), cs=cs.reshape(-1),
                tot=jnp.sum(c, axis=1), tail_start=gstart + group,
                tail_pieces=(padded - group) // PIECE, n_used=n_used[None],
                tile_expert=tile_expert, first=first.astype(i32), order=order, nxt=nxt,
                slot_t=slot_t)


def kernel(x_prompt, x_sample, state_conv, state_pool, norm_mix_g, w_in, conv_w, pool_w, pool_scale,
           w_out, norm_ffn_g, router_w, router_b, exp_w1, exp_b1, exp_w2, exp_b2, final_norm_g):
    l = 0
    xp = x_prompt.reshape(N_PROMPT, D_MODEL)
    xs = jnp.transpose(x_sample, (1, 0, 2)).reshape(N_SAMPLE, D_MODEL)
    sc_t = jnp.transpose(state_conv[l], (1, 0, 2))
    sp_t = jnp.transpose(state_pool[l], (1, 0, 2))
    rw_t = router_w[l].T
    rwh = rw_t.astype(bf16)
    rwl = (rw_t - rwh.astype(f32)).astype(bf16)

    (x1, xn, idx_t, rank_t, gate_t, cnt, npc, npp, nsc, nsv) = _mixer(
        xp, xs, sc_t, sp_t, norm_mix_g[l][None, :], w_in[l].astype(bf16), conv_w[l],
        pool_w[l].astype(bf16), pool_scale[l][None, :], w_out[l].astype(bf16),
        norm_ffn_g[l][None, :], rwh, rwl, router_b[l][:, None])

    t = _routing_tables(cnt, idx_t, rank_t)
    xs_sorted = _dispatch(t["lo"], t["npieces"], t["cs"], t["tot"], t["tail_start"],
                          t["tail_pieces"], t["n_used"], t["slot_t"], xn)
    ys = _experts(t["tile_expert"], t["n_used"], t["first"], t["order"], t["nxt"], xs_sorted,
                  exp_w1[l], exp_b1[l][:, None, :], exp_w2[l], exp_b2[l][:, None, :])
    y_p, y_s = _combine(t["lo"], t["npieces"], t["cs"], t["tot"], t["slot_t"].T, gate_t.T, x1,
                        final_norm_g[None, :], ys)

    y_prompt = y_p.reshape(BATCH, SEQ, D_MODEL)
    y_sample = jnp.transpose(y_s.reshape(DEC_SEQ, DEC_BATCH, D_MODEL), (1, 0, 2))
    new_conv_prompt = npc[None, :, 8 - (CONV_W - 1):, :]
    new_pool_prompt = npp[None, :, HIST - POOL_BUF:, :]
    new_conv_sample = jnp.transpose(nsc, (1, 0, 2))[None]
    new_pool_sample = jnp.concatenate(
        [state_pool[l][:, DEC_SEQ:, :], jnp.transpose(nsv, (1, 0, 2))], axis=1)[None]
    return (y_prompt, y_sample, new_conv_prompt, new_pool_prompt, new_conv_sample, new_pool_sample)
```

```python
import jax
import jax.numpy as jnp
from jax import lax
from jax.experimental import pallas as pl
from jax.experimental.pallas import tpu as pltpu

D_MODEL = 1024
D_CONV = 512
D_POOL = 512
D_IN = 3 * D_CONV + D_POOL
CONV_W = 3
POOL_WINDOWS = (2, 4, 8, 16)
POOL_GROUP = 128
POOL_BUF = 15
N_EXPERTS = 32
TOP_K = 4
D_FF = 1024
SWIGLU_LIMIT = 7.0
SWIGLU_ALPHA = 1.702
EPS = 1e-5
PAST_LEN = 16384

BATCH, SEQ = 8, 2048
DEC_BATCH, DEC_SEQ = 128, 4
N_PROMPT = BATCH * SEQ
N_SAMPLE = DEC_BATCH * DEC_SEQ
N_TOK = N_PROMPT + N_SAMPLE

TS = 512
SEQ_TILES = SEQ // TS
N_PROMPT_TILES = N_PROMPT // TS
N_TILES = N_PROMPT_TILES + 1
HIST = 16
TM = 256
PIECE = 16
PIECES = TM // PIECE
S_BLK = 256
S_MAX = (TS * TOP_K + N_EXPERTS * (PIECE - 1) + PIECE + S_BLK - 1) // S_BLK * S_BLK
ZERO_ROW = S_MAX - PIECE
N_ROW_TILES = (N_TOK * TOP_K + N_TILES * N_EXPERTS * (PIECE - 1) + N_EXPERTS * (TM - PIECE)) // TM
N_SORT_ROWS = N_TILES * S_MAX
DUMP_ROW = N_SORT_ROWS

VMEM_LIMIT = 56 * 1024 * 1024

f32 = jnp.float32
bf16 = jnp.bfloat16
i32 = jnp.int32


def _rms(x, g):
    return x * lax.rsqrt(jnp.mean(x * x, axis=-1, keepdims=True) + EPS) * g


def _pool_project(p, pw_ref, ps_ref):
    outs = []
    for g in range(len(POOL_WINDOWS)):
        sl = slice(g * POOL_GROUP, (g + 1) * POOL_GROUP)
        outs.append(jnp.dot(p[:, sl].astype(bf16), pw_ref[g], preferred_element_type=f32))
    return jnp.concatenate(outs, axis=1) * ps_ref[...]


def _route_and_sort(x, z, wout_ref, g2_ref, rwh_ref, rwl_ref, rb_ref,
                    x1_ref, srt_ref, slot_ref, gate_ref, cnt_ref):
    n = x.shape[0]
    x1 = x + jnp.dot(z.astype(bf16), wout_ref[...], preferred_element_type=f32)
    x1_ref[...] = x1
    xn = _rms(x1, g2_ref[...])
    xb = xn.astype(bf16)

    xlo = (xn - xb.astype(f32)).astype(bf16)
    nt = (((1,), (1,)), ((), ()))
    logits = (lax.dot_general(rwh_ref[...], xb, nt, preferred_element_type=f32)
              + lax.dot_general(rwl_ref[...], xb, nt, preferred_element_type=f32)
              + lax.dot_general(rwh_ref[...], xlo, nt, preferred_element_type=f32)
              + rb_ref[...])

    e_iota = lax.broadcasted_iota(i32, (N_EXPERTS, n), 0)
    work = logits
    hots, vals = [], []
    for _ in range(TOP_K):
        m = jnp.max(work, axis=0, keepdims=True)
        ik = jnp.min(jnp.where(work == m, e_iota, N_EXPERTS), axis=0, keepdims=True)
        hot = e_iota == ik
        work = jnp.where(hot, -jnp.inf, work)
        hots.append(hot); vals.append(m)
    exps = [jnp.exp(v - vals[0]) for v in vals]
    den = exps[0] + exps[1] + exps[2] + exps[3]
    gate_ref[...] = jnp.concatenate([e / den for e in exps], axis=0)

    multi = jnp.where(hots[0] | hots[1] | hots[2] | hots[3], 1.0, 0.0)
    r_io = lax.broadcasted_iota(i32, (n, n), 0)
    c_io = lax.broadcasted_iota(i32, (n, n), 1)
    tri = jnp.where(r_io < c_io, 1.0, 0.0).astype(bf16)
    before = jnp.dot(multi.astype(bf16), tri, preferred_element_type=f32)
    count = jnp.sum(multi, axis=1, keepdims=True)
    cnt_ref[...] = count
    pieces = jnp.floor((count + (PIECE - 1)) * (1.0 / PIECE))
    er = lax.broadcasted_iota(i32, (N_EXPERTS, N_EXPERTS), 0)
    ec = lax.broadcasted_iota(i32, (N_EXPERTS, N_EXPERTS), 1)
    lower = jnp.where(ec < er, 1.0, 0.0).astype(bf16)
    pieces_b = jnp.broadcast_to(pieces, (N_EXPERTS, 128)).astype(bf16)
    start = PIECE * jnp.dot(lower, pieces_b, preferred_element_type=f32)[:, 0:1]
    base = before + start
    slot = jnp.concatenate(
        [jnp.sum(jnp.where(h, base, 0.0), axis=0, keepdims=True) for h in hots], axis=0).astype(i32)
    slot_ref[...] = slot

    def sort_block(b, c):
        row0 = pl.multiple_of(b * S_BLK, S_BLK)
        s_io = row0 + lax.broadcasted_iota(i32, (S_BLK, n), 0)
        hit = s_io == slot[0:1, :]
        for k in range(1, TOP_K):
            hit = hit | (s_io == slot[k:k + 1, :])
        sel = jnp.where(hit, 1.0, 0.0).astype(bf16)
        srt_ref[pl.ds(row0, S_BLK), :] = jnp.dot(sel, xb, preferred_element_type=f32).astype(bf16)
        return c
    lax.fori_loop(0, S_MAX // S_BLK, sort_block, 0)


def _mixer_kernel(xp_in_ref, xs_in_ref, sc_ref, sp_ref, g1_ref, win_ref, cw_ref, pw_ref, ps_ref,
                  wout_ref, g2_ref, rwh_ref, rwl_ref, rb_ref,
                  x1_ref, srt_ref, slot_ref, gate_ref, cnt_ref,
                  npc_ref, npp_ref, nsc_ref, nsv_ref,
                  ubuf, vbuf, xcur, zcur):
    i = pl.program_id(0)

    @pl.when(i < N_PROMPT_TILES)
    def _():
        s = i % SEQ_TILES

        @pl.when(s == 0)
        def _():
            ubuf[0:HIST, :] = jnp.zeros((HIST, D_CONV), f32)
            vbuf[0:HIST, :] = jnp.zeros((HIST, D_POOL), f32)

        x = xp_in_ref[...]
        xcur[...] = x
        xn = _rms(x, g1_ref[...]).astype(bf16)
        proj = jnp.dot(xn, win_ref[...], preferred_element_type=f32)
        bg = proj[:, :D_CONV]
        u = proj[:, D_CONV:2 * D_CONV] * proj[:, 2 * D_CONV:3 * D_CONV]
        v = proj[:, 3 * D_CONV:]
        ubuf[HIST:HIST + TS, :] = u
        vbuf[HIST:HIST + TS, :] = v

        y = (ubuf[HIST - 2:HIST - 2 + TS, :] * cw_ref[0:1, :]
             + ubuf[HIST - 1:HIST - 1 + TS, :] * cw_ref[1:2, :]
             + u * cw_ref[2:3, :])
        zcur[:, :D_CONV] = bg * y

        pos = s * TS + lax.broadcasted_iota(i32, (TS, 1), 0)
        ps = []
        for g, w in enumerate(POOL_WINDOWS):
            sl = slice(g * POOL_GROUP, (g + 1) * POOL_GROUP)
            acc = vbuf[:, sl]
            for step in range(g + 1):
                acc = acc + pltpu.roll(acc, 1 << step, axis=0)
            cnt = jnp.minimum(w, pos + 1).astype(f32)
            ps.append(acc[HIST:, :] * (1.0 / cnt) - v[:, sl])
        zcur[:, D_CONV:] = _pool_project(jnp.concatenate(ps, axis=1), pw_ref, ps_ref)

        ubuf[HIST - 8:HIST, :] = ubuf[TS + HIST - 8:TS + HIST, :]
        vbuf[0:HIST, :] = vbuf[TS:TS + HIST, :]

        @pl.when(s == SEQ_TILES - 1)
        def _():
            npc_ref[0] = ubuf[HIST - 8:HIST, :]
            npp_ref[0] = vbuf[0:HIST, :]

    @pl.when(i == N_PROMPT_TILES)
    def _():
        x = xs_in_ref[...]
        xcur[...] = x
        xn = _rms(x, g1_ref[...]).astype(bf16)
        proj = jnp.dot(xn, win_ref[...], preferred_element_type=f32)
        nb = DEC_BATCH
        rows = lambda a, t: a[t * nb:(t + 1) * nb]
        bg = proj[:, :D_CONV]
        u = proj[:, D_CONV:2 * D_CONV] * proj[:, 2 * D_CONV:3 * D_CONV]
        v = proj[:, 3 * D_CONV:]
        up = [sc_ref[j] for j in range(CONV_W - 1)] + [rows(u, t) for t in range(DEC_SEQ)]
        vp = [sp_ref[j] for j in range(POOL_BUF)] + [rows(v, t) for t in range(DEC_SEQ)]
        zc, ps = [], []
        for t in range(DEC_SEQ):
            y = up[t] * cw_ref[0:1, :]
            for k in range(1, CONV_W):
                y = y + up[t + k] * cw_ref[k:k + 1, :]
            zc.append(rows(bg, t) * y)
            pg = []
            for g, w in enumerate(POOL_WINDOWS):
                sl = slice(g * POOL_GROUP, (g + 1) * POOL_GROUP)
                lo = t + POOL_BUF - w + 1
                acc = vp[lo][:, sl]
                for j in range(lo + 1, t + POOL_BUF + 1):
                    acc = acc + vp[j][:, sl]
                cnt = float(min(w, PAST_LEN + t + 1))
                pg.append(acc * (1.0 / cnt) - vp[t + POOL_BUF][:, sl])
            ps.append(jnp.concatenate(pg, axis=1))
        zcur[:, :D_CONV] = jnp.concatenate(zc, axis=0)
        zcur[:, D_CONV:] = _pool_project(jnp.concatenate(ps, axis=0), pw_ref, ps_ref)
        for j in range(CONV_W - 1):
            nsc_ref[j] = up[DEC_SEQ + j]
        for t in range(DEC_SEQ):
            nsv_ref[t] = rows(v, t)

    _route_and_sort(xcur[...], zcur[...], wout_ref, g2_ref, rwh_ref, rwl_ref, rb_ref,
                    x1_ref, srt_ref, slot_ref, gate_ref, cnt_ref)


def _mixer(xp, xs, sc_t, sp_t, g1, win, cw, pw, ps, wout, g2, rwh, rwl, rb):
    const = lambda shape: pl.BlockSpec(shape, lambda i: (0,) * len(shape),
                                       pipeline_mode=pl.Buffered(1))
    ptile = lambda i: jnp.minimum(i, N_PROMPT_TILES - 1)
    pbatch = lambda i: jnp.minimum(i, N_PROMPT_TILES - 1) // SEQ_TILES
    out_shape = (
        jax.ShapeDtypeStruct((N_TOK, D_MODEL), f32),
        jax.ShapeDtypeStruct((N_SORT_ROWS, D_MODEL), bf16),
        jax.ShapeDtypeStruct((TOP_K, N_TOK), i32),
        jax.ShapeDtypeStruct((TOP_K, N_TOK), f32),
        jax.ShapeDtypeStruct((N_TILES * N_EXPERTS, 1), f32),
        jax.ShapeDtypeStruct((BATCH, 8, D_CONV), f32),
        jax.ShapeDtypeStruct((BATCH, HIST, D_POOL), f32),
        jax.ShapeDtypeStruct((CONV_W - 1, DEC_BATCH, D_CONV), f32),
        jax.ShapeDtypeStruct((DEC_SEQ, DEC_BATCH, D_POOL), f32),
    )
    out_specs = (
        pl.BlockSpec((TS, D_MODEL), lambda i: (i, 0)),
        pl.BlockSpec((S_MAX, D_MODEL), lambda i: (i, 0)),
        pl.BlockSpec((TOP_K, TS), lambda i: (0, i)),
        pl.BlockSpec((TOP_K, TS), lambda i: (0, i)),
        pl.BlockSpec((N_EXPERTS, 1), lambda i: (i, 0)),
        pl.BlockSpec((1, 8, D_CONV), lambda i: (pbatch(i), 0, 0)),
        pl.BlockSpec((1, HIST, D_POOL), lambda i: (pbatch(i), 0, 0)),
        pl.BlockSpec((CONV_W - 1, DEC_BATCH, D_CONV), lambda i: (0, 0, 0)),
        pl.BlockSpec((DEC_SEQ, DEC_BATCH, D_POOL), lambda i: (0, 0, 0)),
    )
    in_specs = [
        pl.BlockSpec((TS, D_MODEL), lambda i: (ptile(i), 0)),
        const((N_SAMPLE, D_MODEL)),
        const((CONV_W - 1, DEC_BATCH, D_CONV)),
        const((POOL_BUF, DEC_BATCH, D_POOL)),
        const((1, D_MODEL)),
        const((D_MODEL, D_IN)),
        const((CONV_W, D_CONV)),
        const((len(POOL_WINDOWS), POOL_GROUP, POOL_GROUP)),
        const((1, D_POOL)),
        const((D_MODEL, D_MODEL)),
        const((1, D_MODEL)),
        const((N_EXPERTS, D_MODEL)),
        const((N_EXPERTS, D_MODEL)),
        const((N_EXPERTS, 1)),
    ]
    return pl.pallas_call(
        _mixer_kernel,
        grid=(N_TILES,),
        in_specs=in_specs,
        out_specs=out_specs,
        out_shape=out_shape,
        scratch_shapes=[
            pltpu.VMEM((TS + HIST, D_CONV), f32),
            pltpu.VMEM((TS + HIST, D_POOL), f32),
            pltpu.VMEM((TS, D_MODEL), f32),
            pltpu.VMEM((TS, D_MODEL), f32),
        ],
        compiler_params=pltpu.CompilerParams(
            dimension_semantics=("arbitrary",), vmem_limit_bytes=VMEM_LIMIT),
        name="mixer",
    )(xp, xs, sc_t, sp_t, g1, win, cw, pw, ps, wout, g2, rwh, rwl, rb)


def _expert_kernel(te_ref, nu_ref, first_ref, ord_ref, nxt_ref, pt_ref, tot_ref,
                   b1_ref, b2_ref, srt_hbm, w1_hbm, w2_hbm, ys_hbm,
                   xbuf, ybuf, zpiece, w1f, w2f, w1b, w2b, sem_w, sem_x, sem_y, sem_z):
    i = pl.program_id(0)
    par = i % 2
    n_used = nu_ref[0]
    live = i < n_used

    def weight_copies(e, slot):
        return (pltpu.make_async_copy(w1_hbm.at[e], w1f.at[slot], sem_w.at[slot]),
                pltpu.make_async_copy(w2_hbm.at[e], w2f.at[slot], sem_w.at[2 + slot]))

    def start_x(tile, b):
        for p in range(PIECES):
            src = pt_ref[tile * PIECES + p]
            src = pl.multiple_of(jnp.where(src < 0, ZERO_ROW, src), PIECE)
            pltpu.make_async_copy(srt_hbm.at[pl.ds(src, PIECE)],
                                  xbuf.at[b, pl.ds(p * PIECE, PIECE)], sem_x.at[b]).start()

    def start_y(tile, b):
        for p in range(PIECES):
            dst = pt_ref[tile * PIECES + p]
            dst = pl.multiple_of(jnp.where(dst < 0, DUMP_ROW + b * TM + p * PIECE, dst), PIECE)
            pltpu.make_async_copy(ybuf.at[b, pl.ds(p * PIECE, PIECE)],
                                  ys_hbm.at[pl.ds(dst, PIECE)], sem_y.at[b]).start()

    def wait_x(b):
        pltpu.make_async_copy(srt_hbm.at[pl.ds(0, TM)], xbuf.at[b], sem_x.at[b]).wait()

    def wait_y(b):
        pltpu.make_async_copy(ybuf.at[b], ys_hbm.at[pl.ds(0, TM)], sem_y.at[b]).wait()

    @pl.when(i == 0)
    def _():
        zpiece[...] = jnp.zeros_like(zpiece)

        def fill_tile(t, c):
            def dst(p):
                row = pl.multiple_of(t * S_MAX + tot_ref[t] + p * PIECE, PIECE)
                return ys_hbm.at[pl.ds(row, PIECE)]

            def start(p, c2):
                pltpu.make_async_copy(zpiece, dst(p), sem_z.at[0]).start()
                return c2

            def wait(p, c2):
                pltpu.make_async_copy(zpiece, dst(p), sem_z.at[0]).wait()
                return c2
            n_fill = (S_MAX - tot_ref[t]) // PIECE
            lax.fori_loop(0, n_fill, start, 0)
            return lax.fori_loop(0, n_fill, wait, c)
        lax.fori_loop(0, N_TILES, fill_tile, 0)

        def fill_dump(p, c):
            row = pl.multiple_of(DUMP_ROW + p * PIECE, PIECE)
            cp = pltpu.make_async_copy(zpiece, ys_hbm.at[pl.ds(row, PIECE)], sem_z.at[0])
            cp.start()
            cp.wait()
            return c
        lax.fori_loop(0, 2 * PIECES, fill_dump, 0)
        start_x(0, 0)

    @pl.when(live)
    def _():
        wait_x(par)

        @pl.when(i + 1 < n_used)
        def _():
            start_x(i + 1, 1 - par)

        @pl.when(first_ref[i] == 1)
        def _():
            slot = ord_ref[i] % 2

            @pl.when(i == 0)
            def _():
                for cp in weight_copies(te_ref[0], 0):
                    cp.start()
            for cp in weight_copies(te_ref[i], slot):
                cp.wait()

            @pl.when(nxt_ref[i] >= 0)
            def _():
                for cp in weight_copies(nxt_ref[i], 1 - slot):
                    cp.start()
            w1b[...] = w1f[slot].astype(bf16)
            w2b[...] = w2f[slot].astype(bf16)

        @pl.when(i >= 2)
        def _():
            wait_y(par)

        gu = jnp.dot(xbuf[par], w1b[...], preferred_element_type=f32) + b1_ref[0]
        gate = jnp.minimum(gu[:, :D_FF], SWIGLU_LIMIT)
        lin = jnp.clip(gu[:, D_FF:], -SWIGLU_LIMIT, SWIGLU_LIMIT)
        glu = gate * jax.nn.sigmoid(SWIGLU_ALPHA * gate)
        h = (glu * (lin + 1.0)).astype(bf16)
        out = jnp.dot(h, w2b[...], preferred_element_type=f32) + b2_ref[0]
        ybuf[par] = out.astype(bf16)
        start_y(i, par)

        @pl.when(i == n_used - 1)
        def _():
            wait_y(par)

            @pl.when(i >= 1)
            def _():
                wait_y(1 - par)


def _experts(tile_expert, n_used, first, order, nxt, piece_row, tot, srt, w1, b1, w2, b2):
    wsel = lambda i, te, *_: (te[i], 0, 0)
    grid_spec = pltpu.PrefetchScalarGridSpec(
        num_scalar_prefetch=7,
        grid=(N_ROW_TILES,),
        in_specs=[
            pl.BlockSpec((1, 1, 2 * D_FF), wsel),
            pl.BlockSpec((1, 1, D_MODEL), wsel),
            pl.BlockSpec(memory_space=pl.ANY),
            pl.BlockSpec(memory_space=pl.ANY),
            pl.BlockSpec(memory_space=pl.ANY),
        ],
        out_specs=pl.BlockSpec(memory_space=pl.ANY),
        scratch_shapes=[
            pltpu.VMEM((2, TM, D_MODEL), bf16), pltpu.VMEM((2, TM, D_MODEL), bf16),
            pltpu.VMEM((PIECE, D_MODEL), bf16),
            pltpu.VMEM((2, D_MODEL, 2 * D_FF), f32), pltpu.VMEM((2, D_FF, D_MODEL), f32),
            pltpu.VMEM((D_MODEL, 2 * D_FF), bf16), pltpu.VMEM((D_FF, D_MODEL), bf16),
            pltpu.SemaphoreType.DMA((4,)), pltpu.SemaphoreType.DMA((2,)),
            pltpu.SemaphoreType.DMA((2,)), pltpu.SemaphoreType.DMA((1,)),
        ],
    )
    return pl.pallas_call(
        _expert_kernel,
        grid_spec=grid_spec,
        out_shape=jax.ShapeDtypeStruct((N_SORT_ROWS + 2 * TM, D_MODEL), bf16),
        compiler_params=pltpu.CompilerParams(
            dimension_semantics=("arbitrary",), vmem_limit_bytes=VMEM_LIMIT),
        name="experts",
    )(tile_expert, n_used, first, order, nxt, piece_row, tot, b1, b2, srt, w1, w2)


def _combine_kernel(slot_ref, g_ref, x1_ref, gf_ref, ys_ref, yp_ref, ysm_ref, whi, wlo):
    i = pl.program_id(0)

    slot = slot_ref[...]
    g = g_ref[...]
    for b in range(S_MAX // S_BLK):
        s_io = b * S_BLK + lax.broadcasted_iota(i32, (TS, S_BLK), 1)
        w = jnp.where(s_io == slot[:, 0:1], g[:, 0:1], 0.0)
        for k in range(1, TOP_K):
            w = w + jnp.where(s_io == slot[:, k:k + 1], g[:, k:k + 1], 0.0)
        hi = w.astype(bf16)
        whi[:, b * S_BLK:(b + 1) * S_BLK] = hi
        wlo[:, b * S_BLK:(b + 1) * S_BLK] = (w - hi.astype(f32)).astype(bf16)

    rows = ys_ref[...]
    y = (x1_ref[...] + jnp.dot(whi[...], rows, preferred_element_type=f32)
         + jnp.dot(wlo[...], rows, preferred_element_type=f32))
    out = _rms(y, gf_ref[...])

    @pl.when(i < N_PROMPT_TILES)
    def _():
        yp_ref[...] = out

    @pl.when(i == N_PROMPT_TILES)
    def _():
        ysm_ref[...] = out


def _combine(slot_tm, gates_tm, x1, gf, ys):
    ptile = lambda i: (jnp.minimum(i, N_PROMPT_TILES - 1), 0)
    return pl.pallas_call(
        _combine_kernel,
        grid=(N_TILES,),
        in_specs=[
            pl.BlockSpec((TS, TOP_K), lambda i: (i, 0)),
            pl.BlockSpec((TS, TOP_K), lambda i: (i, 0)),
            pl.BlockSpec((TS, D_MODEL), lambda i: (i, 0)),
            pl.BlockSpec((1, D_MODEL), lambda i: (0, 0)),
            pl.BlockSpec((S_MAX, D_MODEL), lambda i: (i, 0)),
        ],
        out_specs=(
            pl.BlockSpec((TS, D_MODEL), ptile),
            pl.BlockSpec((N_SAMPLE, D_MODEL), lambda i: (0, 0)),
        ),
        out_shape=(
            jax.ShapeDtypeStruct((N_PROMPT, D_MODEL), f32),
            jax.ShapeDtypeStruct((N_SAMPLE, D_MODEL), f32),
        ),
        scratch_shapes=[pltpu.VMEM((TS, S_MAX), bf16), pltpu.VMEM((TS, S_MAX), bf16)],
        compiler_params=pltpu.CompilerParams(
            dimension_semantics=("arbitrary",), vmem_limit_bytes=VMEM_LIMIT),
        name="combine",
    )(slot_tm, gates_tm, x1, gf, ys)


def _routing_tables(cnt):
    n = cnt.reshape(N_TILES, N_EXPERTS).astype(i32)
    c = (n + (PIECE - 1)) // PIECE * PIECE
    lo = jnp.cumsum(c, axis=1) - c
    tot = jnp.sum(c, axis=1)
    group = jnp.sum(c, axis=0)
    padded = (group + (TM - 1)) // TM * TM
    gend = jnp.cumsum(padded)
    gstart = gend - padded
    cs = gstart[None, :] + jnp.cumsum(c, axis=0) - c
    n_used = gend[-1] // TM
    tile_id = jnp.arange(N_ROW_TILES, dtype=i32)
    tile_expert = jnp.sum((tile_id[:, None] * TM >= gend[None, :]).astype(i32), axis=1)
    tile_expert = jnp.minimum(tile_expert, jnp.take(tile_expert, n_used - 1))
    first = (tile_id < n_used) & ((tile_id == 0) | (tile_expert != jnp.roll(tile_expert, 1)))
    order = jnp.maximum(jnp.cumsum(first.astype(i32)) - 1, 0)
    e_ar = jnp.arange(N_EXPERTS, dtype=i32)
    later = (e_ar[None, :] > e_ar[:, None]) & (padded[None, :] > 0)
    nxt_e = jnp.min(jnp.where(later, e_ar[None, :], N_EXPERTS), axis=1)
    nxt_e = jnp.where(nxt_e == N_EXPERTS, -1, nxt_e)
    nxt = jnp.sum(jnp.where(tile_expert[:, None] == e_ar[None, :], nxt_e[None, :], 0), axis=1)
    cs_f = cs.T.reshape(-1)
    c_f = c.T.reshape(-1)
    src_f = (jnp.arange(N_TILES, dtype=i32)[:, None] * S_MAX + lo).T.reshape(-1)
    piece_r = jnp.arange(N_ROW_TILES * PIECES, dtype=i32) * PIECE
    reached = (piece_r[:, None] >= cs_f[None, 1:]).astype(i32)

    def lookup(f):
        return f[0] + jnp.sum(reached * (f[1:] - f[:-1])[None, :], axis=1)
    off = piece_r - lookup(cs_f)
    piece_row = jnp.where(off < lookup(c_f), lookup(src_f) + off, -1)
    return dict(tot=tot, n_used=n_used[None], tile_expert=tile_expert, first=first.astype(i32),
                order=order, nxt=nxt, piece_row=piece_row)


def kernel(x_prompt, x_sample, state_conv, state_pool, norm_mix_g, w_in, conv_w, pool_w, pool_scale,
           w_out, norm_ffn_g, router_w, router_b, exp_w1, exp_b1, exp_w2, exp_b2, final_norm_g):
    l = 0
    xp = x_prompt.reshape(N_PROMPT, D_MODEL)
    xs = jnp.transpose(x_sample, (1, 0, 2)).reshape(N_SAMPLE, D_MODEL)
    sc_t = jnp.transpose(state_conv[l], (1, 0, 2))
    sp_t = jnp.transpose(state_pool[l], (1, 0, 2))
    rw_t = router_w[l].T
    rwh = rw_t.astype(bf16)
    rwl = (rw_t - rwh.astype(f32)).astype(bf16)

    (x1, srt, slot_t, gate_t, cnt, npc, npp, nsc, nsv) = _mixer(
        xp, xs, sc_t, sp_t, norm_mix_g[l][None, :], w_in[l].astype(bf16), conv_w[l],
        pool_w[l].astype(bf16), pool_scale[l][None, :], w_out[l].astype(bf16),
        norm_ffn_g[l][None, :], rwh, rwl, router_b[l][:, None])

    t = _routing_tables(cnt)
    ys = _experts(t["tile_expert"], t["n_used"], t["first"], t["order"], t["nxt"], t["piece_row"],
                  t["tot"], srt, exp_w1[l], exp_b1[l][:, None, :], exp_w2[l], exp_b2[l][:, None, :])
    y_p, y_s = _combine(slot_t.T, gate_t.T, x1, final_norm_g[None, :], ys)

    y_prompt = y_p.reshape(BATCH, SEQ, D_MODEL)
    y_sample = jnp.transpose(y_s.reshape(DEC_SEQ, DEC_BATCH, D_MODEL), (1, 0, 2))
    new_conv_prompt = npc[None, :, 8 - (CONV_W - 1):, :]
    new_pool_prompt = npp[None, :, HIST - POOL_BUF:, :]
    new_conv_sample = jnp.transpose(nsc, (1, 0, 2))[None]
    new_pool_sample = jnp.concatenate(
        [state_pool[l][:, DEC_SEQ:, :], jnp.transpose(nsv, (1, 0, 2))], axis=1)[None]
    return (y_prompt, y_sample, new_conv_prompt, new_pool_prompt, new_conv_sample, new_pool_sample)
```

```python
import jax
import jax.numpy as jnp
from jax import lax
from jax.experimental import pallas as pl
from jax.experimental.pallas import tpu as pltpu

D_MODEL = 1024
D_CONV = 512
D_POOL = 512
D_IN = 3 * D_CONV + D_POOL
CONV_W = 3
POOL_WINDOWS = (2, 4, 8, 16)
POOL_GROUP = 128
POOL_BUF = 15
N_EXPERTS = 32
TOP_K = 4
D_FF = 1024
SWIGLU_LIMIT = 7.0
SWIGLU_ALPHA = 1.702
EPS = 1e-5
PAST_LEN = 16384

BATCH, SEQ = 8, 2048
DEC_BATCH, DEC_SEQ = 128, 4
N_PROMPT = BATCH * SEQ
N_SAMPLE = DEC_BATCH * DEC_SEQ
N_TOK = N_PROMPT + N_SAMPLE

TS = 512
SEQ_TILES = SEQ // TS
N_PROMPT_TILES = N_PROMPT // TS
N_TILES = N_PROMPT_TILES + 1
HIST = 16
TM = 256
PIECE = 16
PIECES = TM // PIECE
S_BLK = 256
S_MAX = (TS * TOP_K + N_EXPERTS * (PIECE - 1) + PIECE + S_BLK - 1) // S_BLK * S_BLK
ZERO_ROW = S_MAX - PIECE
N_ROW_TILES = (N_TOK * TOP_K + N_TILES * N_EXPERTS * (PIECE - 1) + N_EXPERTS * (TM - PIECE)) // TM
N_SORT_ROWS = N_TILES * S_MAX
DUMP_ROW = N_SORT_ROWS

VMEM_LIMIT = 56 * 1024 * 1024

f32 = jnp.float32
bf16 = jnp.bfloat16
i32 = jnp.int32


def _rms(x, g):
    return x * lax.rsqrt(jnp.mean(x * x, axis=-1, keepdims=True) + EPS) * g


def _pool_project(p, pw_ref, ps_ref):
    outs = []
    for g in range(len(POOL_WINDOWS)):
        sl = slice(g * POOL_GROUP, (g + 1) * POOL_GROUP)
        outs.append(jnp.dot(p[:, sl].astype(bf16), pw_ref[g], preferred_element_type=f32))
    return jnp.concatenate(outs, axis=1) * ps_ref[...]


def _route_and_sort(x, z, wout_ref, g2_ref, rwh_ref, rwl_ref, rb_ref,
                    x1_ref, srt_ref, slot_ref, gate_ref, cnt_ref):
    n = x.shape[0]
    x1 = x + jnp.dot(z.astype(bf16), wout_ref[...], preferred_element_type=f32)
    x1_ref[...] = x1
    xn = _rms(x1, g2_ref[...])
    xb = xn.astype(bf16)

    xlo = (xn - xb.astype(f32)).astype(bf16)
    nt = (((1,), (1,)), ((), ()))
    logits = (lax.dot_general(rwh_ref[...], xb, nt, preferred_element_type=f32)
              + lax.dot_general(rwl_ref[...], xb, nt, preferred_element_type=f32)
              + lax.dot_general(rwh_ref[...], xlo, nt, preferred_element_type=f32)
              + rb_ref[...])

    e_iota = lax.broadcasted_iota(i32, (N_EXPERTS, n), 0)
    work = logits
    hots, vals = [], []
    for _ in range(TOP_K):
        m = jnp.max(work, axis=0, keepdims=True)
        ik = jnp.min(jnp.where(work == m, e_iota, N_EXPERTS), axis=0, keepdims=True)
        hot = e_iota == ik
        work = jnp.where(hot, -jnp.inf, work)
        hots.append(hot); vals.append(m)
    exps = [jnp.exp(v - vals[0]) for v in vals]
    den = exps[0] + exps[1] + exps[2] + exps[3]
    gate_ref[...] = jnp.concatenate([e / den for e in exps], axis=0)

    multi = jnp.where(hots[0] | hots[1] | hots[2] | hots[3], 1.0, 0.0)
    r_io = lax.broadcasted_iota(i32, (n, n), 0)
    c_io = lax.broadcasted_iota(i32, (n, n), 1)
    tri = jnp.where(r_io < c_io, 1.0, 0.0).astype(bf16)
    before = jnp.dot(multi.astype(bf16), tri, preferred_element_type=f32)
    count = jnp.sum(multi, axis=1, keepdims=True)
    cnt_ref[...] = count
    pieces = jnp.floor((count + (PIECE - 1)) * (1.0 / PIECE))
    er = lax.broadcasted_iota(i32, (N_EXPERTS, N_EXPERTS), 0)
    ec = lax.broadcasted_iota(i32, (N_EXPERTS, N_EXPERTS), 1)
    lower = jnp.where(ec < er, 1.0, 0.0).astype(bf16)
    pieces_b = jnp.broadcast_to(pieces, (N_EXPERTS, 128)).astype(bf16)
    start = PIECE * jnp.dot(lower, pieces_b, preferred_element_type=f32)[:, 0:1]
    base = before + start
    slot = jnp.concatenate(
        [jnp.sum(jnp.where(h, base, 0.0), axis=0, keepdims=True) for h in hots], axis=0).astype(i32)
    slot_ref[...] = slot

    def sort_blocks(b2, c):
        for half in range(2):
            row0 = pl.multiple_of((2 * b2 + half) * S_BLK, S_BLK)
            s_io = row0 + lax.broadcasted_iota(i32, (S_BLK, n), 0)
            hit = s_io == slot[0:1, :]
            for k in range(1, TOP_K):
                hit = hit | (s_io == slot[k:k + 1, :])
            sel = jnp.where(hit, 1.0, 0.0).astype(bf16)
            srt_ref[pl.ds(row0, S_BLK), :] = jnp.dot(
                sel, xb, preferred_element_type=f32).astype(bf16)
        return c
    lax.fori_loop(0, S_MAX // (2 * S_BLK), sort_blocks, 0)


def _mixer_kernel(xp_in_ref, xs_in_ref, sc_ref, sp_ref, g1_ref, win_ref, cw_ref, pw_ref, ps_ref,
                  wout_ref, g2_ref, rwh_ref, rwl_ref, rb_ref,
                  x1_ref, srt_ref, slot_ref, gate_ref, cnt_ref,
                  npc_ref, npp_ref, nsc_ref, nsv_ref,
                  ubuf, vbuf, xcur, zcur):
    i = pl.program_id(0)

    @pl.when(i < N_PROMPT_TILES)
    def _():
        s = i % SEQ_TILES

        @pl.when(s == 0)
        def _():
            ubuf[0:HIST, :] = jnp.zeros((HIST, D_CONV), f32)
            vbuf[0:HIST, :] = jnp.zeros((HIST, D_POOL), f32)

        x = xp_in_ref[...]
        xcur[...] = x
        xn = _rms(x, g1_ref[...]).astype(bf16)
        proj = jnp.dot(xn, win_ref[...], preferred_element_type=f32)
        bg = proj[:, :D_CONV]
        u = proj[:, D_CONV:2 * D_CONV] * proj[:, 2 * D_CONV:3 * D_CONV]
        v = proj[:, 3 * D_CONV:]
        ubuf[HIST:HIST + TS, :] = u
        vbuf[HIST:HIST + TS, :] = v

        y = (ubuf[HIST - 2:HIST - 2 + TS, :] * cw_ref[0:1, :]
             + ubuf[HIST - 1:HIST - 1 + TS, :] * cw_ref[1:2, :]
             + u * cw_ref[2:3, :])
        zcur[:, :D_CONV] = bg * y

        pos = s * TS + lax.broadcasted_iota(i32, (TS, 1), 0)
        ps = []
        for g, w in enumerate(POOL_WINDOWS):
            sl = slice(g * POOL_GROUP, (g + 1) * POOL_GROUP)
            acc = vbuf[:, sl]
            for step in range(g + 1):
                acc = acc + pltpu.roll(acc, 1 << step, axis=0)
            cnt = jnp.minimum(w, pos + 1).astype(f32)
            ps.append(acc[HIST:, :] * (1.0 / cnt) - v[:, sl])
        zcur[:, D_CONV:] = _pool_project(jnp.concatenate(ps, axis=1), pw_ref, ps_ref)

        ubuf[HIST - 8:HIST, :] = ubuf[TS + HIST - 8:TS + HIST, :]
        vbuf[0:HIST, :] = vbuf[TS:TS + HIST, :]

        @pl.when(s == SEQ_TILES - 1)
        def _():
            npc_ref[0] = ubuf[HIST - 8:HIST, :]
            npp_ref[0] = vbuf[0:HIST, :]

    @pl.when(i == N_PROMPT_TILES)
    def _():
        x = xs_in_ref[...]
        xcur[...] = x
        xn = _rms(x, g1_ref[...]).astype(bf16)
        proj = jnp.dot(xn, win_ref[...], preferred_element_type=f32)
        nb = DEC_BATCH
        rows = lambda a, t: a[t * nb:(t + 1) * nb]
        bg = proj[:, :D_CONV]
        u = proj[:, D_CONV:2 * D_CONV] * proj[:, 2 * D_CONV:3 * D_CONV]
        v = proj[:, 3 * D_CONV:]
        up = [sc_ref[j] for j in range(CONV_W - 1)] + [rows(u, t) for t in range(DEC_SEQ)]
        vp = [sp_ref[j] for j in range(POOL_BUF)] + [rows(v, t) for t in range(DEC_SEQ)]
        zc, ps = [], []
        for t in range(DEC_SEQ):
            y = up[t] * cw_ref[0:1, :]
            for k in range(1, CONV_W):
                y = y + up[t + k] * cw_ref[k:k + 1, :]
            zc.append(rows(bg, t) * y)
            pg = []
            for g, w in enumerate(POOL_WINDOWS):
                sl = slice(g * POOL_GROUP, (g + 1) * POOL_GROUP)
                lo = t + POOL_BUF - w + 1
                acc = vp[lo][:, sl]
                for j in range(lo + 1, t + POOL_BUF + 1):
                    acc = acc + vp[j][:, sl]
                cnt = float(min(w, PAST_LEN + t + 1))
                pg.append(acc * (1.0 / cnt) - vp[t + POOL_BUF][:, sl])
            ps.append(jnp.concatenate(pg, axis=1))
        zcur[:, :D_CONV] = jnp.concatenate(zc, axis=0)
        zcur[:, D_CONV:] = _pool_project(jnp.concatenate(ps, axis=0), pw_ref, ps_ref)
        for j in range(CONV_W - 1):
            nsc_ref[j] = up[DEC_SEQ + j]
        for t in range(DEC_SEQ):
            nsv_ref[t] = rows(v, t)

    _route_and_sort(xcur[...], zcur[...], wout_ref, g2_ref, rwh_ref, rwl_ref, rb_ref,
                    x1_ref, srt_ref, slot_ref, gate_ref, cnt_ref)


def _mixer(xp, xs, sc_t, sp_t, g1, win, cw, pw, ps, wout, g2, rwh, rwl, rb):
    const = lambda shape: pl.BlockSpec(shape, lambda i: (0,) * len(shape),
                                       pipeline_mode=pl.Buffered(1))
    ptile = lambda i: jnp.minimum(i, N_PROMPT_TILES - 1)
    pbatch = lambda i: jnp.minimum(i, N_PROMPT_TILES - 1) // SEQ_TILES
    out_shape = (
        jax.ShapeDtypeStruct((N_TOK, D_MODEL), f32),
        jax.ShapeDtypeStruct((N_SORT_ROWS, D_MODEL), bf16),
        jax.ShapeDtypeStruct((TOP_K, N_TOK), i32),
        jax.ShapeDtypeStruct((TOP_K, N_TOK), f32),
        jax.ShapeDtypeStruct((N_TILES * N_EXPERTS, 1), f32),
        jax.ShapeDtypeStruct((BATCH, 8, D_CONV), f32),
        jax.ShapeDtypeStruct((BATCH, HIST, D_POOL), f32),
        jax.ShapeDtypeStruct((CONV_W - 1, DEC_BATCH, D_CONV), f32),
        jax.ShapeDtypeStruct((DEC_SEQ, DEC_BATCH, D_POOL), f32),
    )
    out_specs = (
        pl.BlockSpec((TS, D_MODEL), lambda i: (i, 0)),
        pl.BlockSpec((S_MAX, D_MODEL), lambda i: (i, 0)),
        pl.BlockSpec((TOP_K, TS), lambda i: (0, i)),
        pl.BlockSpec((TOP_K, TS), lambda i: (0, i)),
        pl.BlockSpec((N_EXPERTS, 1), lambda i: (i, 0)),
        pl.BlockSpec((1, 8, D_CONV), lambda i: (pbatch(i), 0, 0)),
        pl.BlockSpec((1, HIST, D_POOL), lambda i: (pbatch(i), 0, 0)),
        pl.BlockSpec((CONV_W - 1, DEC_BATCH, D_CONV), lambda i: (0, 0, 0)),
        pl.BlockSpec((DEC_SEQ, DEC_BATCH, D_POOL), lambda i: (0, 0, 0)),
    )
    in_specs = [
        pl.BlockSpec((TS, D_MODEL), lambda i: (ptile(i), 0)),
        const((N_SAMPLE, D_MODEL)),
        const((CONV_W - 1, DEC_BATCH, D_CONV)),
        const((POOL_BUF, DEC_BATCH, D_POOL)),
        const((1, D_MODEL)),
        const((D_MODEL, D_IN)),
        const((CONV_W, D_CONV)),
        const((len(POOL_WINDOWS), POOL_GROUP, POOL_GROUP)),
        const((1, D_POOL)),
        const((D_MODEL, D_MODEL)),
        const((1, D_MODEL)),
        const((N_EXPERTS, D_MODEL)),
        const((N_EXPERTS, D_MODEL)),
        const((N_EXPERTS, 1)),
    ]
    return pl.pallas_call(
        _mixer_kernel,
        grid=(N_TILES,),
        in_specs=in_specs,
        out_specs=out_specs,
        out_shape=out_shape,
        scratch_shapes=[
            pltpu.VMEM((TS + HIST, D_CONV), f32),
            pltpu.VMEM((TS + HIST, D_POOL), f32),
            pltpu.VMEM((TS, D_MODEL), f32),
            pltpu.VMEM((TS, D_MODEL), f32),
        ],
        compiler_params=pltpu.CompilerParams(
            dimension_semantics=("arbitrary",), vmem_limit_bytes=VMEM_LIMIT),
        name="mixer",
    )(xp, xs, sc_t, sp_t, g1, win, cw, pw, ps, wout, g2, rwh, rwl, rb)


def _expert_kernel(te_ref, nu_ref, first_ref, ord_ref, nxt_ref, pt_ref, tot_ref,
                   b1_ref, b2_ref, srt_hbm, w1_hbm, w2_hbm, ys_hbm,
                   xbuf, ybuf, zpiece, w1f, w2f, w1b, w2b, sem_w, sem_x, sem_y, sem_z):
    i = pl.program_id(0)
    par = i % 2
    n_used = nu_ref[0]
    live = i < n_used

    def weight_copies(e, slot):
        return (pltpu.make_async_copy(w1_hbm.at[e], w1f.at[slot], sem_w.at[slot]),
                pltpu.make_async_copy(w2_hbm.at[e], w2f.at[slot], sem_w.at[2 + slot]))

    def start_x(tile, b):
        for p in range(PIECES):
            src = pt_ref[tile * PIECES + p]
            src = pl.multiple_of(jnp.where(src < 0, ZERO_ROW, src), PIECE)
            pltpu.make_async_copy(srt_hbm.at[pl.ds(src, PIECE)],
                                  xbuf.at[b, pl.ds(p * PIECE, PIECE)], sem_x.at[b]).start()

    def start_y(tile, b, all_dump=False):
        for p in range(PIECES):
            dst = pt_ref[tile * PIECES + p]
            dump = (dst < 0) | all_dump
            dst = pl.multiple_of(jnp.where(dump, DUMP_ROW + b * TM + p * PIECE, dst), PIECE)
            pltpu.make_async_copy(ybuf.at[b, pl.ds(p * PIECE, PIECE)],
                                  ys_hbm.at[pl.ds(dst, PIECE)], sem_y.at[b]).start()

    def wait_x(b):
        pltpu.make_async_copy(srt_hbm.at[pl.ds(0, TM)], xbuf.at[b], sem_x.at[b]).wait()

    def wait_y(b):
        pltpu.make_async_copy(ybuf.at[b], ys_hbm.at[pl.ds(0, TM)], sem_y.at[b]).wait()

    @pl.when(i == 0)
    def _():
        zpiece[...] = jnp.zeros_like(zpiece)

        def fill_tile(t, c):
            def dst(p):
                row = pl.multiple_of(t * S_MAX + tot_ref[t] + p * PIECE, PIECE)
                return ys_hbm.at[pl.ds(row, PIECE)]

            def start(p, c2):
                pltpu.make_async_copy(zpiece, dst(p), sem_z.at[0]).start()
                return c2

            def wait(p, c2):
                pltpu.make_async_copy(zpiece, dst(p), sem_z.at[0]).wait()
                return c2
            n_fill = (S_MAX - tot_ref[t]) // PIECE
            lax.fori_loop(0, n_fill, start, 0)
            return lax.fori_loop(0, n_fill, wait, c)
        lax.fori_loop(0, N_TILES, fill_tile, 0)
        start_x(0, 0)
        ybuf[...] = jnp.zeros_like(ybuf)
        pltpu.make_async_copy(ybuf.at[0], ys_hbm.at[pl.ds(DUMP_ROW, TM)], sem_y.at[0]).start()

    @pl.when(live)
    def _():
        wait_x(par)

        @pl.when(first_ref[i] == 1)
        def _():
            slot = ord_ref[i] % 2

            @pl.when(i == 0)
            def _():
                for cp in weight_copies(te_ref[0], 0):
                    cp.start()
            for cp in weight_copies(te_ref[i], slot):
                cp.wait()

            @pl.when(nxt_ref[i] >= 0)
            def _():
                for cp in weight_copies(nxt_ref[i], 1 - slot):
                    cp.start()
            w1b[...] = w1f[slot].astype(bf16)
            w2b[...] = w2f[slot].astype(bf16)

        wait_y(par)
        gu = jnp.dot(xbuf[par], w1b[...], preferred_element_type=f32) + b1_ref[0]
        gate = jnp.minimum(gu[:, :D_FF], SWIGLU_LIMIT)
        lin = jnp.clip(gu[:, D_FF:], -SWIGLU_LIMIT, SWIGLU_LIMIT)
        glu = gate * jax.nn.sigmoid(SWIGLU_ALPHA * gate)
        h = (glu * (lin + 1.0)).astype(bf16)
        start_x(jnp.minimum(i + 1, n_used - 1), 1 - par)
        start_y(jnp.maximum(i - 1, 0), 1 - par, all_dump=i == 0)
        out = jnp.dot(h, w2b[...], preferred_element_type=f32) + b2_ref[0]
        ybuf[par] = out.astype(bf16)

        @pl.when(i == n_used - 1)
        def _():
            start_y(i, par)
            wait_y(par)
            wait_y(1 - par)
            wait_x(1 - par)


def _experts(tile_expert, n_used, first, order, nxt, piece_row, tot, srt, w1, b1, w2, b2):
    wsel = lambda i, te, *_: (te[i], 0, 0)
    grid_spec = pltpu.PrefetchScalarGridSpec(
        num_scalar_prefetch=7,
        grid=(N_ROW_TILES,),
        in_specs=[
            pl.BlockSpec((1, 1, 2 * D_FF), wsel),
            pl.BlockSpec((1, 1, D_MODEL), wsel),
            pl.BlockSpec(memory_space=pl.ANY),
            pl.BlockSpec(memory_space=pl.ANY),
            pl.BlockSpec(memory_space=pl.ANY),
        ],
        out_specs=pl.BlockSpec(memory_space=pl.ANY),
        scratch_shapes=[
            pltpu.VMEM((2, TM, D_MODEL), bf16), pltpu.VMEM((2, TM, D_MODEL), bf16),
            pltpu.VMEM((PIECE, D_MODEL), bf16),
            pltpu.VMEM((2, D_MODEL, 2 * D_FF), f32), pltpu.VMEM((2, D_FF, D_MODEL), f32),
            pltpu.VMEM((D_MODEL, 2 * D_FF), bf16), pltpu.VMEM((D_FF, D_MODEL), bf16),
            pltpu.SemaphoreType.DMA((4,)), pltpu.SemaphoreType.DMA((2,)),
            pltpu.SemaphoreType.DMA((2,)), pltpu.SemaphoreType.DMA((1,)),
        ],
    )
    return pl.pallas_call(
        _expert_kernel,
        grid_spec=grid_spec,
        out_shape=jax.ShapeDtypeStruct((N_SORT_ROWS + 2 * TM, D_MODEL), bf16),
        compiler_params=pltpu.CompilerParams(
            dimension_semantics=("arbitrary",), vmem_limit_bytes=VMEM_LIMIT),
        name="experts",
    )(tile_expert, n_used, first, order, nxt, piece_row, tot, b1, b2, srt, w1, w2)


def _combine_kernel(slot_ref, g_ref, x1_ref, gf_ref, ys_ref, yp_ref, ysm_ref, whi, wlo):
    i = pl.program_id(0)

    slot = slot_ref[...]
    g = g_ref[...]
    for b in range(S_MAX // S_BLK):
        s_io = b * S_BLK + lax.broadcasted_iota(i32, (TS, S_BLK), 1)
        w = jnp.where(s_io == slot[:, 0:1], g[:, 0:1], 0.0)
        for k in range(1, TOP_K):
            w = w + jnp.where(s_io == slot[:, k:k + 1], g[:, k:k + 1], 0.0)
        hi = w.astype(bf16)
        whi[:, b * S_BLK:(b + 1) * S_BLK] = hi
        wlo[:, b * S_BLK:(b + 1) * S_BLK] = (w - hi.astype(f32)).astype(bf16)

    rows = ys_ref[...]
    y = (x1_ref[...] + jnp.dot(whi[...], rows, preferred_element_type=f32)
         + jnp.dot(wlo[...], rows, preferred_element_type=f32))
    out = _rms(y, gf_ref[...])

    @pl.when(i < N_PROMPT_TILES)
    def _():
        yp_ref[...] = out

    @pl.when(i == N_PROMPT_TILES)
    def _():
        ysm_ref[...] = out


def _combine(slot_tm, gates_tm, x1, gf, ys):
    ptile = lambda i: (jnp.minimum(i, N_PROMPT_TILES - 1), 0)
    return pl.pallas_call(
        _combine_kernel,
        grid=(N_TILES,),
        in_specs=[
            pl.BlockSpec((TS, TOP_K), lambda i: (i, 0)),
            pl.BlockSpec((TS, TOP_K), lambda i: (i, 0)),
            pl.BlockSpec((TS, D_MODEL), lambda i: (i, 0)),
            pl.BlockSpec((1, D_MODEL), lambda i: (0, 0)),
            pl.BlockSpec((S_MAX, D_MODEL), lambda i: (i, 0)),
        ],
        out_specs=(
            pl.BlockSpec((TS, D_MODEL), ptile),
            pl.BlockSpec((N_SAMPLE, D_MODEL), lambda i: (0, 0)),
        ),
        out_shape=(
            jax.ShapeDtypeStruct((N_PROMPT, D_MODEL), f32),
            jax.ShapeDtypeStruct((N_SAMPLE, D_MODEL), f32),
        ),
        scratch_shapes=[pltpu.VMEM((TS, S_MAX), bf16), pltpu.VMEM((TS, S_MAX), bf16)],
        compiler_params=pltpu.CompilerParams(
            dimension_semantics=("arbitrary",), vmem_limit_bytes=VMEM_LIMIT),
        name="combine",
    )(slot_tm, gates_tm, x1, gf, ys)


def _routing_tables(cnt):
    n = cnt.reshape(N_TILES, N_EXPERTS).astype(i32)
    c = (n + (PIECE - 1)) // PIECE * PIECE
    lo = jnp.cumsum(c, axis=1) - c
    tot = jnp.sum(c, axis=1)
    group = jnp.sum(c, axis=0)
    padded = (group + (TM - 1)) // TM * TM
    gend = jnp.cumsum(padded)
    gstart = gend - padded
    cs = gstart[None, :] + jnp.cumsum(c, axis=0) - c
    n_used = gend[-1] // TM
    tile_id = jnp.arange(N_ROW_TILES, dtype=i32)
    tile_expert = jnp.sum((tile_id[:, None] * TM >= gend[None, :]).astype(i32), axis=1)
    tile_expert = jnp.minimum(tile_expert, jnp.take(tile_expert, n_used - 1))
    first = (tile_id < n_used) & ((tile_id == 0) | (tile_expert != jnp.roll(tile_expert, 1)))
    order = jnp.maximum(jnp.cumsum(first.astype(i32)) - 1, 0)
    e_ar = jnp.arange(N_EXPERTS, dtype=i32)
    later = (e_ar[None, :] > e_ar[:, None]) & (padded[None, :] > 0)
    nxt_e = jnp.min(jnp.where(later, e_ar[None, :], N_EXPERTS), axis=1)
    nxt_e = jnp.where(nxt_e == N_EXPERTS, -1, nxt_e)
    nxt = jnp.sum(jnp.where(tile_expert[:, None] == e_ar[None, :], nxt_e[None, :], 0), axis=1)
    cs_f = cs.T.reshape(-1)
    c_f = c.T.reshape(-1)
    src_f = (jnp.arange(N_TILES, dtype=i32)[:, None] * S_MAX + lo).T.reshape(-1)
    piece_r = jnp.arange(N_ROW_TILES * PIECES, dtype=i32) * PIECE
    reached = (piece_r[:, None] >= cs_f[None, 1:]).astype(i32)

    def lookup(f):
        return f[0] + jnp.sum(reached * (f[1:] - f[:-1])[None, :], axis=1)
    off = piece_r - lookup(cs_f)
    piece_row = jnp.where(off < lookup(c_f), lookup(src_f) + off, -1)
    return dict(tot=tot, n_used=n_used[None], tile_expert=tile_expert, first=first.astype(i32),
                order=order, nxt=nxt, piece_row=piece_row)


def kernel(x_prompt, x_sample, state_conv, state_pool, norm_mix_g, w_in, conv_w, pool_w, pool_scale,
           w_out, norm_ffn_g, router_w, router_b, exp_w1, exp_b1, exp_w2, exp_b2, final_norm_g):
    l = 0
    xp = x_prompt.reshape(N_PROMPT, D_MODEL)
    xs = jnp.transpose(x_sample, (1, 0, 2)).reshape(N_SAMPLE, D_MODEL)
    sc_t = jnp.transpose(state_conv[l], (1, 0, 2))
    sp_t = jnp.transpose(state_pool[l], (1, 0, 2))
    rw_t = router_w[l].T
    rwh = rw_t.astype(bf16)
    rwl = (rw_t - rwh.astype(f32)).astype(bf16)

    (x1, srt, slot_t, gate_t, cnt, npc, npp, nsc, nsv) = _mixer(
        xp, xs, sc_t, sp_t, norm_mix_g[l][None, :], w_in[l].astype(bf16), conv_w[l],
        pool_w[l].astype(bf16), pool_scale[l][None, :], w_out[l].astype(bf16),
        norm_ffn_g[l][None, :], rwh, rwl, router_b[l][:, None])

    t = _routing_tables(cnt)
    ys = _experts(t["tile_expert"], t["n_used"], t["first"], t["order"], t["nxt"], t["piece_row"],
                  t["tot"], srt, exp_w1[l], exp_b1[l][:, None, :], exp_w2[l], exp_b2[l][:, None, :])
    y_p, y_s = _combine(slot_t.T, gate_t.T, x1, final_norm_g[None, :], ys)

    y_prompt = y_p.reshape(BATCH, SEQ, D_MODEL)
    y_sample = jnp.transpose(y_s.reshape(DEC_SEQ, DEC_BATCH, D_MODEL), (1, 0, 2))
    new_conv_prompt = npc[None, :, 8 - (CONV_W - 1):, :]
    new_pool_prompt = npp[None, :, HIST - POOL_BUF:, :]
    new_conv_sample = jnp.transpose(nsc, (1, 0, 2))[None]
    new_pool_sample = jnp.concatenate(
        [state_pool[l][:, DEC_SEQ:, :], jnp.transpose(nsv, (1, 0, 2))], axis=1)[None]
    return (y_prompt, y_sample, new_conv_prompt, new_pool_prompt, new_conv_sample, new_pool_sample)
```

```python
import jax
import jax.numpy as jnp
from jax import lax
from jax.experimental import pallas as pl
from jax.experimental.pallas import tpu as pltpu

D_MODEL = 1024
D_CONV = 512
D_POOL = 512
D_IN = 3 * D_CONV + D_POOL
CONV_W = 3
POOL_WINDOWS = (2, 4, 8, 16)
POOL_GROUP = 128
POOL_BUF = 15
N_EXPERTS = 32
TOP_K = 4
D_FF = 1024
SWIGLU_LIMIT = 7.0
SWIGLU_ALPHA = 1.702
EPS = 1e-5
PAST_LEN = 16384

BATCH, SEQ = 8, 2048
DEC_BATCH, DEC_SEQ = 128, 4
N_PROMPT = BATCH * SEQ
N_SAMPLE = DEC_BATCH * DEC_SEQ
N_TOK = N_PROMPT + N_SAMPLE

TS = 512
SEQ_TILES = SEQ // TS
N_PROMPT_TILES = N_PROMPT // TS
N_TILES = N_PROMPT_TILES + 1
HIST = 16
TM = 256
PIECE = 16
PIECES = TM // PIECE
S_BLK = 256
S_MAX = (TS * TOP_K + N_EXPERTS * (PIECE - 1) + PIECE + S_BLK - 1) // S_BLK * S_BLK
ZERO_ROW = S_MAX - PIECE
N_ROW_TILES = (N_TOK * TOP_K + N_TILES * N_EXPERTS * (PIECE - 1) + N_EXPERTS * (TM - PIECE)) // TM
N_SORT_ROWS = N_TILES * S_MAX
DUMP_ROW = N_SORT_ROWS

VMEM_LIMIT = 56 * 1024 * 1024

f32 = jnp.float32
bf16 = jnp.bfloat16
i32 = jnp.int32


def _rms(x, g):
    return x * lax.rsqrt(jnp.mean(x * x, axis=-1, keepdims=True) + EPS) * g


def _pool_project(p, pw_ref, ps_ref):
    outs = []
    for g in range(len(POOL_WINDOWS)):
        sl = slice(g * POOL_GROUP, (g + 1) * POOL_GROUP)
        outs.append(jnp.dot(p[:, sl].astype(bf16), pw_ref[g], preferred_element_type=f32))
    return jnp.concatenate(outs, axis=1) * ps_ref[...]


def _route_and_sort(x, z, wout_ref, g2_ref, rwh_ref, rwl_ref, rb_ref,
                    x1_ref, srt_ref, slot_ref, gate_ref, cnt_ref):
    n = x.shape[0]
    x1 = x + jnp.dot(z.astype(bf16), wout_ref[...], preferred_element_type=f32)
    x1_ref[...] = x1
    xn = _rms(x1, g2_ref[...])
    xb = xn.astype(bf16)

    xlo = (xn - xb.astype(f32)).astype(bf16)
    nt = (((1,), (1,)), ((), ()))
    logits = (lax.dot_general(rwh_ref[...], xb, nt, preferred_element_type=f32)
              + lax.dot_general(rwl_ref[...], xb, nt, preferred_element_type=f32)
              + lax.dot_general(rwh_ref[...], xlo, nt, preferred_element_type=f32)
              + rb_ref[...])

    e_iota = lax.broadcasted_iota(i32, (N_EXPERTS, n), 0)
    work = logits
    hots, vals = [], []
    for _ in range(TOP_K):
        m = jnp.max(work, axis=0, keepdims=True)
        ik = jnp.min(jnp.where(work == m, e_iota, N_EXPERTS), axis=0, keepdims=True)
        hot = e_iota == ik
        work = jnp.where(hot, -jnp.inf, work)
        hots.append(hot); vals.append(m)
    exps = [jnp.exp(v - vals[0]) for v in vals]
    den = exps[0] + exps[1] + exps[2] + exps[3]
    gate_ref[...] = jnp.concatenate([e / den for e in exps], axis=0)

    multi = jnp.where(hots[0] | hots[1] | hots[2] | hots[3], 1.0, 0.0)
    r_io = lax.broadcasted_iota(i32, (n, n), 0)
    c_io = lax.broadcasted_iota(i32, (n, n), 1)
    tri = jnp.where(r_io < c_io, 1.0, 0.0).astype(bf16)
    before = jnp.dot(multi.astype(bf16), tri, preferred_element_type=f32)
    count = jnp.sum(multi, axis=1, keepdims=True)
    cnt_ref[...] = count
    pieces = jnp.floor((count + (PIECE - 1)) * (1.0 / PIECE))
    er = lax.broadcasted_iota(i32, (N_EXPERTS, N_EXPERTS), 0)
    ec = lax.broadcasted_iota(i32, (N_EXPERTS, N_EXPERTS), 1)
    lower = jnp.where(ec < er, 1.0, 0.0).astype(bf16)
    pieces_b = jnp.broadcast_to(pieces, (N_EXPERTS, 128)).astype(bf16)
    start = PIECE * jnp.dot(lower, pieces_b, preferred_element_type=f32)[:, 0:1]
    base = before + start
    slot = jnp.concatenate(
        [jnp.sum(jnp.where(h, base, 0.0), axis=0, keepdims=True) for h in hots], axis=0).astype(i32)
    slot_ref[...] = slot

    def sort_blocks(b2, c):
        for half in range(2):
            row0 = pl.multiple_of((2 * b2 + half) * S_BLK, S_BLK)
            s_io = row0 + lax.broadcasted_iota(i32, (S_BLK, n), 0)
            hit = s_io == slot[0:1, :]
            for k in range(1, TOP_K):
                hit = hit | (s_io == slot[k:k + 1, :])
            sel = jnp.where(hit, 1.0, 0.0).astype(bf16)
            srt_ref[pl.ds(row0, S_BLK), :] = jnp.dot(
                sel, xb, preferred_element_type=f32).astype(bf16)
        return c
    lax.fori_loop(0, S_MAX // (2 * S_BLK), sort_blocks, 0)


def _mixer_kernel(xp_in_ref, xs_in_ref, sc_ref, sp_ref, g1_ref, win_ref, cw_ref, pw_ref, ps_ref,
                  wout_ref, g2_ref, rwh_ref, rwl_ref, rb_ref,
                  x1_ref, srt_ref, slot_ref, gate_ref, cnt_ref,
                  npc_ref, npp_ref, nsc_ref, nsv_ref,
                  ubuf, vbuf, xcur, zcur):
    i = pl.program_id(0)

    @pl.when(i < N_PROMPT_TILES)
    def _():
        s = i % SEQ_TILES

        @pl.when(s == 0)
        def _():
            ubuf[0:HIST, :] = jnp.zeros((HIST, D_CONV), f32)
            vbuf[0:HIST, :] = jnp.zeros((HIST, D_POOL), f32)

        x = xp_in_ref[...]
        xcur[...] = x
        xn = _rms(x, g1_ref[...]).astype(bf16)
        proj = jnp.dot(xn, win_ref[...], preferred_element_type=f32)
        bg = proj[:, :D_CONV]
        u = proj[:, D_CONV:2 * D_CONV] * proj[:, 2 * D_CONV:3 * D_CONV]
        v = proj[:, 3 * D_CONV:]
        ubuf[HIST:HIST + TS, :] = u
        vbuf[HIST:HIST + TS, :] = v

        y = (ubuf[HIST - 2:HIST - 2 + TS, :] * cw_ref[0:1, :]
             + ubuf[HIST - 1:HIST - 1 + TS, :] * cw_ref[1:2, :]
             + u * cw_ref[2:3, :])
        zcur[:, :D_CONV] = bg * y

        pos = s * TS + lax.broadcasted_iota(i32, (TS, 1), 0)
        ps = []
        for g, w in enumerate(POOL_WINDOWS):
            sl = slice(g * POOL_GROUP, (g + 1) * POOL_GROUP)
            acc = vbuf[:, sl]
            for step in range(g + 1):
                acc = acc + pltpu.roll(acc, 1 << step, axis=0)
            cnt = jnp.minimum(w, pos + 1).astype(f32)
            ps.append(acc[HIST:, :] * (1.0 / cnt) - v[:, sl])
        zcur[:, D_CONV:] = _pool_project(jnp.concatenate(ps, axis=1), pw_ref, ps_ref)

        ubuf[HIST - 8:HIST, :] = ubuf[TS + HIST - 8:TS + HIST, :]
        vbuf[0:HIST, :] = vbuf[TS:TS + HIST, :]

        @pl.when(s == SEQ_TILES - 1)
        def _():
            npc_ref[0] = ubuf[HIST - 8:HIST, :]
            npp_ref[0] = vbuf[0:HIST, :]

    @pl.when(i == N_PROMPT_TILES)
    def _():
        x = xs_in_ref[...]
        xcur[...] = x
        xn = _rms(x, g1_ref[...]).astype(bf16)
        proj = jnp.dot(xn, win_ref[...], preferred_element_type=f32)
        nb = DEC_BATCH
        rows = lambda a, t: a[t * nb:(t + 1) * nb]
        bg = proj[:, :D_CONV]
        u = proj[:, D_CONV:2 * D_CONV] * proj[:, 2 * D_CONV:3 * D_CONV]
        v = proj[:, 3 * D_CONV:]
        up = [sc_ref[j] for j in range(CONV_W - 1)] + [rows(u, t) for t in range(DEC_SEQ)]
        vp = [sp_ref[j] for j in range(POOL_BUF)] + [rows(v, t) for t in range(DEC_SEQ)]
        zc, ps = [], []
        for t in range(DEC_SEQ):
            y = up[t] * cw_ref[0:1, :]
            for k in range(1, CONV_W):
                y = y + up[t + k] * cw_ref[k:k + 1, :]
            zc.append(rows(bg, t) * y)
            pg = []
            for g, w in enumerate(POOL_WINDOWS):
                sl = slice(g * POOL_GROUP, (g + 1) * POOL_GROUP)
                lo = t + POOL_BUF - w + 1
                acc = vp[lo][:, sl]
                for j in range(lo + 1, t + POOL_BUF + 1):
                    acc = acc + vp[j][:, sl]
                cnt = float(min(w, PAST_LEN + t + 1))
                pg.append(acc * (1.0 / cnt) - vp[t + POOL_BUF][:, sl])
            ps.append(jnp.concatenate(pg, axis=1))
        zcur[:, :D_CONV] = jnp.concatenate(zc, axis=0)
        zcur[:, D_CONV:] = _pool_project(jnp.concatenate(ps, axis=0), pw_ref, ps_ref)
        for j in range(CONV_W - 1):
            nsc_ref[j] = up[DEC_SEQ + j]
        for t in range(DEC_SEQ):
            nsv_ref[t] = rows(v, t)

    _route_and_sort(xcur[...], zcur[...], wout_ref, g2_ref, rwh_ref, rwl_ref, rb_ref,
                    x1_ref, srt_ref, slot_ref, gate_ref, cnt_ref)


def _mixer(xp, xs, sc_t, sp_t, g1, win, cw, pw, ps, wout, g2, rwh, rwl, rb):
    const = lambda shape: pl.BlockSpec(shape, lambda i: (0,) * len(shape),
                                       pipeline_mode=pl.Buffered(1))
    ptile = lambda i: jnp.minimum(i, N_PROMPT_TILES - 1)
    pbatch = lambda i: jnp.minimum(i, N_PROMPT_TILES - 1) // SEQ_TILES
    out_shape = (
        jax.ShapeDtypeStruct((N_TOK, D_MODEL), f32),
        jax.ShapeDtypeStruct((N_SORT_ROWS, D_MODEL), bf16),
        jax.ShapeDtypeStruct((TOP_K, N_TOK), i32),
        jax.ShapeDtypeStruct((TOP_K, N_TOK), f32),
        jax.ShapeDtypeStruct((N_TILES * N_EXPERTS, 1), f32),
        jax.ShapeDtypeStruct((BATCH, 8, D_CONV), f32),
        jax.ShapeDtypeStruct((BATCH, HIST, D_POOL), f32),
        jax.ShapeDtypeStruct((CONV_W - 1, DEC_BATCH, D_CONV), f32),
        jax.ShapeDtypeStruct((DEC_SEQ, DEC_BATCH, D_POOL), f32),
    )
    out_specs = (
        pl.BlockSpec((TS, D_MODEL), lambda i: (i, 0)),
        pl.BlockSpec((S_MAX, D_MODEL), lambda i: (i, 0)),
        pl.BlockSpec((TOP_K, TS), lambda i: (0, i)),
        pl.BlockSpec((TOP_K, TS), lambda i: (0, i)),
        pl.BlockSpec((N_EXPERTS, 1), lambda i: (i, 0)),
        pl.BlockSpec((1, 8, D_CONV), lambda i: (pbatch(i), 0, 0)),
        pl.BlockSpec((1, HIST, D_POOL), lambda i: (pbatch(i), 0, 0)),
        pl.BlockSpec((CONV_W - 1, DEC_BATCH, D_CONV), lambda i: (0, 0, 0)),
        pl.BlockSpec((DEC_SEQ, DEC_BATCH, D_POOL), lambda i: (0, 0, 0)),
    )
    in_specs = [
        pl.BlockSpec((TS, D_MODEL), lambda i: (ptile(i), 0)),
        const((N_SAMPLE, D_MODEL)),
        const((CONV_W - 1, DEC_BATCH, D_CONV)),
        const((POOL_BUF, DEC_BATCH, D_POOL)),
        const((1, D_MODEL)),
        const((D_MODEL, D_IN)),
        const((CONV_W, D_CONV)),
        const((len(POOL_WINDOWS), POOL_GROUP, POOL_GROUP)),
        const((1, D_POOL)),
        const((D_MODEL, D_MODEL)),
        const((1, D_MODEL)),
        const((N_EXPERTS, D_MODEL)),
        const((N_EXPERTS, D_MODEL)),
        const((N_EXPERTS, 1)),
    ]
    return pl.pallas_call(
        _mixer_kernel,
        grid=(N_TILES,),
        in_specs=in_specs,
        out_specs=out_specs,
        out_shape=out_shape,
        scratch_shapes=[
            pltpu.VMEM((TS + HIST, D_CONV), f32),
            pltpu.VMEM((TS + HIST, D_POOL), f32),
            pltpu.VMEM((TS, D_MODEL), f32),
            pltpu.VMEM((TS, D_MODEL), f32),
        ],
        compiler_params=pltpu.CompilerParams(
            dimension_semantics=("arbitrary",), vmem_limit_bytes=VMEM_LIMIT),
        name="mixer",
    )(xp, xs, sc_t, sp_t, g1, win, cw, pw, ps, wout, g2, rwh, rwl, rb)


def _expert_kernel(te_ref, nu_ref, first_ref, ord_ref, nxt_ref, pt_ref, tot_ref,
                   b1_ref, b2_ref, srt_hbm, w1_hbm, w2_hbm, ys_hbm,
                   xbuf, ybuf, zpiece, w1f, w2f, w1b, w2b, sem_w, sem_x, sem_y, sem_z):
    i = pl.program_id(0)
    par = i % 2
    n_used = nu_ref[0]
    live = i < n_used

    def weight_copies(e, slot):
        return (pltpu.make_async_copy(w1_hbm.at[e], w1f.at[slot], sem_w.at[slot]),
                pltpu.make_async_copy(w2_hbm.at[e], w2f.at[slot], sem_w.at[2 + slot]))

    def start_x(tile, b):
        for p in range(PIECES):
            src = pt_ref[tile * PIECES + p]
            src = pl.multiple_of(jnp.where(src < 0, ZERO_ROW, src), PIECE)
            pltpu.make_async_copy(srt_hbm.at[pl.ds(src, PIECE)],
                                  xbuf.at[b, pl.ds(p * PIECE, PIECE)], sem_x.at[b]).start()

    def start_y(tile, b, all_dump=False):
        for p in range(PIECES):
            dst = pt_ref[tile * PIECES + p]
            dump = (dst < 0) | all_dump
            dst = pl.multiple_of(jnp.where(dump, DUMP_ROW + b * TM + p * PIECE, dst), PIECE)
            pltpu.make_async_copy(ybuf.at[b, pl.ds(p * PIECE, PIECE)],
                                  ys_hbm.at[pl.ds(dst, PIECE)], sem_y.at[b]).start()

    def wait_x(b):
        pltpu.make_async_copy(srt_hbm.at[pl.ds(0, TM)], xbuf.at[b], sem_x.at[b]).wait()

    def wait_y(b):
        pltpu.make_async_copy(ybuf.at[b], ys_hbm.at[pl.ds(0, TM)], sem_y.at[b]).wait()

    @pl.when(i == 0)
    def _():
        zpiece[...] = jnp.zeros_like(zpiece)

        def fill_tile(t, c):
            def dst(p):
                row = pl.multiple_of(t * S_MAX + tot_ref[t] + p * PIECE, PIECE)
                return ys_hbm.at[pl.ds(row, PIECE)]

            def start(p, c2):
                pltpu.make_async_copy(zpiece, dst(p), sem_z.at[0]).start()
                return c2

            def wait(p, c2):
                pltpu.make_async_copy(zpiece, dst(p), sem_z.at[0]).wait()
                return c2
            n_fill = (S_MAX - tot_ref[t]) // PIECE
            lax.fori_loop(0, n_fill, start, 0)
            return lax.fori_loop(0, n_fill, wait, c)
        lax.fori_loop(0, N_TILES, fill_tile, 0)
        start_x(0, 0)
        start_x(jnp.minimum(1, n_used - 1), 1)
        ybuf[...] = jnp.zeros_like(ybuf)
        for b in range(2):
            pltpu.make_async_copy(ybuf.at[b], ys_hbm.at[pl.ds(DUMP_ROW + b * TM, TM)],
                                  sem_y.at[b]).start()

    @pl.when(live)
    def _():
        xslot = i % 3
        wait_x(xslot)

        @pl.when(first_ref[i] == 1)
        def _():
            slot = ord_ref[i] % 2

            @pl.when(i == 0)
            def _():
                for cp in weight_copies(te_ref[0], 0):
                    cp.start()
            for cp in weight_copies(te_ref[i], slot):
                cp.wait()

            @pl.when(nxt_ref[i] >= 0)
            def _():
                for cp in weight_copies(nxt_ref[i], 1 - slot):
                    cp.start()
            w1b[...] = w1f[slot].astype(bf16)
            w2b[...] = w2f[slot].astype(bf16)

        wait_y(xslot)
        gu =jnp.dot(xbuf[xslot], w1b[...], preferred_element_type=f32) + b1_ref[0]
        gate = jnp.minimum(gu[:, :D_FF], SWIGLU_LIMIT)
        lin = jnp.clip(gu[:, D_FF:], -SWIGLU_LIMIT, SWIGLU_LIMIT)
        glu = gate * jax.nn.sigmoid(SWIGLU_ALPHA * gate)
        h = (glu * (lin + 1.0)).astype(bf16)
        start_x(jnp.minimum(i + 2, n_used - 1), (i + 2) % 3)
        start_y(jnp.maximum(i - 1, 0), (i + 2) % 3, all_dump=i == 0)
        out = jnp.dot(h, w2b[...], preferred_element_type=f32) + b2_ref[0]
        ybuf[xslot] = out.astype(bf16)

        @pl.when(i == n_used - 1)
        def _():
            start_y(i, xslot)
            for b in range(3):
                wait_y(b)
            wait_x((i + 1) % 3)
            wait_x((i + 2) % 3)


def _experts(tile_expert, n_used, first, order, nxt, piece_row, tot, srt, w1, b1, w2, b2):
    wsel = lambda i, te, *_: (te[i], 0, 0)
    grid_spec = pltpu.PrefetchScalarGridSpec(
        num_scalar_prefetch=7,
        grid=(N_ROW_TILES,),
        in_specs=[
            pl.BlockSpec((1, 1, 2 * D_FF), wsel),
            pl.BlockSpec((1, 1, D_MODEL), wsel),
            pl.BlockSpec(memory_space=pl.ANY),
            pl.BlockSpec(memory_space=pl.ANY),
            pl.BlockSpec(memory_space=pl.ANY),
        ],
        out_specs=pl.BlockSpec(memory_space=pl.ANY),
        scratch_shapes=[
            pltpu.VMEM((3, TM, D_MODEL), bf16), pltpu.VMEM((3, TM, D_MODEL), bf16),
            pltpu.VMEM((PIECE, D_MODEL), bf16),
            pltpu.VMEM((2, D_MODEL, 2 * D_FF), f32), pltpu.VMEM((2, D_FF, D_MODEL), f32),
            pltpu.VMEM((D_MODEL, 2 * D_FF), bf16), pltpu.VMEM((D_FF, D_MODEL), bf16),
            pltpu.SemaphoreType.DMA((4,)), pltpu.SemaphoreType.DMA((3,)),
            pltpu.SemaphoreType.DMA((3,)), pltpu.SemaphoreType.DMA((1,)),
        ],
    )
    return pl.pallas_call(
        _expert_kernel,
        grid_spec=grid_spec,
        out_shape=jax.ShapeDtypeStruct((N_SORT_ROWS + 3 * TM, D_MODEL), bf16),
        compiler_params=pltpu.CompilerParams(
            dimension_semantics=("arbitrary",), vmem_limit_bytes=VMEM_LIMIT),
        name="experts",
    )(tile_expert, n_used, first, order, nxt, piece_row, tot, b1, b2, srt, w1, w2)


def _combine_kernel(slot_ref, g_ref, x1_ref, gf_ref, ys_ref, yp_ref, ysm_ref, whi, wlo):
    i = pl.program_id(0)

    slot = slot_ref[...]
    g = g_ref[...]
    for b in range(S_MAX // S_BLK):
        s_io = b * S_BLK + lax.broadcasted_iota(i32, (TS, S_BLK), 1)
        w = jnp.where(s_io == slot[:, 0:1], g[:, 0:1], 0.0)
        for k in range(1, TOP_K):
            w = w + jnp.where(s_io == slot[:, k:k + 1], g[:, k:k + 1], 0.0)
        hi = w.astype(bf16)
        whi[:, b * S_BLK:(b + 1) * S_BLK] = hi
        wlo[:, b * S_BLK:(b + 1) * S_BLK] = (w - hi.astype(f32)).astype(bf16)

    rows = ys_ref[...]
    y = (x1_ref[...] + jnp.dot(whi[...], rows, preferred_element_type=f32)
         + jnp.dot(wlo[...], rows, preferred_element_type=f32))
    out = _rms(y, gf_ref[...])

    @pl.when(i < N_PROMPT_TILES)
    def _():
        yp_ref[...] = out

    @pl.when(i == N_PROMPT_TILES)
    def _():
        ysm_ref[...] = out


def _combine(slot_tm, gates_tm, x1, gf, ys):
    ptile = lambda i: (jnp.minimum(i, N_PROMPT_TILES - 1), 0)
    return pl.pallas_call(
        _combine_kernel,
        grid=(N_TILES,),
        in_specs=[
            pl.BlockSpec((TS, TOP_K), lambda i: (i, 0)),
            pl.BlockSpec((TS, TOP_K), lambda i: (i, 0)),
            pl.BlockSpec((TS, D_MODEL), lambda i: (i, 0)),
            pl.BlockSpec((1, D_MODEL), lambda i: (0, 0)),
            pl.BlockSpec((S_MAX, D_MODEL), lambda i: (i, 0)),
        ],
        out_specs=(
            pl.BlockSpec((TS, D_MODEL), ptile),
            pl.BlockSpec((N_SAMPLE, D_MODEL), lambda i: (0, 0)),
        ),
        out_shape=(
            jax.ShapeDtypeStruct((N_PROMPT, D_MODEL), f32),
            jax.ShapeDtypeStruct((N_SAMPLE, D_MODEL), f32),
        ),
        scratch_shapes=[pltpu.VMEM((TS, S_MAX), bf16), pltpu.VMEM((TS, S_MAX), bf16)],
        compiler_params=pltpu.CompilerParams(
            dimension_semantics=("arbitrary",), vmem_limit_bytes=VMEM_LIMIT),
        name="combine",
    )(slot_tm, gates_tm, x1, gf, ys)


def _routing_tables(cnt):
    n = cnt.reshape(N_TILES, N_EXPERTS).astype(i32)
    c = (n + (PIECE - 1)) // PIECE * PIECE
    lo = jnp.cumsum(c, axis=1) - c
    tot = jnp.sum(c, axis=1)
    group = jnp.sum(c, axis=0)
    padded = (group + (TM - 1)) // TM * TM
    gend = jnp.cumsum(padded)
    gstart = gend - padded
    cs = gstart[None, :] + jnp.cumsum(c, axis=0) - c
    n_used = gend[-1] // TM
    tile_id = jnp.arange(N_ROW_TILES, dtype=i32)
    tile_expert = jnp.sum((tile_id[:, None] * TM >= gend[None, :]).astype(i32), axis=1)
    tile_expert = jnp.minimum(tile_expert, jnp.take(tile_expert, n_used - 1))
    first = (tile_id < n_used) & ((tile_id == 0) | (tile_expert != jnp.roll(tile_expert, 1)))
    order = jnp.maximum(jnp.cumsum(first.astype(i32)) - 1, 0)
    e_ar = jnp.arange(N_EXPERTS, dtype=i32)
    later = (e_ar[None, :] > e_ar[:, None]) & (padded[None, :] > 0)
    nxt_e = jnp.min(jnp.where(later, e_ar[None, :], N_EXPERTS), axis=1)
    nxt_e = jnp.where(nxt_e == N_EXPERTS, -1, nxt_e)
    nxt = jnp.sum(jnp.where(tile_expert[:, None] == e_ar[None, :], nxt_e[None, :], 0), axis=1)
    cs_f = cs.T.reshape(-1)
    c_f = c.T.reshape(-1)
    src_f = (jnp.arange(N_TILES, dtype=i32)[:, None] * S_MAX + lo).T.reshape(-1)
    piece_r = jnp.arange(N_ROW_TILES * PIECES, dtype=i32) * PIECE
    reached = (piece_r[:, None] >= cs_f[None, 1:]).astype(i32)

    def lookup(f):
        return f[0] + jnp.sum(reached * (f[1:] - f[:-1])[None, :], axis=1)
    off = piece_r - lookup(cs_f)
    piece_row = jnp.where(off < lookup(c_f), lookup(src_f) + off, -1)
    return dict(tot=tot, n_used=n_used[None], tile_expert=tile_expert, first=first.astype(i32),
                order=order, nxt=nxt, piece_row=piece_row)


def kernel(x_prompt, x_sample, state_conv, state_pool, norm_mix_g, w_in, conv_w, pool_w, pool_scale,
           w_out, norm_ffn_g, router_w, router_b, exp_w1, exp_b1, exp_w2, exp_b2, final_norm_g):
    l = 0
    xp = x_prompt.reshape(N_PROMPT, D_MODEL)
    xs = jnp.transpose(x_sample, (1, 0, 2)).reshape(N_SAMPLE, D_MODEL)
    sc_t = jnp.transpose(state_conv[l], (1, 0, 2))
    sp_t = jnp.transpose(state_pool[l], (1, 0, 2))
    rw_t = router_w[l].T
    rwh = rw_t.astype(bf16)
    rwl = (rw_t - rwh.astype(f32)).astype(bf16)

    (x1, srt, slot_t, gate_t, cnt, npc, npp, nsc, nsv) = _mixer(
        xp, xs, sc_t, sp_t, norm_mix_g[l][None, :], w_in[l].astype(bf16), conv_w[l],
        pool_w[l].astype(bf16), pool_scale[l][None, :], w_out[l].astype(bf16),
        norm_ffn_g[l][None, :], rwh, rwl, router_b[l][:, None])

    t = _routing_tables(cnt)
    ys = _experts(t["tile_expert"], t["n_used"], t["first"], t["order"], t["nxt"], t["piece_row"],
                  t["tot"], srt, exp_w1[l], exp_b1[l][:, None, :], exp_w2[l], exp_b2[l][:, None, :])
    y_p, y_s = _combine(slot_t.T, gate_t.T, x1, final_norm_g[None, :], ys)

    y_prompt = y_p.reshape(BATCH, SEQ, D_MODEL)
    y_sample = jnp.transpose(y_s.reshape(DEC_SEQ, DEC_BATCH, D_MODEL), (1, 0, 2))
    new_conv_prompt = npc[None, :, 8 - (CONV_W - 1):, :]
    new_pool_prompt = npp[None, :, HIST - POOL_BUF:, :]
    new_conv_sample = jnp.transpose(nsc, (1, 0, 2))[None]
    new_pool_sample = jnp.concatenate(
        [state_pool[l][:, DEC_SEQ:, :], jnp.transpose(nsv, (1, 0, 2))], axis=1)[None]
    return (y_prompt, y_sample, new_conv_prompt, new_pool_prompt, new_conv_sample, new_pool_sample)
```

```python
import jax
import jax.numpy as jnp
from jax import lax
from jax.experimental import pallas as pl
from jax.experimental.pallas import tpu as pltpu

D_MODEL = 1024
D_CONV = 512
D_POOL = 512
D_IN = 3 * D_CONV + D_POOL
CONV_W = 3
POOL_WINDOWS = (2, 4, 8, 16)
POOL_GROUP = 128
POOL_BUF = 15
N_EXPERTS = 32
TOP_K = 4
D_FF = 1024
SWIGLU_LIMIT = 7.0
SWIGLU_ALPHA = 1.702
EPS = 1e-5
PAST_LEN = 16384

BATCH, SEQ = 8, 2048
DEC_BATCH, DEC_SEQ = 128, 4
N_PROMPT = BATCH * SEQ
N_SAMPLE = DEC_BATCH * DEC_SEQ
N_TOK = N_PROMPT + N_SAMPLE

TS = 512
SEQ_TILES = SEQ // TS
N_PROMPT_TILES = N_PROMPT // TS
N_TILES = N_PROMPT_TILES + 1
HIST = 16
TM = 256
PIECE = 16
PIECES = TM // PIECE
S_BLK = 256
SORT_UNROLL = 5
S_MAX = (TS * TOP_K + N_EXPERTS * (PIECE - 1) + PIECE + S_BLK - 1) // S_BLK * S_BLK
ZERO_ROW = S_MAX - PIECE
N_ROW_TILES = (N_TOK * TOP_K + N_TILES * N_EXPERTS * (PIECE - 1) + N_EXPERTS * (TM - PIECE)) // TM
N_SORT_ROWS = N_TILES * S_MAX
DUMP_ROW = N_SORT_ROWS

VMEM_LIMIT = 56 * 1024 * 1024

f32 = jnp.float32
bf16 = jnp.bfloat16
i32 = jnp.int32


def _rms(x, g):
    return x * lax.rsqrt(jnp.mean(x * x, axis=-1, keepdims=True) + EPS) * g


def _pool_project(p, pw_ref, ps_ref):
    outs = []
    for g in range(len(POOL_WINDOWS)):
        sl = slice(g * POOL_GROUP, (g + 1) * POOL_GROUP)
        outs.append(jnp.dot(p[:, sl].astype(bf16), pw_ref[g], preferred_element_type=f32))
    return jnp.concatenate(outs, axis=1) * ps_ref[...]


def _route_and_sort(x, z, wout_ref, g2_ref, rwh_ref, rwl_ref, rb_ref,
                    x1_ref, srt_ref, slot_ref, gate_ref, cnt_ref):
    n = x.shape[0]
    x1 = x + jnp.dot(z.astype(bf16), wout_ref[...], preferred_element_type=f32)
    x1_ref[...] = x1
    xn = _rms(x1, g2_ref[...])
    xb = xn.astype(bf16)

    xlo = (xn - xb.astype(f32)).astype(bf16)
    nt = (((1,), (1,)), ((), ()))
    logits = (lax.dot_general(rwh_ref[...], xb, nt, preferred_element_type=f32)
              + lax.dot_general(rwl_ref[...], xb, nt, preferred_element_type=f32)
              + lax.dot_general(rwh_ref[...], xlo, nt, preferred_element_type=f32)
              + rb_ref[...])

    e_iota = lax.broadcasted_iota(i32, (N_EXPERTS, n), 0)
    work = logits
    hots, vals = [], []
    for _ in range(TOP_K):
        m = jnp.max(work, axis=0, keepdims=True)
        ik = jnp.min(jnp.where(work == m, e_iota, N_EXPERTS), axis=0, keepdims=True)
        hot = e_iota == ik
        work = jnp.where(hot, -jnp.inf, work)
        hots.append(hot); vals.append(m)
    exps = [jnp.exp(v - vals[0]) for v in vals]
    den = exps[0] + exps[1] + exps[2] + exps[3]
    gate_ref[...] = jnp.concatenate([e / den for e in exps], axis=0)

    multi = jnp.where(hots[0] | hots[1] | hots[2] | hots[3], 1.0, 0.0)
    r_io = lax.broadcasted_iota(i32, (n, n), 0)
    c_io = lax.broadcasted_iota(i32, (n, n), 1)
    tri = jnp.where(r_io < c_io, 1.0, 0.0).astype(bf16)
    before = jnp.dot(multi.astype(bf16), tri, preferred_element_type=f32)
    count = jnp.sum(multi, axis=1, keepdims=True)
    cnt_ref[...] = count
    pieces = jnp.floor((count + (PIECE - 1)) * (1.0 / PIECE))
    er = lax.broadcasted_iota(i32, (N_EXPERTS, N_EXPERTS), 0)
    ec = lax.broadcasted_iota(i32, (N_EXPERTS, N_EXPERTS), 1)
    lower = jnp.where(ec < er, 1.0, 0.0).astype(bf16)
    pieces_b = jnp.broadcast_to(pieces, (N_EXPERTS, 128)).astype(bf16)
    start = PIECE * jnp.dot(lower, pieces_b, preferred_element_type=f32)[:, 0:1]
    base = before + start
    slot = jnp.concatenate(
        [jnp.sum(jnp.where(h, base, 0.0), axis=0, keepdims=True) for h in hots], axis=0).astype(i32)
    slot_ref[...] = slot

    def sort_blocks(b2, c):
        for half in range(SORT_UNROLL):
            row0 = pl.multiple_of((SORT_UNROLL * b2 + half) * S_BLK, S_BLK)
            s_io = row0 + lax.broadcasted_iota(i32, (S_BLK, n), 0)
            hit = s_io == slot[0:1, :]
            for k in range(1, TOP_K):
                hit = hit | (s_io == slot[k:k + 1, :])
            sel = jnp.where(hit, 1.0, 0.0).astype(bf16)
            srt_ref[pl.ds(row0, S_BLK), :] = jnp.dot(
                sel, xb, preferred_element_type=f32).astype(bf16)
        return c
    lax.fori_loop(0, S_MAX // (SORT_UNROLL * S_BLK), sort_blocks, 0)


def _mixer_kernel(xp_in_ref, xs_in_ref, sc_ref, sp_ref, g1_ref, win_ref, cw_ref, pw_ref, ps_ref,
                  wout_ref, g2_ref, rwh_ref, rwl_ref, rb_ref,
                  x1_ref, srt_ref, slot_ref, gate_ref, cnt_ref,
                  npc_ref, npp_ref, nsc_ref, nsv_ref,
                  ubuf, vbuf, xcur, zcur):
    i = pl.program_id(0)

    @pl.when(i < N_PROMPT_TILES)
    def _():
        s = i % SEQ_TILES

        @pl.when(s == 0)
        def _():
            ubuf[0:HIST, :] = jnp.zeros((HIST, D_CONV), f32)
            vbuf[0:HIST, :] = jnp.zeros((HIST, D_POOL), f32)

        x = xp_in_ref[...]
        xcur[...] = x
        xn = _rms(x, g1_ref[...]).astype(bf16)
        proj = jnp.dot(xn, win_ref[...], preferred_element_type=f32)
        bg = proj[:, :D_CONV]
        u = proj[:, D_CONV:2 * D_CONV] * proj[:, 2 * D_CONV:3 * D_CONV]
        v = proj[:, 3 * D_CONV:]
        ubuf[HIST:HIST + TS, :] = u
        vbuf[HIST:HIST + TS, :] = v

        y = (ubuf[HIST - 2:HIST - 2 + TS, :] * cw_ref[0:1, :]
             + ubuf[HIST - 1:HIST - 1 + TS, :] * cw_ref[1:2, :]
             + u * cw_ref[2:3, :])
        zcur[:, :D_CONV] = bg * y

        pos = s * TS + lax.broadcasted_iota(i32, (TS, 1), 0)
        ps = []
        for g, w in enumerate(POOL_WINDOWS):
            sl = slice(g * POOL_GROUP, (g + 1) * POOL_GROUP)
            acc = vbuf[:, sl]
            for step in range(g + 1):
                acc = acc + pltpu.roll(acc, 1 << step, axis=0)
            cnt = jnp.minimum(w, pos + 1).astype(f32)
            ps.append(acc[HIST:, :] * (1.0 / cnt) - v[:, sl])
        zcur[:, D_CONV:] = _pool_project(jnp.concatenate(ps, axis=1), pw_ref, ps_ref)

        ubuf[HIST - 8:HIST, :] = ubuf[TS + HIST - 8:TS + HIST, :]
        vbuf[0:HIST, :] = vbuf[TS:TS + HIST, :]

        @pl.when(s == SEQ_TILES - 1)
        def _():
            npc_ref[0] = ubuf[HIST - 8:HIST, :]
            npp_ref[0] = vbuf[0:HIST, :]

    @pl.when(i == N_PROMPT_TILES)
    def _():
        x = xs_in_ref[...]
        xcur[...] = x
        xn = _rms(x, g1_ref[...]).astype(bf16)
        proj = jnp.dot(xn, win_ref[...], preferred_element_type=f32)
        nb = DEC_BATCH
        rows = lambda a, t: a[t * nb:(t + 1) * nb]
        bg = proj[:, :D_CONV]
        u = proj[:, D_CONV:2 * D_CONV] * proj[:, 2 * D_CONV:3 * D_CONV]
        v = proj[:, 3 * D_CONV:]
        up = [sc_ref[j] for j in range(CONV_W - 1)] + [rows(u, t) for t in range(DEC_SEQ)]
        vp = [sp_ref[j] for j in range(POOL_BUF)] + [rows(v, t) for t in range(DEC_SEQ)]
        zc, ps = [], []
        for t in range(DEC_SEQ):
            y = up[t] * cw_ref[0:1, :]
            for k in range(1, CONV_W):
                y = y + up[t + k] * cw_ref[k:k + 1, :]
            zc.append(rows(bg, t) * y)
            pg = []
            for g, w in enumerate(POOL_WINDOWS):
                sl = slice(g * POOL_GROUP, (g + 1) * POOL_GROUP)
                lo = t + POOL_BUF - w + 1
                acc = vp[lo][:, sl]
                for j in range(lo + 1, t + POOL_BUF + 1):
                    acc = acc + vp[j][:, sl]
                cnt = float(min(w, PAST_LEN + t + 1))
                pg.append(acc * (1.0 / cnt) - vp[t + POOL_BUF][:, sl])
            ps.append(jnp.concatenate(pg, axis=1))
        zcur[:, :D_CONV] = jnp.concatenate(zc, axis=0)
        zcur[:, D_CONV:] = _pool_project(jnp.concatenate(ps, axis=0), pw_ref, ps_ref)
        for j in range(CONV_W - 1):
            nsc_ref[j] = up[DEC_SEQ + j]
        for t in range(DEC_SEQ):
            nsv_ref[t] = rows(v, t)

    _route_and_sort(xcur[...], zcur[...], wout_ref, g2_ref, rwh_ref, rwl_ref, rb_ref,
                    x1_ref, srt_ref, slot_ref, gate_ref, cnt_ref)


def _mixer(xp, xs, sc_t, sp_t, g1, win, cw, pw, ps, wout, g2, rwh, rwl, rb):
    const = lambda shape: pl.BlockSpec(shape, lambda i: (0,) * len(shape),
                                       pipeline_mode=pl.Buffered(1))
    ptile = lambda i: jnp.minimum(i, N_PROMPT_TILES - 1)
    pbatch = lambda i: jnp.minimum(i, N_PROMPT_TILES - 1) // SEQ_TILES
    out_shape = (
        jax.ShapeDtypeStruct((N_TOK, D_MODEL), f32),
        jax.ShapeDtypeStruct((N_SORT_ROWS, D_MODEL), bf16),
        jax.ShapeDtypeStruct((TOP_K, N_TOK), i32),
        jax.ShapeDtypeStruct((TOP_K, N_TOK), f32),
        jax.ShapeDtypeStruct((N_TILES * N_EXPERTS, 1), f32),
        jax.ShapeDtypeStruct((BATCH, 8, D_CONV), f32),
        jax.ShapeDtypeStruct((BATCH, HIST, D_POOL), f32),
        jax.ShapeDtypeStruct((CONV_W - 1, DEC_BATCH, D_CONV), f32),
        jax.ShapeDtypeStruct((DEC_SEQ, DEC_BATCH, D_POOL), f32),
    )
    out_specs = (
        pl.BlockSpec((TS, D_MODEL), lambda i: (i, 0)),
        pl.BlockSpec((S_MAX, D_MODEL), lambda i: (i, 0)),
        pl.BlockSpec((TOP_K, TS), lambda i: (0, i)),
        pl.BlockSpec((TOP_K, TS), lambda i: (0, i)),
        pl.BlockSpec((N_EXPERTS, 1), lambda i: (i, 0)),
        pl.BlockSpec((1, 8, D_CONV), lambda i: (pbatch(i), 0, 0)),
        pl.BlockSpec((1, HIST, D_POOL), lambda i: (pbatch(i), 0, 0)),
        pl.BlockSpec((CONV_W - 1, DEC_BATCH, D_CONV), lambda i: (0, 0, 0)),
        pl.BlockSpec((DEC_SEQ, DEC_BATCH, D_POOL), lambda i: (0, 0, 0)),
    )
    in_specs = [
        pl.BlockSpec((TS, D_MODEL), lambda i: (ptile(i), 0)),
        const((N_SAMPLE, D_MODEL)),
        const((CONV_W - 1, DEC_BATCH, D_CONV)),
        const((POOL_BUF, DEC_BATCH, D_POOL)),
        const((1, D_MODEL)),
        const((D_MODEL, D_IN)),
        const((CONV_W, D_CONV)),
        const((len(POOL_WINDOWS), POOL_GROUP, POOL_GROUP)),
        const((1, D_POOL)),
        const((D_MODEL, D_MODEL)),
        const((1, D_MODEL)),
        const((N_EXPERTS, D_MODEL)),
        const((N_EXPERTS, D_MODEL)),
        const((N_EXPERTS, 1)),
    ]
    return pl.pallas_call(
        _mixer_kernel,
        grid=(N_TILES,),
        in_specs=in_specs,
        out_specs=out_specs,
        out_shape=out_shape,
        scratch_shapes=[
            pltpu.VMEM((TS + HIST, D_CONV), f32),
            pltpu.VMEM((TS + HIST, D_POOL), f32),
            pltpu.VMEM((TS, D_MODEL), f32),
            pltpu.VMEM((TS, D_MODEL), f32),
        ],
        compiler_params=pltpu.CompilerParams(
            dimension_semantics=("arbitrary",), vmem_limit_bytes=VMEM_LIMIT),
        name="mixer",
    )(xp, xs, sc_t, sp_t, g1, win, cw, pw, ps, wout, g2, rwh, rwl, rb)


def _expert_kernel(te_ref, nu_ref, first_ref, ord_ref, nxt_ref, pt_ref, tot_ref,
                   b1_ref, b2_ref, srt_hbm, w1_hbm, w2_hbm, ys_hbm,
                   xbuf, ybuf, zpiece, w1f, w2f, w1b, w2b, sem_w, sem_x, sem_y, sem_z):
    i = pl.program_id(0)
    par = i % 2
    n_used = nu_ref[0]
    live = i < n_used

    def weight_copies(e, slot):
        return (pltpu.make_async_copy(w1_hbm.at[e], w1f.at[slot], sem_w.at[slot]),
                pltpu.make_async_copy(w2_hbm.at[e], w2f.at[slot], sem_w.at[2 + slot]))

    def start_x(tile, b):
        for p in range(PIECES):
            src = pt_ref[tile * PIECES + p]
            src = pl.multiple_of(jnp.where(src < 0, ZERO_ROW, src), PIECE)
            pltpu.make_async_copy(srt_hbm.at[pl.ds(src, PIECE)],
                                  xbuf.at[b, pl.ds(p * PIECE, PIECE)], sem_x.at[b]).start()

    def start_y(tile, b, all_dump=False):
        for p in range(PIECES):
            dst = pt_ref[tile * PIECES + p]
            dump = (dst < 0) | all_dump
            dst = pl.multiple_of(jnp.where(dump, DUMP_ROW + b * TM + p * PIECE, dst), PIECE)
            pltpu.make_async_copy(ybuf.at[b, pl.ds(p * PIECE, PIECE)],
                                  ys_hbm.at[pl.ds(dst, PIECE)], sem_y.at[b]).start()

    def wait_x(b):
        pltpu.make_async_copy(srt_hbm.at[pl.ds(0, TM)], xbuf.at[b], sem_x.at[b]).wait()

    def wait_y(b):
        pltpu.make_async_copy(ybuf.at[b], ys_hbm.at[pl.ds(0, TM)], sem_y.at[b]).wait()

    @pl.when(i == 0)
    def _():
        zpiece[...] = jnp.zeros_like(zpiece)

        def fill_tile(t, c):
            def dst(p):
                row = pl.multiple_of(t * S_MAX + tot_ref[t] + p * PIECE, PIECE)
                return ys_hbm.at[pl.ds(row, PIECE)]

            def start(p, c2):
                pltpu.make_async_copy(zpiece, dst(p), sem_z.at[0]).start()
                return c2

            def wait(p, c2):
                pltpu.make_async_copy(zpiece, dst(p), sem_z.at[0]).wait()
                return c2
            n_fill = (S_MAX - tot_ref[t]) // PIECE
            lax.fori_loop(0, n_fill, start, 0)
            return lax.fori_loop(0, n_fill, wait, c)
        lax.fori_loop(0, N_TILES, fill_tile, 0)
        start_x(0, 0)
        start_x(jnp.minimum(1, n_used - 1), 1)
        ybuf[...] = jnp.zeros_like(ybuf)
        for b in range(2):
            pltpu.make_async_copy(ybuf.at[b], ys_hbm.at[pl.ds(DUMP_ROW + b * TM, TM)],
                                  sem_y.at[b]).start()

    @pl.when(live)
    def _():
        xslot = i % 3
        wait_x(xslot)

        @pl.when(first_ref[i] == 1)
        def _():
            slot = ord_ref[i] % 2

            @pl.when(i == 0)
            def _():
                for cp in weight_copies(te_ref[0], 0):
                    cp.start()
            for cp in weight_copies(te_ref[i], slot):
                cp.wait()

            @pl.when(nxt_ref[i] >= 0)
            def _():
                for cp in weight_copies(nxt_ref[i], 1 - slot):
                    cp.start()
            w1b[...] = w1f[slot].astype(bf16)
            w2b[...] = w2f[slot].astype(bf16)

        wait_y(xslot)
        gu =jnp.dot(xbuf[xslot], w1b[...], preferred_element_type=f32) + b1_ref[0]
        gate = jnp.minimum(gu[:, :D_FF], SWIGLU_LIMIT)
        lin = jnp.clip(gu[:, D_FF:], -SWIGLU_LIMIT, SWIGLU_LIMIT)
        glu = gate * jax.nn.sigmoid(SWIGLU_ALPHA * gate)
        h = (glu * (lin + 1.0)).astype(bf16)
        start_x(jnp.minimum(i + 2, n_used - 1), (i + 2) % 3)
        start_y(jnp.maximum(i - 1, 0), (i + 2) % 3, all_dump=i == 0)
        out = jnp.dot(h, w2b[...], preferred_element_type=f32) + b2_ref[0]
        ybuf[xslot] = out.astype(bf16)

        @pl.when(i == n_used - 1)
        def _():
            start_y(i, xslot)
            for b in range(3):
                wait_y(b)
            wait_x((i + 1) % 3)
            wait_x((i + 2) % 3)


def _experts(tile_expert, n_used, first, order, nxt, piece_row, tot, srt, w1, b1, w2, b2):
    wsel = lambda i, te, *_: (te[i], 0, 0)
    grid_spec = pltpu.PrefetchScalarGridSpec(
        num_scalar_prefetch=7,
        grid=(N_ROW_TILES,),
        in_specs=[
            pl.BlockSpec((1, 1, 2 * D_FF), wsel),
            pl.BlockSpec((1, 1, D_MODEL), wsel),
            pl.BlockSpec(memory_space=pl.ANY),
            pl.BlockSpec(memory_space=pl.ANY),
            pl.BlockSpec(memory_space=pl.ANY),
        ],
        out_specs=pl.BlockSpec(memory_space=pl.ANY),
        scratch_shapes=[
            pltpu.VMEM((3, TM, D_MODEL), bf16), pltpu.VMEM((3, TM, D_MODEL), bf16),
            pltpu.VMEM((PIECE, D_MODEL), bf16),
            pltpu.VMEM((2, D_MODEL, 2 * D_FF), f32), pltpu.VMEM((2, D_FF, D_MODEL), f32),
            pltpu.VMEM((D_MODEL, 2 * D_FF), bf16), pltpu.VMEM((D_FF, D_MODEL), bf16),
            pltpu.SemaphoreType.DMA((4,)), pltpu.SemaphoreType.DMA((3,)),
            pltpu.SemaphoreType.DMA((3,)), pltpu.SemaphoreType.DMA((1,)),
        ],
    )
    return pl.pallas_call(
        _expert_kernel,
        grid_spec=grid_spec,
        out_shape=jax.ShapeDtypeStruct((N_SORT_ROWS + 3 * TM, D_MODEL), bf16),
        compiler_params=pltpu.CompilerParams(
            dimension_semantics=("arbitrary",), vmem_limit_bytes=VMEM_LIMIT),
        name="experts",
    )(tile_expert, n_used, first, order, nxt, piece_row, tot, b1, b2, srt, w1, w2)


def _combine_kernel(slot_ref, g_ref, x1_ref, gf_ref, ys_ref, yp_ref, ysm_ref, wgt):
    i = pl.program_id(0)

    slot = slot_ref[...]
    g = g_ref[...]
    for b in range(S_MAX // S_BLK):
        s_io = b * S_BLK + lax.broadcasted_iota(i32, (TS, S_BLK), 1)
        w = jnp.where(s_io == slot[:, 0:1], g[:, 0:1], 0.0)
        for k in range(1, TOP_K):
            w = w + jnp.where(s_io == slot[:, k:k + 1], g[:, k:k + 1], 0.0)
        wgt[:, b * S_BLK:(b + 1) * S_BLK] = w.astype(bf16)

    y = x1_ref[...] + jnp.dot(wgt[...], ys_ref[...], preferred_element_type=f32)
    out = _rms(y, gf_ref[...])

    @pl.when(i < N_PROMPT_TILES)
    def _():
        yp_ref[...] = out

    @pl.when(i == N_PROMPT_TILES)
    def _():
        ysm_ref[...] = out


def _combine(slot_tm, gates_tm, x1, gf, ys):
    ptile = lambda i: (jnp.minimum(i, N_PROMPT_TILES - 1), 0)
    return pl.pallas_call(
        _combine_kernel,
        grid=(N_TILES,),
        in_specs=[
            pl.BlockSpec((TS, TOP_K), lambda i: (i, 0)),
            pl.BlockSpec((TS, TOP_K), lambda i: (i, 0)),
            pl.BlockSpec((TS, D_MODEL), lambda i: (i, 0)),
            pl.BlockSpec((1, D_MODEL), lambda i: (0, 0)),
            pl.BlockSpec((S_MAX, D_MODEL), lambda i: (i, 0)),
        ],
        out_specs=(
            pl.BlockSpec((TS, D_MODEL), ptile),
            pl.BlockSpec((N_SAMPLE, D_MODEL), lambda i: (0, 0)),
        ),
        out_shape=(
            jax.ShapeDtypeStruct((N_PROMPT, D_MODEL), f32),
            jax.ShapeDtypeStruct((N_SAMPLE, D_MODEL), f32),
        ),
        scratch_shapes=[pltpu.VMEM((TS, S_MAX), bf16)],
        compiler_params=pltpu.CompilerParams(
            dimension_semantics=("arbitrary",), vmem_limit_bytes=VMEM_LIMIT),
        name="combine",
    )(slot_tm, gates_tm, x1, gf, ys)


def _routing_tables(cnt):
    n = cnt.reshape(N_TILES, N_EXPERTS).astype(i32)
    c = (n + (PIECE - 1)) // PIECE * PIECE
    lo = jnp.cumsum(c, axis=1) - c
    tot = jnp.sum(c, axis=1)
    group = jnp.sum(c, axis=0)
    padded = (group + (TM - 1)) // TM * TM
    gend = jnp.cumsum(padded)
    gstart = gend - padded
    cs = gstart[None, :] + jnp.cumsum(c, axis=0) - c
    n_used = gend[-1] // TM
    tile_id = jnp.arange(N_ROW_TILES, dtype=i32)
    tile_expert = jnp.sum((tile_id[:, None] * TM >= gend[None, :]).astype(i32), axis=1)
    tile_expert = jnp.minimum(tile_expert, jnp.take(tile_expert, n_used - 1))
    first = (tile_id < n_used) & ((tile_id == 0) | (tile_expert != jnp.roll(tile_expert, 1)))
    order = jnp.maximum(jnp.cumsum(first.astype(i32)) - 1, 0)
    e_ar = jnp.arange(N_EXPERTS, dtype=i32)
    later = (e_ar[None, :] > e_ar[:, None]) & (padded[None, :] > 0)
    nxt_e = jnp.min(jnp.where(later, e_ar[None, :], N_EXPERTS), axis=1)
    nxt_e = jnp.where(nxt_e == N_EXPERTS, -1, nxt_e)
    nxt = jnp.sum(jnp.where(tile_expert[:, None] == e_ar[None, :], nxt_e[None, :], 0), axis=1)
    cs_f = cs.T.reshape(-1)
    c_f = c.T.reshape(-1)
    src_f = (jnp.arange(N_TILES, dtype=i32)[:, None] * S_MAX + lo).T.reshape(-1)
    piece_r = jnp.arange(N_ROW_TILES * PIECES, dtype=i32) * PIECE
    reached = (piece_r[:, None] >= cs_f[None, 1:]).astype(i32)

    def lookup(f):
        return f[0] + jnp.sum(reached * (f[1:] - f[:-1])[None, :], axis=1)
    off = piece_r - lookup(cs_f)
    piece_row = jnp.where(off < lookup(c_f), lookup(src_f) + off, -1)
    return dict(tot=tot, n_used=n_used[None], tile_expert=tile_expert, first=first.astype(i32),
                order=order, nxt=nxt, piece_row=piece_row)


def kernel(x_prompt, x_sample, state_conv, state_pool, norm_mix_g, w_in, conv_w, pool_w, pool_scale,
           w_out, norm_ffn_g, router_w, router_b, exp_w1, exp_b1, exp_w2, exp_b2, final_norm_g):
    l = 0
    xp = x_prompt.reshape(N_PROMPT, D_MODEL)
    xs = jnp.transpose(x_sample, (1, 0, 2)).reshape(N_SAMPLE, D_MODEL)
    sc_t = jnp.transpose(state_conv[l], (1, 0, 2))
    sp_t = jnp.transpose(state_pool[l], (1, 0, 2))
    rw_t = router_w[l].T
    rwh = rw_t.astype(bf16)
    rwl = (rw_t - rwh.astype(f32)).astype(bf16)

    (x1, srt, slot_t, gate_t, cnt, npc, npp, nsc, nsv) = _mixer(
        xp, xs, sc_t, sp_t, norm_mix_g[l][None, :], w_in[l].astype(bf16), conv_w[l],
        pool_w[l].astype(bf16), pool_scale[l][None, :], w_out[l].astype(bf16),
        norm_ffn_g[l][None, :], rwh, rwl, router_b[l][:, None])

    t = _routing_tables(cnt)
    ys = _experts(t["tile_expert"], t["n_used"], t["first"], t["order"], t["nxt"], t["piece_row"],
                  t["tot"], srt, exp_w1[l], exp_b1[l][:, None, :], exp_w2[l], exp_b2[l][:, None, :])
    y_p, y_s = _combine(slot_t.T, gate_t.T, x1, final_norm_g[None, :], ys)

    y_prompt = y_p.reshape(BATCH, SEQ, D_MODEL)
    y_sample = jnp.transpose(y_s.reshape(DEC_SEQ, DEC_BATCH, D_MODEL), (1, 0, 2))
    new_conv_prompt = npc[None, :, 8 - (CONV_W - 1):, :]
    new_pool_prompt = npp[None, :, HIST - POOL_BUF:, :]
    new_conv_sample = jnp.transpose(nsc, (1, 0, 2))[None]
    new_pool_sample = jnp.concatenate(
        [state_pool[l][:, DEC_SEQ:, :], jnp.transpose(nsv, (1, 0, 2))], axis=1)[None]
    return (y_prompt, y_sample, new_conv_prompt, new_pool_prompt, new_conv_sample, new_pool_sample)
```

```python
import jax
import jax.numpy as jnp
from jax import lax
from jax.experimental import pallas as pl
from jax.experimental.pallas import tpu as pltpu

D_MODEL = 1024
D_CONV = 512
D_POOL = 512
D_IN = 3 * D_CONV + D_POOL
CONV_W = 3
POOL_WINDOWS = (2, 4, 8, 16)
POOL_GROUP = 128
POOL_BUF = 15
N_EXPERTS = 32
TOP_K = 4
D_FF = 1024
SWIGLU_LIMIT = 7.0
SWIGLU_ALPHA = 1.702
EPS = 1e-5
PAST_LEN = 16384

BATCH, SEQ = 8, 2048
DEC_BATCH, DEC_SEQ = 128, 4
N_PROMPT = BATCH * SEQ
N_SAMPLE = DEC_BATCH * DEC_SEQ
N_TOK = N_PROMPT + N_SAMPLE

TS = 512
SEQ_TILES = SEQ // TS
N_PROMPT_TILES = N_PROMPT // TS
N_TILES = N_PROMPT_TILES + 1
HIST = 16
TM = 256
PIECE = 8
PIECES = TM // PIECE
S_BLK = 256
SORT_UNROLL = 3
S_MAX = (TS * TOP_K + N_EXPERTS * (PIECE - 1) + PIECE + S_BLK - 1) // S_BLK * S_BLK
ZERO_ROW = S_MAX - PIECE
N_ROW_TILES = (N_TOK * TOP_K + N_TILES * N_EXPERTS * (PIECE - 1) + N_EXPERTS * (TM - PIECE)) // TM
N_SORT_ROWS = N_TILES * S_MAX
DUMP_ROW = N_SORT_ROWS

VMEM_LIMIT = 56 * 1024 * 1024

f32 = jnp.float32
bf16 = jnp.bfloat16
i32 = jnp.int32


def _rms(x, g):
    return x * lax.rsqrt(jnp.mean(x * x, axis=-1, keepdims=True) + EPS) * g


def _pool_project(p, pw_ref, ps_ref):
    outs = []
    for g in range(len(POOL_WINDOWS)):
        sl = slice(g * POOL_GROUP, (g + 1) * POOL_GROUP)
        outs.append(jnp.dot(p[:, sl].astype(bf16), pw_ref[g], preferred_element_type=f32))
    return jnp.concatenate(outs, axis=1) * ps_ref[...]


def _route_and_sort(x, z, wout_ref, g2_ref, rwh_ref, rwl_ref, rb_ref,
                    x1_ref, srt_ref, slot_ref, gate_ref, cnt_ref):
    n = x.shape[0]
    x1 = x + jnp.dot(z.astype(bf16), wout_ref[...], preferred_element_type=f32)
    x1_ref[...] = x1
    xn = _rms(x1, g2_ref[...])
    xb = xn.astype(bf16)

    xlo = (xn - xb.astype(f32)).astype(bf16)
    nt = (((1,), (1,)), ((), ()))
    logits = (lax.dot_general(rwh_ref[...], xb, nt, preferred_element_type=f32)
              + lax.dot_general(rwl_ref[...], xb, nt, preferred_element_type=f32)
              + lax.dot_general(rwh_ref[...], xlo, nt, preferred_element_type=f32)
              + rb_ref[...])

    e_iota = lax.broadcasted_iota(i32, (N_EXPERTS, n), 0)
    work = logits
    hots, vals = [], []
    for _ in range(TOP_K):
        m = jnp.max(work, axis=0, keepdims=True)
        ik = jnp.min(jnp.where(work == m, e_iota, N_EXPERTS), axis=0, keepdims=True)
        hot = e_iota == ik
        work = jnp.where(hot, -jnp.inf, work)
        hots.append(hot); vals.append(m)
    exps = [jnp.exp(v - vals[0]) for v in vals]
    den = exps[0] + exps[1] + exps[2] + exps[3]
    gate_ref[...] = jnp.concatenate([e / den for e in exps], axis=0)

    multi = jnp.where(hots[0] | hots[1] | hots[2] | hots[3], 1.0, 0.0)
    r_io = lax.broadcasted_iota(i32, (n, n), 0)
    c_io = lax.broadcasted_iota(i32, (n, n), 1)
    tri = jnp.where(r_io < c_io, 1.0, 0.0).astype(bf16)
    before = jnp.dot(multi.astype(bf16), tri, preferred_element_type=f32)
    count = jnp.sum(multi, axis=1, keepdims=True)
    cnt_ref[...] = count
    pieces = jnp.floor((count + (PIECE - 1)) * (1.0 / PIECE))
    er = lax.broadcasted_iota(i32, (N_EXPERTS, N_EXPERTS), 0)
    ec = lax.broadcasted_iota(i32, (N_EXPERTS, N_EXPERTS), 1)
    lower = jnp.where(ec < er, 1.0, 0.0).astype(bf16)
    pieces_b = jnp.broadcast_to(pieces, (N_EXPERTS, 128)).astype(bf16)
    start = PIECE * jnp.dot(lower, pieces_b, preferred_element_type=f32)[:, 0:1]
    base = before + start
    slot = jnp.concatenate(
        [jnp.sum(jnp.where(h, base, 0.0), axis=0, keepdims=True) for h in hots], axis=0).astype(i32)
    slot_ref[...] = slot

    def sort_blocks(b2, c):
        for half in range(SORT_UNROLL):
            row0 = pl.multiple_of((SORT_UNROLL * b2 + half) * S_BLK, S_BLK)
            s_io = row0 + lax.broadcasted_iota(i32, (S_BLK, n), 0)
            hit = s_io == slot[0:1, :]
            for k in range(1, TOP_K):
                hit = hit | (s_io == slot[k:k + 1, :])
            sel = jnp.where(hit, 1.0, 0.0).astype(bf16)
            srt_ref[pl.ds(row0, S_BLK), :] = jnp.dot(
                sel, xb, preferred_element_type=f32).astype(bf16)
        return c
    lax.fori_loop(0, S_MAX // (SORT_UNROLL * S_BLK), sort_blocks, 0)


def _mixer_kernel(xp_in_ref, xs_in_ref, sc_ref, sp_ref, g1_ref, win_ref, cw_ref, pw_ref, ps_ref,
                  wout_ref, g2_ref, rwh_ref, rwl_ref, rb_ref,
                  x1_ref, srt_ref, slot_ref, gate_ref, cnt_ref,
                  npc_ref, npp_ref, nsc_ref, nsv_ref,
                  ubuf, vbuf, xcur, zcur):
    i = pl.program_id(0)

    @pl.when(i < N_PROMPT_TILES)
    def _():
        s = i % SEQ_TILES

        @pl.when(s == 0)
        def _():
            ubuf[0:HIST, :] = jnp.zeros((HIST, D_CONV), f32)
            vbuf[0:HIST, :] = jnp.zeros((HIST, D_POOL), f32)

        x = xp_in_ref[...]
        xcur[...] = x
        xn = _rms(x, g1_ref[...]).astype(bf16)
        proj = jnp.dot(xn, win_ref[...], preferred_element_type=f32)
        bg = proj[:, :D_CONV]
        u = proj[:, D_CONV:2 * D_CONV] * proj[:, 2 * D_CONV:3 * D_CONV]
        v = proj[:, 3 * D_CONV:]
        ubuf[HIST:HIST + TS, :] = u
        vbuf[HIST:HIST + TS, :] = v

        y = (ubuf[HIST - 2:HIST - 2 + TS, :] * cw_ref[0:1, :]
             + ubuf[HIST - 1:HIST - 1 + TS, :] * cw_ref[1:2, :]
             + u * cw_ref[2:3, :])
        zcur[:, :D_CONV] = bg * y

        pos = s * TS + lax.broadcasted_iota(i32, (TS, 1), 0)
        ps = []
        for g, w in enumerate(POOL_WINDOWS):
            sl = slice(g * POOL_GROUP, (g + 1) * POOL_GROUP)
            acc = vbuf[:, sl]
            for step in range(g + 1):
                acc = acc + pltpu.roll(acc, 1 << step, axis=0)
            cnt = jnp.minimum(w, pos + 1).astype(f32)
            ps.append(acc[HIST:, :] * (1.0 / cnt) - v[:, sl])
        zcur[:, D_CONV:] = _pool_project(jnp.concatenate(ps, axis=1), pw_ref, ps_ref)

        ubuf[HIST - 8:HIST, :] = ubuf[TS + HIST - 8:TS + HIST, :]
        vbuf[0:HIST, :] = vbuf[TS:TS + HIST, :]

        @pl.when(s == SEQ_TILES - 1)
        def _():
            npc_ref[0] = ubuf[HIST - 8:HIST, :]
            npp_ref[0] = vbuf[0:HIST, :]

    @pl.when(i == N_PROMPT_TILES)
    def _():
        x = xs_in_ref[...]
        xcur[...] = x
        xn = _rms(x, g1_ref[...]).astype(bf16)
        proj = jnp.dot(xn, win_ref[...], preferred_element_type=f32)
        nb = DEC_BATCH
        rows = lambda a, t: a[t * nb:(t + 1) * nb]
        bg = proj[:, :D_CONV]
        u = proj[:, D_CONV:2 * D_CONV] * proj[:, 2 * D_CONV:3 * D_CONV]
        v = proj[:, 3 * D_CONV:]
        up = [sc_ref[j] for j in range(CONV_W - 1)] + [rows(u, t) for t in range(DEC_SEQ)]
        vp = [sp_ref[j] for j in range(POOL_BUF)] + [rows(v, t) for t in range(DEC_SEQ)]
        zc, ps = [], []
        for t in range(DEC_SEQ):
            y = up[t] * cw_ref[0:1, :]
            for k in range(1, CONV_W):
                y = y + up[t + k] * cw_ref[k:k + 1, :]
            zc.append(rows(bg, t) * y)
            pg = []
            for g, w in enumerate(POOL_WINDOWS):
                sl = slice(g * POOL_GROUP, (g + 1) * POOL_GROUP)
                lo = t + POOL_BUF - w + 1
                acc = vp[lo][:, sl]
                for j in range(lo + 1, t + POOL_BUF + 1):
                    acc = acc + vp[j][:, sl]
                cnt = float(min(w, PAST_LEN + t + 1))
                pg.append(acc * (1.0 / cnt) - vp[t + POOL_BUF][:, sl])
            ps.append(jnp.concatenate(pg, axis=1))
        zcur[:, :D_CONV] = jnp.concatenate(zc, axis=0)
        zcur[:, D_CONV:] = _pool_project(jnp.concatenate(ps, axis=0), pw_ref, ps_ref)
        for j in range(CONV_W - 1):
            nsc_ref[j] = up[DEC_SEQ + j]
        for t in range(DEC_SEQ):
            nsv_ref[t] = rows(v, t)

    _route_and_sort(xcur[...], zcur[...], wout_ref, g2_ref, rwh_ref, rwl_ref, rb_ref,
                    x1_ref, srt_ref, slot_ref, gate_ref, cnt_ref)


def _mixer(xp, xs, sc_t, sp_t, g1, win, cw, pw, ps, wout, g2, rwh, rwl, rb):
    const = lambda shape: pl.BlockSpec(shape, lambda i: (0,) * len(shape),
                                       pipeline_mode=pl.Buffered(1))
    ptile = lambda i: jnp.minimum(i, N_PROMPT_TILES - 1)
    pbatch = lambda i: jnp.minimum(i, N_PROMPT_TILES - 1) // SEQ_TILES
    out_shape = (
        jax.ShapeDtypeStruct((N_TOK, D_MODEL), f32),
        jax.ShapeDtypeStruct((N_SORT_ROWS, D_MODEL), bf16),
        jax.ShapeDtypeStruct((TOP_K, N_TOK), i32),
        jax.ShapeDtypeStruct((TOP_K, N_TOK), f32),
        jax.ShapeDtypeStruct((N_TILES * N_EXPERTS, 1), f32),
        jax.ShapeDtypeStruct((BATCH, 8, D_CONV), f32),
        jax.ShapeDtypeStruct((BATCH, HIST, D_POOL), f32),
        jax.ShapeDtypeStruct((CONV_W - 1, DEC_BATCH, D_CONV), f32),
        jax.ShapeDtypeStruct((DEC_SEQ, DEC_BATCH, D_POOL), f32),
    )
    out_specs = (
        pl.BlockSpec((TS, D_MODEL), lambda i: (i, 0)),
        pl.BlockSpec((S_MAX, D_MODEL), lambda i: (i, 0)),
        pl.BlockSpec((TOP_K, TS), lambda i: (0, i)),
        pl.BlockSpec((TOP_K, TS), lambda i: (0, i)),
        pl.BlockSpec((N_EXPERTS, 1), lambda i: (i, 0)),
        pl.BlockSpec((1, 8, D_CONV), lambda i: (pbatch(i), 0, 0)),
        pl.BlockSpec((1, HIST, D_POOL), lambda i: (pbatch(i), 0, 0)),
        pl.BlockSpec((CONV_W - 1, DEC_BATCH, D_CONV), lambda i: (0, 0, 0)),
        pl.BlockSpec((DEC_SEQ, DEC_BATCH, D_POOL), lambda i: (0, 0, 0)),
    )
    in_specs = [
        pl.BlockSpec((TS, D_MODEL), lambda i: (ptile(i), 0)),
        const((N_SAMPLE, D_MODEL)),
        const((CONV_W - 1, DEC_BATCH, D_CONV)),
        const((POOL_BUF, DEC_BATCH, D_POOL)),
        const((1, D_MODEL)),
        const((D_MODEL, D_IN)),
        const((CONV_W, D_CONV)),
        const((len(POOL_WINDOWS), POOL_GROUP, POOL_GROUP)),
        const((1, D_POOL)),
        const((D_MODEL, D_MODEL)),
        const((1, D_MODEL)),
        const((N_EXPERTS, D_MODEL)),
        const((N_EXPERTS, D_MODEL)),
        const((N_EXPERTS, 1)),
    ]
    return pl.pallas_call(
        _mixer_kernel,
        grid=(N_TILES,),
        in_specs=in_specs,
        out_specs=out_specs,
        out_shape=out_shape,
        scratch_shapes=[
            pltpu.VMEM((TS + HIST, D_CONV), f32),
            pltpu.VMEM((TS + HIST, D_POOL), f32),
            pltpu.VMEM((TS, D_MODEL), f32),
            pltpu.VMEM((TS, D_MODEL), f32),
        ],
        compiler_params=pltpu.CompilerParams(
            dimension_semantics=("arbitrary",), vmem_limit_bytes=VMEM_LIMIT),
        name="mixer",
    )(xp, xs, sc_t, sp_t, g1, win, cw, pw, ps, wout, g2, rwh, rwl, rb)


def _expert_kernel(te_ref, nu_ref, first_ref, ord_ref, nxt_ref, pt_ref, tot_ref,
                   b1_ref, b2_ref, srt_hbm, w1_hbm, w2_hbm, ys_hbm,
                   xbuf, ybuf, zpiece, w1f, w2f, w1b, w2b, sem_w, sem_x, sem_y, sem_z):
    i = pl.program_id(0)
    par = i % 2
    n_used = nu_ref[0]
    live = i < n_used

    def weight_copies(e, slot):
        return (pltpu.make_async_copy(w1_hbm.at[e], w1f.at[slot], sem_w.at[slot]),
                pltpu.make_async_copy(w2_hbm.at[e], w2f.at[slot], sem_w.at[2 + slot]))

    def start_x(tile, b):
        for p in range(PIECES):
            src = pt_ref[tile * PIECES + p]
            src = pl.multiple_of(jnp.where(src < 0, ZERO_ROW, src), PIECE)
            pltpu.make_async_copy(srt_hbm.at[pl.ds(src, PIECE)],
                                  xbuf.at[b, pl.ds(p * PIECE, PIECE)], sem_x.at[b]).start()

    def start_y(tile, b, all_dump=False):
        for p in range(PIECES):
            dst = pt_ref[tile * PIECES + p]
            dump = (dst < 0) | all_dump
            dst = pl.multiple_of(jnp.where(dump, DUMP_ROW + b * TM + p * PIECE, dst), PIECE)
            pltpu.make_async_copy(ybuf.at[b, pl.ds(p * PIECE, PIECE)],
                                  ys_hbm.at[pl.ds(dst, PIECE)], sem_y.at[b]).start()

    def wait_x(b):
        pltpu.make_async_copy(srt_hbm.at[pl.ds(0, TM)], xbuf.at[b], sem_x.at[b]).wait()

    def wait_y(b):
        pltpu.make_async_copy(ybuf.at[b], ys_hbm.at[pl.ds(0, TM)], sem_y.at[b]).wait()

    @pl.when(i == 0)
    def _():
        zpiece[...] = jnp.zeros_like(zpiece)

        def fill_tile(t, c):
            def dst(p):
                row = pl.multiple_of(t * S_MAX + tot_ref[t] + p * PIECE, PIECE)
                return ys_hbm.at[pl.ds(row, PIECE)]

            def start(p, c2):
                pltpu.make_async_copy(zpiece, dst(p), sem_z.at[0]).start()
                return c2

            def wait(p, c2):
                pltpu.make_async_copy(zpiece, dst(p), sem_z.at[0]).wait()
                return c2
            n_fill = (S_MAX - tot_ref[t]) // PIECE
            lax.fori_loop(0, n_fill, start, 0)
            return lax.fori_loop(0, n_fill, wait, c)
        lax.fori_loop(0, N_TILES, fill_tile, 0)
        start_x(0, 0)
        start_x(jnp.minimum(1, n_used - 1), 1)
        ybuf[...] = jnp.zeros_like(ybuf)
        for b in range(2):
            pltpu.make_async_copy(ybuf.at[b], ys_hbm.at[pl.ds(DUMP_ROW + b * TM, TM)],
                                  sem_y.at[b]).start()

    @pl.when(live)
    def _():
        xslot = i % 3
        wait_x(xslot)

        @pl.when(first_ref[i] == 1)
        def _():
            slot = ord_ref[i] % 2

            @pl.when(i == 0)
            def _():
                for cp in weight_copies(te_ref[0], 0):
                    cp.start()
            for cp in weight_copies(te_ref[i], slot):
                cp.wait()

            @pl.when(nxt_ref[i] >= 0)
            def _():
                for cp in weight_copies(nxt_ref[i], 1 - slot):
                    cp.start()
            w1b[...] = w1f[slot].astype(bf16)
            w2b[...] = w2f[slot].astype(bf16)

        wait_y(xslot)
        gu =jnp.dot(xbuf[xslot], w1b[...], preferred_element_type=f32) + b1_ref[0]
        gate = jnp.minimum(gu[:, :D_FF], SWIGLU_LIMIT)
        lin = jnp.clip(gu[:, D_FF:], -SWIGLU_LIMIT, SWIGLU_LIMIT)
        glu = gate * jax.nn.sigmoid(SWIGLU_ALPHA * gate)
        h = (glu * (lin + 1.0)).astype(bf16)
        start_x(jnp.minimum(i + 2, n_used - 1), (i + 2) % 3)
        start_y(jnp.maximum(i - 1, 0), (i + 2) % 3, all_dump=i == 0)
        out = jnp.dot(h, w2b[...], preferred_element_type=f32) + b2_ref[0]
        ybuf[xslot] = out.astype(bf16)

        @pl.when(i == n_used - 1)
        def _():
            start_y(i, xslot)
            for b in range(3):
                wait_y(b)
            wait_x((i + 1) % 3)
            wait_x((i + 2) % 3)


def _experts(tile_expert, n_used, first, order, nxt, piece_row, tot, srt, w1, b1, w2, b2):
    wsel = lambda i, te, *_: (te[i], 0, 0)
    grid_spec = pltpu.PrefetchScalarGridSpec(
        num_scalar_prefetch=7,
        grid=(N_ROW_TILES,),
        in_specs=[
            pl.BlockSpec((1, 1, 2 * D_FF), wsel),
            pl.BlockSpec((1, 1, D_MODEL), wsel),
            pl.BlockSpec(memory_space=pl.ANY),
            pl.BlockSpec(memory_space=pl.ANY),
            pl.BlockSpec(memory_space=pl.ANY),
        ],
        out_specs=pl.BlockSpec(memory_space=pl.ANY),
        scratch_shapes=[
            pltpu.VMEM((3, TM, D_MODEL), bf16), pltpu.VMEM((3, TM, D_MODEL), bf16),
            pltpu.VMEM((PIECE, D_MODEL), bf16),
            pltpu.VMEM((2, D_MODEL, 2 * D_FF), f32), pltpu.VMEM((2, D_FF, D_MODEL), f32),
            pltpu.VMEM((D_MODEL, 2 * D_FF), bf16), pltpu.VMEM((D_FF, D_MODEL), bf16),
            pltpu.SemaphoreType.DMA((4,)), pltpu.SemaphoreType.DMA((3,)),
            pltpu.SemaphoreType.DMA((3,)), pltpu.SemaphoreType.DMA((1,)),
        ],
    )
    return pl.pallas_call(
        _expert_kernel,
        grid_spec=grid_spec,
        out_shape=jax.ShapeDtypeStruct((N_SORT_ROWS + 3 * TM, D_MODEL), bf16),
        compiler_params=pltpu.CompilerParams(
            dimension_semantics=("arbitrary",), vmem_limit_bytes=VMEM_LIMIT),
        name="experts",
    )(tile_expert, n_used, first, order, nxt, piece_row, tot, b1, b2, srt, w1, w2)


def _combine_kernel(slot_ref, g_ref, x1_ref, gf_ref, ys_ref, yp_ref, ysm_ref, wgt):
    i = pl.program_id(0)

    slot = slot_ref[...]
    g = g_ref[...]
    for b in range(S_MAX // S_BLK):
        s_io = b * S_BLK + lax.broadcasted_iota(i32, (TS, S_BLK), 1)
        w = jnp.where(s_io == slot[:, 0:1], g[:, 0:1], 0.0)
        for k in range(1, TOP_K):
            w = w + jnp.where(s_io == slot[:, k:k + 1], g[:, k:k + 1], 0.0)
        wgt[:, b * S_BLK:(b + 1) * S_BLK] = w.astype(bf16)

    y = x1_ref[...] + jnp.dot(wgt[...], ys_ref[...], preferred_element_type=f32)
    out = _rms(y, gf_ref[...])

    @pl.when(i < N_PROMPT_TILES)
    def _():
        yp_ref[...] = out

    @pl.when(i == N_PROMPT_TILES)
    def _():
        ysm_ref[...] = out


def _combine(slot_tm, gates_tm, x1, gf, ys):
    ptile = lambda i: (jnp.minimum(i, N_PROMPT_TILES - 1), 0)
    return pl.pallas_call(
        _combine_kernel,
        grid=(N_TILES,),
        in_specs=[
            pl.BlockSpec((TS, TOP_K), lambda i: (i, 0)),
            pl.BlockSpec((TS, TOP_K), lambda i: (i, 0)),
            pl.BlockSpec((TS, D_MODEL), lambda i: (i, 0)),
            pl.BlockSpec((1, D_MODEL), lambda i: (0, 0)),
            pl.BlockSpec((S_MAX, D_MODEL), lambda i: (i, 0)),
        ],
        out_specs=(
            pl.BlockSpec((TS, D_MODEL), ptile),
            pl.BlockSpec((N_SAMPLE, D_MODEL), lambda i: (0, 0)),
        ),
        out_shape=(
            jax.ShapeDtypeStruct((N_PROMPT, D_MODEL), f32),
            jax.ShapeDtypeStruct((N_SAMPLE, D_MODEL), f32),
        ),
        scratch_shapes=[pltpu.VMEM((TS, S_MAX), bf16)],
        compiler_params=pltpu.CompilerParams(
            dimension_semantics=("arbitrary",), vmem_limit_bytes=VMEM_LIMIT),
        name="combine",
    )(slot_tm, gates_tm, x1, gf, ys)


def _routing_tables(cnt):
    n = cnt.reshape(N_TILES, N_EXPERTS).astype(i32)
    c = (n + (PIECE - 1)) // PIECE * PIECE
    lo = jnp.cumsum(c, axis=1) - c
    tot = jnp.sum(c, axis=1)
    group = jnp.sum(c, axis=0)
    padded = (group + (TM - 1)) // TM * TM
    gend = jnp.cumsum(padded)
    gstart = gend - padded
    cs = gstart[None, :] + jnp.cumsum(c, axis=0) - c
    n_used = gend[-1] // TM
    tile_id = jnp.arange(N_ROW_TILES, dtype=i32)
    tile_expert = jnp.sum((tile_id[:, None] * TM >= gend[None, :]).astype(i32), axis=1)
    tile_expert = jnp.minimum(tile_expert, jnp.take(tile_expert, n_used - 1))
    first = (tile_id < n_used) & ((tile_id == 0) | (tile_expert != jnp.roll(tile_expert, 1)))
    order = jnp.maximum(jnp.cumsum(first.astype(i32)) - 1, 0)
    e_ar = jnp.arange(N_EXPERTS, dtype=i32)
    later = (e_ar[None, :] > e_ar[:, None]) & (padded[None, :] > 0)
    nxt_e = jnp.min(jnp.where(later, e_ar[None, :], N_EXPERTS), axis=1)
    nxt_e = jnp.where(nxt_e == N_EXPERTS, -1, nxt_e)
    nxt = jnp.sum(jnp.where(tile_expert[:, None] == e_ar[None, :], nxt_e[None, :], 0), axis=1)
    cs_f = cs.T.reshape(-1)
    c_f = c.T.reshape(-1)
    src_f = (jnp.arange(N_TILES, dtype=i32)[:, None] * S_MAX + lo).T.reshape(-1)
    piece_r = jnp.arange(N_ROW_TILES * PIECES, dtype=i32) * PIECE
    reached = (piece_r[:, None] >= cs_f[None, 1:]).astype(i32)

    def lookup(f):
        return f[0] + jnp.sum(reached * (f[1:] - f[:-1])[None, :], axis=1)
    off = piece_r - lookup(cs_f)
    piece_row = jnp.where(off < lookup(c_f), lookup(src_f) + off, -1)
    return dict(tot=tot, n_used=n_used[None], tile_expert=tile_expert, first=first.astype(i32),
                order=order, nxt=nxt, piece_row=piece_row)


def kernel(x_prompt, x_sample, state_conv, state_pool, norm_mix_g, w_in, conv_w, pool_w, pool_scale,
           w_out, norm_ffn_g, router_w, router_b, exp_w1, exp_b1, exp_w2, exp_b2, final_norm_g):
    l = 0
    xp = x_prompt.reshape(N_PROMPT, D_MODEL)
    xs = jnp.transpose(x_sample, (1, 0, 2)).reshape(N_SAMPLE, D_MODEL)
    sc_t = jnp.transpose(state_conv[l], (1, 0, 2))
    sp_t = jnp.transpose(state_pool[l], (1, 0, 2))
    rw_t = router_w[l].T
    rwh = rw_t.astype(bf16)
    rwl = (rw_t - rwh.astype(f32)).astype(bf16)

    (x1, srt, slot_t, gate_t, cnt, npc, npp, nsc, nsv) = _mixer(
        xp, xs, sc_t, sp_t, norm_mix_g[l][None, :], w_in[l].astype(bf16), conv_w[l],
        pool_w[l].astype(bf16), pool_scale[l][None, :], w_out[l].astype(bf16),
        norm_ffn_g[l][None, :], rwh, rwl, router_b[l][:, None])

    t = _routing_tables(cnt)
    ys = _experts(t["tile_expert"], t["n_used"], t["first"], t["order"], t["nxt"], t["piece_row"],
                  t["tot"], srt, exp_w1[l], exp_b1[l][:, None, :], exp_w2[l], exp_b2[l][:, None, :])
    y_p, y_s = _combine(slot_t.T, gate_t.T, x1, final_norm_g[None, :], ys)

    y_prompt = y_p.reshape(BATCH, SEQ, D_MODEL)
    y_sample = jnp.transpose(y_s.reshape(DEC_SEQ, DEC_BATCH, D_MODEL), (1, 0, 2))
    new_conv_prompt = npc[None, :, 8 - (CONV_W - 1):, :]
    new_pool_prompt = npp[None, :, HIST - POOL_BUF:, :]
    new_conv_sample = jnp.transpose(nsc, (1, 0, 2))[None]
    new_pool_sample = jnp.concatenate(
        [state_pool[l][:, DEC_SEQ:, :], jnp.transpose(nsv, (1, 0, 2))], axis=1)[None]
    return (y_prompt, y_sample, new_conv_prompt, new_pool_prompt, new_conv_sample, new_pool_sample)
```

```python
import jax
import jax.numpy as jnp
from jax import lax
from jax.experimental import pallas as pl
from jax.experimental.pallas import tpu as pltpu

D_MODEL = 1024
D_CONV = 512
D_POOL = 512
D_IN = 3 * D_CONV + D_POOL
CONV_W = 3
POOL_WINDOWS = (2, 4, 8, 16)
POOL_GROUP = 128
POOL_BUF = 15
N_EXPERTS = 32
TOP_K = 4
D_FF = 1024
SWIGLU_LIMIT = 7.0
SWIGLU_ALPHA = 1.702
EPS = 1e-5
PAST_LEN = 16384

BATCH, SEQ = 8, 2048
DEC_BATCH, DEC_SEQ = 128, 4
N_PROMPT = BATCH * SEQ
N_SAMPLE = DEC_BATCH * DEC_SEQ
N_TOK = N_PROMPT + N_SAMPLE

TS = 512
SEQ_TILES = SEQ // TS
N_PROMPT_TILES = N_PROMPT // TS
N_TILES = N_PROMPT_TILES + 1
HIST = 16
TM = 256
PIECE = 8
PIECES = TM // PIECE
S_BLK = 256
SORT_UNROLL = 3
S_MAX = (TS * TOP_K + N_EXPERTS * (PIECE - 1) + PIECE + S_BLK - 1) // S_BLK * S_BLK
ZERO_ROW = S_MAX - PIECE
N_ROW_TILES = (N_TOK * TOP_K + N_TILES * N_EXPERTS * (PIECE - 1) + N_EXPERTS * (TM - PIECE)) // TM
N_SORT_ROWS = N_TILES * S_MAX
DUMP_ROW = N_SORT_ROWS

VMEM_LIMIT = 56 * 1024 * 1024

f32 = jnp.float32
bf16 = jnp.bfloat16
i32 = jnp.int32


def _rms(x, g):
    return x * lax.rsqrt(jnp.mean(x * x, axis=-1, keepdims=True) + EPS) * g


def _pool_project(p, pw_ref, ps_ref):
    outs = []
    for g in range(len(POOL_WINDOWS)):
        sl = slice(g * POOL_GROUP, (g + 1) * POOL_GROUP)
        outs.append(jnp.dot(p[:, sl].astype(bf16), pw_ref[g], preferred_element_type=f32))
    return jnp.concatenate(outs, axis=1) * ps_ref[...]


def _route_and_sort(x, z, wout_ref, g2_ref, rwh_ref, rwl_ref, rb_ref, tri_ref,
                    x1_ref, srt_ref, slot_ref, gate_ref, cnt_ref):
    n = x.shape[0]
    x1 = x + jnp.dot(z.astype(bf16), wout_ref[...], preferred_element_type=f32)
    x1_ref[...] = x1
    xn = _rms(x1, g2_ref[...])
    xb = xn.astype(bf16)

    xlo = (xn - xb.astype(f32)).astype(bf16)
    nt = (((1,), (1,)), ((), ()))
    logits = (lax.dot_general(rwh_ref[...], xb, nt, preferred_element_type=f32)
              + lax.dot_general(rwl_ref[...], xb, nt, preferred_element_type=f32)
              + lax.dot_general(rwh_ref[...], xlo, nt, preferred_element_type=f32)
              + rb_ref[...])

    e_iota = lax.broadcasted_iota(i32, (N_EXPERTS, n), 0)
    work = logits
    hots, vals = [], []
    for _ in range(TOP_K):
        m = jnp.max(work, axis=0, keepdims=True)
        ik = jnp.min(jnp.where(work == m, e_iota, N_EXPERTS), axis=0, keepdims=True)
        hot = e_iota == ik
        work = jnp.where(hot, -jnp.inf, work)
        hots.append(hot); vals.append(m)
    exps = [jnp.exp(v - vals[0]) for v in vals]
    den = exps[0] + exps[1] + exps[2] + exps[3]
    gate_ref[...] = jnp.concatenate([e / den for e in exps], axis=0)

    multi = jnp.where(hots[0] | hots[1] | hots[2] | hots[3], 1.0, 0.0)
    before = jnp.dot(multi.astype(bf16), tri_ref[...], preferred_element_type=f32)
    count = jnp.sum(multi, axis=1, keepdims=True)
    cnt_ref[...] = count
    pieces = jnp.floor((count + (PIECE - 1)) * (1.0 / PIECE))
    er = lax.broadcasted_iota(i32, (N_EXPERTS, N_EXPERTS), 0)
    ec = lax.broadcasted_iota(i32, (N_EXPERTS, N_EXPERTS), 1)
    lower = jnp.where(ec < er, 1.0, 0.0).astype(bf16)
    pieces_b = jnp.broadcast_to(pieces, (N_EXPERTS, 128)).astype(bf16)
    start = PIECE * jnp.dot(lower, pieces_b, preferred_element_type=f32)[:, 0:1]
    base = before + start
    slot = jnp.concatenate(
        [jnp.sum(jnp.where(h, base, 0.0), axis=0, keepdims=True) for h in hots], axis=0).astype(i32)
    slot_ref[...] = slot

    def sort_blocks(b2, c):
        for half in range(SORT_UNROLL):
            row0 = pl.multiple_of((SORT_UNROLL * b2 + half) * S_BLK, S_BLK)
            s_io = row0 + lax.broadcasted_iota(i32, (S_BLK, n), 0)
            hit = s_io == slot[0:1, :]
            for k in range(1, TOP_K):
                hit = hit | (s_io == slot[k:k + 1, :])
            sel = jnp.where(hit, 1.0, 0.0).astype(bf16)
            srt_ref[pl.ds(row0, S_BLK), :] = jnp.dot(
                sel, xb, preferred_element_type=f32).astype(bf16)
        return c
    lax.fori_loop(0, S_MAX // (SORT_UNROLL * S_BLK), sort_blocks, 0)


def _mixer_kernel(xp_in_ref, xs_in_ref, sc_ref, sp_ref, g1_ref, win_ref, cw_ref, pw_ref, ps_ref,
                  wout_ref, g2_ref, rwh_ref, rwl_ref, rb_ref, tri_ref,
                  x1_ref, srt_ref, slot_ref, gate_ref, cnt_ref,
                  npc_ref, npp_ref, nsc_ref, nsv_ref,
                  ubuf, vbuf, xcur, zcur):
    i = pl.program_id(0)

    @pl.when(i < N_PROMPT_TILES)
    def _():
        s = i % SEQ_TILES

        @pl.when(s == 0)
        def _():
            ubuf[0:HIST, :] = jnp.zeros((HIST, D_CONV), f32)
            vbuf[0:HIST, :] = jnp.zeros((HIST, D_POOL), f32)

        x = xp_in_ref[...]
        xcur[...] = x
        xn = _rms(x, g1_ref[...]).astype(bf16)
        proj = jnp.dot(xn, win_ref[...], preferred_element_type=f32)
        bg = proj[:, :D_CONV]
        u = proj[:, D_CONV:2 * D_CONV] * proj[:, 2 * D_CONV:3 * D_CONV]
        v = proj[:, 3 * D_CONV:]
        ubuf[HIST:HIST + TS, :] = u
        vbuf[HIST:HIST + TS, :] = v

        y = (ubuf[HIST - 2:HIST - 2 + TS, :] * cw_ref[0:1, :]
             + ubuf[HIST - 1:HIST - 1 + TS, :] * cw_ref[1:2, :]
             + u * cw_ref[2:3, :])
        zcur[:, :D_CONV] = bg * y

        pos = s * TS + lax.broadcasted_iota(i32, (TS, 1), 0)
        ps = []
        for g, w in enumerate(POOL_WINDOWS):
            sl = slice(g * POOL_GROUP, (g + 1) * POOL_GROUP)
            acc = vbuf[:, sl]
            for step in range(g + 1):
                acc = acc + pltpu.roll(acc, 1 << step, axis=0)
            cnt = jnp.minimum(w, pos + 1).astype(f32)
            ps.append(acc[HIST:, :] * (1.0 / cnt) - v[:, sl])
        zcur[:, D_CONV:] = _pool_project(jnp.concatenate(ps, axis=1), pw_ref, ps_ref)

        ubuf[HIST - 8:HIST, :] = ubuf[TS + HIST - 8:TS + HIST, :]
        vbuf[0:HIST, :] = vbuf[TS:TS + HIST, :]

        @pl.when(s == SEQ_TILES - 1)
        def _():
            npc_ref[0] = ubuf[HIST - 8:HIST, :]
            npp_ref[0] = vbuf[0:HIST, :]

    @pl.when(i == N_PROMPT_TILES)
    def _():
        x = xs_in_ref[...]
        xcur[...] = x
        xn = _rms(x, g1_ref[...]).astype(bf16)
        proj = jnp.dot(xn, win_ref[...], preferred_element_type=f32)
        nb = DEC_BATCH
        rows = lambda a, t: a[t * nb:(t + 1) * nb]
        bg = proj[:, :D_CONV]
        u = proj[:, D_CONV:2 * D_CONV] * proj[:, 2 * D_CONV:3 * D_CONV]
        v = proj[:, 3 * D_CONV:]
        up = [sc_ref[j] for j in range(CONV_W - 1)] + [rows(u, t) for t in range(DEC_SEQ)]
        vp = [sp_ref[j] for j in range(POOL_BUF)] + [rows(v, t) for t in range(DEC_SEQ)]
        zc, ps = [], []
        for t in range(DEC_SEQ):
            y = up[t] * cw_ref[0:1, :]
            for k in range(1, CONV_W):
                y = y + up[t + k] * cw_ref[k:k + 1, :]
            zc.append(rows(bg, t) * y)
            pg = []
            for g, w in enumerate(POOL_WINDOWS):
                sl = slice(g * POOL_GROUP, (g + 1) * POOL_GROUP)
                lo = t + POOL_BUF - w + 1
                acc = vp[lo][:, sl]
                for j in range(lo + 1, t + POOL_BUF + 1):
                    acc = acc + vp[j][:, sl]
                cnt = float(min(w, PAST_LEN + t + 1))
                pg.append(acc * (1.0 / cnt) - vp[t + POOL_BUF][:, sl])
            ps.append(jnp.concatenate(pg, axis=1))
        zcur[:, :D_CONV] = jnp.concatenate(zc, axis=0)
        zcur[:, D_CONV:] = _pool_project(jnp.concatenate(ps, axis=0), pw_ref, ps_ref)
        for j in range(CONV_W - 1):
            nsc_ref[j] = up[DEC_SEQ + j]
        for t in range(DEC_SEQ):
            nsv_ref[t] = rows(v, t)

    _route_and_sort(xcur[...], zcur[...], wout_ref, g2_ref, rwh_ref, rwl_ref, rb_ref, tri_ref,
                    x1_ref, srt_ref, slot_ref, gate_ref, cnt_ref)


def _mixer(xp, xs, sc_t, sp_t, g1, win, cw, pw, ps, wout, g2, rwh, rwl, rb, tri):
    const = lambda shape: pl.BlockSpec(shape, lambda i: (0,) * len(shape),
                                       pipeline_mode=pl.Buffered(1))
    ptile = lambda i: jnp.minimum(i, N_PROMPT_TILES - 1)
    pbatch = lambda i: jnp.minimum(i, N_PROMPT_TILES - 1) // SEQ_TILES
    out_shape = (
        jax.ShapeDtypeStruct((N_TOK, D_MODEL), f32),
        jax.ShapeDtypeStruct((N_SORT_ROWS, D_MODEL), bf16),
        jax.ShapeDtypeStruct((TOP_K, N_TOK), i32),
        jax.ShapeDtypeStruct((TOP_K, N_TOK), f32),
        jax.ShapeDtypeStruct((N_TILES * N_EXPERTS, 1), f32),
        jax.ShapeDtypeStruct((BATCH, 8, D_CONV), f32),
        jax.ShapeDtypeStruct((BATCH, HIST, D_POOL), f32),
        jax.ShapeDtypeStruct((CONV_W - 1, DEC_BATCH, D_CONV), f32),
        jax.ShapeDtypeStruct((DEC_SEQ, DEC_BATCH, D_POOL), f32),
    )
    out_specs = (
        pl.BlockSpec((TS, D_MODEL), lambda i: (i, 0)),
        pl.BlockSpec((S_MAX, D_MODEL), lambda i: (i, 0)),
        pl.BlockSpec((TOP_K, TS), lambda i: (0, i)),
        pl.BlockSpec((TOP_K, TS), lambda i: (0, i)),
        pl.BlockSpec((N_EXPERTS, 1), lambda i: (i, 0)),
        pl.BlockSpec((1, 8, D_CONV), lambda i: (pbatch(i), 0, 0)),
        pl.BlockSpec((1, HIST, D_POOL), lambda i: (pbatch(i), 0, 0)),
        pl.BlockSpec((CONV_W - 1, DEC_BATCH, D_CONV), lambda i: (0, 0, 0)),
        pl.BlockSpec((DEC_SEQ, DEC_BATCH, D_POOL), lambda i: (0, 0, 0)),
    )
    in_specs = [
        pl.BlockSpec((TS, D_MODEL), lambda i: (ptile(i), 0)),
        const((N_SAMPLE, D_MODEL)),
        const((CONV_W - 1, DEC_BATCH, D_CONV)),
        const((POOL_BUF, DEC_BATCH, D_POOL)),
        const((1, D_MODEL)),
        const((D_MODEL, D_IN)),
        const((CONV_W, D_CONV)),
        const((len(POOL_WINDOWS), POOL_GROUP, POOL_GROUP)),
        const((1, D_POOL)),
        const((D_MODEL, D_MODEL)),
        const((1, D_MODEL)),
        const((N_EXPERTS, D_MODEL)),
        const((N_EXPERTS, D_MODEL)),
        const((N_EXPERTS, 1)),
        const((TS, TS)),
    ]
    return pl.pallas_call(
        _mixer_kernel,
        grid=(N_TILES,),
        in_specs=in_specs,
        out_specs=out_specs,
        out_shape=out_shape,
        scratch_shapes=[
            pltpu.VMEM((TS + HIST, D_CONV), f32),
            pltpu.VMEM((TS + HIST, D_POOL), f32),
            pltpu.VMEM((TS, D_MODEL), f32),
            pltpu.VMEM((TS, D_MODEL), f32),
        ],
        compiler_params=pltpu.CompilerParams(
            dimension_semantics=("arbitrary",), vmem_limit_bytes=VMEM_LIMIT),
        name="mixer",
    )(xp, xs, sc_t, sp_t, g1, win, cw, pw, ps, wout, g2, rwh, rwl, rb, tri)


def _expert_kernel(te_ref, nu_ref, first_ref, ord_ref, nxt_ref, pt_ref, tot_ref,
                   b1_ref, b2_ref, srt_hbm, w1_hbm, w2_hbm, ys_hbm,
                   xbuf, ybuf, zpiece, w1f, w2f, w1b, w2b, sem_w, sem_x, sem_y, sem_z):
    i = pl.program_id(0)
    n_used = nu_ref[0]
    live = i < n_used

    def weight_copies(e, slot):
        return (pltpu.make_async_copy(w1_hbm.at[e], w1f.at[slot], sem_w.at[slot]),
                pltpu.make_async_copy(w2_hbm.at[e], w2f.at[slot], sem_w.at[2 + slot]))

    def start_x(tile, b):
        for p in range(PIECES):
            src = pt_ref[tile * PIECES + p]
            src = pl.multiple_of(jnp.where(src < 0, ZERO_ROW, src), PIECE)
            pltpu.make_async_copy(srt_hbm.at[pl.ds(src, PIECE)],
                                  xbuf.at[b, pl.ds(p * PIECE, PIECE)], sem_x.at[b]).start()

    def start_y(tile, b, all_dump=False):
        for p in range(PIECES):
            dst = pt_ref[tile * PIECES + p]
            dump = (dst < 0) | all_dump
            dst = pl.multiple_of(jnp.where(dump, DUMP_ROW + b * TM + p * PIECE, dst), PIECE)
            pltpu.make_async_copy(ybuf.at[b, pl.ds(p * PIECE, PIECE)],
                                  ys_hbm.at[pl.ds(dst, PIECE)], sem_y.at[b]).start()

    def wait_x(b):
        pltpu.make_async_copy(srt_hbm.at[pl.ds(0, TM)], xbuf.at[b], sem_x.at[b]).wait()

    def wait_y(b):
        pltpu.make_async_copy(ybuf.at[b], ys_hbm.at[pl.ds(0, TM)], sem_y.at[b]).wait()

    @pl.when(i == 0)
    def _():
        zpiece[...] = jnp.zeros_like(zpiece)

        def fill_tile(t, c):
            def dst(p):
                row = pl.multiple_of(t * S_MAX + tot_ref[t] + p * PIECE, PIECE)
                return ys_hbm.at[pl.ds(row, PIECE)]

            def start(p, c2):
                pltpu.make_async_copy(zpiece, dst(p), sem_z.at[0]).start()
                return c2

            def wait(p, c2):
                pltpu.make_async_copy(zpiece, dst(p), sem_z.at[0]).wait()
                return c2
            n_fill = (S_MAX - tot_ref[t]) // PIECE
            lax.fori_loop(0, n_fill, start, 0)
            return lax.fori_loop(0, n_fill, wait, c)
        lax.fori_loop(0, N_TILES, fill_tile, 0)
        start_x(0, 0)
        start_x(jnp.minimum(1, n_used - 1), 1)
        ybuf[...] = jnp.zeros_like(ybuf)
        for b in range(2):
            pltpu.make_async_copy(ybuf.at[b], ys_hbm.at[pl.ds(DUMP_ROW + b * TM, TM)],
                                  sem_y.at[b]).start()

    @pl.when(live)
    def _():
        ring = i % 3
        expert = te_ref[i]
        wait_x(ring)

        @pl.when(first_ref[i] == 1)
        def _():
            slot = ord_ref[i] % 2

            @pl.when(i == 0)
            def _():
                for cp in weight_copies(te_ref[0], 0):
                    cp.start()
            for cp in weight_copies(expert, slot):
                cp.wait()

            @pl.when(nxt_ref[i] >= 0)
            def _():
                for cp in weight_copies(nxt_ref[i], 1 - slot):
                    cp.start()
            w1b[...] = w1f[slot].astype(bf16)
            w2b[...] = w2f[slot].astype(bf16)

        wait_y(ring)
        gu = jnp.dot(xbuf[ring], w1b[...], preferred_element_type=f32) + b1_ref[expert]
        gate = jnp.minimum(gu[:, :D_FF], SWIGLU_LIMIT)
        lin = jnp.clip(gu[:, D_FF:], -SWIGLU_LIMIT, SWIGLU_LIMIT)
        glu = gate * jax.nn.sigmoid(SWIGLU_ALPHA * gate)
        h = (glu * (lin + 1.0)).astype(bf16)
        start_x(jnp.minimum(i + 2, n_used - 1), (i + 2) % 3)
        start_y(jnp.maximum(i - 1, 0), (i + 2) % 3, all_dump=i == 0)
        out = jnp.dot(h, w2b[...], preferred_element_type=f32) + b2_ref[expert]
        ybuf[ring] = out.astype(bf16)

        @pl.when(i == n_used - 1)
        def _():
            start_y(i, ring)
            for b in range(3):
                wait_y(b)
            wait_x((i + 1) % 3)
            wait_x((i + 2) % 3)


def _experts(tile_expert, n_used, first, order, nxt, piece_row, tot, srt, w1, b1, w2, b2):
    whole = lambda shape: pl.BlockSpec(shape, lambda i, *_: (0,) * len(shape))
    grid_spec = pltpu.PrefetchScalarGridSpec(
        num_scalar_prefetch=7,
        grid=(N_ROW_TILES,),
        in_specs=[
            whole((N_EXPERTS, 1, 2 * D_FF)),
            whole((N_EXPERTS, 1, D_MODEL)),
            pl.BlockSpec(memory_space=pl.ANY),
            pl.BlockSpec(memory_space=pl.ANY),
            pl.BlockSpec(memory_space=pl.ANY),
        ],
        out_specs=pl.BlockSpec(memory_space=pl.ANY),
        scratch_shapes=[
            pltpu.VMEM((3, TM, D_MODEL), bf16), pltpu.VMEM((3, TM, D_MODEL), bf16),
            pltpu.VMEM((PIECE, D_MODEL), bf16),
            pltpu.VMEM((2, D_MODEL, 2 * D_FF), f32), pltpu.VMEM((2, D_FF, D_MODEL), f32),
            pltpu.VMEM((D_MODEL, 2 * D_FF), bf16), pltpu.VMEM((D_FF, D_MODEL), bf16),
            pltpu.SemaphoreType.DMA((4,)), pltpu.SemaphoreType.DMA((3,)),
            pltpu.SemaphoreType.DMA((3,)), pltpu.SemaphoreType.DMA((1,)),
        ],
    )
    return pl.pallas_call(
        _expert_kernel,
        grid_spec=grid_spec,
        out_shape=jax.ShapeDtypeStruct((N_SORT_ROWS + 3 * TM, D_MODEL), bf16),
        compiler_params=pltpu.CompilerParams(
            dimension_semantics=("arbitrary",), vmem_limit_bytes=VMEM_LIMIT),
        name="experts",
    )(tile_expert, n_used, first, order, nxt, piece_row, tot, b1, b2, srt, w1, w2)


def _combine_kernel(slot_ref, g_ref, x1_ref, gf_ref, ys_ref, yp_ref, ysm_ref, wgt):
    i = pl.program_id(0)

    slot = slot_ref[...]
    g = g_ref[...]
    for b in range(S_MAX // S_BLK):
        s_io = b * S_BLK + lax.broadcasted_iota(i32, (TS, S_BLK), 1)
        w = jnp.zeros((TS, S_BLK), f32)
        for k in range(TOP_K):
            w = jnp.where(s_io == slot[:, k:k + 1], g[:, k:k + 1], w)
        wgt[:, b * S_BLK:(b + 1) * S_BLK] = w.astype(bf16)

    y = x1_ref[...] + jnp.dot(wgt[...], ys_ref[...], preferred_element_type=f32)
    out = _rms(y, gf_ref[...])

    @pl.when(i < N_PROMPT_TILES)
    def _():
        yp_ref[...] = out

    @pl.when(i == N_PROMPT_TILES)
    def _():
        ysm_ref[...] = out


def _combine(slot_tm, gates_tm, x1, gf, ys):
    ptile = lambda i: (jnp.minimum(i, N_PROMPT_TILES - 1), 0)
    return pl.pallas_call(
        _combine_kernel,
        grid=(N_TILES,),
        in_specs=[
            pl.BlockSpec((TS, TOP_K), lambda i: (i, 0)),
            pl.BlockSpec((TS, TOP_K), lambda i: (i, 0)),
            pl.BlockSpec((TS, D_MODEL), lambda i: (i, 0)),
            pl.BlockSpec((1, D_MODEL), lambda i: (0, 0)),
            pl.BlockSpec((S_MAX, D_MODEL), lambda i: (i, 0)),
        ],
        out_specs=(
            pl.BlockSpec((TS, D_MODEL), ptile),
            pl.BlockSpec((N_SAMPLE, D_MODEL), lambda i: (0, 0)),
        ),
        out_shape=(
            jax.ShapeDtypeStruct((N_PROMPT, D_MODEL), f32),
            jax.ShapeDtypeStruct((N_SAMPLE, D_MODEL), f32),
        ),
        scratch_shapes=[pltpu.VMEM((TS, S_MAX), bf16)],
        compiler_params=pltpu.CompilerParams(
            dimension_semantics=("arbitrary",), vmem_limit_bytes=VMEM_LIMIT),
        name="combine",
    )(slot_tm, gates_tm, x1, gf, ys)


def _routing_tables(cnt):
    n = cnt.reshape(N_TILES, N_EXPERTS).astype(i32)
    c = (n + (PIECE - 1)) // PIECE * PIECE
    lo = jnp.cumsum(c, axis=1) - c
    tot = jnp.sum(c, axis=1)
    group = jnp.sum(c, axis=0)
    padded = (group + (TM - 1)) // TM * TM
    gend = jnp.cumsum(padded)
    gstart = gend - padded
    cs = gstart[None, :] + jnp.cumsum(c, axis=0) - c
    n_used = gend[-1] // TM
    tile_id = jnp.arange(N_ROW_TILES, dtype=i32)
    tile_expert = jnp.sum((tile_id[:, None] * TM >= gend[None, :]).astype(i32), axis=1)
    tile_expert = jnp.minimum(tile_expert, jnp.take(tile_expert, n_used - 1))
    first = (tile_id < n_used) & ((tile_id == 0) | (tile_expert != jnp.roll(tile_expert, 1)))
    order = jnp.maximum(jnp.cumsum(first.astype(i32)) - 1, 0)
    e_ar = jnp.arange(N_EXPERTS, dtype=i32)
    later = (e_ar[None, :] > e_ar[:, None]) & (padded[None, :] > 0)
    nxt_e = jnp.min(jnp.where(later, e_ar[None, :], N_EXPERTS), axis=1)
    nxt_e = jnp.where(nxt_e == N_EXPERTS, -1, nxt_e)
    nxt = jnp.sum(jnp.where(tile_expert[:, None] == e_ar[None, :], nxt_e[None, :], 0), axis=1)
    cs_f = cs.T.reshape(-1)
    c_f = c.T.reshape(-1)
    src_f = (jnp.arange(N_TILES, dtype=i32)[:, None] * S_MAX + lo).T.reshape(-1)
    piece_r = jnp.arange(N_ROW_TILES * PIECES, dtype=i32) * PIECE
    reached = (piece_r[:, None] >= cs_f[None, 1:]).astype(i32)

    def lookup(f):
        return f[0] + jnp.sum(reached * (f[1:] - f[:-1])[None, :], axis=1)
    off = piece_r - lookup(cs_f)
    piece_row = jnp.where(off < lookup(c_f), lookup(src_f) + off, -1)
    return dict(tot=tot, n_used=n_used[None], tile_expert=tile_expert, first=first.astype(i32),
                order=order, nxt=nxt, piece_row=piece_row)


def kernel(x_prompt, x_sample, state_conv, state_pool, norm_mix_g, w_in, conv_w, pool_w, pool_scale,
           w_out, norm_ffn_g, router_w, router_b, exp_w1, exp_b1, exp_w2, exp_b2, final_norm_g):
    l = 0
    xp = x_prompt.reshape(N_PROMPT, D_MODEL)
    xs = jnp.transpose(x_sample, (1, 0, 2)).reshape(N_SAMPLE, D_MODEL)
    sc_t = jnp.transpose(state_conv[l], (1, 0, 2))
    sp_t = jnp.transpose(state_pool[l], (1, 0, 2))
    rw_t = router_w[l].T
    rwh = rw_t.astype(bf16)
    rwl = (rw_t - rwh.astype(f32)).astype(bf16)
    tok = jnp.arange(TS, dtype=i32)
    tri = (tok[:, None] < tok[None, :]).astype(bf16)

    (x1, srt, slot_t, gate_t, cnt, npc, npp, nsc, nsv) = _mixer(
        xp, xs, sc_t, sp_t, norm_mix_g[l][None, :], w_in[l].astype(bf16), conv_w[l],
        pool_w[l].astype(bf16), pool_scale[l][None, :], w_out[l].astype(bf16),
        norm_ffn_g[l][None, :], rwh, rwl, router_b[l][:, None], tri)

    t = _routing_tables(cnt)
    ys = _experts(t["tile_expert"], t["n_used"], t["first"], t["order"], t["nxt"], t["piece_row"],
                  t["tot"], srt, exp_w1[l], exp_b1[l][:, None, :], exp_w2[l], exp_b2[l][:, None, :])
    y_p, y_s = _combine(slot_t.T, gate_t.T, x1, final_norm_g[None, :], ys)

    y_prompt = y_p.reshape(BATCH, SEQ, D_MODEL)
    y_sample = jnp.transpose(y_s.reshape(DEC_SEQ, DEC_BATCH, D_MODEL), (1, 0, 2))
    new_conv_prompt = npc[None, :, 8 - (CONV_W - 1):, :]
    new_pool_prompt = npp[None, :, HIST - POOL_BUF:, :]
    new_conv_sample = jnp.transpose(nsc, (1, 0, 2))[None]
    new_pool_sample = jnp.concatenate(
        [state_pool[l][:, DEC_SEQ:, :], jnp.transpose(nsv, (1, 0, 2))], axis=1)[None]
    return (y_prompt, y_sample, new_conv_prompt, new_pool_prompt, new_conv_sample, new_pool_sample)
```

```python
import jax
import jax.numpy as jnp
from jax import lax
from jax.experimental import pallas as pl
from jax.experimental.pallas import tpu as pltpu

D_MODEL = 1024
D_CONV = 512
D_POOL = 512
D_IN = 3 * D_CONV + D_POOL
CONV_W = 3
POOL_WINDOWS = (2, 4, 8, 16)
POOL_GROUP = 128
POOL_BUF = 15
N_EXPERTS = 32
TOP_K = 4
D_FF = 1024
SWIGLU_LIMIT = 7.0
SWIGLU_ALPHA = 1.702
EPS = 1e-5
PAST_LEN = 16384

BATCH, SEQ = 8, 2048
DEC_BATCH, DEC_SEQ = 128, 4
N_PROMPT = BATCH * SEQ
N_SAMPLE = DEC_BATCH * DEC_SEQ
N_TOK = N_PROMPT + N_SAMPLE

TS = 512
SEQ_TILES = SEQ // TS
N_PROMPT_TILES = N_PROMPT // TS
N_TILES = N_PROMPT_TILES + 1
HIST = 16
TM = 512
PIECE = 8
PIECES = TM // PIECE
S_BLK = 256
SORT_UNROLL = 3
S_MAX = (TS * TOP_K + N_EXPERTS * (PIECE - 1) + PIECE + S_BLK - 1) // S_BLK * S_BLK
ZERO_ROW = S_MAX - PIECE
N_ROW_TILES = (N_TOK * TOP_K + N_TILES * N_EXPERTS * (PIECE - 1) + N_EXPERTS * (TM - PIECE)) // TM
N_SORT_ROWS = N_TILES * S_MAX
DUMP_ROW = N_SORT_ROWS

VMEM_LIMIT = 56 * 1024 * 1024

f32 = jnp.float32
bf16 = jnp.bfloat16
i32 = jnp.int32


def _rms(x, g):
    return x * lax.rsqrt(jnp.mean(x * x, axis=-1, keepdims=True) + EPS) * g


def _pool_project(p, pw_ref, ps_ref):
    outs = []
    for g in range(len(POOL_WINDOWS)):
        sl = slice(g * POOL_GROUP, (g + 1) * POOL_GROUP)
        outs.append(jnp.dot(p[:, sl].astype(bf16), pw_ref[g], preferred_element_type=f32))
    return jnp.concatenate(outs, axis=1) * ps_ref[...]


def _route_and_sort(x, z, wout_ref, g2_ref, rwh_ref, rwl_ref, rb_ref, tri_ref,
                    x1_ref, srt_ref, slot_ref, gate_ref, cnt_ref):
    n = x.shape[0]
    x1 = x + jnp.dot(z.astype(bf16), wout_ref[...], preferred_element_type=f32)
    x1_ref[...] = x1
    xn = _rms(x1, g2_ref[...])
    xb = xn.astype(bf16)

    xlo = (xn - xb.astype(f32)).astype(bf16)
    nt = (((1,), (1,)), ((), ()))
    logits = (lax.dot_general(rwh_ref[...], xb, nt, preferred_element_type=f32)
              + lax.dot_general(rwl_ref[...], xb, nt, preferred_element_type=f32)
              + lax.dot_general(rwh_ref[...], xlo, nt, preferred_element_type=f32)
              + rb_ref[...])

    e_iota = lax.broadcasted_iota(i32, (N_EXPERTS, n), 0)
    work = logits
    hots, vals = [], []
    for _ in range(TOP_K):
        m = jnp.max(work, axis=0, keepdims=True)
        ik = jnp.min(jnp.where(work == m, e_iota, N_EXPERTS), axis=0, keepdims=True)
        hot = e_iota == ik
        work = jnp.where(hot, -jnp.inf, work)
        hots.append(hot); vals.append(m)
    exps = [jnp.exp(v - vals[0]) for v in vals]
    den = exps[0] + exps[1] + exps[2] + exps[3]
    gate_ref[...] = jnp.concatenate([e / den for e in exps], axis=0)

    multi = jnp.where(hots[0] | hots[1] | hots[2] | hots[3], 1.0, 0.0)
    before = jnp.dot(multi.astype(bf16), tri_ref[...], preferred_element_type=f32)
    count = jnp.sum(multi, axis=1, keepdims=True)
    cnt_ref[...] = count
    pieces = jnp.floor((count + (PIECE - 1)) * (1.0 / PIECE))
    er = lax.broadcasted_iota(i32, (N_EXPERTS, N_EXPERTS), 0)
    ec = lax.broadcasted_iota(i32, (N_EXPERTS, N_EXPERTS), 1)
    lower = jnp.where(ec < er, 1.0, 0.0).astype(bf16)
    pieces_b = jnp.broadcast_to(pieces, (N_EXPERTS, 128)).astype(bf16)
    start = PIECE * jnp.dot(lower, pieces_b, preferred_element_type=f32)[:, 0:1]
    base = before + start
    slot = jnp.concatenate(
        [jnp.sum(jnp.where(h, base, 0.0), axis=0, keepdims=True) for h in hots], axis=0).astype(i32)
    slot_ref[...] = slot

    def sort_blocks(b2, c):
        for half in range(SORT_UNROLL):
            row0 = pl.multiple_of((SORT_UNROLL * b2 + half) * S_BLK, S_BLK)
            s_io = row0 + lax.broadcasted_iota(i32, (S_BLK, n), 0)
            hit = s_io == slot[0:1, :]
            for k in range(1, TOP_K):
                hit = hit | (s_io == slot[k:k + 1, :])
            sel = jnp.where(hit, 1.0, 0.0).astype(bf16)
            srt_ref[pl.ds(row0, S_BLK), :] = jnp.dot(
                sel, xb, preferred_element_type=f32).astype(bf16)
        return c
    lax.fori_loop(0, S_MAX // (SORT_UNROLL * S_BLK), sort_blocks, 0)


def _mixer_kernel(xp_in_ref, xs_in_ref, sc_ref, sp_ref, g1_ref, win_ref, cw_ref, pw_ref, ps_ref,
                  wout_ref, g2_ref, rwh_ref, rwl_ref, rb_ref, tri_ref,
                  x1_ref, srt_ref, slot_ref, gate_ref, cnt_ref,
                  npc_ref, npp_ref, nsc_ref, nsv_ref,
                  ubuf, vbuf, xcur, zcur):
    i = pl.program_id(0)

    @pl.when(i < N_PROMPT_TILES)
    def _():
        s = i % SEQ_TILES

        @pl.when(s == 0)
        def _():
            ubuf[0:HIST, :] = jnp.zeros((HIST, D_CONV), f32)
            vbuf[0:HIST, :] = jnp.zeros((HIST, D_POOL), f32)

        x = xp_in_ref[...]
        xcur[...] = x
        xn = _rms(x, g1_ref[...]).astype(bf16)
        proj = jnp.dot(xn, win_ref[...], preferred_element_type=f32)
        bg = proj[:, :D_CONV]
        u = proj[:, D_CONV:2 * D_CONV] * proj[:, 2 * D_CONV:3 * D_CONV]
        v = proj[:, 3 * D_CONV:]
        ubuf[HIST:HIST + TS, :] = u
        vbuf[HIST:HIST + TS, :] = v

        y = (ubuf[HIST - 2:HIST - 2 + TS, :] * cw_ref[0:1, :]
             + ubuf[HIST - 1:HIST - 1 + TS, :] * cw_ref[1:2, :]
             + u * cw_ref[2:3, :])
        zcur[:, :D_CONV] = bg * y

        pos = s * TS + lax.broadcasted_iota(i32, (TS, 1), 0)
        ps = []
        for g, w in enumerate(POOL_WINDOWS):
            sl = slice(g * POOL_GROUP, (g + 1) * POOL_GROUP)
            acc = vbuf[:, sl]
            for step in range(g + 1):
                acc = acc + pltpu.roll(acc, 1 << step, axis=0)
            cnt = jnp.minimum(w, pos + 1).astype(f32)
            ps.append(acc[HIST:, :] * (1.0 / cnt) - v[:, sl])
        zcur[:, D_CONV:] = _pool_project(jnp.concatenate(ps, axis=1), pw_ref, ps_ref)

        ubuf[HIST - 8:HIST, :] = ubuf[TS + HIST - 8:TS + HIST, :]
        vbuf[0:HIST, :] = vbuf[TS:TS + HIST, :]

        @pl.when(s == SEQ_TILES - 1)
        def _():
            npc_ref[0] = ubuf[HIST - 8:HIST, :]
            npp_ref[0] = vbuf[0:HIST, :]

    @pl.when(i == N_PROMPT_TILES)
    def _():
        x = xs_in_ref[...]
        xcur[...] = x
        xn = _rms(x, g1_ref[...]).astype(bf16)
        proj = jnp.dot(xn, win_ref[...], preferred_element_type=f32)
        nb = DEC_BATCH
        rows = lambda a, t: a[t * nb:(t + 1) * nb]
        bg = proj[:, :D_CONV]
        u = proj[:, D_CONV:2 * D_CONV] * proj[:, 2 * D_CONV:3 * D_CONV]
        v = proj[:, 3 * D_CONV:]
        up = [sc_ref[j] for j in range(CONV_W - 1)] + [rows(u, t) for t in range(DEC_SEQ)]
        vp = [sp_ref[j] for j in range(POOL_BUF)] + [rows(v, t) for t in range(DEC_SEQ)]
        zc, ps = [], []
        for t in range(DEC_SEQ):
            y = up[t] * cw_ref[0:1, :]
            for k in range(1, CONV_W):
                y = y + up[t + k] * cw_ref[k:k + 1, :]
            zc.append(rows(bg, t) * y)
            pg = []
            for g, w in enumerate(POOL_WINDOWS):
                sl = slice(g * POOL_GROUP, (g + 1) * POOL_GROUP)
                lo = t + POOL_BUF - w + 1
                acc = vp[lo][:, sl]
                for j in range(lo + 1, t + POOL_BUF + 1):
                    acc = acc + vp[j][:, sl]
                cnt = float(min(w, PAST_LEN + t + 1))
                pg.append(acc * (1.0 / cnt) - vp[t + POOL_BUF][:, sl])
            ps.append(jnp.concatenate(pg, axis=1))
        zcur[:, :D_CONV] = jnp.concatenate(zc, axis=0)
        zcur[:, D_CONV:] = _pool_project(jnp.concatenate(ps, axis=0), pw_ref, ps_ref)
        for j in range(CONV_W - 1):
            nsc_ref[j] = up[DEC_SEQ + j]
        for t in range(DEC_SEQ):
            nsv_ref[t] = rows(v, t)

    _route_and_sort(xcur[...], zcur[...], wout_ref, g2_ref, rwh_ref, rwl_ref, rb_ref, tri_ref,
                    x1_ref, srt_ref, slot_ref, gate_ref, cnt_ref)


def _mixer(xp, xs, sc_t, sp_t, g1, win, cw, pw, ps, wout, g2, rwh, rwl, rb, tri):
    const = lambda shape: pl.BlockSpec(shape, lambda i: (0,) * len(shape),
                                       pipeline_mode=pl.Buffered(1))
    ptile = lambda i: jnp.minimum(i, N_PROMPT_TILES - 1)
    pbatch = lambda i: jnp.minimum(i, N_PROMPT_TILES - 1) // SEQ_TILES
    out_shape = (
        jax.ShapeDtypeStruct((N_TOK, D_MODEL), f32),
        jax.ShapeDtypeStruct((N_SORT_ROWS, D_MODEL), bf16),
        jax.ShapeDtypeStruct((TOP_K, N_TOK), i32),
        jax.ShapeDtypeStruct((TOP_K, N_TOK), f32),
        jax.ShapeDtypeStruct((N_TILES * N_EXPERTS, 1), f32),
        jax.ShapeDtypeStruct((BATCH, 8, D_CONV), f32),
        jax.ShapeDtypeStruct((BATCH, HIST, D_POOL), f32),
        jax.ShapeDtypeStruct((CONV_W - 1, DEC_BATCH, D_CONV), f32),
        jax.ShapeDtypeStruct((DEC_SEQ, DEC_BATCH, D_POOL), f32),
    )
    out_specs = (
        pl.BlockSpec((TS, D_MODEL), lambda i: (i, 0)),
        pl.BlockSpec((S_MAX, D_MODEL), lambda i: (i, 0)),
        pl.BlockSpec((TOP_K, TS), lambda i: (0, i)),
        pl.BlockSpec((TOP_K, TS), lambda i: (0, i)),
        pl.BlockSpec((N_EXPERTS, 1), lambda i: (i, 0)),
        pl.BlockSpec((1, 8, D_CONV), lambda i: (pbatch(i), 0, 0)),
        pl.BlockSpec((1, HIST, D_POOL), lambda i: (pbatch(i), 0, 0)),
        pl.BlockSpec((CONV_W - 1, DEC_BATCH, D_CONV), lambda i: (0, 0, 0)),
        pl.BlockSpec((DEC_SEQ, DEC_BATCH, D_POOL), lambda i: (0, 0, 0)),
    )
    in_specs = [
        pl.BlockSpec((TS, D_MODEL), lambda i: (ptile(i), 0)),
        const((N_SAMPLE, D_MODEL)),
        const((CONV_W - 1, DEC_BATCH, D_CONV)),
        const((POOL_BUF, DEC_BATCH, D_POOL)),
        const((1, D_MODEL)),
        const((D_MODEL, D_IN)),
        const((CONV_W, D_CONV)),
        const((len(POOL_WINDOWS), POOL_GROUP, POOL_GROUP)),
        const((1, D_POOL)),
        const((D_MODEL, D_MODEL)),
        const((1, D_MODEL)),
        const((N_EXPERTS, D_MODEL)),
        const((N_EXPERTS, D_MODEL)),
        const((N_EXPERTS, 1)),
        const((TS, TS)),
    ]
    return pl.pallas_call(
        _mixer_kernel,
        grid=(N_TILES,),
        in_specs=in_specs,
        out_specs=out_specs,
        out_shape=out_shape,
        scratch_shapes=[
            pltpu.VMEM((TS + HIST, D_CONV), f32),
            pltpu.VMEM((TS + HIST, D_POOL), f32),
            pltpu.VMEM((TS, D_MODEL), f32),
            pltpu.VMEM((TS, D_MODEL), f32),
        ],
        compiler_params=pltpu.CompilerParams(
            dimension_semantics=("arbitrary",), vmem_limit_bytes=VMEM_LIMIT),
        name="mixer",
    )(xp, xs, sc_t, sp_t, g1, win, cw, pw, ps, wout, g2, rwh, rwl, rb, tri)


def _expert_kernel(te_ref, nu_ref, first_ref, ord_ref, nxt_ref, pt_ref, tot_ref,
                   b1_ref, b2_ref, srt_hbm, w1_hbm, w2_hbm, ys_hbm,
                   xbuf, ybuf, zpiece, w1f, w2f, w1b, w2b, sem_w, sem_x, sem_y, sem_z):
    i = pl.program_id(0)
    n_used = nu_ref[0]
    live = i < n_used

    def weight_copies(e, slot):
        return (pltpu.make_async_copy(w1_hbm.at[e], w1f.at[slot], sem_w.at[slot]),
                pltpu.make_async_copy(w2_hbm.at[e], w2f.at[slot], sem_w.at[2 + slot]))

    def start_x(tile, b):
        for p in range(PIECES):
            src = pt_ref[tile * PIECES + p]
            src = pl.multiple_of(jnp.where(src < 0, ZERO_ROW, src), PIECE)
            pltpu.make_async_copy(srt_hbm.at[pl.ds(src, PIECE)],
                                  xbuf.at[b, pl.ds(p * PIECE, PIECE)], sem_x.at[b]).start()

    def start_y(tile, b, all_dump=False):
        for p in range(PIECES):
            dst = pt_ref[tile * PIECES + p]
            dump = (dst < 0) | all_dump
            dst = pl.multiple_of(jnp.where(dump, DUMP_ROW + b * TM + p * PIECE, dst), PIECE)
            pltpu.make_async_copy(ybuf.at[b, pl.ds(p * PIECE, PIECE)],
                                  ys_hbm.at[pl.ds(dst, PIECE)], sem_y.at[b]).start()

    def wait_x(b):
        pltpu.make_async_copy(srt_hbm.at[pl.ds(0, TM)], xbuf.at[b], sem_x.at[b]).wait()

    def wait_y(b):
        pltpu.make_async_copy(ybuf.at[b], ys_hbm.at[pl.ds(0, TM)], sem_y.at[b]).wait()

    @pl.when(i == 0)
    def _():
        zpiece[...] = jnp.zeros_like(zpiece)

        def fill_tile(t, c):
            def dst(p):
                row = pl.multiple_of(t * S_MAX + tot_ref[t] + p * PIECE, PIECE)
                return ys_hbm.at[pl.ds(row, PIECE)]

            def start(p, c2):
                pltpu.make_async_copy(zpiece, dst(p), sem_z.at[0]).start()
                return c2

            def wait(p, c2):
                pltpu.make_async_copy(zpiece, dst(p), sem_z.at[0]).wait()
                return c2
            n_fill = (S_MAX - tot_ref[t]) // PIECE
            lax.fori_loop(0, n_fill, start, 0)
            return lax.fori_loop(0, n_fill, wait, c)
        lax.fori_loop(0, N_TILES, fill_tile, 0)
        start_x(0, 0)
        start_x(jnp.minimum(1, n_used - 1), 1)
        ybuf[...] = jnp.zeros_like(ybuf)
        for b in range(2):
            pltpu.make_async_copy(ybuf.at[b], ys_hbm.at[pl.ds(DUMP_ROW + b * TM, TM)],
                                  sem_y.at[b]).start()

    @pl.when(live)
    def _():
        ring = i % 3
        expert = te_ref[i]
        wait_x(ring)

        @pl.when(first_ref[i] == 1)
        def _():
            slot = ord_ref[i] % 2

            @pl.when(i == 0)
            def _():
                for cp in weight_copies(te_ref[0], 0):
                    cp.start()
            for cp in weight_copies(expert, slot):
                cp.wait()

            @pl.when(nxt_ref[i] >= 0)
            def _():
                for cp in weight_copies(nxt_ref[i], 1 - slot):
                    cp.start()
            w1b[...] = w1f[slot].astype(bf16)
            w2b[...] = w2f[slot].astype(bf16)

        wait_y(ring)
        gu = jnp.dot(xbuf[ring], w1b[...], preferred_element_type=f32) + b1_ref[expert]
        gate = jnp.minimum(gu[:, :D_FF], SWIGLU_LIMIT)
        lin = jnp.clip(gu[:, D_FF:], -SWIGLU_LIMIT, SWIGLU_LIMIT)
        glu = gate * jax.nn.sigmoid(SWIGLU_ALPHA * gate)
        h = (glu * (lin + 1.0)).astype(bf16)
        start_x(jnp.minimum(i + 2, n_used - 1), (i + 2) % 3)
        start_y(jnp.maximum(i - 1, 0), (i + 2) % 3, all_dump=i == 0)
        out = jnp.dot(h, w2b[...], preferred_element_type=f32) + b2_ref[expert]
        ybuf[ring] = out.astype(bf16)

        @pl.when(i == n_used - 1)
        def _():
            start_y(i, ring)
            for b in range(3):
                wait_y(b)
            wait_x((i + 1) % 3)
            wait_x((i + 2) % 3)


def _experts(tile_expert, n_used, first, order, nxt, piece_row, tot, srt, w1, b1, w2, b2):
    whole = lambda shape: pl.BlockSpec(shape, lambda i, *_: (0,) * len(shape))
    grid_spec = pltpu.PrefetchScalarGridSpec(
        num_scalar_prefetch=7,
        grid=(N_ROW_TILES,),
        in_specs=[
            whole((N_EXPERTS, 1, 2 * D_FF)),
            whole((N_EXPERTS, 1, D_MODEL)),
            pl.BlockSpec(memory_space=pl.ANY),
            pl.BlockSpec(memory_space=pl.ANY),
            pl.BlockSpec(memory_space=pl.ANY),
        ],
        out_specs=pl.BlockSpec(memory_space=pl.ANY),
        scratch_shapes=[
            pltpu.VMEM((3, TM, D_MODEL), bf16), pltpu.VMEM((3, TM, D_MODEL), bf16),
            pltpu.VMEM((PIECE, D_MODEL), bf16),
            pltpu.VMEM((2, D_MODEL, 2 * D_FF), f32), pltpu.VMEM((2, D_FF, D_MODEL), f32),
            pltpu.VMEM((D_MODEL, 2 * D_FF), bf16), pltpu.VMEM((D_FF, D_MODEL), bf16),
            pltpu.SemaphoreType.DMA((4,)), pltpu.SemaphoreType.DMA((3,)),
            pltpu.SemaphoreType.DMA((3,)), pltpu.SemaphoreType.DMA((1,)),
        ],
    )
    return pl.pallas_call(
        _expert_kernel,
        grid_spec=grid_spec,
        out_shape=jax.ShapeDtypeStruct((N_SORT_ROWS + 3 * TM, D_MODEL), bf16),
        compiler_params=pltpu.CompilerParams(
            dimension_semantics=("arbitrary",), vmem_limit_bytes=VMEM_LIMIT),
        name="experts",
    )(tile_expert, n_used, first, order, nxt, piece_row, tot, b1, b2, srt, w1, w2)


def _combine_kernel(slot_ref, g_ref, x1_ref, gf_ref, ys_ref, yp_ref, ysm_ref, wgt):
    i = pl.program_id(0)

    slot = slot_ref[...]
    g = g_ref[...]
    for b in range(S_MAX // S_BLK):
        s_io = b * S_BLK + lax.broadcasted_iota(i32, (TS, S_BLK), 1)
        w = jnp.zeros((TS, S_BLK), f32)
        for k in range(TOP_K):
            w = jnp.where(s_io == slot[:, k:k + 1], g[:, k:k + 1], w)
        wgt[:, b * S_BLK:(b + 1) * S_BLK] = w.astype(bf16)

    y = x1_ref[...] + jnp.dot(wgt[...], ys_ref[...], preferred_element_type=f32)
    out = _rms(y, gf_ref[...])

    @pl.when(i < N_PROMPT_TILES)
    def _():
        yp_ref[...] = out

    @pl.when(i == N_PROMPT_TILES)
    def _():
        ysm_ref[...] = out


def _combine(slot_tm, gates_tm, x1, gf, ys):
    ptile = lambda i: (jnp.minimum(i, N_PROMPT_TILES - 1), 0)
    return pl.pallas_call(
        _combine_kernel,
        grid=(N_TILES,),
        in_specs=[
            pl.BlockSpec((TS, TOP_K), lambda i: (i, 0)),
            pl.BlockSpec((TS, TOP_K), lambda i: (i, 0)),
            pl.BlockSpec((TS, D_MODEL), lambda i: (i, 0)),
            pl.BlockSpec((1, D_MODEL), lambda i: (0, 0)),
            pl.BlockSpec((S_MAX, D_MODEL), lambda i: (i, 0)),
        ],
        out_specs=(
            pl.BlockSpec((TS, D_MODEL), ptile),
            pl.BlockSpec((N_SAMPLE, D_MODEL), lambda i: (0, 0)),
        ),
        out_shape=(
            jax.ShapeDtypeStruct((N_PROMPT, D_MODEL), f32),
            jax.ShapeDtypeStruct((N_SAMPLE, D_MODEL), f32),
        ),
        scratch_shapes=[pltpu.VMEM((TS, S_MAX), bf16)],
        compiler_params=pltpu.CompilerParams(
            dimension_semantics=("arbitrary",), vmem_limit_bytes=VMEM_LIMIT),
        name="combine",
    )(slot_tm, gates_tm, x1, gf, ys)


def _routing_tables(cnt):
    n = cnt.reshape(N_TILES, N_EXPERTS).astype(i32)
    c = (n + (PIECE - 1)) // PIECE * PIECE
    lo = jnp.cumsum(c, axis=1) - c
    tot = jnp.sum(c, axis=1)
    group = jnp.sum(c, axis=0)
    padded = (group + (TM - 1)) // TM * TM
    gend = jnp.cumsum(padded)
    gstart = gend - padded
    cs = gstart[None, :] + jnp.cumsum(c, axis=0) - c
    n_used = gend[-1] // TM
    tile_id = jnp.arange(N_ROW_TILES, dtype=i32)
    tile_expert = jnp.sum((tile_id[:, None] * TM >= gend[None, :]).astype(i32), axis=1)
    tile_expert = jnp.minimum(tile_expert, jnp.take(tile_expert, n_used - 1))
    first = (tile_id < n_used) & ((tile_id == 0) | (tile_expert != jnp.roll(tile_expert, 1)))
    order = jnp.maximum(jnp.cumsum(first.astype(i32)) - 1, 0)
    e_ar = jnp.arange(N_EXPERTS, dtype=i32)
    later = (e_ar[None, :] > e_ar[:, None]) & (padded[None, :] > 0)
    nxt_e = jnp.min(jnp.where(later, e_ar[None, :], N_EXPERTS), axis=1)
    nxt_e = jnp.where(nxt_e == N_EXPERTS, -1, nxt_e)
    nxt = jnp.sum(jnp.where(tile_expert[:, None] == e_ar[None, :], nxt_e[None, :], 0), axis=1)
    cs_f = cs.T.reshape(-1)
    c_f = c.T.reshape(-1)
    src_f = (jnp.arange(N_TILES, dtype=i32)[:, None] * S_MAX + lo).T.reshape(-1)
    piece_r = jnp.arange(N_ROW_TILES * PIECES, dtype=i32) * PIECE
    reached = (piece_r[:, None] >= cs_f[None, 1:]).astype(i32)

    def lookup(f):
        return f[0] + jnp.sum(reached * (f[1:] - f[:-1])[None, :], axis=1)
    off = piece_r - lookup(cs_f)
    piece_row = jnp.where(off < lookup(c_f), lookup(src_f) + off, -1)
    return dict(tot=tot, n_used=n_used[None], tile_expert=tile_expert, first=first.astype(i32),
                order=order, nxt=nxt, piece_row=piece_row)


def kernel(x_prompt, x_sample, state_conv, state_pool, norm_mix_g, w_in, conv_w, pool_w, pool_scale,
           w_out, norm_ffn_g, router_w, router_b, exp_w1, exp_b1, exp_w2, exp_b2, final_norm_g):
    l = 0
    xp = x_prompt.reshape(N_PROMPT, D_MODEL)
    xs = jnp.transpose(x_sample, (1, 0, 2)).reshape(N_SAMPLE, D_MODEL)
    sc_t = jnp.transpose(state_conv[l], (1, 0, 2))
    sp_t = jnp.transpose(state_pool[l], (1, 0, 2))
    rw_t = router_w[l].T
    rwh = rw_t.astype(bf16)
    rwl = (rw_t - rwh.astype(f32)).astype(bf16)
    tok = jnp.arange(TS, dtype=i32)
    tri = (tok[:, None] < tok[None, :]).astype(bf16)

    (x1, srt, slot_t, gate_t, cnt, npc, npp, nsc, nsv) = _mixer(
        xp, xs, sc_t, sp_t, norm_mix_g[l][None, :], w_in[l].astype(bf16), conv_w[l],
        pool_w[l].astype(bf16), pool_scale[l][None, :], w_out[l].astype(bf16),
        norm_ffn_g[l][None, :], rwh, rwl, router_b[l][:, None], tri)

    t = _routing_tables(cnt)
    ys = _experts(t["tile_expert"], t["n_used"], t["first"], t["order"], t["nxt"], t["piece_row"],
                  t["tot"], srt, exp_w1[l], exp_b1[l][:, None, :], exp_w2[l], exp_b2[l][:, None, :])
    y_p, y_s = _combine(slot_t.T, gate_t.T, x1, final_norm_g[None, :], ys)

    y_prompt = y_p.reshape(BATCH, SEQ, D_MODEL)
    y_sample = jnp.transpose(y_s.reshape(DEC_SEQ, DEC_BATCH, D_MODEL), (1, 0, 2))
    new_conv_prompt = npc[None, :, 8 - (CONV_W - 1):, :]
    new_pool_prompt = npp[None, :, HIST - POOL_BUF:, :]
    new_conv_sample = jnp.transpose(nsc, (1, 0, 2))[None]
    new_pool_sample = jnp.concatenate(
        [state_pool[l][:, DEC_SEQ:, :], jnp.transpose(nsv, (1, 0, 2))], axis=1)[None]
    return (y_prompt, y_sample, new_conv_prompt, new_pool_prompt, new_conv_sample, new_pool_sample)
```

```python
import jax
import jax.numpy as jnp
from jax import lax
from jax.experimental import pallas as pl
from jax.experimental.pallas import tpu as pltpu

D_MODEL = 1024
D_CONV = 512
D_POOL = 512
D_IN = 3 * D_CONV + D_POOL
CONV_W = 3
POOL_WINDOWS = (2, 4, 8, 16)
POOL_GROUP = 128
POOL_BUF = 15
N_EXPERTS = 32
TOP_K = 4
D_FF = 1024
SWIGLU_LIMIT = 7.0
SWIGLU_ALPHA = 1.702
EPS = 1e-5
PAST_LEN = 16384

BATCH, SEQ = 8, 2048
DEC_BATCH, DEC_SEQ = 128, 4
N_PROMPT = BATCH * SEQ
N_SAMPLE = DEC_BATCH * DEC_SEQ
N_TOK = N_PROMPT + N_SAMPLE

TS = 512
SEQ_TILES = SEQ // TS
N_PROMPT_TILES = N_PROMPT // TS
N_TILES = N_PROMPT_TILES + 1
HIST = 16
UNIT = 256
TM = 2 * UNIT
PIECE = 8
PIECES = TM // PIECE
UNIT_PIECES = UNIT // PIECE
S_BLK = 256
SORT_BLK = 512
S_MAX = (TS * TOP_K + N_EXPERTS * (PIECE - 1) + PIECE + S_BLK - 1) // S_BLK * S_BLK
ZERO_ROW = S_MAX - PIECE
N_UNITS = (N_TOK * TOP_K + N_TILES * N_EXPERTS * (PIECE - 1) + N_EXPERTS * (UNIT - PIECE)) // UNIT
N_ROW_TILES = (N_UNITS + N_EXPERTS) // 2
N_SORT_ROWS = N_TILES * S_MAX
DUMP_ROW = N_SORT_ROWS

VMEM_LIMIT = 56 * 1024 * 1024

f32 = jnp.float32
bf16 = jnp.bfloat16
i32 = jnp.int32


def _rms(x, g):
    return x * lax.rsqrt(jnp.mean(x * x, axis=-1, keepdims=True) + EPS) * g


def _pool_project(p, pw_ref, ps_ref):
    outs = []
    for g in range(len(POOL_WINDOWS)):
        sl = slice(g * POOL_GROUP, (g + 1) * POOL_GROUP)
        outs.append(jnp.dot(p[:, sl].astype(bf16), pw_ref[g], preferred_element_type=f32))
    return jnp.concatenate(outs, axis=1) * ps_ref[...]


def _route_and_sort(x, z, wout_ref, g2_ref, rwh_ref, rwl_ref, rb_ref, tri_ref,
                    x1_ref, srt_ref, slot_ref, gate_ref, cnt_ref):
    n = x.shape[0]
    x1 = x + jnp.dot(z.astype(bf16), wout_ref[...], preferred_element_type=f32)
    x1_ref[...] = x1
    xn = _rms(x1, g2_ref[...])
    xb = xn.astype(bf16)

    xlo = (xn - xb.astype(f32)).astype(bf16)
    nt = (((1,), (1,)), ((), ()))
    logits = (lax.dot_general(rwh_ref[...], xb, nt, preferred_element_type=f32)
              + lax.dot_general(rwl_ref[...], xb, nt, preferred_element_type=f32)
              + lax.dot_general(rwh_ref[...], xlo, nt, preferred_element_type=f32)
              + rb_ref[...])

    e_iota = lax.broadcasted_iota(i32, (N_EXPERTS, n), 0)
    work = logits
    hots, vals = [], []
    for _ in range(TOP_K):
        m = jnp.max(work, axis=0, keepdims=True)
        ik = jnp.min(jnp.where(work == m, e_iota, N_EXPERTS), axis=0, keepdims=True)
        hot = e_iota == ik
        work = jnp.where(hot, -jnp.inf, work)
        hots.append(hot); vals.append(m)
    exps = [jnp.exp(v - vals[0]) for v in vals]
    den = exps[0] + exps[1] + exps[2] + exps[3]
    gate_ref[...] = jnp.concatenate([e / den for e in exps], axis=0)

    multi = jnp.where(hots[0] | hots[1] | hots[2] | hots[3], 1.0, 0.0)
    before = jnp.dot(multi.astype(bf16), tri_ref[...], preferred_element_type=f32)
    count = jnp.sum(multi, axis=1, keepdims=True)
    cnt_ref[...] = count
    pieces = jnp.floor((count + (PIECE - 1)) * (1.0 / PIECE))
    er = lax.broadcasted_iota(i32, (N_EXPERTS, N_EXPERTS), 0)
    ec = lax.broadcasted_iota(i32, (N_EXPERTS, N_EXPERTS), 1)
    lower = jnp.where(ec < er, 1.0, 0.0).astype(bf16)
    pieces_b = jnp.broadcast_to(pieces, (N_EXPERTS, 128)).astype(bf16)
    start = PIECE * jnp.dot(lower, pieces_b, preferred_element_type=f32)[:, 0:1]
    base = before + start
    slot = jnp.concatenate(
        [jnp.sum(jnp.where(h, base, 0.0), axis=0, keepdims=True) for h in hots], axis=0).astype(i32)
    slot_ref[...] = slot

    for row0 in range(0, S_MAX, SORT_BLK):
        rows = min(SORT_BLK, S_MAX - row0)
        s_io = row0 + lax.broadcasted_iota(i32, (rows, n), 0)
        hit = s_io == slot[0:1, :]
        for k in range(1, TOP_K):
            hit = hit | (s_io == slot[k:k + 1, :])
        sel = jnp.where(hit, 1.0, 0.0).astype(bf16)
        srt_ref[row0:row0 + rows, :] = jnp.dot(sel, xb, preferred_element_type=f32).astype(bf16)


def _mixer_kernel(xp_in_ref, xs_in_ref, sc_ref, sp_ref, g1_ref, win_ref, cw_ref, pw_ref, ps_ref,
                  wout_ref, g2_ref, rwh_ref, rwl_ref, rb_ref, tri_ref,
                  x1_ref, srt_ref, slot_ref, gate_ref, cnt_ref,
                  npc_ref, npp_ref, nsc_ref, nsv_ref,
                  ubuf, vbuf, xcur, zcur):
    i = pl.program_id(0)

    @pl.when(i < N_PROMPT_TILES)
    def _():
        s = i % SEQ_TILES

        @pl.when(s == 0)
        def _():
            ubuf[0:HIST, :] = jnp.zeros((HIST, D_CONV), f32)
            vbuf[0:HIST, :] = jnp.zeros((HIST, D_POOL), f32)

        x = xp_in_ref[...]
        xcur[...] = x
        xn = _rms(x, g1_ref[...]).astype(bf16)
        proj = jnp.dot(xn, win_ref[...], preferred_element_type=f32)
        bg = proj[:, :D_CONV]
        u = proj[:, D_CONV:2 * D_CONV] * proj[:, 2 * D_CONV:3 * D_CONV]
        v = proj[:, 3 * D_CONV:]
        ubuf[HIST:HIST + TS, :] = u
        vbuf[HIST:HIST + TS, :] = v

        y = (ubuf[HIST - 2:HIST - 2 + TS, :] * cw_ref[0:1, :]
             + ubuf[HIST - 1:HIST - 1 + TS, :] * cw_ref[1:2, :]
             + u * cw_ref[2:3, :])
        zcur[:, :D_CONV] = bg * y

        pos = s * TS + lax.broadcasted_iota(i32, (TS, 1), 0)
        ps = []
        for g, w in enumerate(POOL_WINDOWS):
            sl = slice(g * POOL_GROUP, (g + 1) * POOL_GROUP)
            acc = vbuf[:, sl]
            for step in range(g + 1):
                acc = acc + pltpu.roll(acc, 1 << step, axis=0)
            cnt = jnp.minimum(w, pos + 1).astype(f32)
            ps.append(acc[HIST:, :] * (1.0 / cnt) - v[:, sl])
        zcur[:, D_CONV:] = _pool_project(jnp.concatenate(ps, axis=1), pw_ref, ps_ref)

        ubuf[HIST - 8:HIST, :] = ubuf[TS + HIST - 8:TS + HIST, :]
        vbuf[0:HIST, :] = vbuf[TS:TS + HIST, :]

        @pl.when(s == SEQ_TILES - 1)
        def _():
            npc_ref[0] = ubuf[HIST - 8:HIST, :]
            npp_ref[0] = vbuf[0:HIST, :]

    @pl.when(i == N_PROMPT_TILES)
    def _():
        x = xs_in_ref[...]
        xcur[...] = x
        xn = _rms(x, g1_ref[...]).astype(bf16)
        proj = jnp.dot(xn, win_ref[...], preferred_element_type=f32)
        nb = DEC_BATCH
        rows = lambda a, t: a[t * nb:(t + 1) * nb]
        bg = proj[:, :D_CONV]
        u = proj[:, D_CONV:2 * D_CONV] * proj[:, 2 * D_CONV:3 * D_CONV]
        v = proj[:, 3 * D_CONV:]
        up = [sc_ref[j] for j in range(CONV_W - 1)] + [rows(u, t) for t in range(DEC_SEQ)]
        vp = [sp_ref[j] for j in range(POOL_BUF)] + [rows(v, t) for t in range(DEC_SEQ)]
        zc, ps = [], []
        for t in range(DEC_SEQ):
            y = up[t] * cw_ref[0:1, :]
            for k in range(1, CONV_W):
                y = y + up[t + k] * cw_ref[k:k + 1, :]
            zc.append(rows(bg, t) * y)
            pg = []
            for g, w in enumerate(POOL_WINDOWS):
                sl = slice(g * POOL_GROUP, (g + 1) * POOL_GROUP)
                lo = t + POOL_BUF - w + 1
                acc = vp[lo][:, sl]
                for j in range(lo + 1, t + POOL_BUF + 1):
                    acc = acc + vp[j][:, sl]
                cnt = float(min(w, PAST_LEN + t + 1))
                pg.append(acc * (1.0 / cnt) - vp[t + POOL_BUF][:, sl])
            ps.append(jnp.concatenate(pg, axis=1))
        zcur[:, :D_CONV] = jnp.concatenate(zc, axis=0)
        zcur[:, D_CONV:] = _pool_project(jnp.concatenate(ps, axis=0), pw_ref, ps_ref)
        for j in range(CONV_W - 1):
            nsc_ref[j] = up[DEC_SEQ + j]
        for t in range(DEC_SEQ):
            nsv_ref[t] = rows(v, t)

    _route_and_sort(xcur[...], zcur[...], wout_ref, g2_ref, rwh_ref, rwl_ref, rb_ref, tri_ref,
                    x1_ref, srt_ref, slot_ref, gate_ref, cnt_ref)


def _mixer(xp, xs, sc_t, sp_t, g1, win, cw, pw, ps, wout, g2, rwh, rwl, rb, tri):
    const = lambda shape: pl.BlockSpec(shape, lambda i: (0,) * len(shape),
                                       pipeline_mode=pl.Buffered(1))
    ptile = lambda i: jnp.minimum(i, N_PROMPT_TILES - 1)
    pbatch = lambda i: jnp.minimum(i, N_PROMPT_TILES - 1) // SEQ_TILES
    out_shape = (
        jax.ShapeDtypeStruct((N_TOK, D_MODEL), f32),
        jax.ShapeDtypeStruct((N_SORT_ROWS, D_MODEL), bf16),
        jax.ShapeDtypeStruct((TOP_K, N_TOK), i32),
        jax.ShapeDtypeStruct((TOP_K, N_TOK), f32),
        jax.ShapeDtypeStruct((N_TILES * N_EXPERTS, 1), f32),
        jax.ShapeDtypeStruct((BATCH, 8, D_CONV), f32),
        jax.ShapeDtypeStruct((BATCH, HIST, D_POOL), f32),
        jax.ShapeDtypeStruct((CONV_W - 1, DEC_BATCH, D_CONV), f32),
        jax.ShapeDtypeStruct((DEC_SEQ, DEC_BATCH, D_POOL), f32),
    )
    out_specs = (
        pl.BlockSpec((TS, D_MODEL), lambda i: (i, 0)),
        pl.BlockSpec((S_MAX, D_MODEL), lambda i: (i, 0)),
        pl.BlockSpec((TOP_K, TS), lambda i: (0, i)),
        pl.BlockSpec((TOP_K, TS), lambda i: (0, i)),
        pl.BlockSpec((N_EXPERTS, 1), lambda i: (i, 0)),
        pl.BlockSpec((1, 8, D_CONV), lambda i: (pbatch(i), 0, 0)),
        pl.BlockSpec((1, HIST, D_POOL), lambda i: (pbatch(i), 0, 0)),
        pl.BlockSpec((CONV_W - 1, DEC_BATCH, D_CONV), lambda i: (0, 0, 0)),
        pl.BlockSpec((DEC_SEQ, DEC_BATCH, D_POOL), lambda i: (0, 0, 0)),
    )
    in_specs = [
        pl.BlockSpec((TS, D_MODEL), lambda i: (ptile(i), 0)),
        const((N_SAMPLE, D_MODEL)),
        const((CONV_W - 1, DEC_BATCH, D_CONV)),
        const((POOL_BUF, DEC_BATCH, D_POOL)),
        const((1, D_MODEL)),
        const((D_MODEL, D_IN)),
        const((CONV_W, D_CONV)),
        const((len(POOL_WINDOWS), POOL_GROUP, POOL_GROUP)),
        const((1, D_POOL)),
        const((D_MODEL, D_MODEL)),
        const((1, D_MODEL)),
        const((N_EXPERTS, D_MODEL)),
        const((N_EXPERTS, D_MODEL)),
        const((N_EXPERTS, 1)),
        const((TS, TS)),
    ]
    return pl.pallas_call(
        _mixer_kernel,
        grid=(N_TILES,),
        in_specs=in_specs,
        out_specs=out_specs,
        out_shape=out_shape,
        scratch_shapes=[
            pltpu.VMEM((TS + HIST, D_CONV), f32),
            pltpu.VMEM((TS + HIST, D_POOL), f32),
            pltpu.VMEM((TS, D_MODEL), f32),
            pltpu.VMEM((TS, D_MODEL), f32),
        ],
        compiler_params=pltpu.CompilerParams(
            dimension_semantics=("arbitrary",), vmem_limit_bytes=VMEM_LIMIT),
        name="mixer",
    )(xp, xs, sc_t, sp_t, g1, win, cw, pw, ps, wout, g2, rwh, rwl, rb, tri)


def _expert_kernel(te_ref, nu_ref, first_ref, ord_ref, nxt_ref, unit_ref, half_ref, pt_ref, tot_ref,
                   b1_ref, b2_ref, srt_hbm, w1_hbm, w2_hbm, ys_hbm,
                   xbuf, ybuf, zpiece, w1f, w2f, w1b, w2b, sem_w, sem_x, sem_y, sem_z):
    i = pl.program_id(0)
    n_used = nu_ref[0]
    live = i < n_used

    def weight_copies(e, slot):
        return (pltpu.make_async_copy(w1_hbm.at[e], w1f.at[slot], sem_w.at[slot]),
                pltpu.make_async_copy(w2_hbm.at[e], w2f.at[slot], sem_w.at[2 + slot]))

    def is_padding(entry, tile, p):
        if p < UNIT_PIECES:
            return entry < 0
        return (entry < 0) | (half_ref[tile] == 1)

    def start_x(tile, b):
        for p in range(PIECES):
            src = pt_ref[unit_ref[tile] * UNIT_PIECES + p]
            src = pl.multiple_of(jnp.where(is_padding(src, tile, p), ZERO_ROW, src), PIECE)
            pltpu.make_async_copy(srt_hbm.at[pl.ds(src, PIECE)],
                                  xbuf.at[b, pl.ds(p * PIECE, PIECE)], sem_x.at[b]).start()

    def start_y(tile, b, all_dump=False):
        for p in range(PIECES):
            dst = pt_ref[unit_ref[tile] * UNIT_PIECES + p]
            dump = is_padding(dst, tile, p) | all_dump
            dst = pl.multiple_of(jnp.where(dump, DUMP_ROW + b * TM + p * PIECE, dst), PIECE)
            pltpu.make_async_copy(ybuf.at[b, pl.ds(p * PIECE, PIECE)],
                                  ys_hbm.at[pl.ds(dst, PIECE)], sem_y.at[b]).start()

    def wait_x(b):
        pltpu.make_async_copy(srt_hbm.at[pl.ds(0, TM)], xbuf.at[b], sem_x.at[b]).wait()

    def wait_y(b):
        pltpu.make_async_copy(ybuf.at[b], ys_hbm.at[pl.ds(0, TM)], sem_y.at[b]).wait()

    @pl.when(i == 0)
    def _():
        zpiece[...] = jnp.zeros_like(zpiece)

        def fill_tile(t, c):
            def dst(p):
                row = pl.multiple_of(t * S_MAX + tot_ref[t] + p * PIECE, PIECE)
                return ys_hbm.at[pl.ds(row, PIECE)]

            def start(p, c2):
                pltpu.make_async_copy(zpiece, dst(p), sem_z.at[0]).start()
                return c2

            def wait(p, c2):
                pltpu.make_async_copy(zpiece, dst(p), sem_z.at[0]).wait()
                return c2
            n_fill = (S_MAX - tot_ref[t]) // PIECE
            lax.fori_loop(0, n_fill, start, 0)
            return lax.fori_loop(0, n_fill, wait, c)
        lax.fori_loop(0, N_TILES, fill_tile, 0)
        start_x(0, 0)
        start_x(jnp.minimum(1, n_used - 1), 1)
        ybuf[...] = jnp.zeros_like(ybuf)
        for b in range(2):
            pltpu.make_async_copy(ybuf.at[b], ys_hbm.at[pl.ds(DUMP_ROW + b * TM, TM)],
                                  sem_y.at[b]).start()

    @pl.when(live)
    def _():
        ring = i % 3
        expert = te_ref[i]
        wait_x(ring)

        @pl.when(first_ref[i] == 1)
        def _():
            slot = ord_ref[i] % 2

            @pl.when(i == 0)
            def _():
                for cp in weight_copies(te_ref[0], 0):
                    cp.start()
            for cp in weight_copies(expert, slot):
                cp.wait()

            @pl.when(nxt_ref[i] >= 0)
            def _():
                for cp in weight_copies(nxt_ref[i], 1 - slot):
                    cp.start()
            w1b[...] = w1f[slot].astype(bf16)
            w2b[...] = w2f[slot].astype(bf16)

        wait_y(ring)

        def mlp(rows):
            x = xbuf[ring, 0:rows, :]
            gu = jnp.dot(x, w1b[...], preferred_element_type=f32) + b1_ref[expert]
            gate = jnp.minimum(gu[:, :D_FF], SWIGLU_LIMIT)
            lin = jnp.clip(gu[:, D_FF:], -SWIGLU_LIMIT, SWIGLU_LIMIT)
            glu = gate * jax.nn.sigmoid(SWIGLU_ALPHA * gate)
            h = (glu * (lin + 1.0)).astype(bf16)
            start_x(jnp.minimum(i + 2, n_used - 1), (i + 2) % 3)
            start_y(jnp.maximum(i - 1, 0), (i + 2) % 3, all_dump=i == 0)
            out = jnp.dot(h, w2b[...], preferred_element_type=f32) + b2_ref[expert]
            ybuf[ring, 0:rows, :] = out.astype(bf16)

        @pl.when(half_ref[i] == 0)
        def _():
            mlp(TM)

        @pl.when(half_ref[i] == 1)
        def _():
            mlp(UNIT)

        @pl.when(i == n_used - 1)
        def _():
            start_y(i, ring)
            for b in range(3):
                wait_y(b)
            wait_x((i + 1) % 3)
            wait_x((i + 2) % 3)


def _experts(tile_expert, n_used, first, order, nxt, unit0, half, piece_row, tot, srt, w1, b1, w2, b2):
    whole = lambda shape: pl.BlockSpec(shape, lambda i, *_: (0,) * len(shape))
    grid_spec = pltpu.PrefetchScalarGridSpec(
        num_scalar_prefetch=9,
        grid=(N_ROW_TILES,),
        in_specs=[
            whole((N_EXPERTS, 1, 2 * D_FF)),
            whole((N_EXPERTS, 1, D_MODEL)),
            pl.BlockSpec(memory_space=pl.ANY),
            pl.BlockSpec(memory_space=pl.ANY),
            pl.BlockSpec(memory_space=pl.ANY),
        ],
        out_specs=pl.BlockSpec(memory_space=pl.ANY),
        scratch_shapes=[
            pltpu.VMEM((3, TM, D_MODEL), bf16), pltpu.VMEM((3, TM, D_MODEL), bf16),
            pltpu.VMEM((PIECE, D_MODEL), bf16),
            pltpu.VMEM((2, D_MODEL, 2 * D_FF), f32), pltpu.VMEM((2, D_FF, D_MODEL), f32),
            pltpu.VMEM((D_MODEL, 2 * D_FF), bf16), pltpu.VMEM((D_FF, D_MODEL), bf16),
            pltpu.SemaphoreType.DMA((4,)), pltpu.SemaphoreType.DMA((3,)),
            pltpu.SemaphoreType.DMA((3,)), pltpu.SemaphoreType.DMA((1,)),
        ],
    )
    return pl.pallas_call(
        _expert_kernel,
        grid_spec=grid_spec,
        out_shape=jax.ShapeDtypeStruct((N_SORT_ROWS + 3 * TM, D_MODEL), bf16),
        compiler_params=pltpu.CompilerParams(
            dimension_semantics=("arbitrary",), vmem_limit_bytes=VMEM_LIMIT),
        name="experts",
    )(tile_expert, n_used, first, order, nxt, unit0, half, piece_row, tot, b1, b2, srt, w1, w2)


def _combine_kernel(slot_ref, g_ref, x1_ref, gf_ref, ys_ref, yp_ref, ysm_ref, wgt):
    i = pl.program_id(0)

    slot = slot_ref[...]
    g = g_ref[...]
    for b in range(S_MAX // S_BLK):
        s_io = b * S_BLK + lax.broadcasted_iota(i32, (TS, S_BLK), 1)
        w = jnp.zeros((TS, S_BLK), f32)
        for k in range(TOP_K):
            w = jnp.where(s_io == slot[:, k:k + 1], g[:, k:k + 1], w)
        wgt[:, b * S_BLK:(b + 1) * S_BLK] = w.astype(bf16)

    y = x1_ref[...] + jnp.dot(wgt[...], ys_ref[...], preferred_element_type=f32)
    out = _rms(y, gf_ref[...])

    @pl.when(i < N_PROMPT_TILES)
    def _():
        yp_ref[...] = out

    @pl.when(i == N_PROMPT_TILES)
    def _():
        ysm_ref[...] = out


def _combine(slot_tm, gates_tm, x1, gf, ys):
    ptile = lambda i: (jnp.minimum(i, N_PROMPT_TILES - 1), 0)
    return pl.pallas_call(
        _combine_kernel,
        grid=(N_TILES,),
        in_specs=[
            pl.BlockSpec((TS, TOP_K), lambda i: (i, 0)),
            pl.BlockSpec((TS, TOP_K), lambda i: (i, 0)),
            pl.BlockSpec((TS, D_MODEL), lambda i: (i, 0)),
            pl.BlockSpec((1, D_MODEL), lambda i: (0, 0)),
            pl.BlockSpec((S_MAX, D_MODEL), lambda i: (i, 0)),
        ],
        out_specs=(
            pl.BlockSpec((TS, D_MODEL), ptile),
            pl.BlockSpec((N_SAMPLE, D_MODEL), lambda i: (0, 0)),
        ),
        out_shape=(
            jax.ShapeDtypeStruct((N_PROMPT, D_MODEL), f32),
            jax.ShapeDtypeStruct((N_SAMPLE, D_MODEL), f32),
        ),
        scratch_shapes=[pltpu.VMEM((TS, S_MAX), bf16)],
        compiler_params=pltpu.CompilerParams(
            dimension_semantics=("arbitrary",), vmem_limit_bytes=VMEM_LIMIT),
        name="combine",
    )(slot_tm, gates_tm, x1, gf, ys)


def _routing_tables(cnt):
    n = cnt.reshape(N_TILES, N_EXPERTS).astype(i32)
    c = (n + (PIECE - 1)) // PIECE * PIECE
    lo = jnp.cumsum(c, axis=1) - c
    tot = jnp.sum(c, axis=1)
    group = jnp.sum(c, axis=0)
    padded = (group + (UNIT - 1)) // UNIT * UNIT
    gend = jnp.cumsum(padded)
    gstart = gend - padded
    cs = gstart[None, :] + jnp.cumsum(c, axis=0) - c
    units = padded // UNIT
    tiles = (units + 1) // 2
    tend = jnp.cumsum(tiles)
    n_used = tend[-1]
    tile_id = jnp.arange(N_ROW_TILES, dtype=i32)
    tile_expert = jnp.sum((tile_id[:, None] >= tend[None, :]).astype(i32), axis=1)
    tile_expert = jnp.minimum(tile_expert, jnp.take(tile_expert, n_used - 1))
    e_ar = jnp.arange(N_EXPERTS, dtype=i32)
    mine = tile_expert[:, None] == e_ar[None, :]
    pick = lambda v: jnp.sum(jnp.where(mine, v[None, :], 0), axis=1)
    in_group = tile_id - pick(tend - tiles)
    unit0 = pick(gstart // UNIT) + 2 * in_group
    half = (2 * in_group + 1 == pick(units)).astype(i32)
    first = (tile_id < n_used) & ((tile_id == 0) | (tile_expert != jnp.roll(tile_expert, 1)))
    order = jnp.maximum(jnp.cumsum(first.astype(i32)) - 1, 0)
    later = (e_ar[None, :] > e_ar[:, None]) & (padded[None, :] > 0)
    nxt_e = jnp.min(jnp.where(later, e_ar[None, :], N_EXPERTS), axis=1)
    nxt_e = jnp.where(nxt_e == N_EXPERTS, -1, nxt_e)
    nxt = pick(nxt_e)
    cs_f = cs.T.reshape(-1)
    c_f = c.T.reshape(-1)
    src_f = (jnp.arange(N_TILES, dtype=i32)[:, None] * S_MAX + lo).T.reshape(-1)
    piece_r = jnp.arange((N_UNITS + 1) * UNIT_PIECES, dtype=i32) * PIECE
    reached = (piece_r[:, None] >= cs_f[None, 1:]).astype(i32)

    def lookup(f):
        return f[0] + jnp.sum(reached * (f[1:] - f[:-1])[None, :], axis=1)
    off = piece_r - lookup(cs_f)
    piece_row = jnp.where(off < lookup(c_f), lookup(src_f) + off, -1)
    return dict(tot=tot, n_used=n_used[None], tile_expert=tile_expert, first=first.astype(i32),
                order=order, nxt=nxt, unit0=unit0, half=half, piece_row=piece_row)


def kernel(x_prompt, x_sample, state_conv, state_pool, norm_mix_g, w_in, conv_w, pool_w, pool_scale,
           w_out, norm_ffn_g, router_w, router_b, exp_w1, exp_b1, exp_w2, exp_b2, final_norm_g):
    l = 0
    xp = x_prompt.reshape(N_PROMPT, D_MODEL)
    xs = jnp.transpose(x_sample, (1, 0, 2)).reshape(N_SAMPLE, D_MODEL)
    sc_t = jnp.transpose(state_conv[l], (1, 0, 2))
    sp_t = jnp.transpose(state_pool[l], (1, 0, 2))
    rw_t = router_w[l].T
    rwh = rw_t.astype(bf16)
    rwl = (rw_t - rwh.astype(f32)).astype(bf16)
    tok = jnp.arange(TS, dtype=i32)
    tri = (tok[:, None] < tok[None, :]).astype(bf16)

    (x1, srt, slot_t, gate_t, cnt, npc, npp, nsc, nsv) = _mixer(
        xp, xs, sc_t, sp_t, norm_mix_g[l][None, :], w_in[l].astype(bf16), conv_w[l],
        pool_w[l].astype(bf16), pool_scale[l][None, :], w_out[l].astype(bf16),
        norm_ffn_g[l][None, :], rwh, rwl, router_b[l][:, None], tri)

    t = _routing_tables(cnt)
    ys = _experts(t["tile_expert"], t["n_used"], t["first"], t["order"], t["nxt"], t["unit0"],
                  t["half"], t["piece_row"], t["tot"], srt, exp_w1[l], exp_b1[l][:, None, :],
                  exp_w2[l], exp_b2[l][:, None, :])
    y_p, y_s = _combine(slot_t.T, gate_t.T, x1, final_norm_g[None, :], ys)

    y_prompt = y_p.reshape(BATCH, SEQ, D_MODEL)
    y_sample = jnp.transpose(y_s.reshape(DEC_SEQ, DEC_BATCH, D_MODEL), (1, 0, 2))
    new_conv_prompt = npc[None, :, 8 - (CONV_W - 1):, :]
    new_pool_prompt = npp[None, :, HIST - POOL_BUF:, :]
    new_conv_sample = jnp.transpose(nsc, (1, 0, 2))[None]
    new_pool_sample = jnp.concatenate(
        [state_pool[l][:, DEC_SEQ:, :], jnp.transpose(nsv, (1, 0, 2))], axis=1)[None]
    return (y_prompt, y_sample, new_conv_prompt, new_pool_prompt, new_conv_sample, new_pool_sample)
```

```python
import jax
import jax.numpy as jnp
from jax import lax
from jax.experimental import pallas as pl
from jax.experimental.pallas import tpu as pltpu

D_MODEL = 1024
D_CONV = 512
D_POOL = 512
D_IN = 3 * D_CONV + D_POOL
CONV_W = 3
POOL_WINDOWS = (2, 4, 8, 16)
POOL_GROUP = 128
POOL_BUF = 15
N_EXPERTS = 32
TOP_K = 4
D_FF = 1024
SWIGLU_LIMIT = 7.0
SWIGLU_ALPHA = 1.702
EPS = 1e-5
PAST_LEN = 16384

BATCH, SEQ = 8, 2048
DEC_BATCH, DEC_SEQ = 128, 4
N_PROMPT = BATCH * SEQ
N_SAMPLE = DEC_BATCH * DEC_SEQ
N_TOK = N_PROMPT + N_SAMPLE

TS = 512
SEQ_TILES = SEQ // TS
N_PROMPT_TILES = N_PROMPT // TS
N_TILES = N_PROMPT_TILES + 1
HIST = 16
UNIT = 256
TM = 2 * UNIT
PIECE = 8
PIECES = TM // PIECE
UNIT_PIECES = UNIT // PIECE
S_BLK = 256
SORT_BLK = 512
S_MAX = (TS * TOP_K + N_EXPERTS * (PIECE - 1) + PIECE + S_BLK - 1) // S_BLK * S_BLK
ZERO_ROW = S_MAX - PIECE
N_UNITS = (N_TOK * TOP_K + N_TILES * N_EXPERTS * (PIECE - 1) + N_EXPERTS * (UNIT - PIECE)) // UNIT
N_ROW_TILES = (N_UNITS + N_EXPERTS) // 2
N_SORT_ROWS = N_TILES * S_MAX
DUMP_ROW = N_SORT_ROWS

VMEM_LIMIT = 56 * 1024 * 1024

f32 = jnp.float32
bf16 = jnp.bfloat16
i32 = jnp.int32


def _rms(x, g):
    return x * lax.rsqrt(jnp.mean(x * x, axis=-1, keepdims=True) + EPS) * g


def _pool_project(p, pw_ref, ps_ref):
    outs = []
    for g in range(len(POOL_WINDOWS)):
        sl = slice(g * POOL_GROUP, (g + 1) * POOL_GROUP)
        outs.append(jnp.dot(p[:, sl].astype(bf16), pw_ref[g], preferred_element_type=f32))
    return jnp.concatenate(outs, axis=1) * ps_ref[...]


def _route_and_sort(x, z, wout_ref, g2_ref, rw2_ref, rb_ref, tri_ref,
                    x1_ref, srt_ref, route_ref, cnt_ref):
    n = x.shape[0]
    x1 = x + jnp.dot(z.astype(bf16), wout_ref[...], preferred_element_type=f32)
    x1_ref[...] = x1
    xn = _rms(x1, g2_ref[...])
    xb = xn.astype(bf16)

    xlo = (xn - xb.astype(f32)).astype(bf16)
    nt = (((1,), (1,)), ((), ()))
    hi_lo = lax.dot_general(rw2_ref[...], xb, nt, preferred_element_type=f32)
    logits = (hi_lo[:N_EXPERTS] + hi_lo[N_EXPERTS:]
              + lax.dot_general(rw2_ref[0:N_EXPERTS, :], xlo, nt, preferred_element_type=f32)
              + rb_ref[...])

    e_iota = lax.broadcasted_iota(i32, (N_EXPERTS, n), 0)
    work = logits
    hots, vals = [], []
    for _ in range(TOP_K):
        m = jnp.max(work, axis=0, keepdims=True)
        ik = jnp.min(jnp.where(work == m, e_iota, N_EXPERTS), axis=0, keepdims=True)
        hot = e_iota == ik
        work = jnp.where(hot, -jnp.inf, work)
        hots.append(hot); vals.append(m)
    exps = [jnp.exp(v - vals[0]) for v in vals]
    den = exps[0] + exps[1] + exps[2] + exps[3]
    gates = [e / den for e in exps]

    multi = jnp.where(hots[0] | hots[1] | hots[2] | hots[3], 1.0, 0.0)
    before = jnp.dot(multi.astype(bf16), tri_ref[...], preferred_element_type=f32)
    count = jnp.sum(multi, axis=1, keepdims=True)
    cnt_ref[...] = count
    pieces = jnp.floor((count + (PIECE - 1)) * (1.0 / PIECE))
    er = lax.broadcasted_iota(i32, (N_EXPERTS, N_EXPERTS), 0)
    ec = lax.broadcasted_iota(i32, (N_EXPERTS, N_EXPERTS), 1)
    lower = jnp.where(ec < er, 1.0, 0.0).astype(bf16)
    pieces_b = jnp.broadcast_to(pieces, (N_EXPERTS, 128)).astype(bf16)
    start = PIECE * jnp.dot(lower, pieces_b, preferred_element_type=f32)[:, 0:1]
    base = before + start
    slot = jnp.concatenate(
        [jnp.sum(jnp.where(h, base, 0.0), axis=0, keepdims=True) for h in hots], axis=0).astype(i32)
    route_ref[...] = jnp.concatenate([slot.astype(f32)] + gates, axis=0)

    for row0 in range(0, S_MAX, SORT_BLK):
        rows = min(SORT_BLK, S_MAX - row0)
        s_io = row0 + lax.broadcasted_iota(i32, (rows, n), 0)
        hit = s_io == slot[0:1, :]
        for k in range(1, TOP_K):
            hit = hit | (s_io == slot[k:k + 1, :])
        sel = jnp.where(hit, 1.0, 0.0).astype(bf16)
        srt_ref[row0:row0 + rows, :] = jnp.dot(sel, xb, preferred_element_type=f32).astype(bf16)


def _mixer_kernel(xp_in_ref, xs_in_ref, sc_ref, sp_ref, g1_ref, win_ref, cw_ref, pw_ref, ps_ref,
                  wout_ref, g2_ref, rw2_ref, rb_ref, tri_ref,
                  x1_ref, srt_ref, route_ref, cnt_ref,
                  npc_ref, npp_ref, nsc_ref, nsv_ref,
                  ubuf, vbuf, xcur, zcur):
    i = pl.program_id(0)

    @pl.when(i < N_PROMPT_TILES)
    def _():
        s = i % SEQ_TILES

        @pl.when(s == 0)
        def _():
            ubuf[0:HIST, :] = jnp.zeros((HIST, D_CONV), f32)
            vbuf[0:HIST, :] = jnp.zeros((HIST, D_POOL), f32)

        x = xp_in_ref[...]
        xcur[...] = x
        xn = _rms(x, g1_ref[...]).astype(bf16)
        proj = jnp.dot(xn, win_ref[...], preferred_element_type=f32)
        bg = proj[:, :D_CONV]
        u = proj[:, D_CONV:2 * D_CONV] * proj[:, 2 * D_CONV:3 * D_CONV]
        v = proj[:, 3 * D_CONV:]
        ubuf[HIST:HIST + TS, :] = u
        vbuf[HIST:HIST + TS, :] = v

        y = (ubuf[HIST - 2:HIST - 2 + TS, :] * cw_ref[0:1, :]
             + ubuf[HIST - 1:HIST - 1 + TS, :] * cw_ref[1:2, :]
             + u * cw_ref[2:3, :])
        zcur[:, :D_CONV] = bg * y

        pos = s * TS + lax.broadcasted_iota(i32, (TS, 1), 0)
        ps = []
        for g, w in enumerate(POOL_WINDOWS):
            sl = slice(g * POOL_GROUP, (g + 1) * POOL_GROUP)
            acc = vbuf[:, sl]
            for step in range(g + 1):
                acc = acc + pltpu.roll(acc, 1 << step, axis=0)
            cnt = jnp.minimum(w, pos + 1).astype(f32)
            ps.append(acc[HIST:, :] * (1.0 / cnt) - v[:, sl])
        zcur[:, D_CONV:] = _pool_project(jnp.concatenate(ps, axis=1), pw_ref, ps_ref)

        ubuf[HIST - 8:HIST, :] = ubuf[TS + HIST - 8:TS + HIST, :]
        vbuf[0:HIST, :] = vbuf[TS:TS + HIST, :]

        @pl.when(s == SEQ_TILES - 1)
        def _():
            npc_ref[0] = ubuf[HIST - 8:HIST, :]
            npp_ref[0] = vbuf[0:HIST, :]

    @pl.when(i == N_PROMPT_TILES)
    def _():
        x = xs_in_ref[...]
        xcur[...] = x
        xn = _rms(x, g1_ref[...]).astype(bf16)
        proj = jnp.dot(xn, win_ref[...], preferred_element_type=f32)
        nb = DEC_BATCH
        rows = lambda a, t: a[t * nb:(t + 1) * nb]
        bg = proj[:, :D_CONV]
        u = proj[:, D_CONV:2 * D_CONV] * proj[:, 2 * D_CONV:3 * D_CONV]
        v = proj[:, 3 * D_CONV:]
        up = [sc_ref[j] for j in range(CONV_W - 1)] + [rows(u, t) for t in range(DEC_SEQ)]
        vp = [sp_ref[j] for j in range(POOL_BUF)] + [rows(v, t) for t in range(DEC_SEQ)]
        zc, ps = [], []
        for t in range(DEC_SEQ):
            y = up[t] * cw_ref[0:1, :]
            for k in range(1, CONV_W):
                y = y + up[t + k] * cw_ref[k:k + 1, :]
            zc.append(rows(bg, t) * y)
            pg = []
            for g, w in enumerate(POOL_WINDOWS):
                sl = slice(g * POOL_GROUP, (g + 1) * POOL_GROUP)
                lo = t + POOL_BUF - w + 1
                acc = vp[lo][:, sl]
                for j in range(lo + 1, t + POOL_BUF + 1):
                    acc = acc + vp[j][:, sl]
                cnt = float(min(w, PAST_LEN + t + 1))
                pg.append(acc * (1.0 / cnt) - vp[t + POOL_BUF][:, sl])
            ps.append(jnp.concatenate(pg, axis=1))
        zcur[:, :D_CONV] = jnp.concatenate(zc, axis=0)
        zcur[:, D_CONV:] = _pool_project(jnp.concatenate(ps, axis=0), pw_ref, ps_ref)
        for j in range(CONV_W - 1):
            nsc_ref[j] = up[DEC_SEQ + j]
        for t in range(DEC_SEQ):
            nsv_ref[t] = rows(v, t)

    _route_and_sort(xcur[...], zcur[...], wout_ref, g2_ref, rw2_ref, rb_ref, tri_ref,
                    x1_ref, srt_ref, route_ref, cnt_ref)


def _mixer(xp, xs, sc_t, sp_t, g1, win, cw, pw, ps, wout, g2, rw2, rb, tri):
    const = lambda shape: pl.BlockSpec(shape, lambda i: (0,) * len(shape),
                                       pipeline_mode=pl.Buffered(1))
    ptile = lambda i: jnp.minimum(i, N_PROMPT_TILES - 1)
    pbatch = lambda i: jnp.minimum(i, N_PROMPT_TILES - 1) // SEQ_TILES
    out_shape = (
        jax.ShapeDtypeStruct((N_TOK, D_MODEL), f32),
        jax.ShapeDtypeStruct((N_SORT_ROWS, D_MODEL), bf16),
        jax.ShapeDtypeStruct((2 * TOP_K, N_TOK), f32),
        jax.ShapeDtypeStruct((N_TILES * N_EXPERTS, 1), f32),
        jax.ShapeDtypeStruct((BATCH, 8, D_CONV), f32),
        jax.ShapeDtypeStruct((BATCH, HIST, D_POOL), f32),
        jax.ShapeDtypeStruct((CONV_W - 1, DEC_BATCH, D_CONV), f32),
        jax.ShapeDtypeStruct((DEC_SEQ, DEC_BATCH, D_POOL), f32),
    )
    out_specs = (
        pl.BlockSpec((TS, D_MODEL), lambda i: (i, 0)),
        pl.BlockSpec((S_MAX, D_MODEL), lambda i: (i, 0)),
        pl.BlockSpec((2 * TOP_K, TS), lambda i: (0, i)),
        pl.BlockSpec((N_EXPERTS, 1), lambda i: (i, 0)),
        pl.BlockSpec((1, 8, D_CONV), lambda i: (pbatch(i), 0, 0)),
        pl.BlockSpec((1, HIST, D_POOL), lambda i: (pbatch(i), 0, 0)),
        pl.BlockSpec((CONV_W - 1, DEC_BATCH, D_CONV), lambda i: (0, 0, 0)),
        pl.BlockSpec((DEC_SEQ, DEC_BATCH, D_POOL), lambda i: (0, 0, 0)),
    )
    in_specs = [
        pl.BlockSpec((TS, D_MODEL), lambda i: (ptile(i), 0)),
        const((N_SAMPLE, D_MODEL)),
        const((CONV_W - 1, DEC_BATCH, D_CONV)),
        const((POOL_BUF, DEC_BATCH, D_POOL)),
        const((1, D_MODEL)),
        const((D_MODEL, D_IN)),
        const((CONV_W, D_CONV)),
        const((len(POOL_WINDOWS), POOL_GROUP, POOL_GROUP)),
        const((1, D_POOL)),
        const((D_MODEL, D_MODEL)),
        const((1, D_MODEL)),
        const((2 * N_EXPERTS, D_MODEL)),
        const((N_EXPERTS, 1)),
        const((TS, TS)),
    ]
    return pl.pallas_call(
        _mixer_kernel,
        grid=(N_TILES,),
        in_specs=in_specs,
        out_specs=out_specs,
        out_shape=out_shape,
        scratch_shapes=[
            pltpu.VMEM((TS + HIST, D_CONV), f32),
            pltpu.VMEM((TS + HIST, D_POOL), f32),
            pltpu.VMEM((TS, D_MODEL), f32),
            pltpu.VMEM((TS, D_MODEL), f32),
        ],
        compiler_params=pltpu.CompilerParams(
            dimension_semantics=("arbitrary",), vmem_limit_bytes=VMEM_LIMIT),
        name="mixer",
    )(xp, xs, sc_t, sp_t, g1, win, cw, pw, ps, wout, g2, rw2, rb, tri)


def _expert_kernel(te_ref, nu_ref, first_ref, ord_ref, nxt_ref, unit_ref, half_ref, pt_ref, tot_ref,
                   b1_ref, b2_ref, srt_hbm, w1_hbm, w2_hbm, ys_hbm,
                   xbuf, ybuf, zpiece, w1f, w2f, w1b, w2b, sem_w, sem_x, sem_y, sem_z):
    i = pl.program_id(0)
    n_used = nu_ref[0]
    live = i < n_used

    def weight_copies(e, slot):
        return (pltpu.make_async_copy(w1_hbm.at[e], w1f.at[slot], sem_w.at[slot]),
                pltpu.make_async_copy(w2_hbm.at[e], w2f.at[slot], sem_w.at[2 + slot]))

    def is_padding(entry, tile, p):
        if p < UNIT_PIECES:
            return entry < 0
        return (entry < 0) | (half_ref[tile] == 1)

    def start_x(tile, b):
        for p in range(PIECES):
            src = pt_ref[unit_ref[tile] * UNIT_PIECES + p]
            src = pl.multiple_of(jnp.where(is_padding(src, tile, p), ZERO_ROW, src), PIECE)
            pltpu.make_async_copy(srt_hbm.at[pl.ds(src, PIECE)],
                                  xbuf.at[b, pl.ds(p * PIECE, PIECE)], sem_x.at[b]).start()

    def start_y(tile, b, all_dump=False):
        for p in range(PIECES):
            dst = pt_ref[unit_ref[tile] * UNIT_PIECES + p]
            dump = is_padding(dst, tile, p) | all_dump
            dst = pl.multiple_of(jnp.where(dump, DUMP_ROW + b * TM + p * PIECE, dst), PIECE)
            pltpu.make_async_copy(ybuf.at[b, pl.ds(p * PIECE, PIECE)],
                                  ys_hbm.at[pl.ds(dst, PIECE)], sem_y.at[b]).start()

    def wait_x(b):
        pltpu.make_async_copy(srt_hbm.at[pl.ds(0, TM)], xbuf.at[b], sem_x.at[b]).wait()

    def wait_y(b):
        pltpu.make_async_copy(ybuf.at[b], ys_hbm.at[pl.ds(0, TM)], sem_y.at[b]).wait()

    @pl.when(i == 0)
    def _():
        zpiece[...] = jnp.zeros_like(zpiece)

        def fill_tile(t, c):
            def dst(p):
                row = pl.multiple_of(t * S_MAX + tot_ref[t] + p * PIECE, PIECE)
                return ys_hbm.at[pl.ds(row, PIECE)]

            def start(p, c2):
                pltpu.make_async_copy(zpiece, dst(p), sem_z.at[0]).start()
                return c2
            return lax.fori_loop(0, (S_MAX - tot_ref[t]) // PIECE, start, c)
        lax.fori_loop(0, N_TILES, fill_tile, 0)
        start_x(0, 0)
        start_x(jnp.minimum(1, n_used - 1), 1)
        ybuf[...] = jnp.zeros_like(ybuf)
        for b in range(2):
            pltpu.make_async_copy(ybuf.at[b], ys_hbm.at[pl.ds(DUMP_ROW + b * TM, TM)],
                                  sem_y.at[b]).start()

    @pl.when(live)
    def _():
        ring = i % 3
        expert = te_ref[i]
        wait_x(ring)

        @pl.when(first_ref[i] == 1)
        def _():
            slot = ord_ref[i] % 2

            @pl.when(i == 0)
            def _():
                for cp in weight_copies(te_ref[0], 0):
                    cp.start()
            for cp in weight_copies(expert, slot):
                cp.wait()

            @pl.when(nxt_ref[i] >= 0)
            def _():
                for cp in weight_copies(nxt_ref[i], 1 - slot):
                    cp.start()
            w1b[...] = w1f[slot].astype(bf16)
            w2b[...] = w2f[slot].astype(bf16)

        wait_y(ring)

        def mlp(rows):
            x = xbuf[ring, 0:rows, :]
            gu = jnp.dot(x, w1b[...], preferred_element_type=f32) + b1_ref[expert]
            gate = jnp.minimum(gu[:, :D_FF], SWIGLU_LIMIT)
            lin = jnp.clip(gu[:, D_FF:], -SWIGLU_LIMIT, SWIGLU_LIMIT)
            glu = gate * jax.nn.sigmoid(SWIGLU_ALPHA * gate)
            h = (glu * (lin + 1.0)).astype(bf16)
            start_x(jnp.minimum(i + 2, n_used - 1), (i + 2) % 3)
            start_y(jnp.maximum(i - 1, 0), (i + 2) % 3, all_dump=i == 0)
            out = jnp.dot(h, w2b[...], preferred_element_type=f32) + b2_ref[expert]
            ybuf[ring, 0:rows, :] = out.astype(bf16)

        @pl.when(half_ref[i] == 0)
        def _():
            mlp(TM)

        @pl.when(half_ref[i] == 1)
        def _():
            mlp(UNIT)

        @pl.when(i == n_used - 1)
        def _():
            start_y(i, ring)
            for b in range(3):
                wait_y(b)
            wait_x((i + 1) % 3)
            wait_x((i + 2) % 3)

            def drain_tile(t, c):
                def wait(p, c2):
                    pltpu.make_async_copy(zpiece, ys_hbm.at[pl.ds(0, PIECE)], sem_z.at[0]).wait()
                    return c2
                return lax.fori_loop(0, (S_MAX - tot_ref[t]) // PIECE, wait, c)
            lax.fori_loop(0, N_TILES, drain_tile, 0)


def _experts(tile_expert, n_used, first, order, nxt, unit0, half, piece_row, tot, srt, w1, b1, w2, b2):
    whole = lambda shape: pl.BlockSpec(shape, lambda i, *_: (0,) * len(shape))
    grid_spec = pltpu.PrefetchScalarGridSpec(
        num_scalar_prefetch=9,
        grid=(N_ROW_TILES,),
        in_specs=[
            whole((N_EXPERTS, 1, 2 * D_FF)),
            whole((N_EXPERTS, 1, D_MODEL)),
            pl.BlockSpec(memory_space=pl.ANY),
            pl.BlockSpec(memory_space=pl.ANY),
            pl.BlockSpec(memory_space=pl.ANY),
        ],
        out_specs=pl.BlockSpec(memory_space=pl.ANY),
        scratch_shapes=[
            pltpu.VMEM((3, TM, D_MODEL), bf16), pltpu.VMEM((3, TM, D_MODEL), bf16),
            pltpu.VMEM((PIECE, D_MODEL), bf16),
            pltpu.VMEM((2, D_MODEL, 2 * D_FF), f32), pltpu.VMEM((2, D_FF, D_MODEL), f32),
            pltpu.VMEM((D_MODEL, 2 * D_FF), bf16), pltpu.VMEM((D_FF, D_MODEL), bf16),
            pltpu.SemaphoreType.DMA((4,)), pltpu.SemaphoreType.DMA((3,)),
            pltpu.SemaphoreType.DMA((3,)), pltpu.SemaphoreType.DMA((1,)),
        ],
    )
    return pl.pallas_call(
        _expert_kernel,
        grid_spec=grid_spec,
        out_shape=jax.ShapeDtypeStruct((N_SORT_ROWS + 3 * TM, D_MODEL), bf16),
        compiler_params=pltpu.CompilerParams(
            dimension_semantics=("arbitrary",), vmem_limit_bytes=VMEM_LIMIT),
        name="experts",
    )(tile_expert, n_used, first, order, nxt, unit0, half, piece_row, tot, b1, b2, srt, w1, w2)


def _combine_kernel(route_ref, x1_ref, gf_ref, ys_ref, yp_ref, ysm_ref, wgt):
    i = pl.program_id(0)

    slot = route_ref[:, 0:TOP_K].astype(i32)
    g = route_ref[:, TOP_K:2 * TOP_K]
    for b in range(S_MAX // S_BLK):
        s_io = b * S_BLK + lax.broadcasted_iota(i32, (TS, S_BLK), 1)
        w = jnp.zeros((TS, S_BLK), f32)
        for k in range(TOP_K):
            w = jnp.where(s_io == slot[:, k:k + 1], g[:, k:k + 1], w)
        wgt[:, b * S_BLK:(b + 1) * S_BLK] = w.astype(bf16)

    y = x1_ref[...] + jnp.dot(wgt[...], ys_ref[...], preferred_element_type=f32)
    out = _rms(y, gf_ref[...])

    @pl.when(i < N_PROMPT_TILES)
    def _():
        yp_ref[...] = out

    @pl.when(i == N_PROMPT_TILES)
    def _():
        ysm_ref[...] = out


def _combine(route_tm, x1, gf, ys):
    ptile = lambda i: (jnp.minimum(i, N_PROMPT_TILES - 1), 0)
    return pl.pallas_call(
        _combine_kernel,
        grid=(N_TILES,),
        in_specs=[
            pl.BlockSpec((TS, 2 * TOP_K), lambda i: (i, 0)),
            pl.BlockSpec((TS, D_MODEL), lambda i: (i, 0)),
            pl.BlockSpec((1, D_MODEL), lambda i: (0, 0)),
            pl.BlockSpec((S_MAX, D_MODEL), lambda i: (i, 0)),
        ],
        out_specs=(
            pl.BlockSpec((TS, D_MODEL), ptile),
            pl.BlockSpec((N_SAMPLE, D_MODEL), lambda i: (0, 0)),
        ),
        out_shape=(
            jax.ShapeDtypeStruct((N_PROMPT, D_MODEL), f32),
            jax.ShapeDtypeStruct((N_SAMPLE, D_MODEL), f32),
        ),
        scratch_shapes=[pltpu.VMEM((TS, S_MAX), bf16)],
        compiler_params=pltpu.CompilerParams(
            dimension_semantics=("arbitrary",), vmem_limit_bytes=VMEM_LIMIT),
        name="combine",
    )(route_tm, x1, gf, ys)


def _routing_tables(cnt):
    n = cnt.reshape(N_TILES, N_EXPERTS).astype(i32)
    c = (n + (PIECE - 1)) // PIECE * PIECE
    lo = jnp.cumsum(c, axis=1) - c
    tot = jnp.sum(c, axis=1)
    group = jnp.sum(c, axis=0)
    padded = (group + (UNIT - 1)) // UNIT * UNIT
    gend = jnp.cumsum(padded)
    gstart = gend - padded
    cs = gstart[None, :] + jnp.cumsum(c, axis=0) - c
    units = padded // UNIT
    tiles = (units + 1) // 2
    tend = jnp.cumsum(tiles)
    n_used = tend[-1]
    tile_id = jnp.arange(N_ROW_TILES, dtype=i32)
    tile_expert = jnp.sum((tile_id[:, None] >= tend[None, :]).astype(i32), axis=1)
    tile_expert = jnp.minimum(tile_expert, jnp.take(tile_expert, n_used - 1))
    e_ar = jnp.arange(N_EXPERTS, dtype=i32)
    mine = tile_expert[:, None] == e_ar[None, :]
    pick = lambda v: jnp.sum(jnp.where(mine, v[None, :], 0), axis=1)
    in_group = tile_id - pick(tend - tiles)
    unit0 = pick(gstart // UNIT) + 2 * in_group
    half = (2 * in_group + 1 == pick(units)).astype(i32)
    first = (tile_id < n_used) & ((tile_id == 0) | (tile_expert != jnp.roll(tile_expert, 1)))
    order = jnp.maximum(jnp.cumsum(first.astype(i32)) - 1, 0)
    later = (e_ar[None, :] > e_ar[:, None]) & (padded[None, :] > 0)
    nxt_e = jnp.min(jnp.where(later, e_ar[None, :], N_EXPERTS), axis=1)
    nxt_e = jnp.where(nxt_e == N_EXPERTS, -1, nxt_e)
    nxt = pick(nxt_e)
    cs_f = cs.T.reshape(-1)
    c_f = c.T.reshape(-1)
    src_f = (jnp.arange(N_TILES, dtype=i32)[:, None] * S_MAX + lo).T.reshape(-1)
    piece_r = jnp.arange((N_UNITS + 1) * UNIT_PIECES, dtype=i32) * PIECE
    reached = (piece_r[:, None] >= cs_f[None, 1:]).astype(i32)

    def lookup(f):
        return f[0] + jnp.sum(reached * (f[1:] - f[:-1])[None, :], axis=1)
    off = piece_r - lookup(cs_f)
    piece_row = jnp.where(off < lookup(c_f), lookup(src_f) + off, -1)
    return dict(tot=tot, n_used=n_used[None], tile_expert=tile_expert, first=first.astype(i32),
                order=order, nxt=nxt, unit0=unit0, half=half, piece_row=piece_row)


def kernel(x_prompt, x_sample, state_conv, state_pool, norm_mix_g, w_in, conv_w, pool_w, pool_scale,
           w_out, norm_ffn_g, router_w, router_b, exp_w1, exp_b1, exp_w2, exp_b2, final_norm_g):
    l = 0
    xp = x_prompt.reshape(N_PROMPT, D_MODEL)
    xs = jnp.transpose(x_sample, (1, 0, 2)).reshape(N_SAMPLE, D_MODEL)
    sc_t = jnp.transpose(state_conv[l], (1, 0, 2))
    sp_t = jnp.transpose(state_pool[l], (1, 0, 2))
    rw_t = router_w[l].T
    rwh = rw_t.astype(bf16)
    rw2 = jnp.concatenate([rwh, (rw_t - rwh.astype(f32)).astype(bf16)], axis=0)
    tok = jnp.arange(TS, dtype=i32)
    tri = (tok[:, None] < tok[None, :]).astype(bf16)

    (x1, srt, route_t, cnt, npc, npp, nsc, nsv) = _mixer(
        xp, xs, sc_t, sp_t, norm_mix_g[l][None, :], w_in[l].astype(bf16), conv_w[l],
        pool_w[l].astype(bf16), pool_scale[l][None, :], w_out[l].astype(bf16),
        norm_ffn_g[l][None, :], rw2, router_b[l][:, None], tri)

    t = _routing_tables(cnt)
    ys = _experts(t["tile_expert"], t["n_used"], t["first"], t["order"], t["nxt"], t["unit0"],
                  t["half"], t["piece_row"], t["tot"], srt, exp_w1[l], exp_b1[l][:, None, :],
                  exp_w2[l], exp_b2[l][:, None, :])
    y_p, y_s = _combine(route_t.T, x1, final_norm_g[None, :], ys)

    y_prompt = y_p.reshape(BATCH, SEQ, D_MODEL)
    y_sample = jnp.transpose(y_s.reshape(DEC_SEQ, DEC_BATCH, D_MODEL), (1, 0, 2))
    new_conv_prompt = npc[None, :, 8 - (CONV_W - 1):, :]
    new_pool_prompt = npp[None, :, HIST - POOL_BUF:, :]
    new_conv_sample = jnp.transpose(nsc, (1, 0, 2))[None]
    new_pool_sample = jnp.concatenate(
        [state_pool[l][:, DEC_SEQ:, :], jnp.transpose(nsv, (1, 0, 2))], axis=1)[None]
    return (y_prompt, y_sample, new_conv_prompt, new_pool_prompt, new_conv_sample, new_pool_sample)
```

```python
import jax
import jax.numpy as jnp
from jax import lax
from jax.experimental import pallas as pl
from jax.experimental.pallas import tpu as pltpu

D_MODEL = 1024
D_CONV = 512
D_POOL = 512
D_IN = 3 * D_CONV + D_POOL
CONV_W = 3
POOL_WINDOWS = (2, 4, 8, 16)
POOL_GROUP = 128
POOL_BUF = 15
N_EXPERTS = 32
TOP_K = 4
D_FF = 1024
SWIGLU_LIMIT = 7.0
SWIGLU_ALPHA = 1.702
EPS = 1e-5
PAST_LEN = 16384

BATCH, SEQ = 8, 2048
DEC_BATCH, DEC_SEQ = 128, 4
N_PROMPT = BATCH * SEQ
N_SAMPLE = DEC_BATCH * DEC_SEQ
N_TOK = N_PROMPT + N_SAMPLE

TS = 512
SEQ_TILES = SEQ // TS
N_PROMPT_TILES = N_PROMPT // TS
N_TILES = N_PROMPT_TILES + 1
HIST = 16
UNIT = 256
TM = 2 * UNIT
PIECE = 8
PIECES = TM // PIECE
UNIT_PIECES = UNIT // PIECE
S_BLK = 256
SORT_BLK = 512
S_MAX = (TS * TOP_K + N_EXPERTS * (PIECE - 1) + PIECE + S_BLK - 1) // S_BLK * S_BLK
ZERO_ROW = S_MAX - PIECE
N_UNITS = (N_TOK * TOP_K + N_TILES * N_EXPERTS * (PIECE - 1) + N_EXPERTS * (UNIT - PIECE)) // UNIT
N_ROW_TILES = (N_UNITS + N_EXPERTS) // 2
N_SORT_ROWS = N_TILES * S_MAX
DUMP_ROW = N_SORT_ROWS

VMEM_LIMIT = 56 * 1024 * 1024

f32 = jnp.float32
bf16 = jnp.bfloat16
i32 = jnp.int32


def _rms(x, g):
    return x * lax.rsqrt(jnp.mean(x * x, axis=-1, keepdims=True) + EPS) * g


def _pool_project(p, pw_ref, ps_ref):
    outs = []
    for g in range(len(POOL_WINDOWS)):
        sl = slice(g * POOL_GROUP, (g + 1) * POOL_GROUP)
        outs.append(jnp.dot(p[:, sl].astype(bf16), pw_ref[g], preferred_element_type=f32))
    return jnp.concatenate(outs, axis=1) * ps_ref[...]


def _route_and_sort(x, z, wout_ref, g2_ref, rw2_ref, rb_ref, tri_ref,
                    x1_ref, srt_ref, route_ref, cnt_ref):
    n = x.shape[0]
    x1 = x + jnp.dot(z.astype(bf16), wout_ref[...], preferred_element_type=f32)
    x1_ref[...] = x1
    xn = _rms(x1, g2_ref[...])
    xb = xn.astype(bf16)

    xlo = (xn - xb.astype(f32)).astype(bf16)
    nt = (((1,), (1,)), ((), ()))
    hi_lo = lax.dot_general(rw2_ref[...], xb, nt, preferred_element_type=f32)
    logits = (hi_lo[:N_EXPERTS] + hi_lo[N_EXPERTS:]
              + lax.dot_general(rw2_ref[0:N_EXPERTS, :], xlo, nt, preferred_element_type=f32)
              + rb_ref[...])

    e_iota = lax.broadcasted_iota(i32, (N_EXPERTS, n), 0)
    work = logits
    hots, vals = [], []
    for _ in range(TOP_K):
        m = jnp.max(work, axis=0, keepdims=True)
        ik = jnp.min(jnp.where(work == m, e_iota, N_EXPERTS), axis=0, keepdims=True)
        hot = e_iota == ik
        work = jnp.where(hot, -jnp.inf, work)
        hots.append(hot); vals.append(m)
    exps = [jnp.exp(v - vals[0]) for v in vals]
    den = exps[0] + exps[1] + exps[2] + exps[3]
    gates = [e / den for e in exps]

    multi = jnp.where(hots[0] | hots[1] | hots[2] | hots[3], 1.0, 0.0)
    before = jnp.dot(multi.astype(bf16), tri_ref[...], preferred_element_type=f32)
    count = jnp.sum(multi, axis=1, keepdims=True)
    cnt_ref[...] = count
    pieces = jnp.floor((count + (PIECE - 1)) * (1.0 / PIECE))
    er = lax.broadcasted_iota(i32, (N_EXPERTS, N_EXPERTS), 0)
    ec = lax.broadcasted_iota(i32, (N_EXPERTS, N_EXPERTS), 1)
    lower = jnp.where(ec < er, 1.0, 0.0).astype(bf16)
    pieces_b = jnp.broadcast_to(pieces, (N_EXPERTS, 128)).astype(bf16)
    start = PIECE * jnp.dot(lower, pieces_b, preferred_element_type=f32)[:, 0:1]
    base = before + start
    slot = jnp.concatenate(
        [jnp.sum(jnp.where(h, base, 0.0), axis=0, keepdims=True) for h in hots], axis=0).astype(i32)
    route_ref[...] = jnp.concatenate([slot.astype(f32)] + gates, axis=0)

    row_io = lax.broadcasted_iota(i32, (S_BLK, n), 0).astype(bf16)
    slot_f = slot.astype(f32)
    one, zero = jnp.ones((S_BLK, n), bf16), jnp.zeros((S_BLK, n), bf16)
    for row0 in range(0, S_MAX, SORT_BLK):
        rows = min(SORT_BLK, S_MAX - row0)
        parts = []
        for sub in range(row0, row0 + rows, S_BLK):
            local = slot_f - float(sub)
            local = jnp.where((local >= 0.0) & (local < float(S_BLK)), local, -1.0).astype(bf16)
            hit = row_io == local[0:1, :]
            for k in range(1, TOP_K):
                hit = hit | (row_io == local[k:k + 1, :])
            parts.append(jnp.where(hit, one, zero))
        sel = parts[0] if len(parts) == 1 else jnp.concatenate(parts, axis=0)
        srt_ref[row0:row0 + rows, :] = jnp.dot(sel, xb, preferred_element_type=f32).astype(bf16)


def _mixer_kernel(xp_in_ref, xs_in_ref, sc_ref, sp_ref, g1_ref, win_ref, cw_ref, pw_ref, ps_ref,
                  wout_ref, g2_ref, rw2_ref, rb_ref, tri_ref,
                  x1_ref, srt_ref, route_ref, cnt_ref,
                  npc_ref, npp_ref, nsc_ref, nsv_ref,
                  ubuf, vbuf, xcur, zcur):
    i = pl.program_id(0)

    @pl.when(i < N_PROMPT_TILES)
    def _():
        s = i % SEQ_TILES

        @pl.when(s == 0)
        def _():
            ubuf[0:HIST, :] = jnp.zeros((HIST, D_CONV), f32)
            vbuf[0:HIST, :] = jnp.zeros((HIST, D_POOL), f32)

        x = xp_in_ref[...]
        xcur[...] = x
        xn = _rms(x, g1_ref[...]).astype(bf16)
        proj = jnp.dot(xn, win_ref[...], preferred_element_type=f32)
        bg = proj[:, :D_CONV]
        u = proj[:, D_CONV:2 * D_CONV] * proj[:, 2 * D_CONV:3 * D_CONV]
        v = proj[:, 3 * D_CONV:]
        ubuf[HIST:HIST + TS, :] = u
        vbuf[HIST:HIST + TS, :] = v

        y = (ubuf[HIST - 2:HIST - 2 + TS, :] * cw_ref[0:1, :]
             + ubuf[HIST - 1:HIST - 1 + TS, :] * cw_ref[1:2, :]
             + u * cw_ref[2:3, :])
        zcur[:, :D_CONV] = bg * y

        pos = s * TS + lax.broadcasted_iota(i32, (TS, 1), 0)
        ps = []
        for g, w in enumerate(POOL_WINDOWS):
            sl = slice(g * POOL_GROUP, (g + 1) * POOL_GROUP)
            acc = vbuf[:, sl]
            for step in range(g + 1):
                acc = acc + pltpu.roll(acc, 1 << step, axis=0)
            cnt = jnp.minimum(w, pos + 1).astype(f32)
            ps.append(acc[HIST:, :] * (1.0 / cnt) - v[:, sl])
        zcur[:, D_CONV:] = _pool_project(jnp.concatenate(ps, axis=1), pw_ref, ps_ref)

        ubuf[HIST - 8:HIST, :] = ubuf[TS + HIST - 8:TS + HIST, :]
        vbuf[0:HIST, :] = vbuf[TS:TS + HIST, :]

        @pl.when(s == SEQ_TILES - 1)
        def _():
            npc_ref[0] = ubuf[HIST - 8:HIST, :]
            npp_ref[0] = vbuf[0:HIST, :]

    @pl.when(i == N_PROMPT_TILES)
    def _():
        x = xs_in_ref[...]
        xcur[...] = x
        xn = _rms(x, g1_ref[...]).astype(bf16)
        proj = jnp.dot(xn, win_ref[...], preferred_element_type=f32)
        nb = DEC_BATCH
        rows = lambda a, t: a[t * nb:(t + 1) * nb]
        bg = proj[:, :D_CONV]
        u = proj[:, D_CONV:2 * D_CONV] * proj[:, 2 * D_CONV:3 * D_CONV]
        v = proj[:, 3 * D_CONV:]
        up = [sc_ref[j] for j in range(CONV_W - 1)] + [rows(u, t) for t in range(DEC_SEQ)]
        vp = [sp_ref[j] for j in range(POOL_BUF)] + [rows(v, t) for t in range(DEC_SEQ)]
        zc, ps = [], []
        for t in range(DEC_SEQ):
            y = up[t] * cw_ref[0:1, :]
            for k in range(1, CONV_W):
                y = y + up[t + k] * cw_ref[k:k + 1, :]
            zc.append(rows(bg, t) * y)
            pg = []
            for g, w in enumerate(POOL_WINDOWS):
                sl = slice(g * POOL_GROUP, (g + 1) * POOL_GROUP)
                lo = t + POOL_BUF - w + 1
                acc = vp[lo][:, sl]
                for j in range(lo + 1, t + POOL_BUF + 1):
                    acc = acc + vp[j][:, sl]
                cnt = float(min(w, PAST_LEN + t + 1))
                pg.append(acc * (1.0 / cnt) - vp[t + POOL_BUF][:, sl])
            ps.append(jnp.concatenate(pg, axis=1))
        zcur[:, :D_CONV] = jnp.concatenate(zc, axis=0)
        zcur[:, D_CONV:] = _pool_project(jnp.concatenate(ps, axis=0), pw_ref, ps_ref)
        for j in range(CONV_W - 1):
            nsc_ref[j] = up[DEC_SEQ + j]
        for t in range(DEC_SEQ):
            nsv_ref[t] = rows(v, t)

    _route_and_sort(xcur[...], zcur[...], wout_ref, g2_ref, rw2_ref, rb_ref, tri_ref,
                    x1_ref, srt_ref, route_ref, cnt_ref)


def _mixer(xp, xs, sc_t, sp_t, g1, win, cw, pw, ps, wout, g2, rw2, rb, tri):
    const = lambda shape: pl.BlockSpec(shape, lambda i: (0,) * len(shape),
                                       pipeline_mode=pl.Buffered(1))
    ptile = lambda i: jnp.minimum(i, N_PROMPT_TILES - 1)
    pbatch = lambda i: jnp.minimum(i, N_PROMPT_TILES - 1) // SEQ_TILES
    out_shape = (
        jax.ShapeDtypeStruct((N_TOK, D_MODEL), f32),
        jax.ShapeDtypeStruct((N_SORT_ROWS, D_MODEL), bf16),
        jax.ShapeDtypeStruct((2 * TOP_K, N_TOK), f32),
        jax.ShapeDtypeStruct((N_TILES * N_EXPERTS, 1), f32),
        jax.ShapeDtypeStruct((BATCH, 8, D_CONV), f32),
        jax.ShapeDtypeStruct((BATCH, HIST, D_POOL), f32),
        jax.ShapeDtypeStruct((CONV_W - 1, DEC_BATCH, D_CONV), f32),
        jax.ShapeDtypeStruct((DEC_SEQ, DEC_BATCH, D_POOL), f32),
    )
    out_specs = (
        pl.BlockSpec((TS, D_MODEL), lambda i: (i, 0)),
        pl.BlockSpec((S_MAX, D_MODEL), lambda i: (i, 0)),
        pl.BlockSpec((2 * TOP_K, TS), lambda i: (0, i)),
        pl.BlockSpec((N_EXPERTS, 1), lambda i: (i, 0)),
        pl.BlockSpec((1, 8, D_CONV), lambda i: (pbatch(i), 0, 0)),
        pl.BlockSpec((1, HIST, D_POOL), lambda i: (pbatch(i), 0, 0)),
        pl.BlockSpec((CONV_W - 1, DEC_BATCH, D_CONV), lambda i: (0, 0, 0)),
        pl.BlockSpec((DEC_SEQ, DEC_BATCH, D_POOL), lambda i: (0, 0, 0)),
    )
    in_specs = [
        pl.BlockSpec((TS, D_MODEL), lambda i: (ptile(i), 0)),
        const((N_SAMPLE, D_MODEL)),
        const((CONV_W - 1, DEC_BATCH, D_CONV)),
        const((POOL_BUF, DEC_BATCH, D_POOL)),
        const((1, D_MODEL)),
        const((D_MODEL, D_IN)),
        const((CONV_W, D_CONV)),
        const((len(POOL_WINDOWS), POOL_GROUP, POOL_GROUP)),
        const((1, D_POOL)),
        const((D_MODEL, D_MODEL)),
        const((1, D_MODEL)),
        const((2 * N_EXPERTS, D_MODEL)),
        const((N_EXPERTS, 1)),
        const((TS, TS)),
    ]
    return pl.pallas_call(
        _mixer_kernel,
        grid=(N_TILES,),
        in_specs=in_specs,
        out_specs=out_specs,
        out_shape=out_shape,
        scratch_shapes=[
            pltpu.VMEM((TS + HIST, D_CONV), f32),
            pltpu.VMEM((TS + HIST, D_POOL), f32),
            pltpu.VMEM((TS, D_MODEL), f32),
            pltpu.VMEM((TS, D_MODEL), f32),
        ],
        compiler_params=pltpu.CompilerParams(
            dimension_semantics=("arbitrary",), vmem_limit_bytes=VMEM_LIMIT),
        name="mixer",
    )(xp, xs, sc_t, sp_t, g1, win, cw, pw, ps, wout, g2, rw2, rb, tri)


def _expert_kernel(te_ref, nu_ref, first_ref, ord_ref, nxt_ref, unit_ref, half_ref, pt_ref, tot_ref,
                   b1_ref, b2_ref, srt_hbm, w1_hbm, w2_hbm, ys_hbm,
                   xbuf, ybuf, zpiece, w1f, w2f, w1b, w2b, sem_w, sem_x, sem_y, sem_z):
    i = pl.program_id(0)
    n_used = nu_ref[0]
    live = i < n_used

    def weight_copies(e, slot):
        return (pltpu.make_async_copy(w1_hbm.at[e], w1f.at[slot], sem_w.at[slot]),
                pltpu.make_async_copy(w2_hbm.at[e], w2f.at[slot], sem_w.at[2 + slot]))

    def is_padding(entry, tile, p):
        if p < UNIT_PIECES:
            return entry < 0
        return (entry < 0) | (half_ref[tile] == 1)

    def start_x(tile, b):
        for p in range(PIECES):
            src = pt_ref[unit_ref[tile] * UNIT_PIECES + p]
            src = pl.multiple_of(jnp.where(is_padding(src, tile, p), ZERO_ROW, src), PIECE)
            pltpu.make_async_copy(srt_hbm.at[pl.ds(src, PIECE)],
                                  xbuf.at[b, pl.ds(p * PIECE, PIECE)], sem_x.at[b]).start()

    def start_y(tile, b, all_dump=False):
        for p in range(PIECES):
            dst = pt_ref[unit_ref[tile] * UNIT_PIECES + p]
            dump = is_padding(dst, tile, p) | all_dump
            dst = pl.multiple_of(jnp.where(dump, DUMP_ROW + b * TM + p * PIECE, dst), PIECE)
            pltpu.make_async_copy(ybuf.at[b, pl.ds(p * PIECE, PIECE)],
                                  ys_hbm.at[pl.ds(dst, PIECE)], sem_y.at[b]).start()

    def wait_x(b):
        pltpu.make_async_copy(srt_hbm.at[pl.ds(0, TM)], xbuf.at[b], sem_x.at[b]).wait()

    def wait_y(b):
        pltpu.make_async_copy(ybuf.at[b], ys_hbm.at[pl.ds(0, TM)], sem_y.at[b]).wait()

    @pl.when(i == 0)
    def _():
        zpiece[...] = jnp.zeros_like(zpiece)

        def fill_tile(t, c):
            def dst(p):
                row = pl.multiple_of(t * S_MAX + tot_ref[t] + p * PIECE, PIECE)
                return ys_hbm.at[pl.ds(row, PIECE)]

            def start(p, c2):
                pltpu.make_async_copy(zpiece, dst(p), sem_z.at[0]).start()
                return c2
            return lax.fori_loop(0, (S_MAX - tot_ref[t]) // PIECE, start, c)
        lax.fori_loop(0, N_TILES, fill_tile, 0)
        start_x(0, 0)
        start_x(jnp.minimum(1, n_used - 1), 1)
        ybuf[...] = jnp.zeros_like(ybuf)
        for b in range(2):
            pltpu.make_async_copy(ybuf.at[b], ys_hbm.at[pl.ds(DUMP_ROW + b * TM, TM)],
                                  sem_y.at[b]).start()

    @pl.when(live)
    def _():
        ring = i % 3
        expert = te_ref[i]
        wait_x(ring)

        @pl.when(first_ref[i] == 1)
        def _():
            slot = ord_ref[i] % 2

            @pl.when(i == 0)
            def _():
                for cp in weight_copies(te_ref[0], 0):
                    cp.start()
            for cp in weight_copies(expert, slot):
                cp.wait()

            @pl.when(nxt_ref[i] >= 0)
            def _():
                for cp in weight_copies(nxt_ref[i], 1 - slot):
                    cp.start()
            w1b[...] = w1f[slot].astype(bf16)
            w2b[...] = w2f[slot].astype(bf16)

        wait_y(ring)

        def mlp(rows):
            x = xbuf[ring, 0:rows, :]
            gu = jnp.dot(x, w1b[...], preferred_element_type=f32) + b1_ref[expert]
            gate = jnp.minimum(gu[:, :D_FF], SWIGLU_LIMIT)
            lin = jnp.clip(gu[:, D_FF:], -SWIGLU_LIMIT, SWIGLU_LIMIT)
            glu = gate * jax.nn.sigmoid(SWIGLU_ALPHA * gate)
            h = (glu * (lin + 1.0)).astype(bf16)
            start_x(jnp.minimum(i + 2, n_used - 1), (i + 2) % 3)
            start_y(jnp.maximum(i - 1, 0), (i + 2) % 3, all_dump=i == 0)
            out = jnp.dot(h, w2b[...], preferred_element_type=f32) + b2_ref[expert]
            ybuf[ring, 0:rows, :] = out.astype(bf16)

        @pl.when(half_ref[i] == 0)
        def _():
            mlp(TM)

        @pl.when(half_ref[i] == 1)
        def _():
            mlp(UNIT)

        @pl.when(i == n_used - 1)
        def _():
            start_y(i, ring)
            for b in range(3):
                wait_y(b)
            wait_x((i + 1) % 3)
            wait_x((i + 2) % 3)

            def drain_tile(t, c):
                def wait(p, c2):
                    pltpu.make_async_copy(zpiece, ys_hbm.at[pl.ds(0, PIECE)], sem_z.at[0]).wait()
                    return c2
                return lax.fori_loop(0, (S_MAX - tot_ref[t]) // PIECE, wait, c)
            lax.fori_loop(0, N_TILES, drain_tile, 0)


def _experts(tile_expert, n_used, first, order, nxt, unit0, half, piece_row, tot, srt, w1, b1, w2, b2):
    whole = lambda shape: pl.BlockSpec(shape, lambda i, *_: (0,) * len(shape))
    grid_spec = pltpu.PrefetchScalarGridSpec(
        num_scalar_prefetch=9,
        grid=(N_ROW_TILES,),
        in_specs=[
            whole((N_EXPERTS, 1, 2 * D_FF)),
            whole((N_EXPERTS, 1, D_MODEL)),
            pl.BlockSpec(memory_space=pl.ANY),
            pl.BlockSpec(memory_space=pl.ANY),
            pl.BlockSpec(memory_space=pl.ANY),
        ],
        out_specs=pl.BlockSpec(memory_space=pl.ANY),
        scratch_shapes=[
            pltpu.VMEM((3, TM, D_MODEL), bf16), pltpu.VMEM((3, TM, D_MODEL), bf16),
            pltpu.VMEM((PIECE, D_MODEL), bf16),
            pltpu.VMEM((2, D_MODEL, 2 * D_FF), f32), pltpu.VMEM((2, D_FF, D_MODEL), f32),
            pltpu.VMEM((D_MODEL, 2 * D_FF), bf16), pltpu.VMEM((D_FF, D_MODEL), bf16),
            pltpu.SemaphoreType.DMA((4,)), pltpu.SemaphoreType.DMA((3,)),
            pltpu.SemaphoreType.DMA((3,)), pltpu.SemaphoreType.DMA((1,)),
        ],
    )
    return pl.pallas_call(
        _expert_kernel,
        grid_spec=grid_spec,
        out_shape=jax.ShapeDtypeStruct((N_SORT_ROWS + 3 * TM, D_MODEL), bf16),
        compiler_params=pltpu.CompilerParams(
            dimension_semantics=("arbitrary",), vmem_limit_bytes=VMEM_LIMIT),
        name="experts",
    )(tile_expert, n_used, first, order, nxt, unit0, half, piece_row, tot, b1, b2, srt, w1, w2)


def _combine_kernel(route_ref, x1_ref, gf_ref, ys_ref, yp_ref, ysm_ref, wgt):
    i = pl.program_id(0)

    slot = route_ref[:, 0:TOP_K].astype(i32)
    g = route_ref[:, TOP_K:2 * TOP_K]
    for b in range(S_MAX // S_BLK):
        s_io = b * S_BLK + lax.broadcasted_iota(i32, (TS, S_BLK), 1)
        w = jnp.zeros((TS, S_BLK), f32)
        for k in range(TOP_K):
            w = jnp.where(s_io == slot[:, k:k + 1], g[:, k:k + 1], w)
        wgt[:, b * S_BLK:(b + 1) * S_BLK] = w.astype(bf16)

    y = x1_ref[...] + jnp.dot(wgt[...], ys_ref[...], preferred_element_type=f32)
    out = _rms(y, gf_ref[...])

    @pl.when(i < N_PROMPT_TILES)
    def _():
        yp_ref[...] = out

    @pl.when(i == N_PROMPT_TILES)
    def _():
        ysm_ref[...] = out


def _combine(route_tm, x1, gf, ys):
    ptile = lambda i: (jnp.minimum(i, N_PROMPT_TILES - 1), 0)
    return pl.pallas_call(
        _combine_kernel,
        grid=(N_TILES,),
        in_specs=[
            pl.BlockSpec((TS, 2 * TOP_K), lambda i: (i, 0)),
            pl.BlockSpec((TS, D_MODEL), lambda i: (i, 0)),
            pl.BlockSpec((1, D_MODEL), lambda i: (0, 0)),
            pl.BlockSpec((S_MAX, D_MODEL), lambda i: (i, 0)),
        ],
        out_specs=(
            pl.BlockSpec((TS, D_MODEL), ptile),
            pl.BlockSpec((N_SAMPLE, D_MODEL), lambda i: (0, 0)),
        ),
        out_shape=(
            jax.ShapeDtypeStruct((N_PROMPT, D_MODEL), f32),
            jax.ShapeDtypeStruct((N_SAMPLE, D_MODEL), f32),
        ),
        scratch_shapes=[pltpu.VMEM((TS, S_MAX), bf16)],
        compiler_params=pltpu.CompilerParams(
            dimension_semantics=("arbitrary",), vmem_limit_bytes=VMEM_LIMIT),
        name="combine",
    )(route_tm, x1, gf, ys)


def _routing_tables(cnt):
    n = cnt.reshape(N_TILES, N_EXPERTS).astype(i32)
    c = (n + (PIECE - 1)) // PIECE * PIECE
    lo = jnp.cumsum(c, axis=1) - c
    tot = jnp.sum(c, axis=1)
    group = jnp.sum(c, axis=0)
    padded = (group + (UNIT - 1)) // UNIT * UNIT
    gend = jnp.cumsum(padded)
    gstart = gend - padded
    cs = gstart[None, :] + jnp.cumsum(c, axis=0) - c
    units = padded // UNIT
    tiles = (units + 1) // 2
    tend = jnp.cumsum(tiles)
    n_used = tend[-1]
    tile_id = jnp.arange(N_ROW_TILES, dtype=i32)
    tile_expert = jnp.sum((tile_id[:, None] >= tend[None, :]).astype(i32), axis=1)
    tile_expert = jnp.minimum(tile_expert, jnp.take(tile_expert, n_used - 1))
    e_ar = jnp.arange(N_EXPERTS, dtype=i32)
    mine = tile_expert[:, None] == e_ar[None, :]
    pick = lambda v: jnp.sum(jnp.where(mine, v[None, :], 0), axis=1)
    in_group = tile_id - pick(tend - tiles)
    unit0 = pick(gstart // UNIT) + 2 * in_group
    half = (2 * in_group + 1 == pick(units)).astype(i32)
    first = (tile_id < n_used) & ((tile_id == 0) | (tile_expert != jnp.roll(tile_expert, 1)))
    order = jnp.maximum(jnp.cumsum(first.astype(i32)) - 1, 0)
    later = (e_ar[None, :] > e_ar[:, None]) & (padded[None, :] > 0)
    nxt_e = jnp.min(jnp.where(later, e_ar[None, :], N_EXPERTS), axis=1)
    nxt_e = jnp.where(nxt_e == N_EXPERTS, -1, nxt_e)
    nxt = pick(nxt_e)
    cs_f = cs.T.reshape(-1)
    c_f = c.T.reshape(-1)
    src_f = (jnp.arange(N_TILES, dtype=i32)[:, None] * S_MAX + lo).T.reshape(-1)
    piece_r = jnp.arange((N_UNITS + 1) * UNIT_PIECES, dtype=i32) * PIECE
    reached = (piece_r[:, None] >= cs_f[None, 1:]).astype(i32)

    def lookup(f):
        return f[0] + jnp.sum(reached * (f[1:] - f[:-1])[None, :], axis=1)
    off = piece_r - lookup(cs_f)
    piece_row = jnp.where(off < lookup(c_f), lookup(src_f) + off, -1)
    return dict(tot=tot, n_used=n_used[None], tile_expert=tile_expert, first=first.astype(i32),
                order=order, nxt=nxt, unit0=unit0, half=half, piece_row=piece_row)


def kernel(x_prompt, x_sample, state_conv, state_pool, norm_mix_g, w_in, conv_w, pool_w, pool_scale,
           w_out, norm_ffn_g, router_w, router_b, exp_w1, exp_b1, exp_w2, exp_b2, final_norm_g):
    l = 0
    xp = x_prompt.reshape(N_PROMPT, D_MODEL)
    xs = jnp.transpose(x_sample, (1, 0, 2)).reshape(N_SAMPLE, D_MODEL)
    sc_t = jnp.transpose(state_conv[l], (1, 0, 2))
    sp_t = jnp.transpose(state_pool[l], (1, 0, 2))
    rw_t = router_w[l].T
    rwh = rw_t.astype(bf16)
    rw2 = jnp.concatenate([rwh, (rw_t - rwh.astype(f32)).astype(bf16)], axis=0)
    tok = jnp.arange(TS, dtype=i32)
    tri = (tok[:, None] < tok[None, :]).astype(bf16)

    (x1, srt, route_t, cnt, npc, npp, nsc, nsv) = _mixer(
        xp, xs, sc_t, sp_t, norm_mix_g[l][None, :], w_in[l].astype(bf16), conv_w[l],
        pool_w[l].astype(bf16), pool_scale[l][None, :], w_out[l].astype(bf16),
        norm_ffn_g[l][None, :], rw2, router_b[l][:, None], tri)

    t = _routing_tables(cnt)
    ys = _experts(t["tile_expert"], t["n_used"], t["first"], t["order"], t["nxt"], t["unit0"],
                  t["half"], t["piece_row"], t["tot"], srt, exp_w1[l], exp_b1[l][:, None, :],
                  exp_w2[l], exp_b2[l][:, None, :])
    y_p, y_s = _combine(route_t.T, x1, final_norm_g[None, :], ys)

    y_prompt = y_p.reshape(BATCH, SEQ, D_MODEL)
    y_sample = jnp.transpose(y_s.reshape(DEC_SEQ, DEC_BATCH, D_MODEL), (1, 0, 2))
    new_conv_prompt = npc[None, :, 8 - (CONV_W - 1):, :]
    new_pool_prompt = npp[None, :, HIST - POOL_BUF:, :]
    new_conv_sample = jnp.transpose(nsc, (1, 0, 2))[None]
    new_pool_sample = jnp.concatenate(
        [state_pool[l][:, DEC_SEQ:, :], jnp.transpose(nsv, (1, 0, 2))], axis=1)[None]
    return (y_prompt, y_sample, new_conv_prompt, new_pool_prompt, new_conv_sample, new_pool_sample)
```

```python
import jax
import jax.numpy as jnp
from jax import lax
from jax.experimental import pallas as pl
from jax.experimental.pallas import tpu as pltpu

D_MODEL = 1024
D_CONV = 512
D_POOL = 512
D_IN = 3 * D_CONV + D_POOL
CONV_W = 3
POOL_WINDOWS = (2, 4, 8, 16)
POOL_GROUP = 128
POOL_BUF = 15
N_EXPERTS = 32
TOP_K = 4
D_FF = 1024
SWIGLU_LIMIT = 7.0
SWIGLU_ALPHA = 1.702
EPS = 1e-5
PAST_LEN = 16384

BATCH, SEQ = 8, 2048
DEC_BATCH, DEC_SEQ = 128, 4
N_PROMPT = BATCH * SEQ
N_SAMPLE = DEC_BATCH * DEC_SEQ
N_TOK = N_PROMPT + N_SAMPLE

TS = 512
SEQ_TILES = SEQ // TS
N_PROMPT_TILES = N_PROMPT // TS
N_TILES = N_PROMPT_TILES + 1
HIST = 16
UNIT = 256
TM = 2 * UNIT
PIECE = 8
PIECES = TM // PIECE
UNIT_PIECES = UNIT // PIECE
S_BLK = 256
SORT_BLK = 512
S_MAX = (TS * TOP_K + N_EXPERTS * (PIECE - 1) + PIECE + S_BLK - 1) // S_BLK * S_BLK
ZERO_ROW = S_MAX - PIECE
N_UNITS = (N_TOK * TOP_K + N_TILES * N_EXPERTS * (PIECE - 1) + N_EXPERTS * (UNIT - PIECE)) // UNIT
N_ROW_TILES = (N_UNITS + N_EXPERTS) // 2
N_SORT_ROWS = N_TILES * S_MAX
DUMP_ROW = N_SORT_ROWS

VMEM_LIMIT = 56 * 1024 * 1024

f32 = jnp.float32
bf16 = jnp.bfloat16
i32 = jnp.int32


def _rms(x, g):
    return x * lax.rsqrt(jnp.mean(x * x, axis=-1, keepdims=True) + EPS) * g


def _pool_project(p, pw_ref, ps_ref):
    outs = []
    for g in range(len(POOL_WINDOWS)):
        sl = slice(g * POOL_GROUP, (g + 1) * POOL_GROUP)
        outs.append(jnp.dot(p[:, sl].astype(bf16), pw_ref[g], preferred_element_type=f32))
    return jnp.concatenate(outs, axis=1) * ps_ref[...]


def _route_and_sort(x, z, wout_ref, g2_ref, rw2_ref, rb_ref, tri_ref,
                    x1_ref, srt_ref, route_ref, cnt_ref):
    n = x.shape[0]
    x1 = x + jnp.dot(z.astype(bf16), wout_ref[...], preferred_element_type=f32)
    x1_ref[...] = x1
    xn = _rms(x1, g2_ref[...])
    xb = xn.astype(bf16)

    xlo = (xn - xb.astype(f32)).astype(bf16)
    nt = (((1,), (1,)), ((), ()))
    hi_lo = lax.dot_general(rw2_ref[...], xb, nt, preferred_element_type=f32)
    logits = (hi_lo[:N_EXPERTS] + hi_lo[N_EXPERTS:]
              + lax.dot_general(rw2_ref[0:N_EXPERTS, :], xlo, nt, preferred_element_type=f32)
              + rb_ref[...])

    e_iota = lax.broadcasted_iota(i32, (N_EXPERTS, n), 0)
    work = logits
    hots, vals = [], []
    for _ in range(TOP_K):
        m = jnp.max(work, axis=0, keepdims=True)
        ik = jnp.min(jnp.where(work == m, e_iota, N_EXPERTS), axis=0, keepdims=True)
        hot = e_iota == ik
        work = jnp.where(hot, -jnp.inf, work)
        hots.append(hot); vals.append(m)
    exps = [jnp.exp(v - vals[0]) for v in vals]
    den = exps[0] + exps[1] + exps[2] + exps[3]
    gates = [e / den for e in exps]

    multi = jnp.where(hots[0] | hots[1] | hots[2] | hots[3], 1.0, 0.0)
    before = jnp.dot(multi.astype(bf16), tri_ref[...], preferred_element_type=f32)
    count = jnp.sum(multi, axis=1, keepdims=True)
    cnt_ref[...] = count
    pieces = jnp.floor((count + (PIECE - 1)) * (1.0 / PIECE))
    er = lax.broadcasted_iota(i32, (N_EXPERTS, N_EXPERTS), 0)
    ec = lax.broadcasted_iota(i32, (N_EXPERTS, N_EXPERTS), 1)
    lower = jnp.where(ec < er, 1.0, 0.0).astype(bf16)
    pieces_b = jnp.broadcast_to(pieces, (N_EXPERTS, 128)).astype(bf16)
    start = PIECE * jnp.dot(lower, pieces_b, preferred_element_type=f32)[:, 0:1]
    base = before + start
    slot = jnp.concatenate(
        [jnp.sum(jnp.where(h, base, 0.0), axis=0, keepdims=True) for h in hots], axis=0).astype(i32)
    route_ref[...] = jnp.concatenate([slot.astype(f32)] + gates, axis=0)

    row_io = lax.broadcasted_iota(i32, (S_BLK, n), 0).astype(bf16)
    slot_f = slot.astype(f32)
    one, zero = jnp.ones((S_BLK, n), bf16), jnp.zeros((S_BLK, n), bf16)
    for row0 in range(0, S_MAX, SORT_BLK):
        rows = min(SORT_BLK, S_MAX - row0)
        parts = []
        for sub in range(row0, row0 + rows, S_BLK):
            local = slot_f - float(sub)
            local = jnp.where((local >= 0.0) & (local < float(S_BLK)), local, -1.0).astype(bf16)
            hit = row_io == local[0:1, :]
            for k in range(1, TOP_K):
                hit = hit | (row_io == local[k:k + 1, :])
            parts.append(jnp.where(hit, one, zero))
        sel = parts[0] if len(parts) == 1 else jnp.concatenate(parts, axis=0)
        srt_ref[row0:row0 + rows, :] = jnp.dot(sel, xb, preferred_element_type=f32).astype(bf16)


def _mixer_kernel(xp_in_ref, xs_in_ref, sc_ref, sp_ref, g1_ref, win_ref, cw_ref, pw_ref, ps_ref,
                  wout_ref, g2_ref, rw2_ref, rb_ref, tri_ref,
                  x1_ref, srt_ref, route_ref, cnt_ref,
                  npc_ref, npp_ref, nsc_ref, nsv_ref,
                  ubuf, vbuf, xcur, zcur):
    i = pl.program_id(0)

    @pl.when(i < N_PROMPT_TILES)
    def _():
        s = i % SEQ_TILES

        @pl.when(s == 0)
        def _():
            ubuf[0:HIST, :] = jnp.zeros((HIST, D_CONV), f32)
            vbuf[0:HIST, :] = jnp.zeros((HIST, D_POOL), f32)

        x = xp_in_ref[...]
        xcur[...] = x
        xn = _rms(x, g1_ref[...]).astype(bf16)
        proj = jnp.dot(xn, win_ref[...], preferred_element_type=f32)
        bg = proj[:, :D_CONV]
        u = proj[:, D_CONV:2 * D_CONV] * proj[:, 2 * D_CONV:3 * D_CONV]
        v = proj[:, 3 * D_CONV:]
        ubuf[HIST:HIST + TS, :] = u
        vbuf[HIST:HIST + TS, :] = v

        y = (ubuf[HIST - 2:HIST - 2 + TS, :] * cw_ref[0:1, :]
             + ubuf[HIST - 1:HIST - 1 + TS, :] * cw_ref[1:2, :]
             + u * cw_ref[2:3, :])
        zcur[:, :D_CONV] = bg * y

        pos = s * TS + lax.broadcasted_iota(i32, (TS, 1), 0)
        ps = []
        for g, w in enumerate(POOL_WINDOWS):
            sl = slice(g * POOL_GROUP, (g + 1) * POOL_GROUP)
            acc = vbuf[:, sl]
            for step in range(g + 1):
                acc = acc + pltpu.roll(acc, 1 << step, axis=0)
            cnt = jnp.minimum(w, pos + 1).astype(f32)
            ps.append(acc[HIST:, :] * (1.0 / cnt) - v[:, sl])
        zcur[:, D_CONV:] = _pool_project(jnp.concatenate(ps, axis=1), pw_ref, ps_ref)

        ubuf[HIST - 8:HIST, :] = ubuf[TS + HIST - 8:TS + HIST, :]
        vbuf[0:HIST, :] = vbuf[TS:TS + HIST, :]

        @pl.when(s == SEQ_TILES - 1)
        def _():
            npc_ref[0] = ubuf[HIST - 8:HIST, :]
            npp_ref[0] = vbuf[0:HIST, :]

    @pl.when(i == N_PROMPT_TILES)
    def _():
        x = xs_in_ref[...]
        xcur[...] = x
        xn = _rms(x, g1_ref[...]).astype(bf16)
        proj = jnp.dot(xn, win_ref[...], preferred_element_type=f32)
        nb = DEC_BATCH
        rows = lambda a, t: a[t * nb:(t + 1) * nb]
        bg = proj[:, :D_CONV]
        u = proj[:, D_CONV:2 * D_CONV] * proj[:, 2 * D_CONV:3 * D_CONV]
        v = proj[:, 3 * D_CONV:]
        up = [sc_ref[j] for j in range(CONV_W - 1)] + [rows(u, t) for t in range(DEC_SEQ)]
        vp = [sp_ref[j] for j in range(POOL_BUF)] + [rows(v, t) for t in range(DEC_SEQ)]
        zc, ps = [], []
        for t in range(DEC_SEQ):
            y = up[t] * cw_ref[0:1, :]
            for k in range(1, CONV_W):
                y = y + up[t + k] * cw_ref[k:k + 1, :]
            zc.append(rows(bg, t) * y)
            pg = []
            for g, w in enumerate(POOL_WINDOWS):
                sl = slice(g * POOL_GROUP, (g + 1) * POOL_GROUP)
                lo = t + POOL_BUF - w + 1
                acc = vp[lo][:, sl]
                for j in range(lo + 1, t + POOL_BUF + 1):
                    acc = acc + vp[j][:, sl]
                cnt = float(min(w, PAST_LEN + t + 1))
                pg.append(acc * (1.0 / cnt) - vp[t + POOL_BUF][:, sl])
            ps.append(jnp.concatenate(pg, axis=1))
        zcur[:, :D_CONV] = jnp.concatenate(zc, axis=0)
        zcur[:, D_CONV:] = _pool_project(jnp.concatenate(ps, axis=0), pw_ref, ps_ref)
        for j in range(CONV_W - 1):
            nsc_ref[j] = up[DEC_SEQ + j]
        for t in range(DEC_SEQ):
            nsv_ref[t] = rows(v, t)

    _route_and_sort(xcur[...], zcur[...], wout_ref, g2_ref, rw2_ref, rb_ref, tri_ref,
                    x1_ref, srt_ref, route_ref, cnt_ref)


def _mixer(xp, xs, sc_t, sp_t, g1, win, cw, pw, ps, wout, g2, rw2, rb, tri):
    const = lambda shape: pl.BlockSpec(shape, lambda i: (0,) * len(shape),
                                       pipeline_mode=pl.Buffered(1))
    ptile = lambda i: jnp.minimum(i, N_PROMPT_TILES - 1)
    pbatch = lambda i: jnp.minimum(i, N_PROMPT_TILES - 1) // SEQ_TILES
    out_shape = (
        jax.ShapeDtypeStruct((N_TOK, D_MODEL), f32),
        jax.ShapeDtypeStruct((N_SORT_ROWS, D_MODEL), bf16),
        jax.ShapeDtypeStruct((2 * TOP_K, N_TOK), f32),
        jax.ShapeDtypeStruct((N_TILES * N_EXPERTS, 1), f32),
        jax.ShapeDtypeStruct((BATCH, 8, D_CONV), f32),
        jax.ShapeDtypeStruct((BATCH, HIST, D_POOL), f32),
        jax.ShapeDtypeStruct((CONV_W - 1, DEC_BATCH, D_CONV), f32),
        jax.ShapeDtypeStruct((DEC_SEQ, DEC_BATCH, D_POOL), f32),
    )
    out_specs = (
        pl.BlockSpec((TS, D_MODEL), lambda i: (i, 0)),
        pl.BlockSpec((S_MAX, D_MODEL), lambda i: (i, 0)),
        pl.BlockSpec((2 * TOP_K, TS), lambda i: (0, i)),
        pl.BlockSpec((N_EXPERTS, 1), lambda i: (i, 0)),
        pl.BlockSpec((1, 8, D_CONV), lambda i: (pbatch(i), 0, 0)),
        pl.BlockSpec((1, HIST, D_POOL), lambda i: (pbatch(i), 0, 0)),
        pl.BlockSpec((CONV_W - 1, DEC_BATCH, D_CONV), lambda i: (0, 0, 0)),
        pl.BlockSpec((DEC_SEQ, DEC_BATCH, D_POOL), lambda i: (0, 0, 0)),
    )
    in_specs = [
        pl.BlockSpec((TS, D_MODEL), lambda i: (ptile(i), 0)),
        const((N_SAMPLE, D_MODEL)),
        const((CONV_W - 1, DEC_BATCH, D_CONV)),
        const((POOL_BUF, DEC_BATCH, D_POOL)),
        const((1, D_MODEL)),
        const((D_MODEL, D_IN)),
        const((CONV_W, D_CONV)),
        const((len(POOL_WINDOWS), POOL_GROUP, POOL_GROUP)),
        const((1, D_POOL)),
        const((D_MODEL, D_MODEL)),
        const((1, D_MODEL)),
        const((2 * N_EXPERTS, D_MODEL)),
        const((N_EXPERTS, 1)),
        const((TS, TS)),
    ]
    return pl.pallas_call(
        _mixer_kernel,
        grid=(N_TILES,),
        in_specs=in_specs,
        out_specs=out_specs,
        out_shape=out_shape,
        scratch_shapes=[
            pltpu.VMEM((TS + HIST, D_CONV), f32),
            pltpu.VMEM((TS + HIST, D_POOL), f32),
            pltpu.VMEM((TS, D_MODEL), f32),
            pltpu.VMEM((TS, D_MODEL), f32),
        ],
        compiler_params=pltpu.CompilerParams(
            dimension_semantics=("arbitrary",), vmem_limit_bytes=VMEM_LIMIT),
        name="mixer",
    )(xp, xs, sc_t, sp_t, g1, win, cw, pw, ps, wout, g2, rw2, rb, tri)


def _expert_kernel(te_ref, nu_ref, first_ref, ord_ref, nxt_ref, unit_ref, half_ref, pt_ref, tot_ref,
                   b1_ref, b2_ref, srt_hbm, w1_hbm, w2_hbm, ys_hbm,
                   xbuf, ybuf, zpiece, w1f, w2f, w1b, w2b, sem_w, sem_x, sem_y, sem_z):
    i = pl.program_id(0)
    n_used = nu_ref[0]
    live = i < n_used

    def weight_copies(e, slot):
        return (pltpu.make_async_copy(w1_hbm.at[e], w1f.at[slot], sem_w.at[slot]),
                pltpu.make_async_copy(w2_hbm.at[e], w2f.at[slot], sem_w.at[2 + slot]))

    def is_padding(entry, tile, p):
        if p < UNIT_PIECES:
            return entry < 0
        return (entry < 0) | (half_ref[tile] == 1)

    def start_x(tile, b):
        for p in range(PIECES):
            src = pt_ref[unit_ref[tile] * UNIT_PIECES + p]
            src = pl.multiple_of(jnp.where(is_padding(src, tile, p), ZERO_ROW, src), PIECE)
            pltpu.make_async_copy(srt_hbm.at[pl.ds(src, PIECE)],
                                  xbuf.at[b, pl.ds(p * PIECE, PIECE)], sem_x.at[b]).start()

    def start_y(tile, b, all_dump=False):
        for p in range(PIECES):
            dst = pt_ref[unit_ref[tile] * UNIT_PIECES + p]
            dump = is_padding(dst, tile, p) | all_dump
            dst = pl.multiple_of(jnp.where(dump, DUMP_ROW + b * TM + p * PIECE, dst), PIECE)
            pltpu.make_async_copy(ybuf.at[b, pl.ds(p * PIECE, PIECE)],
                                  ys_hbm.at[pl.ds(dst, PIECE)], sem_y.at[b]).start()

    def wait_x(b):
        pltpu.make_async_copy(srt_hbm.at[pl.ds(0, TM)], xbuf.at[b], sem_x.at[b]).wait()

    def wait_y(b):
        pltpu.make_async_copy(ybuf.at[b], ys_hbm.at[pl.ds(0, TM)], sem_y.at[b]).wait()

    @pl.when(i == 0)
    def _():
        zpiece[...] = jnp.zeros_like(zpiece)

        def fill_tile(t, c):
            def dst(p):
                row = pl.multiple_of(t * S_MAX + tot_ref[t] + p * PIECE, PIECE)
                return ys_hbm.at[pl.ds(row, PIECE)]

            def start(p, c2):
                pltpu.make_async_copy(zpiece, dst(p), sem_z.at[0]).start()
                return c2
            return lax.fori_loop(0, (S_MAX - tot_ref[t]) // PIECE, start, c)
        lax.fori_loop(0, N_TILES, fill_tile, 0)
        start_x(0, 0)
        start_x(jnp.minimum(1, n_used - 1), 1)
        ybuf[...] = jnp.zeros_like(ybuf)
        for b in range(2):
            pltpu.make_async_copy(ybuf.at[b], ys_hbm.at[pl.ds(DUMP_ROW + b * TM, TM)],
                                  sem_y.at[b]).start()

    @pl.when(live)
    def _():
        ring = i % 3
        expert = te_ref[i]
        wait_x(ring)

        @pl.when(first_ref[i] == 1)
        def _():
            slot = ord_ref[i] % 2

            @pl.when(i == 0)
            def _():
                for cp in weight_copies(te_ref[0], 0):
                    cp.start()
            for cp in weight_copies(expert, slot):
                cp.wait()

            @pl.when(nxt_ref[i] >= 0)
            def _():
                for cp in weight_copies(nxt_ref[i], 1 - slot):
                    cp.start()
            w1b[...] = w1f[slot].astype(bf16)
            w2b[...] = w2f[slot].astype(bf16)

        wait_y(ring)

        def mlp(rows):
            x = xbuf[ring, 0:rows, :]
            gu = jnp.dot(x, w1b[...], preferred_element_type=f32) + b1_ref[expert]
            gate = jnp.minimum(gu[:, :D_FF], SWIGLU_LIMIT)
            lin = jnp.clip(gu[:, D_FF:], -SWIGLU_LIMIT, SWIGLU_LIMIT)
            glu = gate * jax.nn.sigmoid(SWIGLU_ALPHA * gate)
            h = (glu * (lin + 1.0)).astype(bf16)
            start_x(jnp.minimum(i + 2, n_used - 1), (i + 2) % 3)
            start_y(jnp.maximum(i - 1, 0), (i + 2) % 3, all_dump=i == 0)
            out = jnp.dot(h, w2b[...], preferred_element_type=f32) + b2_ref[expert]
            ybuf[ring, 0:rows, :] = out.astype(bf16)

        @pl.when(half_ref[i] == 0)
        def _():
            mlp(TM)

        @pl.when(half_ref[i] == 1)
        def _():
            mlp(UNIT)

        @pl.when(i == n_used - 1)
        def _():
            start_y(i, ring)
            for b in range(3):
                wait_y(b)
            wait_x((i + 1) % 3)
            wait_x((i + 2) % 3)

            def drain_tile(t, c):
                def wait(p, c2):
                    pltpu.make_async_copy(zpiece, ys_hbm.at[pl.ds(0, PIECE)], sem_z.at[0]).wait()
                    return c2
                return lax.fori_loop(0, (S_MAX - tot_ref[t]) // PIECE, wait, c)
            lax.fori_loop(0, N_TILES, drain_tile, 0)


def _experts(tile_expert, n_used, first, order, nxt, unit0, half, piece_row, tot, srt, w1, b1, w2, b2):
    whole = lambda shape: pl.BlockSpec(shape, lambda i, *_: (0,) * len(shape))
    grid_spec = pltpu.PrefetchScalarGridSpec(
        num_scalar_prefetch=9,
        grid=(N_ROW_TILES,),
        in_specs=[
            whole((N_EXPERTS, 1, 2 * D_FF)),
            whole((N_EXPERTS, 1, D_MODEL)),
            pl.BlockSpec(memory_space=pl.ANY),
            pl.BlockSpec(memory_space=pl.ANY),
            pl.BlockSpec(memory_space=pl.ANY),
        ],
        out_specs=pl.BlockSpec(memory_space=pl.ANY),
        scratch_shapes=[
            pltpu.VMEM((3, TM, D_MODEL), bf16), pltpu.VMEM((3, TM, D_MODEL), bf16),
            pltpu.VMEM((PIECE, D_MODEL), bf16),
            pltpu.VMEM((2, D_MODEL, 2 * D_FF), f32), pltpu.VMEM((2, D_FF, D_MODEL), f32),
            pltpu.VMEM((D_MODEL, 2 * D_FF), bf16), pltpu.VMEM((D_FF, D_MODEL), bf16),
            pltpu.SemaphoreType.DMA((4,)), pltpu.SemaphoreType.DMA((3,)),
            pltpu.SemaphoreType.DMA((3,)), pltpu.SemaphoreType.DMA((1,)),
        ],
    )
    return pl.pallas_call(
        _expert_kernel,
        grid_spec=grid_spec,
        out_shape=jax.ShapeDtypeStruct((N_SORT_ROWS + 3 * TM, D_MODEL), bf16),
        compiler_params=pltpu.CompilerParams(
            dimension_semantics=("arbitrary",), vmem_limit_bytes=VMEM_LIMIT),
        name="experts",
    )(tile_expert, n_used, first, order, nxt, unit0, half, piece_row, tot, b1, b2, srt, w1, w2)


def _combine_kernel(route_ref, x1_ref, gf_ref, ys_ref, yp_ref, ysm_ref, wgt):
    i = pl.program_id(0)

    slot = route_ref[0:TOP_K, :]
    g = route_ref[TOP_K:2 * TOP_K, :].astype(bf16)
    row_io = lax.broadcasted_iota(i32, (S_BLK, TS), 0).astype(bf16)
    for b in range(S_MAX // S_BLK):
        local = slot - float(b * S_BLK)
        local = jnp.where((local >= 0.0) & (local < float(S_BLK)), local, -1.0).astype(bf16)
        w = jnp.zeros((S_BLK, TS), bf16)
        for k in range(TOP_K):
            w = jnp.where(row_io == local[k:k + 1, :], g[k:k + 1, :], w)
        wgt[b * S_BLK:(b + 1) * S_BLK, :] = w

    y = x1_ref[...] + lax.dot_general(wgt[...], ys_ref[...], (((0,), (0,)), ((), ())),
                                      preferred_element_type=f32)
    out = _rms(y, gf_ref[...])

    @pl.when(i < N_PROMPT_TILES)
    def _():
        yp_ref[...] = out

    @pl.when(i == N_PROMPT_TILES)
    def _():
        ysm_ref[...] = out


def _combine(route_tm, x1, gf, ys):
    ptile = lambda i: (jnp.minimum(i, N_PROMPT_TILES - 1), 0)
    return pl.pallas_call(
        _combine_kernel,
        grid=(N_TILES,),
        in_specs=[
            pl.BlockSpec((2 * TOP_K, TS), lambda i: (0, i)),
            pl.BlockSpec((TS, D_MODEL), lambda i: (i, 0)),
            pl.BlockSpec((1, D_MODEL), lambda i: (0, 0)),
            pl.BlockSpec((S_MAX, D_MODEL), lambda i: (i, 0)),
        ],
        out_specs=(
            pl.BlockSpec((TS, D_MODEL), ptile),
            pl.BlockSpec((N_SAMPLE, D_MODEL), lambda i: (0, 0)),
        ),
        out_shape=(
            jax.ShapeDtypeStruct((N_PROMPT, D_MODEL), f32),
            jax.ShapeDtypeStruct((N_SAMPLE, D_MODEL), f32),
        ),
        scratch_shapes=[pltpu.VMEM((S_MAX, TS), bf16)],
        compiler_params=pltpu.CompilerParams(
            dimension_semantics=("arbitrary",), vmem_limit_bytes=VMEM_LIMIT),
        name="combine",
    )(route_tm, x1, gf, ys)


def _routing_tables(cnt):
    n = cnt.reshape(N_TILES, N_EXPERTS).astype(i32)
    c = (n + (PIECE - 1)) // PIECE * PIECE
    lo = jnp.cumsum(c, axis=1) - c
    tot = jnp.sum(c, axis=1)
    group = jnp.sum(c, axis=0)
    padded = (group + (UNIT - 1)) // UNIT * UNIT
    gend = jnp.cumsum(padded)
    gstart = gend - padded
    cs = gstart[None, :] + jnp.cumsum(c, axis=0) - c
    units = padded // UNIT
    tiles = (units + 1) // 2
    tend = jnp.cumsum(tiles)
    n_used = tend[-1]
    tile_id = jnp.arange(N_ROW_TILES, dtype=i32)
    tile_expert = jnp.sum((tile_id[:, None] >= tend[None, :]).astype(i32), axis=1)
    tile_expert = jnp.minimum(tile_expert, jnp.take(tile_expert, n_used - 1))
    e_ar = jnp.arange(N_EXPERTS, dtype=i32)
    mine = tile_expert[:, None] == e_ar[None, :]
    pick = lambda v: jnp.sum(jnp.where(mine, v[None, :], 0), axis=1)
    in_group = tile_id - pick(tend - tiles)
    unit0 = pick(gstart // UNIT) + 2 * in_group
    half = (2 * in_group + 1 == pick(units)).astype(i32)
    first = (tile_id < n_used) & ((tile_id == 0) | (tile_expert != jnp.roll(tile_expert, 1)))
    order = jnp.maximum(jnp.cumsum(first.astype(i32)) - 1, 0)
    later = (e_ar[None, :] > e_ar[:, None]) & (padded[None, :] > 0)
    nxt_e = jnp.min(jnp.where(later, e_ar[None, :], N_EXPERTS), axis=1)
    nxt_e = jnp.where(nxt_e == N_EXPERTS, -1, nxt_e)
    nxt = pick(nxt_e)
    cs_f = cs.T.reshape(-1)
    c_f = c.T.reshape(-1)
    src_f = (jnp.arange(N_TILES, dtype=i32)[:, None] * S_MAX + lo).T.reshape(-1)
    piece_r = jnp.arange((N_UNITS + 1) * UNIT_PIECES, dtype=i32) * PIECE
    reached = (piece_r[:, None] >= cs_f[None, 1:]).astype(i32)

    def lookup(f):
        return f[0] + jnp.sum(reached * (f[1:] - f[:-1])[None, :], axis=1)
    off = piece_r - lookup(cs_f)
    piece_row = jnp.where(off < lookup(c_f), lookup(src_f) + off, -1)
    return dict(tot=tot, n_used=n_used[None], tile_expert=tile_expert, first=first.astype(i32),
                order=order, nxt=nxt, unit0=unit0, half=half, piece_row=piece_row)


def kernel(x_prompt, x_sample, state_conv, state_pool, norm_mix_g, w_in, conv_w, pool_w, pool_scale,
           w_out, norm_ffn_g, router_w, router_b, exp_w1, exp_b1, exp_w2, exp_b2, final_norm_g):
    l = 0
    xp = x_prompt.reshape(N_PROMPT, D_MODEL)
    xs = jnp.transpose(x_sample, (1, 0, 2)).reshape(N_SAMPLE, D_MODEL)
    sc_t = jnp.transpose(state_conv[l], (1, 0, 2))
    sp_t = jnp.transpose(state_pool[l], (1, 0, 2))
    rw_t = router_w[l].T
    rwh = rw_t.astype(bf16)
    rw2 = jnp.concatenate([rwh, (rw_t - rwh.astype(f32)).astype(bf16)], axis=0)
    tok = jnp.arange(TS, dtype=i32)
    tri = (tok[:, None] < tok[None, :]).astype(bf16)

    (x1, srt, route_t, cnt, npc, npp, nsc, nsv) = _mixer(
        xp, xs, sc_t, sp_t, norm_mix_g[l][None, :], w_in[l].astype(bf16), conv_w[l],
        pool_w[l].astype(bf16), pool_scale[l][None, :], w_out[l].astype(bf16),
        norm_ffn_g[l][None, :], rw2, router_b[l][:, None], tri)

    t = _routing_tables(cnt)
    ys = _experts(t["tile_expert"], t["n_used"], t["first"], t["order"], t["nxt"], t["unit0"],
                  t["half"], t["piece_row"], t["tot"], srt, exp_w1[l], exp_b1[l][:, None, :],
                  exp_w2[l], exp_b2[l][:, None, :])
    y_p, y_s = _combine(route_t, x1, final_norm_g[None, :], ys)

    y_prompt = y_p.reshape(BATCH, SEQ, D_MODEL)
    y_sample = jnp.transpose(y_s.reshape(DEC_SEQ, DEC_BATCH, D_MODEL), (1, 0, 2))
    new_conv_prompt = npc[None, :, 8 - (CONV_W - 1):, :]
    new_pool_prompt = npp[None, :, HIST - POOL_BUF:, :]
    new_conv_sample = jnp.transpose(nsc, (1, 0, 2))[None]
    new_pool_sample = jnp.concatenate(
        [state_pool[l][:, DEC_SEQ:, :], jnp.transpose(nsv, (1, 0, 2))], axis=1)[None]
    return (y_prompt, y_sample, new_conv_prompt, new_pool_prompt, new_conv_sample, new_pool_sample)
```

```python
import jax
import jax.numpy as jnp
from jax import lax
from jax.experimental import pallas as pl
from jax.experimental.pallas import tpu as pltpu

D_MODEL = 1024
D_CONV = 512
D_POOL = 512
D_IN = 3 * D_CONV + D_POOL
CONV_W = 3
POOL_WINDOWS = (2, 4, 8, 16)
POOL_GROUP = 128
POOL_BUF = 15
N_EXPERTS = 32
TOP_K = 4
D_FF = 1024
SWIGLU_LIMIT = 7.0
SWIGLU_ALPHA = 1.702
EPS = 1e-5
PAST_LEN = 16384

BATCH, SEQ = 8, 2048
DEC_BATCH, DEC_SEQ = 128, 4
N_PROMPT = BATCH * SEQ
N_SAMPLE = DEC_BATCH * DEC_SEQ
N_TOK = N_PROMPT + N_SAMPLE

TS = 512
SEQ_TILES = SEQ // TS
N_PROMPT_TILES = N_PROMPT // TS
N_TILES = N_PROMPT_TILES + 1
HIST = 16
UNIT = 256
TM = 2 * UNIT
PIECE = 8
PIECES = TM // PIECE
UNIT_PIECES = UNIT // PIECE
S_BLK = 256
SORT_BLK = 512
S_MAX = (TS * TOP_K + N_EXPERTS * (PIECE - 1) + PIECE + S_BLK - 1) // S_BLK * S_BLK
ZERO_ROW = S_MAX - PIECE
N_UNITS = (N_TOK * TOP_K + N_TILES * N_EXPERTS * (PIECE - 1) + N_EXPERTS * (UNIT - PIECE)) // UNIT
N_ROW_TILES = (N_UNITS + N_EXPERTS) // 2
N_SORT_ROWS = N_TILES * S_MAX
DUMP_ROW = N_SORT_ROWS

VMEM_LIMIT = 56 * 1024 * 1024

f32 = jnp.float32
bf16 = jnp.bfloat16
i32 = jnp.int32


def _rms(x, g):
    return x * lax.rsqrt(jnp.mean(x * x, axis=-1, keepdims=True) + EPS) * g


def _pool_project(p, pw_ref, ps_ref):
    outs = []
    for g in range(len(POOL_WINDOWS)):
        sl = slice(g * POOL_GROUP, (g + 1) * POOL_GROUP)
        outs.append(jnp.dot(p[:, sl].astype(bf16), pw_ref[g], preferred_element_type=f32))
    return jnp.concatenate(outs, axis=1) * ps_ref[...]


def _route_and_sort(x, z, wout_ref, g2_ref, rw2_ref, rb_ref, tri_ref,
                    x1_ref, srt_ref, route_ref, cnt_ref):
    n = x.shape[0]
    x1 = x + jnp.dot(z.astype(bf16), wout_ref[...], preferred_element_type=f32)
    x1_ref[...] = x1
    xn = _rms(x1, g2_ref[...])
    xb = xn.astype(bf16)

    xlo = (xn - xb.astype(f32)).astype(bf16)
    nt = (((1,), (1,)), ((), ()))
    hi_lo = lax.dot_general(rw2_ref[...], xb, nt, preferred_element_type=f32)
    logits = (hi_lo[:N_EXPERTS] + hi_lo[N_EXPERTS:]
              + lax.dot_general(rw2_ref[0:N_EXPERTS, :], xlo, nt, preferred_element_type=f32)
              + rb_ref[...])

    e_iota = lax.broadcasted_iota(i32, (N_EXPERTS, n), 0)
    work = logits
    hots, vals = [], []
    for _ in range(TOP_K):
        m = jnp.max(work, axis=0, keepdims=True)
        ik = jnp.min(jnp.where(work == m, e_iota, N_EXPERTS), axis=0, keepdims=True)
        hot = e_iota == ik
        work = jnp.where(hot, -jnp.inf, work)
        hots.append(hot); vals.append(m)
    exps = [jnp.exp(v - vals[0]) for v in vals]
    den = exps[0] + exps[1] + exps[2] + exps[3]
    gates = [e / den for e in exps]

    multi = jnp.where(hots[0] | hots[1] | hots[2] | hots[3], 1.0, 0.0)
    before = jnp.dot(multi.astype(bf16), tri_ref[...], preferred_element_type=f32)
    count = jnp.sum(multi, axis=1, keepdims=True)
    cnt_ref[...] = count
    pieces = jnp.floor((count + (PIECE - 1)) * (1.0 / PIECE))
    er = lax.broadcasted_iota(i32, (N_EXPERTS, N_EXPERTS), 0)
    ec = lax.broadcasted_iota(i32, (N_EXPERTS, N_EXPERTS), 1)
    lower = jnp.where(ec < er, 1.0, 0.0).astype(bf16)
    pieces_b = jnp.broadcast_to(pieces, (N_EXPERTS, 128)).astype(bf16)
    start = PIECE * jnp.dot(lower, pieces_b, preferred_element_type=f32)[:, 0:1]
    base = before + start
    slot = jnp.concatenate(
        [jnp.sum(jnp.where(h, base, 0.0), axis=0, keepdims=True) for h in hots], axis=0).astype(i32)
    route_ref[...] = jnp.concatenate([slot.astype(f32)] + gates, axis=0)

    row_io = lax.broadcasted_iota(i32, (S_BLK, n), 0).astype(bf16)
    slot_f = slot.astype(f32)
    one, zero = jnp.ones((S_BLK, n), bf16), jnp.zeros((S_BLK, n), bf16)
    for row0 in range(0, S_MAX, SORT_BLK):
        rows = min(SORT_BLK, S_MAX - row0)
        parts = []
        for sub in range(row0, row0 + rows, S_BLK):
            local = slot_f - float(sub)
            local = jnp.where((local >= 0.0) & (local < float(S_BLK)), local, -1.0).astype(bf16)
            hit = row_io == local[0:1, :]
            for k in range(1, TOP_K):
                hit = hit | (row_io == local[k:k + 1, :])
            parts.append(jnp.where(hit, one, zero))
        sel = parts[0] if len(parts) == 1 else jnp.concatenate(parts, axis=0)
        srt_ref[row0:row0 + rows, :] = jnp.dot(sel, xb, preferred_element_type=f32).astype(bf16)


def _mixer_kernel(xp_in_ref, xs_in_ref, sc_ref, sp_ref, g1_ref, win_ref, cw_ref, pw_ref, ps_ref,
                  wout_ref, g2_ref, rw2_ref, rb_ref, tri_ref,
                  x1_ref, srt_ref, route_ref, cnt_ref,
                  npc_ref, npp_ref, nsc_ref, nsv_ref,
                  ubuf, vbuf, xcur, zcur):
    i = pl.program_id(0)

    @pl.when(i < N_PROMPT_TILES)
    def _():
        s = i % SEQ_TILES

        @pl.when(s == 0)
        def _():
            ubuf[0:HIST, :] = jnp.zeros((HIST, D_CONV), f32)
            vbuf[0:HIST, :] = jnp.zeros((HIST, D_POOL), f32)

        x = xp_in_ref[...]
        xcur[...] = x
        xn = _rms(x, g1_ref[...]).astype(bf16)
        proj = jnp.dot(xn, win_ref[...], preferred_element_type=f32)
        bg = proj[:, :D_CONV]
        u = proj[:, D_CONV:2 * D_CONV] * proj[:, 2 * D_CONV:3 * D_CONV]
        v = proj[:, 3 * D_CONV:]
        ubuf[HIST:HIST + TS, :] = u
        vbuf[HIST:HIST + TS, :] = v

        uh = ubuf[...]
        y = (pltpu.roll(uh, 2, axis=0) * cw_ref[0:1, :]
             + pltpu.roll(uh, 1, axis=0) * cw_ref[1:2, :])[HIST:, :] + u * cw_ref[2:3, :]
        zcur[:, :D_CONV] = bg * y

        pos = s * TS + lax.broadcasted_iota(i32, (TS, 1), 0)
        ps = []
        for g, w in enumerate(POOL_WINDOWS):
            sl = slice(g * POOL_GROUP, (g + 1) * POOL_GROUP)
            acc = vbuf[:, sl]
            for step in range(g + 1):
                acc = acc + pltpu.roll(acc, 1 << step, axis=0)
            cnt = jnp.minimum(w, pos + 1).astype(f32)
            ps.append(acc[HIST:, :] * (1.0 / cnt) - v[:, sl])
        zcur[:, D_CONV:] = _pool_project(jnp.concatenate(ps, axis=1), pw_ref, ps_ref)

        ubuf[HIST - 8:HIST, :] = ubuf[TS + HIST - 8:TS + HIST, :]
        vbuf[0:HIST, :] = vbuf[TS:TS + HIST, :]

        @pl.when(s == SEQ_TILES - 1)
        def _():
            npc_ref[0] = ubuf[HIST - 8:HIST, :]
            npp_ref[0] = vbuf[0:HIST, :]

    @pl.when(i == N_PROMPT_TILES)
    def _():
        x = xs_in_ref[...]
        xcur[...] = x
        xn = _rms(x, g1_ref[...]).astype(bf16)
        proj = jnp.dot(xn, win_ref[...], preferred_element_type=f32)
        nb = DEC_BATCH
        rows = lambda a, t: a[t * nb:(t + 1) * nb]
        bg = proj[:, :D_CONV]
        u = proj[:, D_CONV:2 * D_CONV] * proj[:, 2 * D_CONV:3 * D_CONV]
        v = proj[:, 3 * D_CONV:]
        up = [sc_ref[j] for j in range(CONV_W - 1)] + [rows(u, t) for t in range(DEC_SEQ)]
        vp = [sp_ref[j] for j in range(POOL_BUF)] + [rows(v, t) for t in range(DEC_SEQ)]
        zc, ps = [], []
        for t in range(DEC_SEQ):
            y = up[t] * cw_ref[0:1, :]
            for k in range(1, CONV_W):
                y = y + up[t + k] * cw_ref[k:k + 1, :]
            zc.append(rows(bg, t) * y)
            pg = []
            for g, w in enumerate(POOL_WINDOWS):
                sl = slice(g * POOL_GROUP, (g + 1) * POOL_GROUP)
                lo = t + POOL_BUF - w + 1
                acc = vp[lo][:, sl]
                for j in range(lo + 1, t + POOL_BUF + 1):
                    acc = acc + vp[j][:, sl]
                cnt = float(min(w, PAST_LEN + t + 1))
                pg.append(acc * (1.0 / cnt) - vp[t + POOL_BUF][:, sl])
            ps.append(jnp.concatenate(pg, axis=1))
        zcur[:, :D_CONV] = jnp.concatenate(zc, axis=0)
        zcur[:, D_CONV:] = _pool_project(jnp.concatenate(ps, axis=0), pw_ref, ps_ref)
        for j in range(CONV_W - 1):
            nsc_ref[j] = up[DEC_SEQ + j]
        for t in range(DEC_SEQ):
            nsv_ref[t] = rows(v, t)

    _route_and_sort(xcur[...], zcur[...], wout_ref, g2_ref, rw2_ref, rb_ref, tri_ref,
                    x1_ref, srt_ref, route_ref, cnt_ref)


def _mixer(xp, xs, sc_t, sp_t, g1, win, cw, pw, ps, wout, g2, rw2, rb, tri):
    const = lambda shape: pl.BlockSpec(shape, lambda i: (0,) * len(shape),
                                       pipeline_mode=pl.Buffered(1))
    ptile = lambda i: jnp.minimum(i, N_PROMPT_TILES - 1)
    pbatch = lambda i: jnp.minimum(i, N_PROMPT_TILES - 1) // SEQ_TILES
    out_shape = (
        jax.ShapeDtypeStruct((N_TOK, D_MODEL), f32),
        jax.ShapeDtypeStruct((N_SORT_ROWS, D_MODEL), bf16),
        jax.ShapeDtypeStruct((2 * TOP_K, N_TOK), f32),
        jax.ShapeDtypeStruct((N_TILES * N_EXPERTS, 1), f32),
        jax.ShapeDtypeStruct((BATCH, 8, D_CONV), f32),
        jax.ShapeDtypeStruct((BATCH, HIST, D_POOL), f32),
        jax.ShapeDtypeStruct((CONV_W - 1, DEC_BATCH, D_CONV), f32),
        jax.ShapeDtypeStruct((DEC_SEQ, DEC_BATCH, D_POOL), f32),
    )
    out_specs = (
        pl.BlockSpec((TS, D_MODEL), lambda i: (i, 0)),
        pl.BlockSpec((S_MAX, D_MODEL), lambda i: (i, 0)),
        pl.BlockSpec((2 * TOP_K, TS), lambda i: (0, i)),
        pl.BlockSpec((N_EXPERTS, 1), lambda i: (i, 0)),
        pl.BlockSpec((1, 8, D_CONV), lambda i: (pbatch(i), 0, 0)),
        pl.BlockSpec((1, HIST, D_POOL), lambda i: (pbatch(i), 0, 0)),
        pl.BlockSpec((CONV_W - 1, DEC_BATCH, D_CONV), lambda i: (0, 0, 0)),
        pl.BlockSpec((DEC_SEQ, DEC_BATCH, D_POOL), lambda i: (0, 0, 0)),
    )
    in_specs = [
        pl.BlockSpec((TS, D_MODEL), lambda i: (ptile(i), 0)),
        const((N_SAMPLE, D_MODEL)),
        const((CONV_W - 1, DEC_BATCH, D_CONV)),
        const((POOL_BUF, DEC_BATCH, D_POOL)),
        const((1, D_MODEL)),
        const((D_MODEL, D_IN)),
        const((CONV_W, D_CONV)),
        const((len(POOL_WINDOWS), POOL_GROUP, POOL_GROUP)),
        const((1, D_POOL)),
        const((D_MODEL, D_MODEL)),
        const((1, D_MODEL)),
        const((2 * N_EXPERTS, D_MODEL)),
        const((N_EXPERTS, 1)),
        const((TS, TS)),
    ]
    return pl.pallas_call(
        _mixer_kernel,
        grid=(N_TILES,),
        in_specs=in_specs,
        out_specs=out_specs,
        out_shape=out_shape,
        scratch_shapes=[
            pltpu.VMEM((TS + HIST, D_CONV), f32),
            pltpu.VMEM((TS + HIST, D_POOL), f32),
            pltpu.VMEM((TS, D_MODEL), f32),
            pltpu.VMEM((TS, D_MODEL), f32),
        ],
        compiler_params=pltpu.CompilerParams(
            dimension_semantics=("arbitrary",), vmem_limit_bytes=VMEM_LIMIT),
        name="mixer",
    )(xp, xs, sc_t, sp_t, g1, win, cw, pw, ps, wout, g2, rw2, rb, tri)


def _expert_kernel(te_ref, nu_ref, first_ref, ord_ref, nxt_ref, unit_ref, half_ref, pt_ref, tot_ref,
                   b1_ref, b2_ref, srt_hbm, w1_hbm, w2_hbm, ys_hbm,
                   xbuf, ybuf, zpiece, w1f, w2f, w1b, w2b, sem_w, sem_x, sem_y, sem_z):
    i = pl.program_id(0)
    n_used = nu_ref[0]
    live = i < n_used

    def weight_copies(e, slot):
        return (pltpu.make_async_copy(w1_hbm.at[e], w1f.at[slot], sem_w.at[slot]),
                pltpu.make_async_copy(w2_hbm.at[e], w2f.at[slot], sem_w.at[2 + slot]))

    def is_padding(entry, tile, p):
        if p < UNIT_PIECES:
            return entry < 0
        return (entry < 0) | (half_ref[tile] == 1)

    def start_x(tile, b):
        for p in range(PIECES):
            src = pt_ref[unit_ref[tile] * UNIT_PIECES + p]
            src = pl.multiple_of(jnp.where(is_padding(src, tile, p), ZERO_ROW, src), PIECE)
            pltpu.make_async_copy(srt_hbm.at[pl.ds(src, PIECE)],
                                  xbuf.at[b, pl.ds(p * PIECE, PIECE)], sem_x.at[b]).start()

    def start_y(tile, b, all_dump=False):
        for p in range(PIECES):
            dst = pt_ref[unit_ref[tile] * UNIT_PIECES + p]
            dump = is_padding(dst, tile, p) | all_dump
            dst = pl.multiple_of(jnp.where(dump, DUMP_ROW + b * TM + p * PIECE, dst), PIECE)
            pltpu.make_async_copy(ybuf.at[b, pl.ds(p * PIECE, PIECE)],
                                  ys_hbm.at[pl.ds(dst, PIECE)], sem_y.at[b]).start()

    def wait_x(b):
        pltpu.make_async_copy(srt_hbm.at[pl.ds(0, TM)], xbuf.at[b], sem_x.at[b]).wait()

    def wait_y(b):
        pltpu.make_async_copy(ybuf.at[b], ys_hbm.at[pl.ds(0, TM)], sem_y.at[b]).wait()

    @pl.when(i == 0)
    def _():
        zpiece[...] = jnp.zeros_like(zpiece)

        def fill_tile(t, c):
            def dst(p):
                row = pl.multiple_of(t * S_MAX + tot_ref[t] + p * PIECE, PIECE)
                return ys_hbm.at[pl.ds(row, PIECE)]

            def start(p, c2):
                pltpu.make_async_copy(zpiece, dst(p), sem_z.at[0]).start()
                return c2
            return lax.fori_loop(0, (S_MAX - tot_ref[t]) // PIECE, start, c)
        lax.fori_loop(0, N_TILES, fill_tile, 0)
        start_x(0, 0)
        start_x(jnp.minimum(1, n_used - 1), 1)
        ybuf[...] = jnp.zeros_like(ybuf)
        for b in range(2):
            pltpu.make_async_copy(ybuf.at[b], ys_hbm.at[pl.ds(DUMP_ROW + b * TM, TM)],
                                  sem_y.at[b]).start()

    @pl.when(live)
    def _():
        ring = i % 3
        expert = te_ref[i]
        wait_x(ring)

        @pl.when(first_ref[i] == 1)
        def _():
            slot = ord_ref[i] % 2

            @pl.when(i == 0)
            def _():
                for cp in weight_copies(te_ref[0], 0):
                    cp.start()
            for cp in weight_copies(expert, slot):
                cp.wait()

            @pl.when(nxt_ref[i] >= 0)
            def _():
                for cp in weight_copies(nxt_ref[i], 1 - slot):
                    cp.start()
            w1b[...] = w1f[slot].astype(bf16)
            w2b[...] = w2f[slot].astype(bf16)

        wait_y(ring)

        def mlp(rows):
            x = xbuf[ring, 0:rows, :]
            gu = jnp.dot(x, w1b[...], preferred_element_type=f32) + b1_ref[expert]
            gate = jnp.minimum(gu[:, :D_FF], SWIGLU_LIMIT)
            lin = jnp.clip(gu[:, D_FF:], -SWIGLU_LIMIT, SWIGLU_LIMIT)
            glu = gate * jax.nn.sigmoid(SWIGLU_ALPHA * gate)
            h = (glu * (lin + 1.0)).astype(bf16)
            start_x(jnp.minimum(i + 2, n_used - 1), (i + 2) % 3)
            start_y(jnp.maximum(i - 1, 0), (i + 2) % 3, all_dump=i == 0)
            out = jnp.dot(h, w2b[...], preferred_element_type=f32) + b2_ref[expert]
            ybuf[ring, 0:rows, :] = out.astype(bf16)

        @pl.when(half_ref[i] == 0)
        def _():
            mlp(TM)

        @pl.when(half_ref[i] == 1)
        def _():
            mlp(UNIT)

        @pl.when(i == n_used - 1)
        def _():
            start_y(i, ring)
            for b in range(3):
                wait_y(b)
            wait_x((i + 1) % 3)
            wait_x((i + 2) % 3)

            def drain_tile(t, c):
                def wait(p, c2):
                    pltpu.make_async_copy(zpiece, ys_hbm.at[pl.ds(0, PIECE)], sem_z.at[0]).wait()
                    return c2
                return lax.fori_loop(0, (S_MAX - tot_ref[t]) // PIECE, wait, c)
            lax.fori_loop(0, N_TILES, drain_tile, 0)


def _experts(tile_expert, n_used, first, order, nxt, unit0, half, piece_row, tot, srt, w1, b1, w2, b2):
    whole = lambda shape: pl.BlockSpec(shape, lambda i, *_: (0,) * len(shape))
    grid_spec = pltpu.PrefetchScalarGridSpec(
        num_scalar_prefetch=9,
        grid=(N_ROW_TILES,),
        in_specs=[
            whole((N_EXPERTS, 1, 2 * D_FF)),
            whole((N_EXPERTS, 1, D_MODEL)),
            pl.BlockSpec(memory_space=pl.ANY),
            pl.BlockSpec(memory_space=pl.ANY),
            pl.BlockSpec(memory_space=pl.ANY),
        ],
        out_specs=pl.BlockSpec(memory_space=pl.ANY),
        scratch_shapes=[
            pltpu.VMEM((3, TM, D_MODEL), bf16), pltpu.VMEM((3, TM, D_MODEL), bf16),
            pltpu.VMEM((PIECE, D_MODEL), bf16),
            pltpu.VMEM((2, D_MODEL, 2 * D_FF), f32), pltpu.VMEM((2, D_FF, D_MODEL), f32),
            pltpu.VMEM((D_MODEL, 2 * D_FF), bf16), pltpu.VMEM((D_FF, D_MODEL), bf16),
            pltpu.SemaphoreType.DMA((4,)), pltpu.SemaphoreType.DMA((3,)),
            pltpu.SemaphoreType.DMA((3,)), pltpu.SemaphoreType.DMA((1,)),
        ],
    )
    return pl.pallas_call(
        _expert_kernel,
        grid_spec=grid_spec,
        out_shape=jax.ShapeDtypeStruct((N_SORT_ROWS + 3 * TM, D_MODEL), bf16),
        compiler_params=pltpu.CompilerParams(
            dimension_semantics=("arbitrary",), vmem_limit_bytes=VMEM_LIMIT),
        name="experts",
    )(tile_expert, n_used, first, order, nxt, unit0, half, piece_row, tot, b1, b2, srt, w1, w2)


def _combine_kernel(route_ref, x1_ref, gf_ref, ys_ref, yp_ref, ysm_ref, wgt):
    i = pl.program_id(0)

    slot = route_ref[0:TOP_K, :]
    g = route_ref[TOP_K:2 * TOP_K, :].astype(bf16)
    row_io = lax.broadcasted_iota(i32, (S_BLK, TS), 0).astype(bf16)
    for b in range(S_MAX // S_BLK):
        local = slot - float(b * S_BLK)
        local = jnp.where((local >= 0.0) & (local < float(S_BLK)), local, -1.0).astype(bf16)
        w = jnp.zeros((S_BLK, TS), bf16)
        for k in range(TOP_K):
            w = jnp.where(row_io == local[k:k + 1, :], g[k:k + 1, :], w)
        wgt[b * S_BLK:(b + 1) * S_BLK, :] = w

    y = x1_ref[...] + lax.dot_general(wgt[...], ys_ref[...], (((0,), (0,)), ((), ())),
                                      preferred_element_type=f32)
    out = _rms(y, gf_ref[...])

    @pl.when(i < N_PROMPT_TILES)
    def _():
        yp_ref[...] = out

    @pl.when(i == N_PROMPT_TILES)
    def _():
        ysm_ref[...] = out


def _combine(route_tm, x1, gf, ys):
    ptile = lambda i: (jnp.minimum(i, N_PROMPT_TILES - 1), 0)
    return pl.pallas_call(
        _combine_kernel,
        grid=(N_TILES,),
        in_specs=[
            pl.BlockSpec((2 * TOP_K, TS), lambda i: (0, i)),
            pl.BlockSpec((TS, D_MODEL), lambda i: (i, 0)),
            pl.BlockSpec((1, D_MODEL), lambda i: (0, 0)),
            pl.BlockSpec((S_MAX, D_MODEL), lambda i: (i, 0)),
        ],
        out_specs=(
            pl.BlockSpec((TS, D_MODEL), ptile),
            pl.BlockSpec((N_SAMPLE, D_MODEL), lambda i: (0, 0)),
        ),
        out_shape=(
            jax.ShapeDtypeStruct((N_PROMPT, D_MODEL), f32),
            jax.ShapeDtypeStruct((N_SAMPLE, D_MODEL), f32),
        ),
        scratch_shapes=[pltpu.VMEM((S_MAX, TS), bf16)],
        compiler_params=pltpu.CompilerParams(
            dimension_semantics=("arbitrary",), vmem_limit_bytes=VMEM_LIMIT),
        name="combine",
    )(route_tm, x1, gf, ys)


def _routing_tables(cnt):
    n = cnt.reshape(N_TILES, N_EXPERTS).astype(i32)
    c = (n + (PIECE - 1)) // PIECE * PIECE
    lo = jnp.cumsum(c, axis=1) - c
    tot = jnp.sum(c, axis=1)
    group = jnp.sum(c, axis=0)
    padded = (group + (UNIT - 1)) // UNIT * UNIT
    gend = jnp.cumsum(padded)
    gstart = gend - padded
    cs = gstart[None, :] + jnp.cumsum(c, axis=0) - c
    units = padded // UNIT
    tiles = (units + 1) // 2
    tend = jnp.cumsum(tiles)
    n_used = tend[-1]
    tile_id = jnp.arange(N_ROW_TILES, dtype=i32)
    tile_expert = jnp.sum((tile_id[:, None] >= tend[None, :]).astype(i32), axis=1)
    tile_expert = jnp.minimum(tile_expert, jnp.take(tile_expert, n_used - 1))
    e_ar = jnp.arange(N_EXPERTS, dtype=i32)
    mine = tile_expert[:, None] == e_ar[None, :]
    pick = lambda v: jnp.sum(jnp.where(mine, v[None, :], 0), axis=1)
    in_group = tile_id - pick(tend - tiles)
    unit0 = pick(gstart // UNIT) + 2 * in_group
    half = (2 * in_group + 1 == pick(units)).astype(i32)
    first = (tile_id < n_used) & ((tile_id == 0) | (tile_expert != jnp.roll(tile_expert, 1)))
    order = jnp.maximum(jnp.cumsum(first.astype(i32)) - 1, 0)
    later = (e_ar[None, :] > e_ar[:, None]) & (padded[None, :] > 0)
    nxt_e = jnp.min(jnp.where(later, e_ar[None, :], N_EXPERTS), axis=1)
    nxt_e = jnp.where(nxt_e == N_EXPERTS, -1, nxt_e)
    nxt = pick(nxt_e)
    unit_row = jnp.arange(N_UNITS + 1, dtype=i32) * UNIT
    unit_e = jnp.minimum(jnp.sum((unit_row[:, None] >= gend[None, :]).astype(i32), axis=1),
                         N_EXPERTS - 1)
    of_unit = (unit_e[:, None] == e_ar[None, :])[:, None, :]
    per_unit = lambda tab: jnp.sum(jnp.where(of_unit, tab[None], 0), axis=2)
    cs_u, c_u = per_unit(cs), per_unit(c)
    src_u = per_unit(jnp.arange(N_TILES, dtype=i32)[:, None] * S_MAX + lo)
    piece_r = unit_row[:, None] + jnp.arange(UNIT_PIECES, dtype=i32)[None, :] * PIECE
    chunk = jnp.sum((piece_r[:, :, None] >= cs_u[:, None, :]).astype(i32), axis=2) - 1
    is_chunk = chunk[:, :, None] == jnp.arange(N_TILES, dtype=i32)[None, None, :]
    of_chunk = lambda tab_u: jnp.sum(jnp.where(is_chunk, tab_u[:, None, :], 0), axis=2)
    off = piece_r - of_chunk(cs_u)
    piece_row = jnp.where(off < of_chunk(c_u), of_chunk(src_u) + off, -1).reshape(-1)
    return dict(tot=tot, n_used=n_used[None], tile_expert=tile_expert, first=first.astype(i32),
                order=order, nxt=nxt, unit0=unit0, half=half, piece_row=piece_row)


def kernel(x_prompt, x_sample, state_conv, state_pool, norm_mix_g, w_in, conv_w, pool_w, pool_scale,
           w_out, norm_ffn_g, router_w, router_b, exp_w1, exp_b1, exp_w2, exp_b2, final_norm_g):
    l = 0
    xp = x_prompt.reshape(N_PROMPT, D_MODEL)
    xs = jnp.transpose(x_sample, (1, 0, 2)).reshape(N_SAMPLE, D_MODEL)
    sc_t = jnp.transpose(state_conv[l], (1, 0, 2))
    sp_t = jnp.transpose(state_pool[l], (1, 0, 2))
    rw_t = router_w[l].T
    rwh = rw_t.astype(bf16)
    rw2 = jnp.concatenate([rwh, (rw_t - rwh.astype(f32)).astype(bf16)], axis=0)
    tok = jnp.arange(TS, dtype=i32)
    tri = (tok[:, None] < tok[None, :]).astype(bf16)

    (x1, srt, route_t, cnt, npc, npp, nsc, nsv) = _mixer(
        xp, xs, sc_t, sp_t, norm_mix_g[l][None, :], w_in[l].astype(bf16), conv_w[l],
        pool_w[l].astype(bf16), pool_scale[l][None, :], w_out[l].astype(bf16),
        norm_ffn_g[l][None, :], rw2, router_b[l][:, None], tri)

    t = _routing_tables(cnt)
    ys = _experts(t["tile_expert"], t["n_used"], t["first"], t["order"], t["nxt"], t["unit0"],
                  t["half"], t["piece_row"], t["tot"], srt, exp_w1[l], exp_b1[l][:, None, :],
                  exp_w2[l], exp_b2[l][:, None, :])
    y_p, y_s = _combine(route_t, x1, final_norm_g[None, :], ys)

    y_prompt = y_p.reshape(BATCH, SEQ, D_MODEL)
    y_sample = jnp.transpose(y_s.reshape(DEC_SEQ, DEC_BATCH, D_MODEL), (1, 0, 2))
    new_conv_prompt = npc[None, :, 8 - (CONV_W - 1):, :]
    new_pool_prompt = npp[None, :, HIST - POOL_BUF:, :]
    new_conv_sample = jnp.transpose(nsc, (1, 0, 2))[None]
    new_pool_sample = jnp.concatenate(
        [state_pool[l][:, DEC_SEQ:, :], jnp.transpose(nsv, (1, 0, 2))], axis=1)[None]
    return (y_prompt, y_sample, new_conv_prompt, new_pool_prompt, new_conv_sample, new_pool_sample)
```

```python
import jax
import jax.numpy as jnp
from jax import lax
from jax.experimental import pallas as pl
from jax.experimental.pallas import tpu as pltpu

D_MODEL = 1024
D_CONV = 512
D_POOL = 512
D_IN = 3 * D_CONV + D_POOL
CONV_W = 3
POOL_WINDOWS = (2, 4, 8, 16)
POOL_GROUP = 128
POOL_BUF = 15
N_EXPERTS = 32
TOP_K = 4
D_FF = 1024
SWIGLU_LIMIT = 7.0
SWIGLU_ALPHA = 1.702
EPS = 1e-5
PAST_LEN = 16384

BATCH, SEQ = 8, 2048
DEC_BATCH, DEC_SEQ = 128, 4
N_PROMPT = BATCH * SEQ
N_SAMPLE = DEC_BATCH * DEC_SEQ
N_TOK = N_PROMPT + N_SAMPLE

TS = 512
SEQ_TILES = SEQ // TS
N_PROMPT_TILES = N_PROMPT // TS
N_TILES = N_PROMPT_TILES + 1
HIST = 16
UNIT = 256
TM = 2 * UNIT
PIECE = 8
PIECES = TM // PIECE
UNIT_PIECES = UNIT // PIECE
S_BLK = 256
SORT_BLK = 512
S_MAX = (TS * TOP_K + N_EXPERTS * (PIECE - 1) + PIECE + S_BLK - 1) // S_BLK * S_BLK
ZERO_ROW = S_MAX - PIECE
N_UNITS = (N_TOK * TOP_K + N_TILES * N_EXPERTS * (PIECE - 1) + N_EXPERTS * (UNIT - PIECE)) // UNIT
N_ROW_TILES = (N_UNITS + N_EXPERTS) // 2
N_SORT_ROWS = N_TILES * S_MAX
DUMP_ROW = N_SORT_ROWS

VMEM_LIMIT = 58 * 1024 * 1024

f32 = jnp.float32
bf16 = jnp.bfloat16
i32 = jnp.int32


def _rms(x, g):
    return x * lax.rsqrt(jnp.mean(x * x, axis=-1, keepdims=True) + EPS) * g


def _pool_project(p, pw_ref, ps_ref):
    outs = []
    for g in range(len(POOL_WINDOWS)):
        sl = slice(g * POOL_GROUP, (g + 1) * POOL_GROUP)
        outs.append(jnp.dot(p[:, sl].astype(bf16), pw_ref[g], preferred_element_type=f32))
    return jnp.concatenate(outs, axis=1) * ps_ref[...]


def _route_and_sort(x, z, wout_ref, g2_ref, rw2_ref, rb_ref, tri_ref,
                    x1_ref, srt_ref, route_ref, cnt_ref):
    n = x.shape[0]
    x1 = x + jnp.dot(z.astype(bf16), wout_ref[...], preferred_element_type=f32)
    x1_ref[...] = x1
    xn = _rms(x1, g2_ref[...])
    xb = xn.astype(bf16)

    xlo = (xn - xb.astype(f32)).astype(bf16)
    nt = (((1,), (1,)), ((), ()))
    hi_lo = lax.dot_general(rw2_ref[...], xb, nt, preferred_element_type=f32)
    logits = (hi_lo[:N_EXPERTS] + hi_lo[N_EXPERTS:]
              + lax.dot_general(rw2_ref[0:N_EXPERTS, :], xlo, nt, preferred_element_type=f32)
              + rb_ref[...])

    e_iota = lax.broadcasted_iota(i32, (N_EXPERTS, n), 0)
    work = logits
    hots, vals = [], []
    for _ in range(TOP_K):
        m = jnp.max(work, axis=0, keepdims=True)
        ik = jnp.min(jnp.where(work == m, e_iota, N_EXPERTS), axis=0, keepdims=True)
        hot = e_iota == ik
        work = jnp.where(hot, -jnp.inf, work)
        hots.append(hot); vals.append(m)
    exps = [jnp.exp(v - vals[0]) for v in vals]
    den = exps[0] + exps[1] + exps[2] + exps[3]
    gates = [e / den for e in exps]

    multi = jnp.where(hots[0] | hots[1] | hots[2] | hots[3], 1.0, 0.0)
    before = jnp.dot(multi.astype(bf16), tri_ref[...], preferred_element_type=f32)
    count = jnp.sum(multi, axis=1, keepdims=True)
    cnt_ref[...] = count
    pieces = jnp.floor((count + (PIECE - 1)) * (1.0 / PIECE))
    er = lax.broadcasted_iota(i32, (N_EXPERTS, N_EXPERTS), 0)
    ec = lax.broadcasted_iota(i32, (N_EXPERTS, N_EXPERTS), 1)
    lower = jnp.where(ec < er, 1.0, 0.0).astype(bf16)
    pieces_b = jnp.broadcast_to(pieces, (N_EXPERTS, 128)).astype(bf16)
    start = PIECE * jnp.dot(lower, pieces_b, preferred_element_type=f32)[:, 0:1]
    base = before + start
    slot = jnp.concatenate(
        [jnp.sum(jnp.where(h, base, 0.0), axis=0, keepdims=True) for h in hots], axis=0).astype(i32)
    route_ref[...] = jnp.concatenate([slot.astype(f32)] + gates, axis=0)

    row_io = lax.broadcasted_iota(i32, (S_BLK, n), 0).astype(bf16)
    slot_f = slot.astype(f32)
    one, zero = jnp.ones((S_BLK, n), bf16), jnp.zeros((S_BLK, n), bf16)
    for row0 in range(0, S_MAX, SORT_BLK):
        rows = min(SORT_BLK, S_MAX - row0)
        parts = []
        for sub in range(row0, row0 + rows, S_BLK):
            local = slot_f - float(sub)
            local = jnp.where((local >= 0.0) & (local < float(S_BLK)), local, -1.0).astype(bf16)
            hit = row_io == local[0:1, :]
            for k in range(1, TOP_K):
                hit = hit | (row_io == local[k:k + 1, :])
            parts.append(jnp.where(hit, one, zero))
        sel = parts[0] if len(parts) == 1 else jnp.concatenate(parts, axis=0)
        srt_ref[row0:row0 + rows, :] = jnp.dot(sel, xb, preferred_element_type=f32).astype(bf16)


def _mixer_kernel(xp_in_ref, xs_in_ref, sc_ref, sp_ref, g1_ref, win_f32_ref, cw_ref, pw_ref, ps_ref,
                  wout_f32_ref, g2_ref, rw2_ref, rb_ref, tri_ref,
                  x1_ref, srt_ref, route_ref, cnt_ref,
                  npc_ref, npp_ref, nsc_ref, nsv_ref,
                  ubuf, vbuf, xcur, zcur, win_ref, wout_ref):
    i = pl.program_id(0)

    @pl.when(i == 0)
    def _():
        win_ref[...] = win_f32_ref[...].astype(bf16)
        wout_ref[...] = wout_f32_ref[...].astype(bf16)

    @pl.when(i < N_PROMPT_TILES)
    def _():
        s = i % SEQ_TILES

        @pl.when(s == 0)
        def _():
            ubuf[0:HIST, :] = jnp.zeros((HIST, D_CONV), f32)
            vbuf[0:HIST, :] = jnp.zeros((HIST, D_POOL), f32)

        x = xp_in_ref[...]
        xcur[...] = x
        xn = _rms(x, g1_ref[...]).astype(bf16)
        proj = jnp.dot(xn, win_ref[...], preferred_element_type=f32)
        bg = proj[:, :D_CONV]
        u = proj[:, D_CONV:2 * D_CONV] * proj[:, 2 * D_CONV:3 * D_CONV]
        v = proj[:, 3 * D_CONV:]
        ubuf[HIST:HIST + TS, :] = u
        vbuf[HIST:HIST + TS, :] = v

        uh = ubuf[...]
        y = (pltpu.roll(uh, 2, axis=0) * cw_ref[0:1, :]
             + pltpu.roll(uh, 1, axis=0) * cw_ref[1:2, :])[HIST:, :] + u * cw_ref[2:3, :]
        zcur[:, :D_CONV] = bg * y

        pos = s * TS + lax.broadcasted_iota(i32, (TS, 1), 0)
        ps = []
        for g, w in enumerate(POOL_WINDOWS):
            sl = slice(g * POOL_GROUP, (g + 1) * POOL_GROUP)
            acc = vbuf[:, sl]
            for step in range(g + 1):
                acc = acc + pltpu.roll(acc, 1 << step, axis=0)
            cnt = jnp.minimum(w, pos + 1).astype(f32)
            ps.append(acc[HIST:, :] * (1.0 / cnt) - v[:, sl])
        zcur[:, D_CONV:] = _pool_project(jnp.concatenate(ps, axis=1), pw_ref, ps_ref)

        ubuf[HIST - 8:HIST, :] = ubuf[TS + HIST - 8:TS + HIST, :]
        vbuf[0:HIST, :] = vbuf[TS:TS + HIST, :]

        @pl.when(s == SEQ_TILES - 1)
        def _():
            npc_ref[0] = ubuf[HIST - 8:HIST, :]
            npp_ref[0] = vbuf[0:HIST, :]

    @pl.when(i == N_PROMPT_TILES)
    def _():
        x = xs_in_ref[...]
        xcur[...] = x
        xn = _rms(x, g1_ref[...]).astype(bf16)
        proj = jnp.dot(xn, win_ref[...], preferred_element_type=f32)
        nb = DEC_BATCH
        rows = lambda a, t: a[t * nb:(t + 1) * nb]
        bg = proj[:, :D_CONV]
        u = proj[:, D_CONV:2 * D_CONV] * proj[:, 2 * D_CONV:3 * D_CONV]
        v = proj[:, 3 * D_CONV:]
        up = [sc_ref[j] for j in range(CONV_W - 1)] + [rows(u, t) for t in range(DEC_SEQ)]
        vp = [sp_ref[j] for j in range(POOL_BUF)] + [rows(v, t) for t in range(DEC_SEQ)]
        zc, ps = [], []
        for t in range(DEC_SEQ):
            y = up[t] * cw_ref[0:1, :]
            for k in range(1, CONV_W):
                y = y + up[t + k] * cw_ref[k:k + 1, :]
            zc.append(rows(bg, t) * y)
            pg = []
            for g, w in enumerate(POOL_WINDOWS):
                sl = slice(g * POOL_GROUP, (g + 1) * POOL_GROUP)
                lo = t + POOL_BUF - w + 1
                acc = vp[lo][:, sl]
                for j in range(lo + 1, t + POOL_BUF + 1):
                    acc = acc + vp[j][:, sl]
                cnt = float(min(w, PAST_LEN + t + 1))
                pg.append(acc * (1.0 / cnt) - vp[t + POOL_BUF][:, sl])
            ps.append(jnp.concatenate(pg, axis=1))
        zcur[:, :D_CONV] = jnp.concatenate(zc, axis=0)
        zcur[:, D_CONV:] = _pool_project(jnp.concatenate(ps, axis=0), pw_ref, ps_ref)
        for j in range(CONV_W - 1):
            nsc_ref[j] = up[DEC_SEQ + j]
        for t in range(DEC_SEQ):
            nsv_ref[t] = rows(v, t)

    _route_and_sort(xcur[...], zcur[...], wout_ref, g2_ref, rw2_ref, rb_ref, tri_ref,
                    x1_ref, srt_ref, route_ref, cnt_ref)


def _mixer(xp, xs, sc_t, sp_t, g1, win, cw, pw, ps, wout, g2, rw2, rb, tri):
    const = lambda shape: pl.BlockSpec(shape, lambda i: (0,) * len(shape),
                                       pipeline_mode=pl.Buffered(1))
    ptile = lambda i: jnp.minimum(i, N_PROMPT_TILES - 1)
    pbatch = lambda i: jnp.minimum(i, N_PROMPT_TILES - 1) // SEQ_TILES
    out_shape = (
        jax.ShapeDtypeStruct((N_TOK, D_MODEL), f32),
        jax.ShapeDtypeStruct((N_SORT_ROWS, D_MODEL), bf16),
        jax.ShapeDtypeStruct((2 * TOP_K, N_TOK), f32),
        jax.ShapeDtypeStruct((N_TILES * N_EXPERTS, 1), f32),
        jax.ShapeDtypeStruct((BATCH, 8, D_CONV), f32),
        jax.ShapeDtypeStruct((BATCH, HIST, D_POOL), f32),
        jax.ShapeDtypeStruct((CONV_W - 1, DEC_BATCH, D_CONV), f32),
        jax.ShapeDtypeStruct((DEC_SEQ, DEC_BATCH, D_POOL), f32),
    )
    out_specs = (
        pl.BlockSpec((TS, D_MODEL), lambda i: (i, 0)),
        pl.BlockSpec((S_MAX, D_MODEL), lambda i: (i, 0)),
        pl.BlockSpec((2 * TOP_K, TS), lambda i: (0, i)),
        pl.BlockSpec((N_EXPERTS, 1), lambda i: (i, 0)),
        pl.BlockSpec((1, 8, D_CONV), lambda i: (pbatch(i), 0, 0)),
        pl.BlockSpec((1, HIST, D_POOL), lambda i: (pbatch(i), 0, 0)),
        pl.BlockSpec((CONV_W - 1, DEC_BATCH, D_CONV), lambda i: (0, 0, 0)),
        pl.BlockSpec((DEC_SEQ, DEC_BATCH, D_POOL), lambda i: (0, 0, 0)),
    )
    in_specs = [
        pl.BlockSpec((TS, D_MODEL), lambda i: (ptile(i), 0)),
        const((N_SAMPLE, D_MODEL)),
        const((CONV_W - 1, DEC_BATCH, D_CONV)),
        const((POOL_BUF, DEC_BATCH, D_POOL)),
        const((1, D_MODEL)),
        const((D_MODEL, D_IN)),
        const((CONV_W, D_CONV)),
        const((len(POOL_WINDOWS), POOL_GROUP, POOL_GROUP)),
        const((1, D_POOL)),
        const((D_MODEL, D_MODEL)),
        const((1, D_MODEL)),
        const((2 * N_EXPERTS, D_MODEL)),
        const((N_EXPERTS, 1)),
        const((TS, TS)),
    ]
    return pl.pallas_call(
        _mixer_kernel,
        grid=(N_TILES,),
        in_specs=in_specs,
        out_specs=out_specs,
        out_shape=out_shape,
        scratch_shapes=[
            pltpu.VMEM((TS + HIST, D_CONV), f32),
            pltpu.VMEM((TS + HIST, D_POOL), f32),
            pltpu.VMEM((TS, D_MODEL), f32),
            pltpu.VMEM((TS, D_MODEL), f32),
            pltpu.VMEM((D_MODEL, D_IN), bf16),
            pltpu.VMEM((D_MODEL, D_MODEL), bf16),
        ],
        compiler_params=pltpu.CompilerParams(
            dimension_semantics=("arbitrary",), vmem_limit_bytes=VMEM_LIMIT),
        name="mixer",
    )(xp, xs, sc_t, sp_t, g1, win, cw, pw, ps, wout, g2, rw2, rb, tri)


def _expert_kernel(te_ref, nu_ref, first_ref, ord_ref, nxt_ref, unit_ref, half_ref, pt_ref, tot_ref,
                   b1_ref, b2_ref, srt_hbm, w1_hbm, w2_hbm, ys_hbm,
                   xbuf, ybuf, zpiece, w1f, w2f, w1b, w2b, sem_w, sem_x, sem_y, sem_z):
    i = pl.program_id(0)
    n_used = nu_ref[0]
    live = i < n_used

    def weight_copies(e, slot):
        return (pltpu.make_async_copy(w1_hbm.at[e], w1f.at[slot], sem_w.at[slot]),
                pltpu.make_async_copy(w2_hbm.at[e], w2f.at[slot], sem_w.at[2 + slot]))

    def is_padding(entry, tile, p):
        if p < UNIT_PIECES:
            return entry < 0
        return (entry < 0) | (half_ref[tile] == 1)

    def start_x(tile, b):
        for p in range(PIECES):
            src = pt_ref[unit_ref[tile] * UNIT_PIECES + p]
            src = pl.multiple_of(jnp.where(is_padding(src, tile, p), ZERO_ROW, src), PIECE)
            pltpu.make_async_copy(srt_hbm.at[pl.ds(src, PIECE)],
                                  xbuf.at[b, pl.ds(p * PIECE, PIECE)], sem_x.at[b]).start()

    def start_y(tile, b, all_dump=False):
        for p in range(PIECES):
            dst = pt_ref[unit_ref[tile] * UNIT_PIECES + p]
            dump = is_padding(dst, tile, p) | all_dump
            dst = pl.multiple_of(jnp.where(dump, DUMP_ROW + b * TM + p * PIECE, dst), PIECE)
            pltpu.make_async_copy(ybuf.at[b, pl.ds(p * PIECE, PIECE)],
                                  ys_hbm.at[pl.ds(dst, PIECE)], sem_y.at[b]).start()

    def wait_x(b):
        pltpu.make_async_copy(srt_hbm.at[pl.ds(0, TM)], xbuf.at[b], sem_x.at[b]).wait()

    def wait_y(b):
        pltpu.make_async_copy(ybuf.at[b], ys_hbm.at[pl.ds(0, TM)], sem_y.at[b]).wait()

    @pl.when(i == 0)
    def _():
        zpiece[...] = jnp.zeros_like(zpiece)

        def fill_tile(t, c):
            def dst(p):
                row = pl.multiple_of(t * S_MAX + tot_ref[t] + p * PIECE, PIECE)
                return ys_hbm.at[pl.ds(row, PIECE)]

            def start(p, c2):
                pltpu.make_async_copy(zpiece, dst(p), sem_z.at[0]).start()
                return c2
            return lax.fori_loop(0, (S_MAX - tot_ref[t]) // PIECE, start, c)
        lax.fori_loop(0, N_TILES, fill_tile, 0)
        start_x(0, 0)
        start_x(jnp.minimum(1, n_used - 1), 1)
        ybuf[...] = jnp.zeros_like(ybuf)
        for b in range(2):
            pltpu.make_async_copy(ybuf.at[b], ys_hbm.at[pl.ds(DUMP_ROW + b * TM, TM)],
                                  sem_y.at[b]).start()

    @pl.when(live)
    def _():
        ring = i % 3
        expert = te_ref[i]
        wait_x(ring)

        @pl.when(first_ref[i] == 1)
        def _():
            slot = ord_ref[i] % 2

            @pl.when(i == 0)
            def _():
                for cp in weight_copies(te_ref[0], 0):
                    cp.start()
            for cp in weight_copies(expert, slot):
                cp.wait()

            @pl.when(nxt_ref[i] >= 0)
            def _():
                for cp in weight_copies(nxt_ref[i], 1 - slot):
                    cp.start()
            w1b[...] = w1f[slot].astype(bf16)
            w2b[...] = w2f[slot].astype(bf16)

        wait_y(ring)

        def mlp(rows):
            x = xbuf[ring, 0:rows, :]
            gu = jnp.dot(x, w1b[...], preferred_element_type=f32) + b1_ref[expert]
            gate = jnp.minimum(gu[:, :D_FF], SWIGLU_LIMIT)
            lin = jnp.clip(gu[:, D_FF:], -SWIGLU_LIMIT, SWIGLU_LIMIT)
            glu = gate * jax.nn.sigmoid(SWIGLU_ALPHA * gate)
            h = (glu * (lin + 1.0)).astype(bf16)
            start_x(jnp.minimum(i + 2, n_used - 1), (i + 2) % 3)
            start_y(jnp.maximum(i - 1, 0), (i + 2) % 3, all_dump=i == 0)
            out = jnp.dot(h, w2b[...], preferred_element_type=f32) + b2_ref[expert]
            ybuf[ring, 0:rows, :] = out.astype(bf16)

        @pl.when(half_ref[i] == 0)
        def _():
            mlp(TM)

        @pl.when(half_ref[i] == 1)
        def _():
            mlp(UNIT)

        @pl.when(i == n_used - 1)
        def _():
            start_y(i, ring)
            for b in range(3):
                wait_y(b)
            wait_x((i + 1) % 3)
            wait_x((i + 2) % 3)

            def drain_tile(t, c):
                def wait(p, c2):
                    pltpu.make_async_copy(zpiece, ys_hbm.at[pl.ds(0, PIECE)], sem_z.at[0]).wait()
                    return c2
                return lax.fori_loop(0, (S_MAX - tot_ref[t]) // PIECE, wait, c)
            lax.fori_loop(0, N_TILES, drain_tile, 0)


def _experts(tile_expert, n_used, first, order, nxt, unit0, half, piece_row, tot, srt, w1, b1, w2, b2):
    whole = lambda shape: pl.BlockSpec(shape, lambda i, *_: (0,) * len(shape))
    grid_spec = pltpu.PrefetchScalarGridSpec(
        num_scalar_prefetch=9,
        grid=(N_ROW_TILES,),
        in_specs=[
            whole((N_EXPERTS, 1, 2 * D_FF)),
            whole((N_EXPERTS, 1, D_MODEL)),
            pl.BlockSpec(memory_space=pl.ANY),
            pl.BlockSpec(memory_space=pl.ANY),
            pl.BlockSpec(memory_space=pl.ANY),
        ],
        out_specs=pl.BlockSpec(memory_space=pl.ANY),
        scratch_shapes=[
            pltpu.VMEM((3, TM, D_MODEL), bf16), pltpu.VMEM((3, TM, D_MODEL), bf16),
            pltpu.VMEM((PIECE, D_MODEL), bf16),
            pltpu.VMEM((2, D_MODEL, 2 * D_FF), f32), pltpu.VMEM((2, D_FF, D_MODEL), f32),
            pltpu.VMEM((D_MODEL, 2 * D_FF), bf16), pltpu.VMEM((D_FF, D_MODEL), bf16),
            pltpu.SemaphoreType.DMA((4,)), pltpu.SemaphoreType.DMA((3,)),
            pltpu.SemaphoreType.DMA((3,)), pltpu.SemaphoreType.DMA((1,)),
        ],
    )
    return pl.pallas_call(
        _expert_kernel,
        grid_spec=grid_spec,
        out_shape=jax.ShapeDtypeStruct((N_SORT_ROWS + 3 * TM, D_MODEL), bf16),
        compiler_params=pltpu.CompilerParams(
            dimension_semantics=("arbitrary",), vmem_limit_bytes=VMEM_LIMIT),
        name="experts",
    )(tile_expert, n_used, first, order, nxt, unit0, half, piece_row, tot, b1, b2, srt, w1, w2)


def _combine_kernel(route_ref, x1_ref, gf_ref, ys_ref, yp_ref, ysm_ref, wgt):
    i = pl.program_id(0)

    slot = route_ref[0:TOP_K, :]
    g = route_ref[TOP_K:2 * TOP_K, :].astype(bf16)
    row_io = lax.broadcasted_iota(i32, (S_BLK, TS), 0).astype(bf16)
    for b in range(S_MAX // S_BLK):
        local = slot - float(b * S_BLK)
        local = jnp.where((local >= 0.0) & (local < float(S_BLK)), local, -1.0).astype(bf16)
        w = jnp.zeros((S_BLK, TS), bf16)
        for k in range(TOP_K):
            w = jnp.where(row_io == local[k:k + 1, :], g[k:k + 1, :], w)
        wgt[b * S_BLK:(b + 1) * S_BLK, :] = w

    y = x1_ref[...] + lax.dot_general(wgt[...], ys_ref[...], (((0,), (0,)), ((), ())),
                                      preferred_element_type=f32)
    out = _rms(y, gf_ref[...])

    @pl.when(i < N_PROMPT_TILES)
    def _():
        yp_ref[...] = out

    @pl.when(i == N_PROMPT_TILES)
    def _():
        ysm_ref[...] = out


def _combine(route_tm, x1, gf, ys):
    ptile = lambda i: (jnp.minimum(i, N_PROMPT_TILES - 1), 0)
    return pl.pallas_call(
        _combine_kernel,
        grid=(N_TILES,),
        in_specs=[
            pl.BlockSpec((2 * TOP_K, TS), lambda i: (0, i)),
            pl.BlockSpec((TS, D_MODEL), lambda i: (i, 0)),
            pl.BlockSpec((1, D_MODEL), lambda i: (0, 0)),
            pl.BlockSpec((S_MAX, D_MODEL), lambda i: (i, 0)),
        ],
        out_specs=(
            pl.BlockSpec((TS, D_MODEL), ptile),
            pl.BlockSpec((N_SAMPLE, D_MODEL), lambda i: (0, 0)),
        ),
        out_shape=(
            jax.ShapeDtypeStruct((N_PROMPT, D_MODEL), f32),
            jax.ShapeDtypeStruct((N_SAMPLE, D_MODEL), f32),
        ),
        scratch_shapes=[pltpu.VMEM((S_MAX, TS), bf16)],
        compiler_params=pltpu.CompilerParams(
            dimension_semantics=("arbitrary",), vmem_limit_bytes=VMEM_LIMIT),
        name="combine",
    )(route_tm, x1, gf, ys)


def _routing_tables(cnt):
    n = cnt.reshape(N_TILES, N_EXPERTS).astype(i32)
    c = (n + (PIECE - 1)) // PIECE * PIECE
    lo = jnp.cumsum(c, axis=1) - c
    tot = jnp.sum(c, axis=1)
    group = jnp.sum(c, axis=0)
    padded = (group + (UNIT - 1)) // UNIT * UNIT
    gend = jnp.cumsum(padded)
    gstart = gend - padded
    cs = gstart[None, :] + jnp.cumsum(c, axis=0) - c
    units = padded // UNIT
    tiles = (units + 1) // 2
    tend = jnp.cumsum(tiles)
    n_used = tend[-1]
    tile_id = jnp.arange(N_ROW_TILES, dtype=i32)
    tile_expert = jnp.sum((tile_id[:, None] >= tend[None, :]).astype(i32), axis=1)
    tile_expert = jnp.minimum(tile_expert, jnp.take(tile_expert, n_used - 1))
    e_ar = jnp.arange(N_EXPERTS, dtype=i32)
    mine = tile_expert[:, None] == e_ar[None, :]
    pick = lambda v: jnp.sum(jnp.where(mine, v[None, :], 0), axis=1)
    in_group = tile_id - pick(tend - tiles)
    unit0 = pick(gstart // UNIT) + 2 * in_group
    half = (2 * in_group + 1 == pick(units)).astype(i32)
    first = (tile_id < n_used) & ((tile_id == 0) | (tile_expert != jnp.roll(tile_expert, 1)))
    order = jnp.maximum(jnp.cumsum(first.astype(i32)) - 1, 0)
    later = (e_ar[None, :] > e_ar[:, None]) & (padded[None, :] > 0)
    nxt_e = jnp.min(jnp.where(later, e_ar[None, :], N_EXPERTS), axis=1)
    nxt_e = jnp.where(nxt_e == N_EXPERTS, -1, nxt_e)
    nxt = pick(nxt_e)
    unit_row = jnp.arange(N_UNITS + 1, dtype=i32) * UNIT
    unit_e = jnp.minimum(jnp.sum((unit_row[:, None] >= gend[None, :]).astype(i32), axis=1),
                         N_EXPERTS - 1)
    of_unit = (unit_e[:, None] == e_ar[None, :])[:, None, :]
    per_unit = lambda tab: jnp.sum(jnp.where(of_unit, tab[None], 0), axis=2)
    cs_u, c_u = per_unit(cs), per_unit(c)
    src_u = per_unit(jnp.arange(N_TILES, dtype=i32)[:, None] * S_MAX + lo)
    piece_r = unit_row[:, None] + jnp.arange(UNIT_PIECES, dtype=i32)[None, :] * PIECE
    chunk = jnp.sum((piece_r[:, :, None] >= cs_u[:, None, :]).astype(i32), axis=2) - 1
    is_chunk = chunk[:, :, None] == jnp.arange(N_TILES, dtype=i32)[None, None, :]
    of_chunk = lambda tab_u: jnp.sum(jnp.where(is_chunk, tab_u[:, None, :], 0), axis=2)
    off = piece_r - of_chunk(cs_u)
    piece_row = jnp.where(off < of_chunk(c_u), of_chunk(src_u) + off, -1).reshape(-1)
    return dict(tot=tot, n_used=n_used[None], tile_expert=tile_expert, first=first.astype(i32),
                order=order, nxt=nxt, unit0=unit0, half=half, piece_row=piece_row)


def kernel(x_prompt, x_sample, state_conv, state_pool, norm_mix_g, w_in, conv_w, pool_w, pool_scale,
           w_out, norm_ffn_g, router_w, router_b, exp_w1, exp_b1, exp_w2, exp_b2, final_norm_g):
    l = 0
    xp = x_prompt.reshape(N_PROMPT, D_MODEL)
    xs = jnp.transpose(x_sample, (1, 0, 2)).reshape(N_SAMPLE, D_MODEL)
    sc_t = jnp.transpose(state_conv[l], (1, 0, 2))
    sp_t = jnp.transpose(state_pool[l], (1, 0, 2))
    rw_t = router_w[l].T
    rwh = rw_t.astype(bf16)
    rw2 = jnp.concatenate([rwh, (rw_t - rwh.astype(f32)).astype(bf16)], axis=0)
    tok = jnp.arange(TS, dtype=i32)
    tri = (tok[:, None] < tok[None, :]).astype(bf16)

    (x1, srt, route_t, cnt, npc, npp, nsc, nsv) = _mixer(
        xp, xs, sc_t, sp_t, norm_mix_g[l][None, :], w_in[l], conv_w[l],
        pool_w[l].astype(bf16), pool_scale[l][None, :], w_out[l],
        norm_ffn_g[l][None, :], rw2, router_b[l][:, None], tri)

    t = _routing_tables(cnt)
    ys = _experts(t["tile_expert"], t["n_used"], t["first"], t["order"], t["nxt"], t["unit0"],
                  t["half"], t["piece_row"], t["tot"], srt, exp_w1[l], exp_b1[l][:, None, :],
                  exp_w2[l], exp_b2[l][:, None, :])
    y_p, y_s = _combine(route_t, x1, final_norm_g[None, :], ys)

    y_prompt = y_p.reshape(BATCH, SEQ, D_MODEL)
    y_sample = jnp.transpose(y_s.reshape(DEC_SEQ, DEC_BATCH, D_MODEL), (1, 0, 2))
    new_conv_prompt = npc[None, :, 8 - (CONV_W - 1):, :]
    new_pool_prompt = npp[None, :, HIST - POOL_BUF:, :]
    new_conv_sample = jnp.transpose(nsc, (1, 0, 2))[None]
    new_pool_sample = jnp.concatenate(
        [state_pool[l][:, DEC_SEQ:, :], jnp.transpose(nsv, (1, 0, 2))], axis=1)[None]
    return (y_prompt, y_sample, new_conv_prompt, new_pool_prompt, new_conv_sample, new_pool_sample)
```

```python
import jax
import jax.numpy as jnp
from jax import lax
from jax.experimental import pallas as pl
from jax.experimental.pallas import tpu as pltpu

D_MODEL = 1024
D_CONV = 512
D_POOL = 512
D_IN = 3 * D_CONV + D_POOL
CONV_W = 3
POOL_WINDOWS = (2, 4, 8, 16)
POOL_GROUP = 128
POOL_BUF = 15
N_EXPERTS = 32
TOP_K = 4
D_FF = 1024
SWIGLU_LIMIT = 7.0
SWIGLU_ALPHA = 1.702
EPS = 1e-5
PAST_LEN = 16384

BATCH, SEQ = 8, 2048
DEC_BATCH, DEC_SEQ = 128, 4
N_PROMPT = BATCH * SEQ
N_SAMPLE = DEC_BATCH * DEC_SEQ
N_TOK = N_PROMPT + N_SAMPLE

TS = 512
SEQ_TILES = SEQ // TS
N_PROMPT_TILES = N_PROMPT // TS
N_TILES = N_PROMPT_TILES + 1
HIST = 16
UNIT = 128
TILE_UNITS = 4
TM = TILE_UNITS * UNIT
PIECE = 8
PIECES = TM // PIECE
UNIT_PIECES = UNIT // PIECE
S_BLK = 256
SORT_BLK = 512
S_MAX = (TS * TOP_K + N_EXPERTS * (PIECE - 1) + PIECE + S_BLK - 1) // S_BLK * S_BLK
ZERO_ROW = S_MAX - PIECE
N_UNITS = (N_TOK * TOP_K + N_TILES * N_EXPERTS * (PIECE - 1) + N_EXPERTS * (UNIT - PIECE)) // UNIT
N_ROW_TILES = (N_UNITS + (TILE_UNITS - 1) * N_EXPERTS) // TILE_UNITS
N_SORT_ROWS = N_TILES * S_MAX
DUMP_ROW = N_SORT_ROWS

VMEM_LIMIT = 56 * 1024 * 1024

f32 = jnp.float32
bf16 = jnp.bfloat16
i32 = jnp.int32


def _rms(x, g):
    return x * lax.rsqrt(jnp.mean(x * x, axis=-1, keepdims=True) + EPS) * g


def _pool_project(p, pw_ref, ps_ref):
    outs = []
    for g in range(len(POOL_WINDOWS)):
        sl = slice(g * POOL_GROUP, (g + 1) * POOL_GROUP)
        outs.append(jnp.dot(p[:, sl].astype(bf16), pw_ref[g], preferred_element_type=f32))
    return jnp.concatenate(outs, axis=1) * ps_ref[...]


def _route_and_sort(x, z, wout_ref, g2_ref, rw2_ref, rb_ref, tri_ref,
                    x1_ref, srt_ref, route_ref, cnt_ref):
    n = x.shape[0]
    x1 = x + jnp.dot(z.astype(bf16), wout_ref[...], preferred_element_type=f32)
    x1_ref[...] = x1
    xn = _rms(x1, g2_ref[...])
    xb = xn.astype(bf16)

    xlo = (xn - xb.astype(f32)).astype(bf16)
    nt = (((1,), (1,)), ((), ()))
    hi_lo = lax.dot_general(rw2_ref[...], xb, nt, preferred_element_type=f32)
    logits = (hi_lo[:N_EXPERTS] + hi_lo[N_EXPERTS:]
              + lax.dot_general(rw2_ref[0:N_EXPERTS, :], xlo, nt, preferred_element_type=f32)
              + rb_ref[...])

    e_iota = lax.broadcasted_iota(i32, (N_EXPERTS, n), 0)
    work = logits
    hots, vals = [], []
    for _ in range(TOP_K):
        m = jnp.max(work, axis=0, keepdims=True)
        ik = jnp.min(jnp.where(work == m, e_iota, N_EXPERTS), axis=0, keepdims=True)
        hot = e_iota == ik
        work = jnp.where(hot, -jnp.inf, work)
        hots.append(hot); vals.append(m)
    exps = [jnp.exp(v - vals[0]) for v in vals]
    den = exps[0] + exps[1] + exps[2] + exps[3]
    gates = [e / den for e in exps]

    multi = jnp.where(hots[0] | hots[1] | hots[2] | hots[3], 1.0, 0.0)
    before = jnp.dot(multi.astype(bf16), tri_ref[...], preferred_element_type=f32)
    count = jnp.sum(multi, axis=1, keepdims=True)
    cnt_ref[...] = count
    pieces = jnp.floor((count + (PIECE - 1)) * (1.0 / PIECE))
    er = lax.broadcasted_iota(i32, (N_EXPERTS, N_EXPERTS), 0)
    ec = lax.broadcasted_iota(i32, (N_EXPERTS, N_EXPERTS), 1)
    lower = jnp.where(ec < er, 1.0, 0.0).astype(bf16)
    pieces_b = jnp.broadcast_to(pieces, (N_EXPERTS, 128)).astype(bf16)
    start = PIECE * jnp.dot(lower, pieces_b, preferred_element_type=f32)[:, 0:1]
    base = before + start
    slot = jnp.concatenate(
        [jnp.sum(jnp.where(h, base, 0.0), axis=0, keepdims=True) for h in hots], axis=0).astype(i32)
    route_ref[...] = jnp.concatenate([slot.astype(f32)] + gates, axis=0)

    row_io = lax.broadcasted_iota(i32, (S_BLK, n), 0).astype(bf16)
    slot_f = slot.astype(f32)
    one, zero = jnp.ones((S_BLK, n), bf16), jnp.zeros((S_BLK, n), bf16)
    for row0 in range(0, S_MAX, SORT_BLK):
        rows = min(SORT_BLK, S_MAX - row0)
        parts = []
        for sub in range(row0, row0 + rows, S_BLK):
            local = slot_f - float(sub)
            local = jnp.where((local >= 0.0) & (local < float(S_BLK)), local, -1.0).astype(bf16)
            hit = row_io == local[0:1, :]
            for k in range(1, TOP_K):
                hit = hit | (row_io == local[k:k + 1, :])
            parts.append(jnp.where(hit, one, zero))
        sel = parts[0] if len(parts) == 1 else jnp.concatenate(parts, axis=0)
        srt_ref[row0:row0 + rows, :] = jnp.dot(sel, xb, preferred_element_type=f32).astype(bf16)


def _mixer_kernel(xp_in_ref, xs_in_ref, sc_ref, sp_ref, g1_ref, win_ref, cw_ref, pw_ref, ps_ref,
                  wout_ref, g2_ref, rw2_ref, rb_ref, tri_ref,
                  x1_ref, srt_ref, route_ref, cnt_ref,
                  npc_ref, npp_ref, nsc_ref, nsv_ref,
                  ubuf, vbuf, xcur, zcur):
    i = pl.program_id(0)

    @pl.when(i < N_PROMPT_TILES)
    def _():
        s = i % SEQ_TILES

        @pl.when(s == 0)
        def _():
            ubuf[0:HIST, :] = jnp.zeros((HIST, D_CONV), f32)
            vbuf[0:HIST, :] = jnp.zeros((HIST, D_POOL), f32)

        x = xp_in_ref[...]
        xcur[...] = x
        xn = _rms(x, g1_ref[...]).astype(bf16)
        proj = jnp.dot(xn, win_ref[...], preferred_element_type=f32)
        bg = proj[:, :D_CONV]
        u = proj[:, D_CONV:2 * D_CONV] * proj[:, 2 * D_CONV:3 * D_CONV]
        v = proj[:, 3 * D_CONV:]
        ubuf[HIST:HIST + TS, :] = u
        vbuf[HIST:HIST + TS, :] = v

        uh = ubuf[...]
        y = (pltpu.roll(uh, 2, axis=0) * cw_ref[0:1, :]
             + pltpu.roll(uh, 1, axis=0) * cw_ref[1:2, :])[HIST:, :] + u * cw_ref[2:3, :]
        zcur[:, :D_CONV] = bg * y

        pos = s * TS + lax.broadcasted_iota(i32, (TS, 1), 0)
        ps = []
        for g, w in enumerate(POOL_WINDOWS):
            sl = slice(g * POOL_GROUP, (g + 1) * POOL_GROUP)
            acc = vbuf[:, sl]
            for step in range(g + 1):
                acc = acc + pltpu.roll(acc, 1 << step, axis=0)
            cnt = jnp.minimum(w, pos + 1).astype(f32)
            ps.append(acc[HIST:, :] * (1.0 / cnt) - v[:, sl])
        zcur[:, D_CONV:] = _pool_project(jnp.concatenate(ps, axis=1), pw_ref, ps_ref)

        ubuf[HIST - 8:HIST, :] = ubuf[TS + HIST - 8:TS + HIST, :]
        vbuf[0:HIST, :] = vbuf[TS:TS + HIST, :]

        @pl.when(s == SEQ_TILES - 1)
        def _():
            npc_ref[0] = ubuf[HIST - 8:HIST, :]
            npp_ref[0] = vbuf[0:HIST, :]

    @pl.when(i == N_PROMPT_TILES)
    def _():
        x = xs_in_ref[...]
        xcur[...] = x
        xn = _rms(x, g1_ref[...]).astype(bf16)
        proj = jnp.dot(xn, win_ref[...], preferred_element_type=f32)
        nb = DEC_BATCH
        rows = lambda a, t: a[t * nb:(t + 1) * nb]
        bg = proj[:, :D_CONV]
        u = proj[:, D_CONV:2 * D_CONV] * proj[:, 2 * D_CONV:3 * D_CONV]
        v = proj[:, 3 * D_CONV:]
        up = [sc_ref[j] for j in range(CONV_W - 1)] + [rows(u, t) for t in range(DEC_SEQ)]
        vp = [sp_ref[j] for j in range(POOL_BUF)] + [rows(v, t) for t in range(DEC_SEQ)]
        zc, ps = [], []
        for t in range(DEC_SEQ):
            y = up[t] * cw_ref[0:1, :]
            for k in range(1, CONV_W):
                y = y + up[t + k] * cw_ref[k:k + 1, :]
            zc.append(rows(bg, t) * y)
            pg = []
            for g, w in enumerate(POOL_WINDOWS):
                sl = slice(g * POOL_GROUP, (g + 1) * POOL_GROUP)
                lo = t + POOL_BUF - w + 1
                acc = vp[lo][:, sl]
                for j in range(lo + 1, t + POOL_BUF + 1):
                    acc = acc + vp[j][:, sl]
                cnt = float(min(w, PAST_LEN + t + 1))
                pg.append(acc * (1.0 / cnt) - vp[t + POOL_BUF][:, sl])
            ps.append(jnp.concatenate(pg, axis=1))
        zcur[:, :D_CONV] = jnp.concatenate(zc, axis=0)
        zcur[:, D_CONV:] = _pool_project(jnp.concatenate(ps, axis=0), pw_ref, ps_ref)
        for j in range(CONV_W - 1):
            nsc_ref[j] = up[DEC_SEQ + j]
        for t in range(DEC_SEQ):
            nsv_ref[t] = rows(v, t)

    _route_and_sort(xcur[...], zcur[...], wout_ref, g2_ref, rw2_ref, rb_ref, tri_ref,
                    x1_ref, srt_ref, route_ref, cnt_ref)


def _mixer(xp, xs, sc_t, sp_t, g1, win, cw, pw, ps, wout, g2, rw2, rb, tri):
    const = lambda shape: pl.BlockSpec(shape, lambda i: (0,) * len(shape),
                                       pipeline_mode=pl.Buffered(1))
    ptile = lambda i: jnp.minimum(i, N_PROMPT_TILES - 1)
    pbatch = lambda i: jnp.minimum(i, N_PROMPT_TILES - 1) // SEQ_TILES
    out_shape = (
        jax.ShapeDtypeStruct((N_TOK, D_MODEL), f32),
        jax.ShapeDtypeStruct((N_SORT_ROWS, D_MODEL), bf16),
        jax.ShapeDtypeStruct((2 * TOP_K, N_TOK), f32),
        jax.ShapeDtypeStruct((N_TILES * N_EXPERTS, 1), f32),
        jax.ShapeDtypeStruct((BATCH, 8, D_CONV), f32),
        jax.ShapeDtypeStruct((BATCH, HIST, D_POOL), f32),
        jax.ShapeDtypeStruct((CONV_W - 1, DEC_BATCH, D_CONV), f32),
        jax.ShapeDtypeStruct((DEC_SEQ, DEC_BATCH, D_POOL), f32),
    )
    out_specs = (
        pl.BlockSpec((TS, D_MODEL), lambda i: (i, 0)),
        pl.BlockSpec((S_MAX, D_MODEL), lambda i: (i, 0)),
        pl.BlockSpec((2 * TOP_K, TS), lambda i: (0, i)),
        pl.BlockSpec((N_EXPERTS, 1), lambda i: (i, 0)),
        pl.BlockSpec((1, 8, D_CONV), lambda i: (pbatch(i), 0, 0)),
        pl.BlockSpec((1, HIST, D_POOL), lambda i: (pbatch(i), 0, 0)),
        pl.BlockSpec((CONV_W - 1, DEC_BATCH, D_CONV), lambda i: (0, 0, 0)),
        pl.BlockSpec((DEC_SEQ, DEC_BATCH, D_POOL), lambda i: (0, 0, 0)),
    )
    in_specs = [
        pl.BlockSpec((TS, D_MODEL), lambda i: (ptile(i), 0)),
        const((N_SAMPLE, D_MODEL)),
        const((CONV_W - 1, DEC_BATCH, D_CONV)),
        const((POOL_BUF, DEC_BATCH, D_POOL)),
        const((1, D_MODEL)),
        const((D_MODEL, D_IN)),
        const((CONV_W, D_CONV)),
        const((len(POOL_WINDOWS), POOL_GROUP, POOL_GROUP)),
        const((1, D_POOL)),
        const((D_MODEL, D_MODEL)),
        const((1, D_MODEL)),
        const((2 * N_EXPERTS, D_MODEL)),
        const((N_EXPERTS, 1)),
        const((TS, TS)),
    ]
    return pl.pallas_call(
        _mixer_kernel,
        grid=(N_TILES,),
        in_specs=in_specs,
        out_specs=out_specs,
        out_shape=out_shape,
        scratch_shapes=[
            pltpu.VMEM((TS + HIST, D_CONV), f32),
            pltpu.VMEM((TS + HIST, D_POOL), f32),
            pltpu.VMEM((TS, D_MODEL), f32),
            pltpu.VMEM((TS, D_MODEL), f32),
        ],
        compiler_params=pltpu.CompilerParams(
            dimension_semantics=("arbitrary",), vmem_limit_bytes=VMEM_LIMIT),
        name="mixer",
    )(xp, xs, sc_t, sp_t, g1, win, cw, pw, ps, wout, g2, rw2, rb, tri)


def _expert_kernel(te_ref, nu_ref, first_ref, ord_ref, nxt_ref, unit_ref, len_ref, pt_ref, tot_ref,
                   b1_ref, b2_ref, srt_hbm, w1_hbm, w2_hbm, ys_hbm,
                   xbuf, ybuf, zpiece, w1f, w2f, w1b, w2b, sem_w, sem_x, sem_y, sem_z):
    i = pl.program_id(0)
    n_used = nu_ref[0]
    live = i < n_used

    def weight_copies(e, slot):
        return (pltpu.make_async_copy(w1_hbm.at[e], w1f.at[slot], sem_w.at[slot]),
                pltpu.make_async_copy(w2_hbm.at[e], w2f.at[slot], sem_w.at[2 + slot]))

    def is_padding(entry, tile, p):
        if p < UNIT_PIECES:
            return entry < 0
        return (entry < 0) | (p >= len_ref[tile] * UNIT_PIECES)

    def start_x(tile, b):
        for p in range(PIECES):
            src = pt_ref[unit_ref[tile] * UNIT_PIECES + p]
            src = pl.multiple_of(jnp.where(is_padding(src, tile, p), ZERO_ROW, src), PIECE)
            pltpu.make_async_copy(srt_hbm.at[pl.ds(src, PIECE)],
                                  xbuf.at[b, pl.ds(p * PIECE, PIECE)], sem_x.at[b]).start()

    def start_y(tile, b, all_dump=False):
        for p in range(PIECES):
            dst = pt_ref[unit_ref[tile] * UNIT_PIECES + p]
            dump = is_padding(dst, tile, p) | all_dump
            dst = pl.multiple_of(jnp.where(dump, DUMP_ROW + b * TM + p * PIECE, dst), PIECE)
            pltpu.make_async_copy(ybuf.at[b, pl.ds(p * PIECE, PIECE)],
                                  ys_hbm.at[pl.ds(dst, PIECE)], sem_y.at[b]).start()

    def wait_x(b):
        pltpu.make_async_copy(srt_hbm.at[pl.ds(0, TM)], xbuf.at[b], sem_x.at[b]).wait()

    def wait_y(b):
        pltpu.make_async_copy(ybuf.at[b], ys_hbm.at[pl.ds(0, TM)], sem_y.at[b]).wait()

    @pl.when(i == 0)
    def _():
        zpiece[...] = jnp.zeros_like(zpiece)

        def fill_tile(t, c):
            def dst(p):
                row = pl.multiple_of(t * S_MAX + tot_ref[t] + p * PIECE, PIECE)
                return ys_hbm.at[pl.ds(row, PIECE)]

            def start(p, c2):
                pltpu.make_async_copy(zpiece, dst(p), sem_z.at[0]).start()
                return c2
            return lax.fori_loop(0, (S_MAX - tot_ref[t]) // PIECE, start, c)
        lax.fori_loop(0, N_TILES, fill_tile, 0)
        start_x(0, 0)
        start_x(jnp.minimum(1, n_used - 1), 1)
        ybuf[...] = jnp.zeros_like(ybuf)
        for b in range(2):
            pltpu.make_async_copy(ybuf.at[b], ys_hbm.at[pl.ds(DUMP_ROW + b * TM, TM)],
                                  sem_y.at[b]).start()

    @pl.when(live)
    def _():
        ring = i % 3
        expert = te_ref[i]
        wait_x(ring)

        @pl.when(first_ref[i] == 1)
        def _():
            slot = ord_ref[i] % 2

            @pl.when(i == 0)
            def _():
                for cp in weight_copies(te_ref[0], 0):
                    cp.start()
            for cp in weight_copies(expert, slot):
                cp.wait()

            @pl.when(nxt_ref[i] >= 0)
            def _():
                for cp in weight_copies(nxt_ref[i], 1 - slot):
                    cp.start()
            w1b[...] = w1f[slot].astype(bf16)
            w2b[...] = w2f[slot].astype(bf16)

        wait_y(ring)

        def mlp(rows):
            x = xbuf[ring, 0:rows, :]
            gu = jnp.dot(x, w1b[...], preferred_element_type=f32) + b1_ref[expert]
            gate = jnp.minimum(gu[:, :D_FF], SWIGLU_LIMIT)
            lin = jnp.clip(gu[:, D_FF:], -SWIGLU_LIMIT, SWIGLU_LIMIT)
            glu = gate * jax.nn.sigmoid(SWIGLU_ALPHA * gate)
            h = (glu * (lin + 1.0)).astype(bf16)
            start_x(jnp.minimum(i + 2, n_used - 1), (i + 2) % 3)
            start_y(jnp.maximum(i - 1, 0), (i + 2) % 3, all_dump=i == 0)
            out = jnp.dot(h, w2b[...], preferred_element_type=f32) + b2_ref[expert]
            ybuf[ring, 0:rows, :] = out.astype(bf16)

        for n_units in range(1, TILE_UNITS + 1):
            @pl.when(len_ref[i] == n_units)
            def _(n_units=n_units):
                mlp(n_units * UNIT)

        @pl.when(i == n_used - 1)
        def _():
            start_y(i, ring)
            for b in range(3):
                wait_y(b)
            wait_x((i + 1) % 3)
            wait_x((i + 2) % 3)

            def drain_tile(t, c):
                def wait(p, c2):
                    pltpu.make_async_copy(zpiece, ys_hbm.at[pl.ds(0, PIECE)], sem_z.at[0]).wait()
                    return c2
                return lax.fori_loop(0, (S_MAX - tot_ref[t]) // PIECE, wait, c)
            lax.fori_loop(0, N_TILES, drain_tile, 0)


def _experts(tile_expert, n_used, first, order, nxt, unit0, length, piece_row, tot, srt, w1, b1, w2, b2):
    whole = lambda shape: pl.BlockSpec(shape, lambda i, *_: (0,) * len(shape))
    grid_spec = pltpu.PrefetchScalarGridSpec(
        num_scalar_prefetch=9,
        grid=(N_ROW_TILES,),
        in_specs=[
            whole((N_EXPERTS, 1, 2 * D_FF)),
            whole((N_EXPERTS, 1, D_MODEL)),
            pl.BlockSpec(memory_space=pl.ANY),
            pl.BlockSpec(memory_space=pl.ANY),
            pl.BlockSpec(memory_space=pl.ANY),
        ],
        out_specs=pl.BlockSpec(memory_space=pl.ANY),
        scratch_shapes=[
            pltpu.VMEM((3, TM, D_MODEL), bf16), pltpu.VMEM((3, TM, D_MODEL), bf16),
            pltpu.VMEM((PIECE, D_MODEL), bf16),
            pltpu.VMEM((2, D_MODEL, 2 * D_FF), f32), pltpu.VMEM((2, D_FF, D_MODEL), f32),
            pltpu.VMEM((D_MODEL, 2 * D_FF), bf16), pltpu.VMEM((D_FF, D_MODEL), bf16),
            pltpu.SemaphoreType.DMA((4,)), pltpu.SemaphoreType.DMA((3,)),
            pltpu.SemaphoreType.DMA((3,)), pltpu.SemaphoreType.DMA((1,)),
        ],
    )
    return pl.pallas_call(
        _expert_kernel,
        grid_spec=grid_spec,
        out_shape=jax.ShapeDtypeStruct((N_SORT_ROWS + 3 * TM, D_MODEL), bf16),
        compiler_params=pltpu.CompilerParams(
            dimension_semantics=("arbitrary",), vmem_limit_bytes=VMEM_LIMIT),
        name="experts",
    )(tile_expert, n_used, first, order, nxt, unit0, length, piece_row, tot, b1, b2, srt, w1, w2)


def _combine_kernel(route_ref, x1_ref, gf_ref, ys_ref, yp_ref, ysm_ref, wgt):
    i = pl.program_id(0)

    slot = route_ref[0:TOP_K, :]
    g = route_ref[TOP_K:2 * TOP_K, :].astype(bf16)
    row_io = lax.broadcasted_iota(i32, (S_BLK, TS), 0).astype(bf16)
    for b in range(S_MAX // S_BLK):
        local = slot - float(b * S_BLK)
        local = jnp.where((local >= 0.0) & (local < float(S_BLK)), local, -1.0).astype(bf16)
        w = jnp.zeros((S_BLK, TS), bf16)
        for k in range(TOP_K):
            w = jnp.where(row_io == local[k:k + 1, :], g[k:k + 1, :], w)
        wgt[b * S_BLK:(b + 1) * S_BLK, :] = w

    y = x1_ref[...] + lax.dot_general(wgt[...], ys_ref[...], (((0,), (0,)), ((), ())),
                                      preferred_element_type=f32)
    out = _rms(y, gf_ref[...])

    @pl.when(i < N_PROMPT_TILES)
    def _():
        yp_ref[...] = out

    @pl.when(i == N_PROMPT_TILES)
    def _():
        ysm_ref[...] = out


def _combine(route_tm, x1, gf, ys):
    ptile = lambda i: (jnp.minimum(i, N_PROMPT_TILES - 1), 0)
    return pl.pallas_call(
        _combine_kernel,
        grid=(N_TILES,),
        in_specs=[
            pl.BlockSpec((2 * TOP_K, TS), lambda i: (0, i)),
            pl.BlockSpec((TS, D_MODEL), lambda i: (i, 0)),
            pl.BlockSpec((1, D_MODEL), lambda i: (0, 0)),
            pl.BlockSpec((S_MAX, D_MODEL), lambda i: (i, 0)),
        ],
        out_specs=(
            pl.BlockSpec((TS, D_MODEL), ptile),
            pl.BlockSpec((N_SAMPLE, D_MODEL), lambda i: (0, 0)),
        ),
        out_shape=(
            jax.ShapeDtypeStruct((N_PROMPT, D_MODEL), f32),
            jax.ShapeDtypeStruct((N_SAMPLE, D_MODEL), f32),
        ),
        scratch_shapes=[pltpu.VMEM((S_MAX, TS), bf16)],
        compiler_params=pltpu.CompilerParams(
            dimension_semantics=("arbitrary",), vmem_limit_bytes=VMEM_LIMIT),
        name="combine",
    )(route_tm, x1, gf, ys)


def _routing_tables(cnt):
    n = cnt.reshape(N_TILES, N_EXPERTS).astype(i32)
    c = (n + (PIECE - 1)) // PIECE * PIECE
    lo = jnp.cumsum(c, axis=1) - c
    tot = jnp.sum(c, axis=1)
    group = jnp.sum(c, axis=0)
    padded = (group + (UNIT - 1)) // UNIT * UNIT
    gend = jnp.cumsum(padded)
    gstart = gend - padded
    cs = gstart[None, :] + jnp.cumsum(c, axis=0) - c
    units = padded // UNIT
    tiles = (units + (TILE_UNITS - 1)) // TILE_UNITS
    tend = jnp.cumsum(tiles)
    n_used = tend[-1]
    tile_id = jnp.arange(N_ROW_TILES, dtype=i32)
    tile_expert = jnp.sum((tile_id[:, None] >= tend[None, :]).astype(i32), axis=1)
    tile_expert = jnp.minimum(tile_expert, jnp.take(tile_expert, n_used - 1))
    e_ar = jnp.arange(N_EXPERTS, dtype=i32)
    mine = tile_expert[:, None] == e_ar[None, :]
    pick = lambda v: jnp.sum(jnp.where(mine, v[None, :], 0), axis=1)
    in_group = tile_id - pick(tend - tiles)
    unit0 = pick(gstart // UNIT) + TILE_UNITS * in_group
    length = jnp.clip(pick(units) - TILE_UNITS * in_group, 1, TILE_UNITS)
    first = (tile_id < n_used) & ((tile_id == 0) | (tile_expert != jnp.roll(tile_expert, 1)))
    order = jnp.maximum(jnp.cumsum(first.astype(i32)) - 1, 0)
    later = (e_ar[None, :] > e_ar[:, None]) & (padded[None, :] > 0)
    nxt_e = jnp.min(jnp.where(later, e_ar[None, :], N_EXPERTS), axis=1)
    nxt_e = jnp.where(nxt_e == N_EXPERTS, -1, nxt_e)
    nxt = pick(nxt_e)
    unit_row = jnp.arange(N_UNITS + TILE_UNITS - 1, dtype=i32) * UNIT
    unit_e = jnp.minimum(jnp.sum((unit_row[:, None] >= gend[None, :]).astype(i32), axis=1),
                         N_EXPERTS - 1)
    of_unit = (unit_e[:, None] == e_ar[None, :])[:, None, :]
    per_unit = lambda tab: jnp.sum(jnp.where(of_unit, tab[None], 0), axis=2)
    cs_u, c_u = per_unit(cs), per_unit(c)
    src_u = per_unit(jnp.arange(N_TILES, dtype=i32)[:, None] * S_MAX + lo)
    piece_r = unit_row[:, None] + jnp.arange(UNIT_PIECES, dtype=i32)[None, :] * PIECE
    chunk = jnp.sum((piece_r[:, :, None] >= cs_u[:, None, :]).astype(i32), axis=2) - 1
    is_chunk = chunk[:, :, None] == jnp.arange(N_TILES, dtype=i32)[None, None, :]
    of_chunk = lambda tab_u: jnp.sum(jnp.where(is_chunk, tab_u[:, None, :], 0), axis=2)
    off = piece_r - of_chunk(cs_u)
    piece_row = jnp.where(off < of_chunk(c_u), of_chunk(src_u) + off, -1).reshape(-1)
    return dict(tot=tot, n_used=n_used[None], tile_expert=tile_expert, first=first.astype(i32),
                order=order, nxt=nxt, unit0=unit0, length=length, piece_row=piece_row)


def kernel(x_prompt, x_sample, state_conv, state_pool, norm_mix_g, w_in, conv_w, pool_w, pool_scale,
           w_out, norm_ffn_g, router_w, router_b, exp_w1, exp_b1, exp_w2, exp_b2, final_norm_g):
    l = 0
    xp = x_prompt.reshape(N_PROMPT, D_MODEL)
    xs = jnp.transpose(x_sample, (1, 0, 2)).reshape(N_SAMPLE, D_MODEL)
    sc_t = jnp.transpose(state_conv[l], (1, 0, 2))
    sp_t = jnp.transpose(state_pool[l], (1, 0, 2))
    rw_t = router_w[l].T
    rwh = rw_t.astype(bf16)
    rw2 = jnp.concatenate([rwh, (rw_t - rwh.astype(f32)).astype(bf16)], axis=0)
    tok = jnp.arange(TS, dtype=i32)
    tri = (tok[:, None] < tok[None, :]).astype(bf16)

    (x1, srt, route_t, cnt, npc, npp, nsc, nsv) = _mixer(
        xp, xs, sc_t, sp_t, norm_mix_g[l][None, :], w_in[l].astype(bf16), conv_w[l],
        pool_w[l].astype(bf16), pool_scale[l][None, :], w_out[l].astype(bf16),
        norm_ffn_g[l][None, :], rw2, router_b[l][:, None], tri)

    t = _routing_tables(cnt)
    ys = _experts(t["tile_expert"], t["n_used"], t["first"], t["order"], t["nxt"], t["unit0"],
                  t["length"], t["piece_row"], t["tot"], srt, exp_w1[l], exp_b1[l][:, None, :],
                  exp_w2[l], exp_b2[l][:, None, :])
    y_p, y_s = _combine(route_t, x1, final_norm_g[None, :], ys)

    y_prompt = y_p.reshape(BATCH, SEQ, D_MODEL)
    y_sample = jnp.transpose(y_s.reshape(DEC_SEQ, DEC_BATCH, D_MODEL), (1, 0, 2))
    new_conv_prompt = npc[None, :, 8 - (CONV_W - 1):, :]
    new_pool_prompt = npp[None, :, HIST - POOL_BUF:, :]
    new_conv_sample = jnp.transpose(nsc, (1, 0, 2))[None]
    new_pool_sample = jnp.concatenate(
        [state_pool[l][:, DEC_SEQ:, :], jnp.transpose(nsv, (1, 0, 2))], axis=1)[None]
    return (y_prompt, y_sample, new_conv_prompt, new_pool_prompt, new_conv_sample, new_pool_sample)
```

```python
import jax
import jax.numpy as jnp
from jax import lax
from jax.experimental import pallas as pl
from jax.experimental.pallas import tpu as pltpu

D_MODEL = 1024
D_CONV = 512
D_POOL = 512
D_IN = 3 * D_CONV + D_POOL
CONV_W = 3
POOL_WINDOWS = (2, 4, 8, 16)
POOL_GROUP = 128
POOL_BUF = 15
N_EXPERTS = 32
TOP_K = 4
D_FF = 1024
SWIGLU_LIMIT = 7.0
SWIGLU_ALPHA = 1.702
EPS = 1e-5
PAST_LEN = 16384

BATCH, SEQ = 8, 2048
DEC_BATCH, DEC_SEQ = 128, 4
N_PROMPT = BATCH * SEQ
N_SAMPLE = DEC_BATCH * DEC_SEQ
N_TOK = N_PROMPT + N_SAMPLE

TS = 512
SEQ_TILES = SEQ // TS
N_PROMPT_TILES = N_PROMPT // TS
N_TILES = N_PROMPT_TILES + 1
HIST = 16
UNIT = 128
TILE_UNITS = 4
TM = TILE_UNITS * UNIT
PIECE = 8
PIECES = TM // PIECE
UNIT_PIECES = UNIT // PIECE
S_BLK = 256
SORT_BLK = 512
S_MAX = (TS * TOP_K + N_EXPERTS * (PIECE - 1) + PIECE + S_BLK - 1) // S_BLK * S_BLK
ZERO_ROW = S_MAX - PIECE
N_UNITS = (N_TOK * TOP_K + N_TILES * N_EXPERTS * (PIECE - 1) + N_EXPERTS * (UNIT - PIECE)) // UNIT
N_ROW_TILES = (N_UNITS + (TILE_UNITS - 1) * N_EXPERTS) // TILE_UNITS
N_SORT_ROWS = N_TILES * S_MAX
DUMP_ROW = N_SORT_ROWS

VMEM_LIMIT = 56 * 1024 * 1024

f32 = jnp.float32
bf16 = jnp.bfloat16
i32 = jnp.int32


def _rms(x, g):
    return x * lax.rsqrt(jnp.mean(x * x, axis=-1, keepdims=True) + EPS) * g


def _pool_project(p, pw_ref, ps_ref):
    outs = []
    for g in range(len(POOL_WINDOWS)):
        sl = slice(g * POOL_GROUP, (g + 1) * POOL_GROUP)
        outs.append(jnp.dot(p[:, sl].astype(bf16), pw_ref[g], preferred_element_type=f32))
    return jnp.concatenate(outs, axis=1) * ps_ref[...]


def _route_and_sort(x, z, wout_ref, g2_ref, rw2_ref, rb_ref, tri_ref,
                    x1_ref, srt_ref, route_ref, cnt_ref):
    n = x.shape[0]
    x1 = x + jnp.dot(z.astype(bf16), wout_ref[...], preferred_element_type=f32)
    x1_ref[...] = x1
    xn = _rms(x1, g2_ref[...])
    xb = xn.astype(bf16)

    xlo = (xn - xb.astype(f32)).astype(bf16)
    nt = (((1,), (1,)), ((), ()))
    hi_lo = lax.dot_general(rw2_ref[...], xb, nt, preferred_element_type=f32)
    logits = (hi_lo[:N_EXPERTS] + hi_lo[N_EXPERTS:]
              + lax.dot_general(rw2_ref[0:N_EXPERTS, :], xlo, nt, preferred_element_type=f32)
              + rb_ref[...])

    e_iota = lax.broadcasted_iota(i32, (N_EXPERTS, n), 0)
    work = logits
    hots, vals = [], []
    for _ in range(TOP_K):
        m = jnp.max(work, axis=0, keepdims=True)
        ik = jnp.min(jnp.where(work == m, e_iota, N_EXPERTS), axis=0, keepdims=True)
        hot = e_iota == ik
        work = jnp.where(hot, -jnp.inf, work)
        hots.append(hot); vals.append(m)
    exps = [jnp.exp(v - vals[0]) for v in vals]
    den = exps[0] + exps[1] + exps[2] + exps[3]
    gates = [e / den for e in exps]

    multi = jnp.where(hots[0] | hots[1] | hots[2] | hots[3], 1.0, 0.0)
    before = jnp.dot(multi.astype(bf16), tri_ref[...], preferred_element_type=f32)
    count = jnp.sum(multi, axis=1, keepdims=True)
    cnt_ref[...] = count
    pieces = jnp.floor((count + (PIECE - 1)) * (1.0 / PIECE))
    er = lax.broadcasted_iota(i32, (N_EXPERTS, N_EXPERTS), 0)
    ec = lax.broadcasted_iota(i32, (N_EXPERTS, N_EXPERTS), 1)
    lower = jnp.where(ec < er, 1.0, 0.0).astype(bf16)
    pieces_b = jnp.broadcast_to(pieces, (N_EXPERTS, 128)).astype(bf16)
    start = PIECE * jnp.dot(lower, pieces_b, preferred_element_type=f32)[:, 0:1]
    base = before + start
    slot = jnp.concatenate(
        [jnp.sum(jnp.where(h, base, 0.0), axis=0, keepdims=True) for h in hots], axis=0).astype(i32)
    route_ref[...] = jnp.concatenate([slot.astype(f32)] + gates, axis=0)

    row_io = lax.broadcasted_iota(i32, (S_BLK, n), 0).astype(bf16)
    slot_f = slot.astype(f32)
    one, zero = jnp.ones((S_BLK, n), bf16), jnp.zeros((S_BLK, n), bf16)
    for row0 in range(0, S_MAX, SORT_BLK):
        rows = min(SORT_BLK, S_MAX - row0)
        parts = []
        for sub in range(row0, row0 + rows, S_BLK):
            local = slot_f - float(sub)
            local = jnp.where((local >= 0.0) & (local < float(S_BLK)), local, -1.0).astype(bf16)
            hit = row_io == local[0:1, :]
            for k in range(1, TOP_K):
                hit = hit | (row_io == local[k:k + 1, :])
            parts.append(jnp.where(hit, one, zero))
        sel = parts[0] if len(parts) == 1 else jnp.concatenate(parts, axis=0)
        srt_ref[row0:row0 + rows, :] = jnp.dot(sel, xb, preferred_element_type=f32).astype(bf16)


def _mixer_kernel(xp_in_ref, xs_in_ref, sc_ref, sp_ref, g1_ref, win_ref, cw_ref, pw_ref, ps_ref,
                  wout_ref, g2_ref, rw2_ref, rb_ref, tri_ref,
                  x1_ref, srt_ref, route_ref, cnt_ref,
                  npc_ref, npp_ref, nsc_ref, nsv_ref,
                  ubuf, vbuf, xcur, zcur):
    i = pl.program_id(0)

    @pl.when(i < N_PROMPT_TILES)
    def _():
        s = i % SEQ_TILES

        @pl.when(s == 0)
        def _():
            ubuf[0:HIST, :] = jnp.zeros((HIST, D_CONV), f32)
            vbuf[0:HIST, :] = jnp.zeros((HIST, D_POOL), f32)

        x = xp_in_ref[...]
        xcur[...] = x
        xn = _rms(x, g1_ref[...]).astype(bf16)
        proj = jnp.dot(xn, win_ref[...], preferred_element_type=f32)
        bg = proj[:, :D_CONV]
        u = proj[:, D_CONV:2 * D_CONV] * proj[:, 2 * D_CONV:3 * D_CONV]
        v = proj[:, 3 * D_CONV:]
        ubuf[HIST:HIST + TS, :] = u
        vbuf[HIST:HIST + TS, :] = v

        uh = ubuf[...]
        y = (pltpu.roll(uh, 2, axis=0) * cw_ref[0:1, :]
             + pltpu.roll(uh, 1, axis=0) * cw_ref[1:2, :])[HIST:, :] + u * cw_ref[2:3, :]
        zcur[:, :D_CONV] = bg * y

        pos = s * TS + lax.broadcasted_iota(i32, (TS, 1), 0)
        ps = []
        for g, w in enumerate(POOL_WINDOWS):
            sl = slice(g * POOL_GROUP, (g + 1) * POOL_GROUP)
            acc = vbuf[:, sl]
            for step in range(g + 1):
                acc = acc + pltpu.roll(acc, 1 << step, axis=0)
            cnt = jnp.minimum(w, pos + 1).astype(f32)
            ps.append(acc[HIST:, :] * (1.0 / cnt) - v[:, sl])
        zcur[:, D_CONV:] = _pool_project(jnp.concatenate(ps, axis=1), pw_ref, ps_ref)

        ubuf[HIST - 8:HIST, :] = ubuf[TS + HIST - 8:TS + HIST, :]
        vbuf[0:HIST, :] = vbuf[TS:TS + HIST, :]

        @pl.when(s == SEQ_TILES - 1)
        def _():
            npc_ref[0] = ubuf[HIST - 8:HIST, :]
            npp_ref[0] = vbuf[0:HIST, :]

    @pl.when(i == N_PROMPT_TILES)
    def _():
        x = xs_in_ref[...]
        xcur[...] = x
        xn = _rms(x, g1_ref[...]).astype(bf16)
        proj = jnp.dot(xn, win_ref[...], preferred_element_type=f32)
        nb = DEC_BATCH
        rows = lambda a, t: a[t * nb:(t + 1) * nb]
        bg = proj[:, :D_CONV]
        u = proj[:, D_CONV:2 * D_CONV] * proj[:, 2 * D_CONV:3 * D_CONV]
        v = proj[:, 3 * D_CONV:]
        up = [sc_ref[j] for j in range(CONV_W - 1)] + [rows(u, t) for t in range(DEC_SEQ)]
        vp = [sp_ref[j] for j in range(POOL_BUF)] + [rows(v, t) for t in range(DEC_SEQ)]
        zc, ps = [], []
        for t in range(DEC_SEQ):
            y = up[t] * cw_ref[0:1, :]
            for k in range(1, CONV_W):
                y = y + up[t + k] * cw_ref[k:k + 1, :]
            zc.append(rows(bg, t) * y)
            pg = []
            for g, w in enumerate(POOL_WINDOWS):
                sl = slice(g * POOL_GROUP, (g + 1) * POOL_GROUP)
                lo = t + POOL_BUF - w + 1
                acc = vp[lo][:, sl]
                for j in range(lo + 1, t + POOL_BUF + 1):
                    acc = acc + vp[j][:, sl]
                cnt = float(min(w, PAST_LEN + t + 1))
                pg.append(acc * (1.0 / cnt) - vp[t + POOL_BUF][:, sl])
            ps.append(jnp.concatenate(pg, axis=1))
        zcur[:, :D_CONV] = jnp.concatenate(zc, axis=0)
        zcur[:, D_CONV:] = _pool_project(jnp.concatenate(ps, axis=0), pw_ref, ps_ref)
        for j in range(CONV_W - 1):
            nsc_ref[j] = up[DEC_SEQ + j]
        for t in range(DEC_SEQ):
            nsv_ref[t] = rows(v, t)

    _route_and_sort(xcur[...], zcur[...], wout_ref, g2_ref, rw2_ref, rb_ref, tri_ref,
                    x1_ref, srt_ref, route_ref, cnt_ref)


def _mixer(xp, xs, sc_t, sp_t, g1, win, cw, pw, ps, wout, g2, rw2, rb, tri):
    const = lambda shape: pl.BlockSpec(shape, lambda i: (0,) * len(shape),
                                       pipeline_mode=pl.Buffered(1))
    ptile = lambda i: jnp.minimum(i, N_PROMPT_TILES - 1)
    pbatch = lambda i: jnp.minimum(i, N_PROMPT_TILES - 1) // SEQ_TILES
    out_shape = (
        jax.ShapeDtypeStruct((N_TOK, D_MODEL), f32),
        jax.ShapeDtypeStruct((N_SORT_ROWS, D_MODEL), bf16),
        jax.ShapeDtypeStruct((2 * TOP_K, N_TOK), f32),
        jax.ShapeDtypeStruct((N_TILES * N_EXPERTS, 1), f32),
        jax.ShapeDtypeStruct((BATCH, 8, D_CONV), f32),
        jax.ShapeDtypeStruct((BATCH, HIST, D_POOL), f32),
        jax.ShapeDtypeStruct((CONV_W - 1, DEC_BATCH, D_CONV), f32),
        jax.ShapeDtypeStruct((DEC_SEQ, DEC_BATCH, D_POOL), f32),
    )
    out_specs = (
        pl.BlockSpec((TS, D_MODEL), lambda i: (i, 0)),
        pl.BlockSpec((S_MAX, D_MODEL), lambda i: (i, 0)),
        pl.BlockSpec((2 * TOP_K, TS), lambda i: (0, i)),
        pl.BlockSpec((N_EXPERTS, 1), lambda i: (i, 0)),
        pl.BlockSpec((1, 8, D_CONV), lambda i: (pbatch(i), 0, 0)),
        pl.BlockSpec((1, HIST, D_POOL), lambda i: (pbatch(i), 0, 0)),
        pl.BlockSpec((CONV_W - 1, DEC_BATCH, D_CONV), lambda i: (0, 0, 0)),
        pl.BlockSpec((DEC_SEQ, DEC_BATCH, D_POOL), lambda i: (0, 0, 0)),
    )
    in_specs = [
        pl.BlockSpec((TS, D_MODEL), lambda i: (ptile(i), 0)),
        const((N_SAMPLE, D_MODEL)),
        const((CONV_W - 1, DEC_BATCH, D_CONV)),
        const((POOL_BUF, DEC_BATCH, D_POOL)),
        const((1, D_MODEL)),
        const((D_MODEL, D_IN)),
        const((CONV_W, D_CONV)),
        const((len(POOL_WINDOWS), POOL_GROUP, POOL_GROUP)),
        const((1, D_POOL)),
        const((D_MODEL, D_MODEL)),
        const((1, D_MODEL)),
        const((2 * N_EXPERTS, D_MODEL)),
        const((N_EXPERTS, 1)),
        const((TS, TS)),
    ]
    return pl.pallas_call(
        _mixer_kernel,
        grid=(N_TILES,),
        in_specs=in_specs,
        out_specs=out_specs,
        out_shape=out_shape,
        scratch_shapes=[
            pltpu.VMEM((TS + HIST, D_CONV), f32),
            pltpu.VMEM((TS + HIST, D_POOL), f32),
            pltpu.VMEM((TS, D_MODEL), f32),
            pltpu.VMEM((TS, D_MODEL), f32),
        ],
        compiler_params=pltpu.CompilerParams(
            dimension_semantics=("arbitrary",), vmem_limit_bytes=VMEM_LIMIT),
        name="mixer",
    )(xp, xs, sc_t, sp_t, g1, win, cw, pw, ps, wout, g2, rw2, rb, tri)


def _expert_kernel(te_ref, nu_ref, first_ref, ord_ref, nxt_ref, unit_ref, len_ref, pt_ref, tot_ref,
                   b1_ref, b2_ref, srt_hbm, w1_hbm, w2_hbm, ys_hbm,
                   xbuf, ybuf, zpiece, w1f, w2f, w1b, w2b, sem_w, sem_x, sem_y, sem_z):
    i = pl.program_id(0)
    n_used = nu_ref[0]
    live = i < n_used

    def weight_copies(e, slot):
        return (pltpu.make_async_copy(w1_hbm.at[e], w1f.at[slot], sem_w.at[slot]),
                pltpu.make_async_copy(w2_hbm.at[e], w2f.at[slot], sem_w.at[2 + slot]))

    def is_padding(entry, tile, p):
        if p < UNIT_PIECES:
            return entry < 0
        return (entry < 0) | (p >= len_ref[tile] * UNIT_PIECES)

    def start_x(tile, b):
        for p in range(PIECES):
            src = pt_ref[unit_ref[tile] * UNIT_PIECES + p]
            src = pl.multiple_of(jnp.where(is_padding(src, tile, p), ZERO_ROW, src), PIECE)
            pltpu.make_async_copy(srt_hbm.at[pl.ds(src, PIECE)],
                                  xbuf.at[b, pl.ds(p * PIECE, PIECE)], sem_x.at[b]).start()

    def start_y(tile, b, all_dump=False):
        for p in range(PIECES):
            dst = pt_ref[unit_ref[tile] * UNIT_PIECES + p]
            dump = is_padding(dst, tile, p) | all_dump
            dst = pl.multiple_of(jnp.where(dump, DUMP_ROW + b * TM + p * PIECE, dst), PIECE)
            pltpu.make_async_copy(ybuf.at[b, pl.ds(p * PIECE, PIECE)],
                                  ys_hbm.at[pl.ds(dst, PIECE)], sem_y.at[b]).start(priority=p % 2)

    def wait_x(b):
        pltpu.make_async_copy(srt_hbm.at[pl.ds(0, TM)], xbuf.at[b], sem_x.at[b]).wait()

    def wait_y(b):
        pltpu.make_async_copy(ybuf.at[b], ys_hbm.at[pl.ds(0, TM)], sem_y.at[b]).wait()

    @pl.when(i == 0)
    def _():
        zpiece[...] = jnp.zeros_like(zpiece)

        def fill_tile(t, c):
            def dst(p):
                row = pl.multiple_of(t * S_MAX + tot_ref[t] + p * PIECE, PIECE)
                return ys_hbm.at[pl.ds(row, PIECE)]

            def start(p, c2):
                pltpu.make_async_copy(zpiece, dst(p), sem_z.at[0]).start()
                return c2
            return lax.fori_loop(0, (S_MAX - tot_ref[t]) // PIECE, start, c)
        lax.fori_loop(0, N_TILES, fill_tile, 0)
        start_x(0, 0)
        start_x(jnp.minimum(1, n_used - 1), 1)
        ybuf[...] = jnp.zeros_like(ybuf)
        for b in range(2):
            pltpu.make_async_copy(ybuf.at[b], ys_hbm.at[pl.ds(DUMP_ROW + b * TM, TM)],
                                  sem_y.at[b]).start()

    @pl.when(live)
    def _():
        ring = i % 3
        expert = te_ref[i]
        wait_x(ring)

        @pl.when(first_ref[i] == 1)
        def _():
            slot = ord_ref[i] % 2

            @pl.when(i == 0)
            def _():
                for cp in weight_copies(te_ref[0], 0):
                    cp.start(priority=1)
            for cp in weight_copies(expert, slot):
                cp.wait()

            @pl.when(nxt_ref[i] >= 0)
            def _():
                for cp in weight_copies(nxt_ref[i], 1 - slot):
                    cp.start(priority=1)
            w1b[...] = w1f[slot].astype(bf16)
            w2b[...] = w2f[slot].astype(bf16)

        wait_y(ring)

        def mlp(rows):
            x = xbuf[ring, 0:rows, :]
            gu = jnp.dot(x, w1b[...], preferred_element_type=f32) + b1_ref[expert]
            gate = jnp.minimum(gu[:, :D_FF], SWIGLU_LIMIT)
            lin = jnp.clip(gu[:, D_FF:], -SWIGLU_LIMIT, SWIGLU_LIMIT)
            glu = gate * jax.nn.sigmoid(SWIGLU_ALPHA * gate)
            h = (glu * (lin + 1.0)).astype(bf16)
            start_x(jnp.minimum(i + 2, n_used - 1), (i + 2) % 3)
            start_y(jnp.maximum(i - 1, 0), (i + 2) % 3, all_dump=i == 0)
            out = jnp.dot(h, w2b[...], preferred_element_type=f32) + b2_ref[expert]
            ybuf[ring, 0:rows, :] = out.astype(bf16)

        for n_units in range(1, TILE_UNITS + 1):
            @pl.when(len_ref[i] == n_units)
            def _(n_units=n_units):
                mlp(n_units * UNIT)

        @pl.when(i == n_used - 1)
        def _():
            start_y(i, ring)
            for b in range(3):
                wait_y(b)
            wait_x((i + 1) % 3)
            wait_x((i + 2) % 3)

            def drain_tile(t, c):
                def wait(p, c2):
                    pltpu.make_async_copy(zpiece, ys_hbm.at[pl.ds(0, PIECE)], sem_z.at[0]).wait()
                    return c2
                return lax.fori_loop(0, (S_MAX - tot_ref[t]) // PIECE, wait, c)
            lax.fori_loop(0, N_TILES, drain_tile, 0)


def _experts(tile_expert, n_used, first, order, nxt, unit0, length, piece_row, tot, srt, w1, b1, w2, b2):
    whole = lambda shape: pl.BlockSpec(shape, lambda i, *_: (0,) * len(shape))
    grid_spec = pltpu.PrefetchScalarGridSpec(
        num_scalar_prefetch=9,
        grid=(N_ROW_TILES,),
        in_specs=[
            whole((N_EXPERTS, 1, 2 * D_FF)),
            whole((N_EXPERTS, 1, D_MODEL)),
            pl.BlockSpec(memory_space=pl.ANY),
            pl.BlockSpec(memory_space=pl.ANY),
            pl.BlockSpec(memory_space=pl.ANY),
        ],
        out_specs=pl.BlockSpec(memory_space=pl.ANY),
        scratch_shapes=[
            pltpu.VMEM((3, TM, D_MODEL), bf16), pltpu.VMEM((3, TM, D_MODEL), bf16),
            pltpu.VMEM((PIECE, D_MODEL), bf16),
            pltpu.VMEM((2, D_MODEL, 2 * D_FF), f32), pltpu.VMEM((2, D_FF, D_MODEL), f32),
            pltpu.VMEM((D_MODEL, 2 * D_FF), bf16), pltpu.VMEM((D_FF, D_MODEL), bf16),
            pltpu.SemaphoreType.DMA((4,)), pltpu.SemaphoreType.DMA((3,)),
            pltpu.SemaphoreType.DMA((3,)), pltpu.SemaphoreType.DMA((1,)),
        ],
    )
    return pl.pallas_call(
        _expert_kernel,
        grid_spec=grid_spec,
        out_shape=jax.ShapeDtypeStruct((N_SORT_ROWS + 3 * TM, D_MODEL), bf16),
        compiler_params=pltpu.CompilerParams(
            dimension_semantics=("arbitrary",), vmem_limit_bytes=VMEM_LIMIT),
        name="experts",
    )(tile_expert, n_used, first, order, nxt, unit0, length, piece_row, tot, b1, b2, srt, w1, w2)


def _combine_kernel(route_ref, x1_ref, gf_ref, ys_ref, yp_ref, ysm_ref, wgt):
    i = pl.program_id(0)

    slot = route_ref[0:TOP_K, :]
    g = route_ref[TOP_K:2 * TOP_K, :].astype(bf16)
    row_io = lax.broadcasted_iota(i32, (S_BLK, TS), 0).astype(bf16)
    for b in range(S_MAX // S_BLK):
        local = slot - float(b * S_BLK)
        local = jnp.where((local >= 0.0) & (local < float(S_BLK)), local, -1.0).astype(bf16)
        w = jnp.zeros((S_BLK, TS), bf16)
        for k in range(TOP_K):
            w = jnp.where(row_io == local[k:k + 1, :], g[k:k + 1, :], w)
        wgt[b * S_BLK:(b + 1) * S_BLK, :] = w

    y = x1_ref[...] + lax.dot_general(wgt[...], ys_ref[...], (((0,), (0,)), ((), ())),
                                      preferred_element_type=f32)
    out = _rms(y, gf_ref[...])

    @pl.when(i < N_PROMPT_TILES)
    def _():
        yp_ref[...] = out

    @pl.when(i == N_PROMPT_TILES)
    def _():
        ysm_ref[...] = out


def _combine(route_tm, x1, gf, ys):
    ptile = lambda i: (jnp.minimum(i, N_PROMPT_TILES - 1), 0)
    return pl.pallas_call(
        _combine_kernel,
        grid=(N_TILES,),
        in_specs=[
            pl.BlockSpec((2 * TOP_K, TS), lambda i: (0, i)),
            pl.BlockSpec((TS, D_MODEL), lambda i: (i, 0)),
            pl.BlockSpec((1, D_MODEL), lambda i: (0, 0)),
            pl.BlockSpec((S_MAX, D_MODEL), lambda i: (i, 0)),
        ],
        out_specs=(
            pl.BlockSpec((TS, D_MODEL), ptile),
            pl.BlockSpec((N_SAMPLE, D_MODEL), lambda i: (0, 0)),
        ),
        out_shape=(
            jax.ShapeDtypeStruct((N_PROMPT, D_MODEL), f32),
            jax.ShapeDtypeStruct((N_SAMPLE, D_MODEL), f32),
        ),
        scratch_shapes=[pltpu.VMEM((S_MAX, TS), bf16)],
        compiler_params=pltpu.CompilerParams(
            dimension_semantics=("arbitrary",), vmem_limit_bytes=VMEM_LIMIT),
        name="combine",
    )(route_tm, x1, gf, ys)


def _routing_tables(cnt):
    n = cnt.reshape(N_TILES, N_EXPERTS).astype(i32)
    c = (n + (PIECE - 1)) // PIECE * PIECE
    lo = jnp.cumsum(c, axis=1) - c
    tot = jnp.sum(c, axis=1)
    group = jnp.sum(c, axis=0)
    padded = (group + (UNIT - 1)) // UNIT * UNIT
    gend = jnp.cumsum(padded)
    gstart = gend - padded
    cs = gstart[None, :] + jnp.cumsum(c, axis=0) - c
    units = padded // UNIT
    tiles = (units + (TILE_UNITS - 1)) // TILE_UNITS
    tend = jnp.cumsum(tiles)
    n_used = tend[-1]
    tile_id = jnp.arange(N_ROW_TILES, dtype=i32)
    tile_expert = jnp.sum((tile_id[:, None] >= tend[None, :]).astype(i32), axis=1)
    tile_expert = jnp.minimum(tile_expert, jnp.take(tile_expert, n_used - 1))
    e_ar = jnp.arange(N_EXPERTS, dtype=i32)
    mine = tile_expert[:, None] == e_ar[None, :]
    pick = lambda v: jnp.sum(jnp.where(mine, v[None, :], 0), axis=1)
    in_group = tile_id - pick(tend - tiles)
    unit0 = pick(gstart // UNIT) + TILE_UNITS * in_group
    length = jnp.clip(pick(units) - TILE_UNITS * in_group, 1, TILE_UNITS)
    first = (tile_id < n_used) & ((tile_id == 0) | (tile_expert != jnp.roll(tile_expert, 1)))
    order = jnp.maximum(jnp.cumsum(first.astype(i32)) - 1, 0)
    later = (e_ar[None, :] > e_ar[:, None]) & (padded[None, :] > 0)
    nxt_e = jnp.min(jnp.where(later, e_ar[None, :], N_EXPERTS), axis=1)
    nxt_e = jnp.where(nxt_e == N_EXPERTS, -1, nxt_e)
    nxt = pick(nxt_e)
    unit_row = jnp.arange(N_UNITS + TILE_UNITS - 1, dtype=i32) * UNIT
    unit_e = jnp.minimum(jnp.sum((unit_row[:, None] >= gend[None, :]).astype(i32), axis=1),
                         N_EXPERTS - 1)
    of_unit = (unit_e[:, None] == e_ar[None, :])[:, None, :]
    per_unit = lambda tab: jnp.sum(jnp.where(of_unit, tab[None], 0), axis=2)
    cs_u, c_u = per_unit(cs), per_unit(c)
    src_u = per_unit(jnp.arange(N_TILES, dtype=i32)[:, None] * S_MAX + lo)
    piece_r = unit_row[:, None] + jnp.arange(UNIT_PIECES, dtype=i32)[None, :] * PIECE
    chunk = jnp.sum((piece_r[:, :, None] >= cs_u[:, None, :]).astype(i32), axis=2) - 1
    is_chunk = chunk[:, :, None] == jnp.arange(N_TILES, dtype=i32)[None, None, :]
    of_chunk = lambda tab_u: jnp.sum(jnp.where(is_chunk, tab_u[:, None, :], 0), axis=2)
    off = piece_r - of_chunk(cs_u)
    piece_row = jnp.where(off < of_chunk(c_u), of_chunk(src_u) + off, -1).reshape(-1)
    return dict(tot=tot, n_used=n_used[None], tile_expert=tile_expert, first=first.astype(i32),
                order=order, nxt=nxt, unit0=unit0, length=length, piece_row=piece_row)


def kernel(x_prompt, x_sample, state_conv, state_pool, norm_mix_g, w_in, conv_w, pool_w, pool_scale,
           w_out, norm_ffn_g, router_w, router_b, exp_w1, exp_b1, exp_w2, exp_b2, final_norm_g):
    l = 0
    xp = x_prompt.reshape(N_PROMPT, D_MODEL)
    xs = jnp.transpose(x_sample, (1, 0, 2)).reshape(N_SAMPLE, D_MODEL)
    sc_t = jnp.transpose(state_conv[l], (1, 0, 2))
    sp_t = jnp.transpose(state_pool[l], (1, 0, 2))
    rw_t = router_w[l].T
    rwh = rw_t.astype(bf16)
    rw2 = jnp.concatenate([rwh, (rw_t - rwh.astype(f32)).astype(bf16)], axis=0)
    tok = jnp.arange(TS, dtype=i32)
    tri = (tok[:, None] < tok[None, :]).astype(bf16)

    (x1, srt, route_t, cnt, npc, npp, nsc, nsv) = _mixer(
        xp, xs, sc_t, sp_t, norm_mix_g[l][None, :], w_in[l].astype(bf16), conv_w[l],
        pool_w[l].astype(bf16), pool_scale[l][None, :], w_out[l].astype(bf16),
        norm_ffn_g[l][None, :], rw2, router_b[l][:, None], tri)

    t = _routing_tables(cnt)
    ys = _experts(t["tile_expert"], t["n_used"], t["first"], t["order"], t["nxt"], t["unit0"],
                  t["length"], t["piece_row"], t["tot"], srt, exp_w1[l], exp_b1[l][:, None, :],
                  exp_w2[l], exp_b2[l][:, None, :])
    y_p, y_s = _combine(route_t, x1, final_norm_g[None, :], ys)

    y_prompt = y_p.reshape(BATCH, SEQ, D_MODEL)
    y_sample = jnp.transpose(y_s.reshape(DEC_SEQ, DEC_BATCH, D_MODEL), (1, 0, 2))
    new_conv_prompt = npc[None, :, 8 - (CONV_W - 1):, :]
    new_pool_prompt = npp[None, :, HIST - POOL_BUF:, :]
    new_conv_sample = jnp.transpose(nsc, (1, 0, 2))[None]
    new_pool_sample = jnp.concatenate(
        [state_pool[l][:, DEC_SEQ:, :], jnp.transpose(nsv, (1, 0, 2))], axis=1)[None]
    return (y_prompt, y_sample, new_conv_prompt, new_pool_prompt, new_conv_sample, new_pool_sample)
```

```python
import functools

import jax
import jax.numpy as jnp
from jax import lax
from jax.experimental import pallas as pl
from jax.experimental.pallas import tpu as pltpu

D_MODEL = 1024
D_CONV = 512
D_POOL = 512
D_IN = 3 * D_CONV + D_POOL
CONV_W = 3
POOL_WINDOWS = (2, 4, 8, 16)
POOL_GROUP = 128
POOL_BUF = 15
N_EXPERTS = 32
TOP_K = 4
D_FF = 1024
SWIGLU_LIMIT = 7.0
SWIGLU_ALPHA = 1.702
EPS = 1e-5
PAST_LEN = 16384

BATCH, SEQ = 8, 2048
DEC_BATCH, DEC_SEQ = 128, 4
N_PROMPT = BATCH * SEQ
N_SAMPLE = DEC_BATCH * DEC_SEQ
N_TOK = N_PROMPT + N_SAMPLE

TS = 512
SEQ_TILES = SEQ // TS
N_PROMPT_TILES = N_PROMPT // TS
N_TILES = N_PROMPT_TILES + 1
HIST = 16
UNIT = 128
TILE_UNITS = 4
TM = TILE_UNITS * UNIT
PIECE = 8
PIECES = TM // PIECE
UNIT_PIECES = UNIT // PIECE
S_BLK = 256
SORT_BLK = 512
S_MAX = (TS * TOP_K + N_EXPERTS * (PIECE - 1) + PIECE + S_BLK - 1) // S_BLK * S_BLK
ZERO_ROW = S_MAX - PIECE
N_UNITS = (N_TOK * TOP_K + N_TILES * N_EXPERTS * (PIECE - 1) + N_EXPERTS * (UNIT - PIECE)) // UNIT
N_ROW_TILES = (N_UNITS + (TILE_UNITS - 1) * N_EXPERTS) // TILE_UNITS
N_SORT_ROWS = N_TILES * S_MAX
DUMP_ROW = N_SORT_ROWS

VMEM_LIMIT = 56 * 1024 * 1024

f32 = jnp.float32
bf16 = jnp.bfloat16
i32 = jnp.int32


def _rms(x, g):
    return x * lax.rsqrt(jnp.mean(x * x, axis=-1, keepdims=True) + EPS) * g


def _pool_project(p, pw_ref, ps_ref):
    outs = []
    for g in range(len(POOL_WINDOWS)):
        sl = slice(g * POOL_GROUP, (g + 1) * POOL_GROUP)
        outs.append(jnp.dot(p[:, sl].astype(bf16), pw_ref[g], preferred_element_type=f32))
    return jnp.concatenate(outs, axis=1) * ps_ref[...]


def _route_and_sort(x, z, wout_ref, g2_ref, rw2_ref, rb_ref, tri_ref,
                    x1_ref, srt_ref, route_ref, cnt_ref):
    n = x.shape[0]
    x1 = x + jnp.dot(z.astype(bf16), wout_ref[...], preferred_element_type=f32)
    x1_ref[...] = x1
    xn = _rms(x1, g2_ref[...])
    xb = xn.astype(bf16)

    xlo = (xn - xb.astype(f32)).astype(bf16)
    nt = (((1,), (1,)), ((), ()))
    hi_lo = lax.dot_general(rw2_ref[...], xb, nt, preferred_element_type=f32)
    logits = (hi_lo[:N_EXPERTS] + hi_lo[N_EXPERTS:]
              + lax.dot_general(rw2_ref[0:N_EXPERTS, :], xlo, nt, preferred_element_type=f32)
              + rb_ref[...])

    e_iota = lax.broadcasted_iota(i32, (N_EXPERTS, n), 0)
    work = logits
    hots, vals = [], []
    for _ in range(TOP_K):
        m = jnp.max(work, axis=0, keepdims=True)
        ik = jnp.min(jnp.where(work == m, e_iota, N_EXPERTS), axis=0, keepdims=True)
        hot = e_iota == ik
        work = jnp.where(hot, -jnp.inf, work)
        hots.append(hot); vals.append(m)
    exps = [jnp.exp(v - vals[0]) for v in vals]
    den = exps[0] + exps[1] + exps[2] + exps[3]
    gates = [e / den for e in exps]

    multi = jnp.where(hots[0] | hots[1] | hots[2] | hots[3], 1.0, 0.0)
    before = jnp.dot(multi.astype(bf16), tri_ref[...], preferred_element_type=f32)
    count = jnp.sum(multi, axis=1, keepdims=True)
    cnt_ref[...] = count
    pieces = jnp.floor((count + (PIECE - 1)) * (1.0 / PIECE))
    er = lax.broadcasted_iota(i32, (N_EXPERTS, N_EXPERTS), 0)
    ec = lax.broadcasted_iota(i32, (N_EXPERTS, N_EXPERTS), 1)
    lower = jnp.where(ec < er, 1.0, 0.0).astype(bf16)
    pieces_b = jnp.broadcast_to(pieces, (N_EXPERTS, 128)).astype(bf16)
    start = PIECE * jnp.dot(lower, pieces_b, preferred_element_type=f32)[:, 0:1]
    base = before + start
    slot = jnp.concatenate(
        [jnp.sum(jnp.where(h, base, 0.0), axis=0, keepdims=True) for h in hots], axis=0).astype(i32)
    route_ref[...] = jnp.concatenate([slot.astype(f32)] + gates, axis=0)

    row_io = lax.broadcasted_iota(i32, (S_BLK, n), 0).astype(bf16)
    slot_f = slot.astype(f32)
    one, zero = jnp.ones((S_BLK, n), bf16), jnp.zeros((S_BLK, n), bf16)
    for row0 in range(0, S_MAX, SORT_BLK):
        rows = min(SORT_BLK, S_MAX - row0)
        parts = []
        for sub in range(row0, row0 + rows, S_BLK):
            local = slot_f - float(sub)
            local = jnp.where((local >= 0.0) & (local < float(S_BLK)), local, -1.0).astype(bf16)
            hit = row_io == local[0:1, :]
            for k in range(1, TOP_K):
                hit = hit | (row_io == local[k:k + 1, :])
            parts.append(jnp.where(hit, one, zero))
        sel = parts[0] if len(parts) == 1 else jnp.concatenate(parts, axis=0)
        srt_ref[row0:row0 + rows, :] = jnp.dot(sel, xb, preferred_element_type=f32).astype(bf16)


def _mixer_kernel(xp_in_ref, xs_in_ref, sc_ref, sp_ref, g1_ref, win_ref, cw_ref, pw_ref, ps_ref,
                  wout_ref, g2_ref, rw2_ref, rb_ref, tri_ref,
                  x1_ref, srt_ref, route_ref, cnt_ref,
                  npc_ref, npp_ref, nsc_ref, nsv_ref,
                  ubuf, vbuf, xcur, zcur):
    i = pl.program_id(0)

    @pl.when(i < N_PROMPT_TILES)
    def _():
        s = i % SEQ_TILES

        @pl.when(s == 0)
        def _():
            ubuf[0:HIST, :] = jnp.zeros((HIST, D_CONV), f32)
            vbuf[0:HIST, :] = jnp.zeros((HIST, D_POOL), f32)

        x = xp_in_ref[...]
        xcur[...] = x
        xn = _rms(x, g1_ref[...]).astype(bf16)
        proj = jnp.dot(xn, win_ref[...], preferred_element_type=f32)
        bg = proj[:, :D_CONV]
        u = proj[:, D_CONV:2 * D_CONV] * proj[:, 2 * D_CONV:3 * D_CONV]
        v = proj[:, 3 * D_CONV:]
        ubuf[HIST:HIST + TS, :] = u
        vbuf[HIST:HIST + TS, :] = v

        uh = ubuf[...]
        y = (pltpu.roll(uh, 2, axis=0) * cw_ref[0:1, :]
             + pltpu.roll(uh, 1, axis=0) * cw_ref[1:2, :])[HIST:, :] + u * cw_ref[2:3, :]
        zcur[:, :D_CONV] = bg * y

        pos = s * TS + lax.broadcasted_iota(i32, (TS, 1), 0)
        ps = []
        for g, w in enumerate(POOL_WINDOWS):
            sl = slice(g * POOL_GROUP, (g + 1) * POOL_GROUP)
            acc = vbuf[:, sl]
            for step in range(g + 1):
                acc = acc + pltpu.roll(acc, 1 << step, axis=0)
            cnt = jnp.minimum(w, pos + 1).astype(f32)
            ps.append(acc[HIST:, :] * (1.0 / cnt) - v[:, sl])
        zcur[:, D_CONV:] = _pool_project(jnp.concatenate(ps, axis=1), pw_ref, ps_ref)

        ubuf[HIST - 8:HIST, :] = ubuf[TS + HIST - 8:TS + HIST, :]
        vbuf[0:HIST, :] = vbuf[TS:TS + HIST, :]

        @pl.when(s == SEQ_TILES - 1)
        def _():
            npc_ref[0] = ubuf[HIST - 8:HIST, :]
            npp_ref[0] = vbuf[0:HIST, :]

    @pl.when(i == N_PROMPT_TILES)
    def _():
        x = xs_in_ref[...]
        xcur[...] = x
        xn = _rms(x, g1_ref[...]).astype(bf16)
        proj = jnp.dot(xn, win_ref[...], preferred_element_type=f32)
        nb = DEC_BATCH
        rows = lambda a, t: a[t * nb:(t + 1) * nb]
        bg = proj[:, :D_CONV]
        u = proj[:, D_CONV:2 * D_CONV] * proj[:, 2 * D_CONV:3 * D_CONV]
        v = proj[:, 3 * D_CONV:]
        up = [sc_ref[j] for j in range(CONV_W - 1)] + [rows(u, t) for t in range(DEC_SEQ)]
        vp = [sp_ref[j] for j in range(POOL_BUF)] + [rows(v, t) for t in range(DEC_SEQ)]
        zc, ps = [], []
        for t in range(DEC_SEQ):
            y = up[t] * cw_ref[0:1, :]
            for k in range(1, CONV_W):
                y = y + up[t + k] * cw_ref[k:k + 1, :]
            zc.append(rows(bg, t) * y)
            pg = []
            for g, w in enumerate(POOL_WINDOWS):
                sl = slice(g * POOL_GROUP, (g + 1) * POOL_GROUP)
                lo = t + POOL_BUF - w + 1
                acc = vp[lo][:, sl]
                for j in range(lo + 1, t + POOL_BUF + 1):
                    acc = acc + vp[j][:, sl]
                cnt = float(min(w, PAST_LEN + t + 1))
                pg.append(acc * (1.0 / cnt) - vp[t + POOL_BUF][:, sl])
            ps.append(jnp.concatenate(pg, axis=1))
        zcur[:, :D_CONV] = jnp.concatenate(zc, axis=0)
        zcur[:, D_CONV:] = _pool_project(jnp.concatenate(ps, axis=0), pw_ref, ps_ref)
        for j in range(CONV_W - 1):
            nsc_ref[j] = up[DEC_SEQ + j]
        for t in range(DEC_SEQ):
            nsv_ref[t] = rows(v, t)

    _route_and_sort(xcur[...], zcur[...], wout_ref, g2_ref, rw2_ref, rb_ref, tri_ref,
                    x1_ref, srt_ref, route_ref, cnt_ref)


def _mixer(xp, xs, sc_t, sp_t, g1, win, cw, pw, ps, wout, g2, rw2, rb, tri):
    const = lambda shape: pl.BlockSpec(shape, lambda i: (0,) * len(shape),
                                       pipeline_mode=pl.Buffered(1))
    ptile = lambda i: jnp.minimum(i, N_PROMPT_TILES - 1)
    pbatch = lambda i: jnp.minimum(i, N_PROMPT_TILES - 1) // SEQ_TILES
    out_shape = (
        jax.ShapeDtypeStruct((N_TOK, D_MODEL), f32),
        jax.ShapeDtypeStruct((N_SORT_ROWS, D_MODEL), bf16),
        jax.ShapeDtypeStruct((2 * TOP_K, N_TOK), f32),
        jax.ShapeDtypeStruct((N_TILES * N_EXPERTS, 1), f32),
        jax.ShapeDtypeStruct((BATCH, 8, D_CONV), f32),
        jax.ShapeDtypeStruct((BATCH, HIST, D_POOL), f32),
        jax.ShapeDtypeStruct((CONV_W - 1, DEC_BATCH, D_CONV), f32),
        jax.ShapeDtypeStruct((DEC_SEQ, DEC_BATCH, D_POOL), f32),
    )
    out_specs = (
        pl.BlockSpec((TS, D_MODEL), lambda i: (i, 0)),
        pl.BlockSpec((S_MAX, D_MODEL), lambda i: (i, 0)),
        pl.BlockSpec((2 * TOP_K, TS), lambda i: (0, i)),
        pl.BlockSpec((N_EXPERTS, 1), lambda i: (i, 0)),
        pl.BlockSpec((1, 8, D_CONV), lambda i: (pbatch(i), 0, 0)),
        pl.BlockSpec((1, HIST, D_POOL), lambda i: (pbatch(i), 0, 0)),
        pl.BlockSpec((CONV_W - 1, DEC_BATCH, D_CONV), lambda i: (0, 0, 0)),
        pl.BlockSpec((DEC_SEQ, DEC_BATCH, D_POOL), lambda i: (0, 0, 0)),
    )
    in_specs = [
        pl.BlockSpec((TS, D_MODEL), lambda i: (ptile(i), 0)),
        const((N_SAMPLE, D_MODEL)),
        const((CONV_W - 1, DEC_BATCH, D_CONV)),
        const((POOL_BUF, DEC_BATCH, D_POOL)),
        const((1, D_MODEL)),
        const((D_MODEL, D_IN)),
        const((CONV_W, D_CONV)),
        const((len(POOL_WINDOWS), POOL_GROUP, POOL_GROUP)),
        const((1, D_POOL)),
        const((D_MODEL, D_MODEL)),
        const((1, D_MODEL)),
        const((2 * N_EXPERTS, D_MODEL)),
        const((N_EXPERTS, 1)),
        const((TS, TS)),
    ]
    return pl.pallas_call(
        _mixer_kernel,
        grid=(N_TILES,),
        in_specs=in_specs,
        out_specs=out_specs,
        out_shape=out_shape,
        scratch_shapes=[
            pltpu.VMEM((TS + HIST, D_CONV), f32),
            pltpu.VMEM((TS + HIST, D_POOL), f32),
            pltpu.VMEM((TS, D_MODEL), f32),
            pltpu.VMEM((TS, D_MODEL), f32),
        ],
        compiler_params=pltpu.CompilerParams(
            dimension_semantics=("arbitrary",), vmem_limit_bytes=VMEM_LIMIT),
        name="mixer",
    )(xp, xs, sc_t, sp_t, g1, win, cw, pw, ps, wout, g2, rw2, rb, tri)


def _expert_kernel(te_ref, nu_ref, first_ref, ord_ref, nxt_ref, unit_ref, len_ref, pt_ref, tot_ref,
                   b1_ref, b2_ref, srt_hbm, w1_hbm, w2_hbm, ys_hbm,
                   xbuf, ybuf, zpiece, w1f, w2f, w1b, w2b, sem_w, sem_x, sem_y, sem_z):
    i = pl.program_id(0)
    n_used = nu_ref[0]
    live = i < n_used

    def weight_copies(e, slot):
        return (pltpu.make_async_copy(w1_hbm.at[e], w1f.at[slot], sem_w.at[slot]),
                pltpu.make_async_copy(w2_hbm.at[e], w2f.at[slot], sem_w.at[2 + slot]))

    def is_padding(entry, tile, p):
        if p < UNIT_PIECES:
            return entry < 0
        return (entry < 0) | (p >= len_ref[tile] * UNIT_PIECES)

    def start_x(tile, b):
        for p in range(PIECES):
            src = pt_ref[unit_ref[tile] * UNIT_PIECES + p]
            src = pl.multiple_of(jnp.where(is_padding(src, tile, p), ZERO_ROW, src), PIECE)
            pltpu.make_async_copy(srt_hbm.at[pl.ds(src, PIECE)],
                                  xbuf.at[b, pl.ds(p * PIECE, PIECE)], sem_x.at[b]).start()

    def start_y(tile, b, all_dump=False):
        for p in range(PIECES):
            dst = pt_ref[unit_ref[tile] * UNIT_PIECES + p]
            dump = is_padding(dst, tile, p) | all_dump
            dst = pl.multiple_of(jnp.where(dump, DUMP_ROW + b * TM + p * PIECE, dst), PIECE)
            pltpu.make_async_copy(ybuf.at[b, pl.ds(p * PIECE, PIECE)],
                                  ys_hbm.at[pl.ds(dst, PIECE)], sem_y.at[b]).start(priority=p % 2)

    def wait_x(b):
        pltpu.make_async_copy(srt_hbm.at[pl.ds(0, TM)], xbuf.at[b], sem_x.at[b]).wait()

    def wait_y(b):
        pltpu.make_async_copy(ybuf.at[b], ys_hbm.at[pl.ds(0, TM)], sem_y.at[b]).wait()

    @pl.when(i == 0)
    def _():
        zpiece[...] = jnp.zeros_like(zpiece)

        def fill_tile(t, c):
            def dst(p):
                row = pl.multiple_of(t * S_MAX + tot_ref[t] + p * PIECE, PIECE)
                return ys_hbm.at[pl.ds(row, PIECE)]

            def start(p, c2):
                pltpu.make_async_copy(zpiece, dst(p), sem_z.at[0]).start()
                return c2
            return lax.fori_loop(0, (S_MAX - tot_ref[t]) // PIECE, start, c)
        lax.fori_loop(0, N_TILES, fill_tile, 0)
        start_x(0, 0)
        start_x(jnp.minimum(1, n_used - 1), 1)
        ybuf[...] = jnp.zeros_like(ybuf)
        for b in range(2):
            pltpu.make_async_copy(ybuf.at[b], ys_hbm.at[pl.ds(DUMP_ROW + b * TM, TM)],
                                  sem_y.at[b]).start()

    @pl.when(live)
    def _():
        ring = i % 3
        expert = te_ref[i]
        wait_x(ring)

        @pl.when(first_ref[i] == 1)
        def _():
            slot = ord_ref[i] % 2

            @pl.when(i == 0)
            def _():
                for cp in weight_copies(te_ref[0], 0):
                    cp.start(priority=1)
            for cp in weight_copies(expert, slot):
                cp.wait()

            @pl.when(nxt_ref[i] >= 0)
            def _():
                for cp in weight_copies(nxt_ref[i], 1 - slot):
                    cp.start(priority=1)
            w1b[...] = w1f[slot].astype(bf16)
            w2b[...] = w2f[slot].astype(bf16)

        wait_y(ring)

        def mlp(rows):
            x = xbuf[ring, 0:rows, :]
            gu = jnp.dot(x, w1b[...], preferred_element_type=f32) + b1_ref[expert]
            gate = jnp.minimum(gu[:, :D_FF], SWIGLU_LIMIT)
            lin = jnp.clip(gu[:, D_FF:], -SWIGLU_LIMIT, SWIGLU_LIMIT)
            glu = gate * jax.nn.sigmoid(SWIGLU_ALPHA * gate)
            h = (glu * (lin + 1.0)).astype(bf16)
            start_x(jnp.minimum(i + 2, n_used - 1), (i + 2) % 3)
            start_y(jnp.maximum(i - 1, 0), (i + 2) % 3, all_dump=i == 0)
            out = jnp.dot(h, w2b[...], preferred_element_type=f32) + b2_ref[expert]
            ybuf[ring, 0:rows, :] = out.astype(bf16)

        for n_units in range(1, TILE_UNITS + 1):
            @pl.when(len_ref[i] == n_units)
            def _(n_units=n_units):
                mlp(n_units * UNIT)

        @pl.when(i == n_used - 1)
        def _():
            start_y(i, ring)
            for b in range(3):
                wait_y(b)
            wait_x((i + 1) % 3)
            wait_x((i + 2) % 3)

            def drain_tile(t, c):
                def wait(p, c2):
                    pltpu.make_async_copy(zpiece, ys_hbm.at[pl.ds(0, PIECE)], sem_z.at[0]).wait()
                    return c2
                return lax.fori_loop(0, (S_MAX - tot_ref[t]) // PIECE, wait, c)
            lax.fori_loop(0, N_TILES, drain_tile, 0)


def _experts(tile_expert, n_used, first, order, nxt, unit0, length, piece_row, tot, srt, w1, b1, w2, b2):
    whole = lambda shape: pl.BlockSpec(shape, lambda i, *_: (0,) * len(shape))
    grid_spec = pltpu.PrefetchScalarGridSpec(
        num_scalar_prefetch=9,
        grid=(N_ROW_TILES,),
        in_specs=[
            whole((N_EXPERTS, 1, 2 * D_FF)),
            whole((N_EXPERTS, 1, D_MODEL)),
            pl.BlockSpec(memory_space=pl.ANY),
            pl.BlockSpec(memory_space=pl.ANY),
            pl.BlockSpec(memory_space=pl.ANY),
        ],
        out_specs=pl.BlockSpec(memory_space=pl.ANY),
        scratch_shapes=[
            pltpu.VMEM((3, TM, D_MODEL), bf16), pltpu.VMEM((3, TM, D_MODEL), bf16),
            pltpu.VMEM((PIECE, D_MODEL), bf16),
            pltpu.VMEM((2, D_MODEL, 2 * D_FF), f32), pltpu.VMEM((2, D_FF, D_MODEL), f32),
            pltpu.VMEM((D_MODEL, 2 * D_FF), bf16), pltpu.VMEM((D_FF, D_MODEL), bf16),
            pltpu.SemaphoreType.DMA((4,)), pltpu.SemaphoreType.DMA((3,)),
            pltpu.SemaphoreType.DMA((3,)), pltpu.SemaphoreType.DMA((1,)),
        ],
    )
    return pl.pallas_call(
        _expert_kernel,
        grid_spec=grid_spec,
        out_shape=jax.ShapeDtypeStruct((N_SORT_ROWS + 3 * TM, D_MODEL), bf16),
        compiler_params=pltpu.CompilerParams(
            dimension_semantics=("arbitrary",), vmem_limit_bytes=VMEM_LIMIT),
        name="experts",
    )(tile_expert, n_used, first, order, nxt, unit0, length, piece_row, tot, b1, b2, srt, w1, w2)


def _combine_tile(route_ref, x1_ref, ys_ref, out_ref, *, gf_ref, wgt):
    slot = route_ref[0:TOP_K, :]
    g = route_ref[TOP_K:2 * TOP_K, :].astype(bf16)
    row_io = lax.broadcasted_iota(i32, (S_BLK, TS), 0).astype(bf16)
    for b in range(S_MAX // S_BLK):
        local = slot - float(b * S_BLK)
        local = jnp.where((local >= 0.0) & (local < float(S_BLK)), local, -1.0).astype(bf16)
        w = jnp.zeros((S_BLK, TS), bf16)
        for k in range(TOP_K):
            w = jnp.where(row_io == local[k:k + 1, :], g[k:k + 1, :], w)
        wgt[b * S_BLK:(b + 1) * S_BLK, :] = w

    y = x1_ref[...] + lax.dot_general(wgt[...], ys_ref[...], (((0,), (0,)), ((), ())),
                                      preferred_element_type=f32)
    out_ref[...] = _rms(y, gf_ref[...])


def _combine_kernel(route_hbm, x1_hbm, gf_ref, ys_hbm, yp_hbm, ysm_hbm, wgt):
    body = functools.partial(_combine_tile, gf_ref=gf_ref, wgt=wgt)

    def specs(first_tile):
        deep = dict(pipeline_mode=pl.Buffered(3))
        return dict(
            in_specs=[
                pl.BlockSpec((2 * TOP_K, TS), lambda i: (0, first_tile + i)),
                pl.BlockSpec((TS, D_MODEL), lambda i: (first_tile + i, 0), **deep),
                pl.BlockSpec((S_MAX, D_MODEL), lambda i: (first_tile + i, 0), **deep),
            ],
            out_specs=[pl.BlockSpec((TS, D_MODEL), lambda i: (i, 0))])

    pltpu.emit_pipeline(body, grid=(N_PROMPT_TILES,), **specs(0))(
        route_hbm, x1_hbm, ys_hbm, yp_hbm)
    pltpu.emit_pipeline(body, grid=(1,), **specs(N_PROMPT_TILES))(
        route_hbm, x1_hbm, ys_hbm, ysm_hbm)


def _combine(route_t, x1, gf, ys):
    return pl.pallas_call(
        _combine_kernel,
        in_specs=[
            pl.BlockSpec(memory_space=pl.ANY),
            pl.BlockSpec(memory_space=pl.ANY),
            pl.BlockSpec(memory_space=pltpu.VMEM),
            pl.BlockSpec(memory_space=pl.ANY),
        ],
        out_specs=(
            pl.BlockSpec(memory_space=pl.ANY),
            pl.BlockSpec(memory_space=pl.ANY),
        ),
        out_shape=(
            jax.ShapeDtypeStruct((N_PROMPT, D_MODEL), f32),
            jax.ShapeDtypeStruct((N_SAMPLE, D_MODEL), f32),
        ),
        scratch_shapes=[pltpu.VMEM((S_MAX, TS), bf16)],
        compiler_params=pltpu.CompilerParams(vmem_limit_bytes=VMEM_LIMIT),
        name="combine",
    )(route_t, x1, gf, ys)


def _routing_tables(cnt):
    n = cnt.reshape(N_TILES, N_EXPERTS).astype(i32)
    c = (n + (PIECE - 1)) // PIECE * PIECE
    lo = jnp.cumsum(c, axis=1) - c
    tot = jnp.sum(c, axis=1)
    group = jnp.sum(c, axis=0)
    padded = (group + (UNIT - 1)) // UNIT * UNIT
    gend = jnp.cumsum(padded)
    gstart = gend - padded
    cs = gstart[None, :] + jnp.cumsum(c, axis=0) - c
    units = padded // UNIT
    tiles = (units + (TILE_UNITS - 1)) // TILE_UNITS
    tend = jnp.cumsum(tiles)
    n_used = tend[-1]
    tile_id = jnp.arange(N_ROW_TILES, dtype=i32)
    tile_expert = jnp.sum((tile_id[:, None] >= tend[None, :]).astype(i32), axis=1)
    tile_expert = jnp.minimum(tile_expert, jnp.take(tile_expert, n_used - 1))
    e_ar = jnp.arange(N_EXPERTS, dtype=i32)
    mine = tile_expert[:, None] == e_ar[None, :]
    pick = lambda v: jnp.sum(jnp.where(mine, v[None, :], 0), axis=1)
    in_group = tile_id - pick(tend - tiles)
    unit0 = pick(gstart // UNIT) + TILE_UNITS * in_group
    length = jnp.clip(pick(units) - TILE_UNITS * in_group, 1, TILE_UNITS)
    first = (tile_id < n_used) & ((tile_id == 0) | (tile_expert != jnp.roll(tile_expert, 1)))
    order = jnp.maximum(jnp.cumsum(first.astype(i32)) - 1, 0)
    later = (e_ar[None, :] > e_ar[:, None]) & (padded[None, :] > 0)
    nxt_e = jnp.min(jnp.where(later, e_ar[None, :], N_EXPERTS), axis=1)
    nxt_e = jnp.where(nxt_e == N_EXPERTS, -1, nxt_e)
    nxt = pick(nxt_e)
    unit_row = jnp.arange(N_UNITS + TILE_UNITS - 1, dtype=i32) * UNIT
    unit_e = jnp.minimum(jnp.sum((unit_row[:, None] >= gend[None, :]).astype(i32), axis=1),
                         N_EXPERTS - 1)
    of_unit = (unit_e[:, None] == e_ar[None, :])[:, None, :]
    per_unit = lambda tab: jnp.sum(jnp.where(of_unit, tab[None], 0), axis=2)
    cs_u, c_u = per_unit(cs), per_unit(c)
    src_u = per_unit(jnp.arange(N_TILES, dtype=i32)[:, None] * S_MAX + lo)
    piece_r = unit_row[:, None] + jnp.arange(UNIT_PIECES, dtype=i32)[None, :] * PIECE
    chunk = jnp.sum((piece_r[:, :, None] >= cs_u[:, None, :]).astype(i32), axis=2) - 1
    is_chunk = chunk[:, :, None] == jnp.arange(N_TILES, dtype=i32)[None, None, :]
    of_chunk = lambda tab_u: jnp.sum(jnp.where(is_chunk, tab_u[:, None, :], 0), axis=2)
    off = piece_r - of_chunk(cs_u)
    piece_row = jnp.where(off < of_chunk(c_u), of_chunk(src_u) + off, -1).reshape(-1)
    return dict(tot=tot, n_used=n_used[None], tile_expert=tile_expert, first=first.astype(i32),
                order=order, nxt=nxt, unit0=unit0, length=length, piece_row=piece_row)


def kernel(x_prompt, x_sample, state_conv, state_pool, norm_mix_g, w_in, conv_w, pool_w, pool_scale,
           w_out, norm_ffn_g, router_w, router_b, exp_w1, exp_b1, exp_w2, exp_b2, final_norm_g):
    l = 0
    xp = x_prompt.reshape(N_PROMPT, D_MODEL)
    xs = jnp.transpose(x_sample, (1, 0, 2)).reshape(N_SAMPLE, D_MODEL)
    sc_t = jnp.transpose(state_conv[l], (1, 0, 2))
    sp_t = jnp.transpose(state_pool[l], (1, 0, 2))
    rw_t = router_w[l].T
    rwh = rw_t.astype(bf16)
    rw2 = jnp.concatenate([rwh, (rw_t - rwh.astype(f32)).astype(bf16)], axis=0)
    tok = jnp.arange(TS, dtype=i32)
    tri = (tok[:, None] < tok[None, :]).astype(bf16)

    (x1, srt, route_t, cnt, npc, npp, nsc, nsv) = _mixer(
        xp, xs, sc_t, sp_t, norm_mix_g[l][None, :], w_in[l].astype(bf16), conv_w[l],
        pool_w[l].astype(bf16), pool_scale[l][None, :], w_out[l].astype(bf16),
        norm_ffn_g[l][None, :], rw2, router_b[l][:, None], tri)

    t = _routing_tables(cnt)
    ys = _experts(t["tile_expert"], t["n_used"], t["first"], t["order"], t["nxt"], t["unit0"],
                  t["length"], t["piece_row"], t["tot"], srt, exp_w1[l], exp_b1[l][:, None, :],
                  exp_w2[l], exp_b2[l][:, None, :])
    y_p, y_s = _combine(route_t, x1, final_norm_g[None, :], ys)

    y_prompt = y_p.reshape(BATCH, SEQ, D_MODEL)
    y_sample = jnp.transpose(y_s.reshape(DEC_SEQ, DEC_BATCH, D_MODEL), (1, 0, 2))
    new_conv_prompt = npc[None, :, 8 - (CONV_W - 1):, :]
    new_pool_prompt = npp[None, :, HIST - POOL_BUF:, :]
    new_conv_sample = jnp.transpose(nsc, (1, 0, 2))[None]
    new_pool_sample = jnp.concatenate(
        [state_pool[l][:, DEC_SEQ:, :], jnp.transpose(nsv, (1, 0, 2))], axis=1)[None]
    return (y_prompt, y_sample, new_conv_prompt, new_pool_prompt, new_conv_sample, new_pool_sample)
```

```python
import functools

import jax
import jax.numpy as jnp
from jax import lax
from jax.experimental import pallas as pl
from jax.experimental.pallas import tpu as pltpu

D_MODEL = 1024
D_CONV = 512
D_POOL = 512
D_IN = 3 * D_CONV + D_POOL
CONV_W = 3
POOL_WINDOWS = (2, 4, 8, 16)
POOL_GROUP = 128
POOL_BUF = 15
N_EXPERTS = 32
TOP_K = 4
D_FF = 1024
SWIGLU_LIMIT = 7.0
SWIGLU_ALPHA = 1.702
EPS = 1e-5
PAST_LEN = 16384

BATCH, SEQ = 8, 2048
DEC_BATCH, DEC_SEQ = 128, 4
N_PROMPT = BATCH * SEQ
N_SAMPLE = DEC_BATCH * DEC_SEQ
N_TOK = N_PROMPT + N_SAMPLE

TS = 512
SEQ_TILES = SEQ // TS
N_PROMPT_TILES = N_PROMPT // TS
N_TILES = N_PROMPT_TILES + 1
HIST = 16
UNIT = 128
TILE_UNITS = 4
TM = TILE_UNITS * UNIT
PIECE = 8
PIECES = TM // PIECE
UNIT_PIECES = UNIT // PIECE
S_BLK = 256
SORT_BLK = 512
S_MAX = (TS * TOP_K + N_EXPERTS * (PIECE - 1) + PIECE + S_BLK - 1) // S_BLK * S_BLK
ZERO_ROW = S_MAX - PIECE
N_UNITS = (N_TOK * TOP_K + N_TILES * N_EXPERTS * (PIECE - 1) + N_EXPERTS * (UNIT - PIECE)) // UNIT
N_ROW_TILES = (N_UNITS + (TILE_UNITS - 1) * N_EXPERTS) // TILE_UNITS
N_SORT_ROWS = N_TILES * S_MAX
DUMP_ROW = N_SORT_ROWS

VMEM_LIMIT = 56 * 1024 * 1024

f32 = jnp.float32
bf16 = jnp.bfloat16
i32 = jnp.int32


def _rms(x, g):
    return x * lax.rsqrt(jnp.mean(x * x, axis=-1, keepdims=True) + EPS) * g


def _pool_project(p, pw_ref, ps_ref):
    outs = []
    for g in range(len(POOL_WINDOWS)):
        sl = slice(g * POOL_GROUP, (g + 1) * POOL_GROUP)
        outs.append(jnp.dot(p[:, sl].astype(bf16), pw_ref[g], preferred_element_type=f32))
    return jnp.concatenate(outs, axis=1) * ps_ref[...]


def _route_and_sort(x, z, wout_ref, g2_ref, rw2_ref, rb_ref, tri_ref,
                    x1_ref, srt_ref, route_ref, cnt_ref):
    n = x.shape[0]
    x1 = x + jnp.dot(z.astype(bf16), wout_ref[...], preferred_element_type=f32)
    x1_ref[...] = x1
    xn = _rms(x1, g2_ref[...])
    xb = xn.astype(bf16)

    xlo = (xn - xb.astype(f32)).astype(bf16)
    nt = (((1,), (1,)), ((), ()))
    hi_lo = lax.dot_general(rw2_ref[...], xb, nt, preferred_element_type=f32)
    logits = (hi_lo[:N_EXPERTS] + hi_lo[N_EXPERTS:]
              + lax.dot_general(rw2_ref[0:N_EXPERTS, :], xlo, nt, preferred_element_type=f32)
              + rb_ref[...])

    e_iota = lax.broadcasted_iota(i32, (N_EXPERTS, n), 0)
    work = logits
    hots, vals = [], []
    for _ in range(TOP_K):
        m = jnp.max(work, axis=0, keepdims=True)
        ik = jnp.min(jnp.where(work == m, e_iota, N_EXPERTS), axis=0, keepdims=True)
        hot = e_iota == ik
        work = jnp.where(hot, -jnp.inf, work)
        hots.append(hot); vals.append(m)
    exps = [jnp.exp(v - vals[0]) for v in vals]
    den = exps[0] + exps[1] + exps[2] + exps[3]
    gates = [e / den for e in exps]

    multi = jnp.where(hots[0] | hots[1] | hots[2] | hots[3], 1.0, 0.0)
    before = jnp.dot(multi.astype(bf16), tri_ref[...], preferred_element_type=f32)
    count = jnp.sum(multi, axis=1, keepdims=True)
    cnt_ref[...] = count
    pieces = jnp.floor((count + (PIECE - 1)) * (1.0 / PIECE))
    er = lax.broadcasted_iota(i32, (N_EXPERTS, N_EXPERTS), 0)
    ec = lax.broadcasted_iota(i32, (N_EXPERTS, N_EXPERTS), 1)
    lower = jnp.where(ec < er, 1.0, 0.0).astype(bf16)
    pieces_b = jnp.broadcast_to(pieces, (N_EXPERTS, 128)).astype(bf16)
    start = PIECE * jnp.dot(lower, pieces_b, preferred_element_type=f32)[:, 0:1]
    base = before + start
    slot = jnp.concatenate(
        [jnp.sum(jnp.where(h, base, 0.0), axis=0, keepdims=True) for h in hots], axis=0).astype(i32)
    route_ref[...] = jnp.concatenate([slot.astype(f32)] + gates, axis=0)

    row_io = lax.broadcasted_iota(i32, (S_BLK, n), 0).astype(bf16)
    slot_f = slot.astype(f32)
    one, zero = jnp.ones((S_BLK, n), bf16), jnp.zeros((S_BLK, n), bf16)
    for row0 in range(0, S_MAX, SORT_BLK):
        rows = min(SORT_BLK, S_MAX - row0)
        parts = []
        for sub in range(row0, row0 + rows, S_BLK):
            local = slot_f - float(sub)
            local = jnp.where((local >= 0.0) & (local < float(S_BLK)), local, -1.0).astype(bf16)
            hit = row_io == local[0:1, :]
            for k in range(1, TOP_K):
                hit = hit | (row_io == local[k:k + 1, :])
            parts.append(jnp.where(hit, one, zero))
        sel = parts[0] if len(parts) == 1 else jnp.concatenate(parts, axis=0)
        srt_ref[row0:row0 + rows, :] = jnp.dot(sel, xb, preferred_element_type=f32).astype(bf16)


def _mixer_kernel(xp_in_ref, xs_in_ref, sc_ref, sp_ref, g1_ref, win_ref, cw_ref, pw_ref, ps_ref,
                  wout_ref, g2_ref, rw2_ref, rb_ref, tri_ref,
                  x1_ref, srt_ref, route_ref, cnt_ref,
                  npc_ref, npp_ref, nsc_ref, nsv_ref,
                  ubuf, vbuf, xcur, zcur):
    i = pl.program_id(0)

    @pl.when(i < N_PROMPT_TILES)
    def _():
        s = i % SEQ_TILES

        @pl.when(s == 0)
        def _():
            ubuf[0:HIST, :] = jnp.zeros((HIST, D_CONV), f32)
            vbuf[0:HIST, :] = jnp.zeros((HIST, D_POOL), f32)

        x = xp_in_ref[...]
        xcur[...] = x
        xn = _rms(x, g1_ref[...]).astype(bf16)
        proj = jnp.dot(xn, win_ref[...], preferred_element_type=f32)
        bg = proj[:, :D_CONV]
        u = proj[:, D_CONV:2 * D_CONV] * proj[:, 2 * D_CONV:3 * D_CONV]
        v = proj[:, 3 * D_CONV:]
        ubuf[HIST:HIST + TS, :] = u
        vbuf[HIST:HIST + TS, :] = v

        uh = ubuf[...]
        y = (pltpu.roll(uh, 2, axis=0) * cw_ref[0:1, :]
             + pltpu.roll(uh, 1, axis=0) * cw_ref[1:2, :])[HIST:, :] + u * cw_ref[2:3, :]
        zcur[:, :D_CONV] = bg * y

        pos = s * TS + lax.broadcasted_iota(i32, (TS, 1), 0)
        ps = []
        for g, w in enumerate(POOL_WINDOWS):
            sl = slice(g * POOL_GROUP, (g + 1) * POOL_GROUP)
            acc = vbuf[:, sl]
            for step in range(g + 1):
                acc = acc + pltpu.roll(acc, 1 << step, axis=0)
            cnt = jnp.minimum(w, pos + 1).astype(f32)
            ps.append(acc[HIST:, :] * (1.0 / cnt) - v[:, sl])
        zcur[:, D_CONV:] = _pool_project(jnp.concatenate(ps, axis=1), pw_ref, ps_ref)

        ubuf[HIST - 8:HIST, :] = ubuf[TS + HIST - 8:TS + HIST, :]
        vbuf[0:HIST, :] = vbuf[TS:TS + HIST, :]

        @pl.when(s == SEQ_TILES - 1)
        def _():
            npc_ref[0] = ubuf[HIST - 8:HIST, :]
            npp_ref[0] = vbuf[0:HIST, :]

    @pl.when(i == N_PROMPT_TILES)
    def _():
        x = xs_in_ref[...]
        xcur[...] = x
        xn = _rms(x, g1_ref[...]).astype(bf16)
        proj = jnp.dot(xn, win_ref[...], preferred_element_type=f32)
        nb = DEC_BATCH
        rows = lambda a, t: a[t * nb:(t + 1) * nb]
        bg = proj[:, :D_CONV]
        u = proj[:, D_CONV:2 * D_CONV] * proj[:, 2 * D_CONV:3 * D_CONV]
        v = proj[:, 3 * D_CONV:]
        up = [sc_ref[j] for j in range(CONV_W - 1)] + [rows(u, t) for t in range(DEC_SEQ)]
        vp = [sp_ref[j] for j in range(POOL_BUF)] + [rows(v, t) for t in range(DEC_SEQ)]
        zc, ps = [], []
        for t in range(DEC_SEQ):
            y = up[t] * cw_ref[0:1, :]
            for k in range(1, CONV_W):
                y = y + up[t + k] * cw_ref[k:k + 1, :]
            zc.append(rows(bg, t) * y)
            pg = []
            for g, w in enumerate(POOL_WINDOWS):
                sl = slice(g * POOL_GROUP, (g + 1) * POOL_GROUP)
                lo = t + POOL_BUF - w + 1
                acc = vp[lo][:, sl]
                for j in range(lo + 1, t + POOL_BUF + 1):
                    acc = acc + vp[j][:, sl]
                cnt = float(min(w, PAST_LEN + t + 1))
                pg.append(acc * (1.0 / cnt) - vp[t + POOL_BUF][:, sl])
            ps.append(jnp.concatenate(pg, axis=1))
        zcur[:, :D_CONV] = jnp.concatenate(zc, axis=0)
        zcur[:, D_CONV:] = _pool_project(jnp.concatenate(ps, axis=0), pw_ref, ps_ref)
        for j in range(CONV_W - 1):
            nsc_ref[j] = up[DEC_SEQ + j]
        for t in range(DEC_SEQ):
            nsv_ref[t] = rows(v, t)

    _route_and_sort(xcur[...], zcur[...], wout_ref, g2_ref, rw2_ref, rb_ref, tri_ref,
                    x1_ref, srt_ref, route_ref, cnt_ref)


def _mixer(xp, xs, sc_t, sp_t, g1, win, cw, pw, ps, wout, g2, rw2, rb, tri):
    const = lambda shape: pl.BlockSpec(shape, lambda i: (0,) * len(shape),
                                       pipeline_mode=pl.Buffered(1))
    ptile = lambda i: jnp.minimum(i, N_PROMPT_TILES - 1)
    pbatch = lambda i: jnp.minimum(i, N_PROMPT_TILES - 1) // SEQ_TILES
    out_shape = (
        jax.ShapeDtypeStruct((N_TOK, D_MODEL), f32),
        jax.ShapeDtypeStruct((N_SORT_ROWS, D_MODEL), bf16),
        jax.ShapeDtypeStruct((2 * TOP_K, N_TOK), f32),
        jax.ShapeDtypeStruct((N_TILES * N_EXPERTS, 1), f32),
        jax.ShapeDtypeStruct((BATCH, 8, D_CONV), f32),
        jax.ShapeDtypeStruct((BATCH, HIST, D_POOL), f32),
        jax.ShapeDtypeStruct((CONV_W - 1, DEC_BATCH, D_CONV), f32),
        jax.ShapeDtypeStruct((DEC_SEQ, DEC_BATCH, D_POOL), f32),
    )
    out_specs = (
        pl.BlockSpec((TS, D_MODEL), lambda i: (i, 0)),
        pl.BlockSpec((S_MAX, D_MODEL), lambda i: (i, 0)),
        pl.BlockSpec((2 * TOP_K, TS), lambda i: (0, i)),
        pl.BlockSpec((N_EXPERTS, 1), lambda i: (i, 0)),
        pl.BlockSpec((1, 8, D_CONV), lambda i: (pbatch(i), 0, 0)),
        pl.BlockSpec((1, HIST, D_POOL), lambda i: (pbatch(i), 0, 0)),
        pl.BlockSpec((CONV_W - 1, DEC_BATCH, D_CONV), lambda i: (0, 0, 0)),
        pl.BlockSpec((DEC_SEQ, DEC_BATCH, D_POOL), lambda i: (0, 0, 0)),
    )
    in_specs = [
        pl.BlockSpec((TS, D_MODEL), lambda i: (ptile(i), 0)),
        const((N_SAMPLE, D_MODEL)),
        const((CONV_W - 1, DEC_BATCH, D_CONV)),
        const((POOL_BUF, DEC_BATCH, D_POOL)),
        const((1, D_MODEL)),
        const((D_MODEL, D_IN)),
        const((CONV_W, D_CONV)),
        const((len(POOL_WINDOWS), POOL_GROUP, POOL_GROUP)),
        const((1, D_POOL)),
        const((D_MODEL, D_MODEL)),
        const((1, D_MODEL)),
        const((2 * N_EXPERTS, D_MODEL)),
        const((N_EXPERTS, 1)),
        const((TS, TS)),
    ]
    return pl.pallas_call(
        _mixer_kernel,
        grid=(N_TILES,),
        in_specs=in_specs,
        out_specs=out_specs,
        out_shape=out_shape,
        scratch_shapes=[
            pltpu.VMEM((TS + HIST, D_CONV), f32),
            pltpu.VMEM((TS + HIST, D_POOL), f32),
            pltpu.VMEM((TS, D_MODEL), f32),
            pltpu.VMEM((TS, D_MODEL), f32),
        ],
        compiler_params=pltpu.CompilerParams(
            dimension_semantics=("arbitrary",), vmem_limit_bytes=VMEM_LIMIT),
        name="mixer",
    )(xp, xs, sc_t, sp_t, g1, win, cw, pw, ps, wout, g2, rw2, rb, tri)


def _expert_kernel(te_ref, nu_ref, first_ref, ord_ref, nxt_ref, unit_ref, len_ref, pt_ref, tot_ref,
                   b1_ref, b2_ref, srt_hbm, w1_hbm, w2_hbm, ys_hbm,
                   xbuf, ybuf, zpiece, w1f, w2f, w1b, w2b, sem_w, sem_x, sem_y, sem_z):
    i = pl.program_id(0)
    n_used = nu_ref[0]
    live = i < n_used

    def weight_copies(e, slot):
        return (pltpu.make_async_copy(w1_hbm.at[e], w1f.at[slot], sem_w.at[slot]),
                pltpu.make_async_copy(w2_hbm.at[e], w2f.at[slot], sem_w.at[2 + slot]))

    def is_padding(entry, tile, p):
        if p < UNIT_PIECES:
            return entry < 0
        return (entry < 0) | (p >= len_ref[tile] * UNIT_PIECES)

    def start_x(tile, b):
        for p in range(PIECES):
            src = pt_ref[unit_ref[tile] * UNIT_PIECES + p]
            src = pl.multiple_of(jnp.where(is_padding(src, tile, p), ZERO_ROW, src), PIECE)
            pltpu.make_async_copy(srt_hbm.at[pl.ds(src, PIECE)],
                                  xbuf.at[b, pl.ds(p * PIECE, PIECE)], sem_x.at[b]).start()

    def start_y(tile, b, all_dump=False):
        for p in range(PIECES):
            dst = pt_ref[unit_ref[tile] * UNIT_PIECES + p]
            dump = is_padding(dst, tile, p) | all_dump
            dst = pl.multiple_of(jnp.where(dump, DUMP_ROW + b * TM + p * PIECE, dst), PIECE)
            pltpu.make_async_copy(ybuf.at[b, pl.ds(p * PIECE, PIECE)],
                                  ys_hbm.at[pl.ds(dst, PIECE)], sem_y.at[b]).start(priority=p % 2)

    def wait_x(b):
        pltpu.make_async_copy(srt_hbm.at[pl.ds(0, TM)], xbuf.at[b], sem_x.at[b]).wait()

    def wait_y(b):
        pltpu.make_async_copy(ybuf.at[b], ys_hbm.at[pl.ds(0, TM)], sem_y.at[b]).wait()

    @pl.when(i == 0)
    def _():
        zpiece[...] = jnp.zeros_like(zpiece)

        def fill_tile(t, c):
            def dst(p):
                row = pl.multiple_of(t * S_MAX + tot_ref[t] + p * PIECE, PIECE)
                return ys_hbm.at[pl.ds(row, PIECE)]

            def start(p, c2):
                pltpu.make_async_copy(zpiece, dst(p), sem_z.at[0]).start()
                return c2
            return lax.fori_loop(0, (S_MAX - tot_ref[t]) // PIECE, start, c)
        lax.fori_loop(0, N_TILES, fill_tile, 0)
        start_x(0, 0)
        start_x(jnp.minimum(1, n_used - 1), 1)
        ybuf[...] = jnp.zeros_like(ybuf)
        for b in range(2):
            pltpu.make_async_copy(ybuf.at[b], ys_hbm.at[pl.ds(DUMP_ROW + b * TM, TM)],
                                  sem_y.at[b]).start()

    @pl.when(live)
    def _():
        ring = i % 3
        expert = te_ref[i]
        wait_x(ring)

        @pl.when(first_ref[i] == 1)
        def _():
            slot = ord_ref[i] % 2

            @pl.when(i == 0)
            def _():
                for cp in weight_copies(te_ref[0], 0):
                    cp.start(priority=1)
            for cp in weight_copies(expert, slot):
                cp.wait()

            @pl.when(nxt_ref[i] >= 0)
            def _():
                for cp in weight_copies(nxt_ref[i], 1 - slot):
                    cp.start(priority=1)
            w1b[...] = w1f[slot].astype(bf16)
            w2b[...] = w2f[slot].astype(bf16)

        wait_y(ring)

        def mlp(rows):
            x = xbuf[ring, 0:rows, :]
            gu = jnp.dot(x, w1b[...], preferred_element_type=f32) + b1_ref[expert]
            gate = jnp.minimum(gu[:, :D_FF], SWIGLU_LIMIT)
            lin = jnp.clip(gu[:, D_FF:], -SWIGLU_LIMIT, SWIGLU_LIMIT)
            glu = gate * jax.nn.sigmoid(SWIGLU_ALPHA * gate)
            h = (glu * (lin + 1.0)).astype(bf16)
            start_x(jnp.minimum(i + 2, n_used - 1), (i + 2) % 3)
            start_y(jnp.maximum(i - 1, 0), (i + 2) % 3, all_dump=i == 0)
            out = jnp.dot(h, w2b[...], preferred_element_type=f32) + b2_ref[expert]
            ybuf[ring, 0:rows, :] = out.astype(bf16)

        for n_units in range(1, TILE_UNITS + 1):
            @pl.when(len_ref[i] == n_units)
            def _(n_units=n_units):
                mlp(n_units * UNIT)

        @pl.when(i == n_used - 1)
        def _():
            start_y(i, ring)
            for b in range(3):
                wait_y(b)
            wait_x((i + 1) % 3)
            wait_x((i + 2) % 3)

            def drain_tile(t, c):
                def wait(p, c2):
                    pltpu.make_async_copy(zpiece, ys_hbm.at[pl.ds(0, PIECE)], sem_z.at[0]).wait()
                    return c2
                return lax.fori_loop(0, (S_MAX - tot_ref[t]) // PIECE, wait, c)
            lax.fori_loop(0, N_TILES, drain_tile, 0)


def _experts(tile_expert, n_used, first, order, nxt, unit0, length, piece_row, tot, srt, w1, b1, w2, b2):
    whole = lambda shape: pl.BlockSpec(shape, lambda i, *_: (0,) * len(shape))
    grid_spec = pltpu.PrefetchScalarGridSpec(
        num_scalar_prefetch=9,
        grid=(N_ROW_TILES,),
        in_specs=[
            whole((N_EXPERTS, 1, 2 * D_FF)),
            whole((N_EXPERTS, 1, D_MODEL)),
            pl.BlockSpec(memory_space=pl.ANY),
            pl.BlockSpec(memory_space=pl.ANY),
            pl.BlockSpec(memory_space=pl.ANY),
        ],
        out_specs=pl.BlockSpec(memory_space=pl.ANY),
        scratch_shapes=[
            pltpu.VMEM((3, TM, D_MODEL), bf16), pltpu.VMEM((3, TM, D_MODEL), bf16),
            pltpu.VMEM((PIECE, D_MODEL), bf16),
            pltpu.VMEM((2, D_MODEL, 2 * D_FF), f32), pltpu.VMEM((2, D_FF, D_MODEL), f32),
            pltpu.VMEM((D_MODEL, 2 * D_FF), bf16), pltpu.VMEM((D_FF, D_MODEL), bf16),
            pltpu.SemaphoreType.DMA((4,)), pltpu.SemaphoreType.DMA((3,)),
            pltpu.SemaphoreType.DMA((3,)), pltpu.SemaphoreType.DMA((1,)),
        ],
    )
    return pl.pallas_call(
        _expert_kernel,
        grid_spec=grid_spec,
        out_shape=jax.ShapeDtypeStruct((N_SORT_ROWS + 3 * TM, D_MODEL), bf16),
        compiler_params=pltpu.CompilerParams(
            dimension_semantics=("arbitrary",), vmem_limit_bytes=VMEM_LIMIT),
        name="experts",
    )(tile_expert, n_used, first, order, nxt, unit0, length, piece_row, tot, b1, b2, srt, w1, w2)


def _combine_tile(route_ref, x1_ref, ys_ref, out_ref, *, gf_ref, wgt):
    slot = route_ref[0:TOP_K, :]
    g = route_ref[TOP_K:2 * TOP_K, :].astype(bf16)
    row_io = lax.broadcasted_iota(i32, (S_BLK, TS), 0).astype(bf16)
    for b in range(S_MAX // S_BLK):
        local = slot - float(b * S_BLK)
        local = jnp.where((local >= 0.0) & (local < float(S_BLK)), local, -1.0).astype(bf16)
        w = jnp.zeros((S_BLK, TS), bf16)
        for k in range(TOP_K):
            w = jnp.where(row_io == local[k:k + 1, :], g[k:k + 1, :], w)
        wgt[b * S_BLK:(b + 1) * S_BLK, :] = w

    y = x1_ref[...] + lax.dot_general(wgt[...], ys_ref[...], (((0,), (0,)), ((), ())),
                                      preferred_element_type=f32)
    out_ref[...] = _rms(y, gf_ref[...])


def _combine_kernel(route_hbm, x1_hbm, gf_ref, ys_hbm, yp_hbm, ysm_hbm, wgt):
    body = functools.partial(_combine_tile, gf_ref=gf_ref, wgt=wgt)

    def specs(first_tile):
        deep = dict(pipeline_mode=pl.Buffered(4))
        return dict(
            in_specs=[
                pl.BlockSpec((2 * TOP_K, TS), lambda i: (0, first_tile + i)),
                pl.BlockSpec((TS, D_MODEL), lambda i: (first_tile + i, 0), **deep),
                pl.BlockSpec((S_MAX, D_MODEL), lambda i: (first_tile + i, 0), **deep),
            ],
            out_specs=[pl.BlockSpec((TS, D_MODEL), lambda i: (i, 0))])

    pltpu.emit_pipeline(body, grid=(N_PROMPT_TILES,), **specs(0))(
        route_hbm, x1_hbm, ys_hbm, yp_hbm)
    pltpu.emit_pipeline(body, grid=(1,), **specs(N_PROMPT_TILES))(
        route_hbm, x1_hbm, ys_hbm, ysm_hbm)


def _combine(route_t, x1, gf, ys):
    return pl.pallas_call(
        _combine_kernel,
        in_specs=[
            pl.BlockSpec(memory_space=pl.ANY),
            pl.BlockSpec(memory_space=pl.ANY),
            pl.BlockSpec(memory_space=pltpu.VMEM),
            pl.BlockSpec(memory_space=pl.ANY),
        ],
        out_specs=(
            pl.BlockSpec(memory_space=pl.ANY),
            pl.BlockSpec(memory_space=pl.ANY),
        ),
        out_shape=(
            jax.ShapeDtypeStruct((N_PROMPT, D_MODEL), f32),
            jax.ShapeDtypeStruct((N_SAMPLE, D_MODEL), f32),
        ),
        scratch_shapes=[pltpu.VMEM((S_MAX, TS), bf16)],
        compiler_params=pltpu.CompilerParams(vmem_limit_bytes=VMEM_LIMIT),
        name="combine",
    )(route_t, x1, gf, ys)


def _routing_tables(cnt):
    n = cnt.reshape(N_TILES, N_EXPERTS).astype(i32)
    c = (n + (PIECE - 1)) // PIECE * PIECE
    lo = jnp.cumsum(c, axis=1) - c
    tot = jnp.sum(c, axis=1)
    group = jnp.sum(c, axis=0)
    padded = (group + (UNIT - 1)) // UNIT * UNIT
    gend = jnp.cumsum(padded)
    gstart = gend - padded
    cs = gstart[None, :] + jnp.cumsum(c, axis=0) - c
    units = padded // UNIT
    tiles = (units + (TILE_UNITS - 1)) // TILE_UNITS
    tend = jnp.cumsum(tiles)
    n_used = tend[-1]
    tile_id = jnp.arange(N_ROW_TILES, dtype=i32)
    tile_expert = jnp.sum((tile_id[:, None] >= tend[None, :]).astype(i32), axis=1)
    tile_expert = jnp.minimum(tile_expert, jnp.take(tile_expert, n_used - 1))
    e_ar = jnp.arange(N_EXPERTS, dtype=i32)
    mine = tile_expert[:, None] == e_ar[None, :]
    pick = lambda v: jnp.sum(jnp.where(mine, v[None, :], 0), axis=1)
    in_group = tile_id - pick(tend - tiles)
    unit0 = pick(gstart // UNIT) + TILE_UNITS * in_group
    length = jnp.clip(pick(units) - TILE_UNITS * in_group, 1, TILE_UNITS)
    first = (tile_id < n_used) & ((tile_id == 0) | (tile_expert != jnp.roll(tile_expert, 1)))
    order = jnp.maximum(jnp.cumsum(first.astype(i32)) - 1, 0)
    later = (e_ar[None, :] > e_ar[:, None]) & (padded[None, :] > 0)
    nxt_e = jnp.min(jnp.where(later, e_ar[None, :], N_EXPERTS), axis=1)
    nxt_e = jnp.where(nxt_e == N_EXPERTS, -1, nxt_e)
    nxt = pick(nxt_e)
    unit_row = jnp.arange(N_UNITS + TILE_UNITS - 1, dtype=i32) * UNIT
    unit_e = jnp.minimum(jnp.sum((unit_row[:, None] >= gend[None, :]).astype(i32), axis=1),
                         N_EXPERTS - 1)
    of_unit = (unit_e[:, None] == e_ar[None, :])[:, None, :]
    per_unit = lambda tab: jnp.sum(jnp.where(of_unit, tab[None], 0), axis=2)
    cs_u, c_u = per_unit(cs), per_unit(c)
    src_u = per_unit(jnp.arange(N_TILES, dtype=i32)[:, None] * S_MAX + lo)
    piece_r = unit_row[:, None] + jnp.arange(UNIT_PIECES, dtype=i32)[None, :] * PIECE
    chunk = jnp.sum((piece_r[:, :, None] >= cs_u[:, None, :]).astype(i32), axis=2) - 1
    is_chunk = chunk[:, :, None] == jnp.arange(N_TILES, dtype=i32)[None, None, :]
    of_chunk = lambda tab_u: jnp.sum(jnp.where(is_chunk, tab_u[:, None, :], 0), axis=2)
    off = piece_r - of_chunk(cs_u)
    piece_row = jnp.where(off < of_chunk(c_u), of_chunk(src_u) + off, -1).reshape(-1)
    return dict(tot=tot, n_used=n_used[None], tile_expert=tile_expert, first=first.astype(i32),
                order=order, nxt=nxt, unit0=unit0, length=length, piece_row=piece_row)


def kernel(x_prompt, x_sample, state_conv, state_pool, norm_mix_g, w_in, conv_w, pool_w, pool_scale,
           w_out, norm_ffn_g, router_w, router_b, exp_w1, exp_b1, exp_w2, exp_b2, final_norm_g):
    l = 0
    xp = x_prompt.reshape(N_PROMPT, D_MODEL)
    xs = jnp.transpose(x_sample, (1, 0, 2)).reshape(N_SAMPLE, D_MODEL)
    sc_t = jnp.transpose(state_conv[l], (1, 0, 2))
    sp_t = jnp.transpose(state_pool[l], (1, 0, 2))
    rw_t = router_w[l].T
    rwh = rw_t.astype(bf16)
    rw2 = jnp.concatenate([rwh, (rw_t - rwh.astype(f32)).astype(bf16)], axis=0)
    tok = jnp.arange(TS, dtype=i32)
    tri = (tok[:, None] < tok[None, :]).astype(bf16)

    (x1, srt, route_t, cnt, npc, npp, nsc, nsv) = _mixer(
        xp, xs, sc_t, sp_t, norm_mix_g[l][None, :], w_in[l].astype(bf16), conv_w[l],
        pool_w[l].astype(bf16), pool_scale[l][None, :], w_out[l].astype(bf16),
        norm_ffn_g[l][None, :], rw2, router_b[l][:, None], tri)

    t = _routing_tables(cnt)
    ys = _experts(t["tile_expert"], t["n_used"], t["first"], t["order"], t["nxt"], t["unit0"],
                  t["length"], t["piece_row"], t["tot"], srt, exp_w1[l], exp_b1[l][:, None, :],
                  exp_w2[l], exp_b2[l][:, None, :])
    y_p, y_s = _combine(route_t, x1, final_norm_g[None, :], ys)

    y_prompt = y_p.reshape(BATCH, SEQ, D_MODEL)
    y_sample = jnp.transpose(y_s.reshape(DEC_SEQ, DEC_BATCH, D_MODEL), (1, 0, 2))
    new_conv_prompt = npc[None, :, 8 - (CONV_W - 1):, :]
    new_pool_prompt = npp[None, :, HIST - POOL_BUF:, :]
    new_conv_sample = jnp.transpose(nsc, (1, 0, 2))[None]
    new_pool_sample = jnp.concatenate(
        [state_pool[l][:, DEC_SEQ:, :], jnp.transpose(nsv, (1, 0, 2))], axis=1)[None]
    return (y_prompt, y_sample, new_conv_prompt, new_pool_prompt, new_conv_sample, new_pool_sample)
```

```python
import functools

import jax
import jax.numpy as jnp
from jax import lax
from jax.experimental import pallas as pl
from jax.experimental.pallas import tpu as pltpu

D_MODEL = 1024
D_CONV = 512
D_POOL = 512
D_IN = 3 * D_CONV + D_POOL
CONV_W = 3
POOL_WINDOWS = (2, 4, 8, 16)
POOL_GROUP = 128
POOL_BUF = 15
N_EXPERTS = 32
TOP_K = 4
D_FF = 1024
SWIGLU_LIMIT = 7.0
SWIGLU_ALPHA = 1.702
EPS = 1e-5
PAST_LEN = 16384

BATCH, SEQ = 8, 2048
DEC_BATCH, DEC_SEQ = 128, 4
N_PROMPT = BATCH * SEQ
N_SAMPLE = DEC_BATCH * DEC_SEQ
N_TOK = N_PROMPT + N_SAMPLE

TS = 512
SEQ_TILES = SEQ // TS
N_PROMPT_TILES = N_PROMPT // TS
N_TILES = N_PROMPT_TILES + 1
HIST = 16
UNIT = 128
TILE_UNITS = 4
TM = TILE_UNITS * UNIT
PIECE = 8
PIECES = TM // PIECE
UNIT_PIECES = UNIT // PIECE
S_BLK = 256
SORT_BLK = 512
S_MAX = (TS * TOP_K + N_EXPERTS * (PIECE - 1) + PIECE + S_BLK - 1) // S_BLK * S_BLK
ZERO_ROW = S_MAX - PIECE
N_UNITS = (N_TOK * TOP_K + N_TILES * N_EXPERTS * (PIECE - 1) + N_EXPERTS * (UNIT - PIECE)) // UNIT
N_ROW_TILES = (N_UNITS + (TILE_UNITS - 1) * N_EXPERTS) // TILE_UNITS
N_SORT_ROWS = N_TILES * S_MAX
DUMP_ROW = N_SORT_ROWS

VMEM_LIMIT = 56 * 1024 * 1024

f32 = jnp.float32
bf16 = jnp.bfloat16
i32 = jnp.int32


def _rms(x, g):
    return x * lax.rsqrt(jnp.mean(x * x, axis=-1, keepdims=True) + EPS) * g


def _pool_project(p, pw_ref, ps_ref):
    outs = []
    for g in range(len(POOL_WINDOWS)):
        sl = slice(g * POOL_GROUP, (g + 1) * POOL_GROUP)
        outs.append(jnp.dot(p[:, sl].astype(bf16), pw_ref[g], preferred_element_type=f32))
    return jnp.concatenate(outs, axis=1) * ps_ref[...]


def _route_and_sort(x, z, wout_ref, g2_ref, rw2_ref, rb_ref, tri_ref,
                    x1_ref, srt_ref, route_ref, cnt_ref):
    n = x.shape[0]
    x1 = x + jnp.dot(z.astype(bf16), wout_ref[...], preferred_element_type=f32)
    x1_ref[...] = x1
    xn = _rms(x1, g2_ref[...])
    xb = xn.astype(bf16)

    xlo = (xn - xb.astype(f32)).astype(bf16)
    nt = (((1,), (1,)), ((), ()))
    hi_lo = lax.dot_general(rw2_ref[...], xb, nt, preferred_element_type=f32)
    logits = (hi_lo[:N_EXPERTS] + hi_lo[N_EXPERTS:]
              + lax.dot_general(rw2_ref[0:N_EXPERTS, :], xlo, nt, preferred_element_type=f32)
              + rb_ref[...])

    e_iota = lax.broadcasted_iota(i32, (N_EXPERTS, n), 0)
    work = logits
    hots, vals = [], []
    for _ in range(TOP_K):
        m = jnp.max(work, axis=0, keepdims=True)
        ik = jnp.min(jnp.where(work == m, e_iota, N_EXPERTS), axis=0, keepdims=True)
        hot = e_iota == ik
        work = jnp.where(hot, -jnp.inf, work)
        hots.append(hot); vals.append(m)
    exps = [jnp.exp(v - vals[0]) for v in vals]
    den = exps[0] + exps[1] + exps[2] + exps[3]
    gates = [e / den for e in exps]

    multi = jnp.where(hots[0] | hots[1] | hots[2] | hots[3], 1.0, 0.0)
    before = jnp.dot(multi.astype(bf16), tri_ref[...], preferred_element_type=f32)
    count = jnp.sum(multi, axis=1, keepdims=True)
    cnt_ref[...] = count
    pieces = jnp.floor((count + (PIECE - 1)) * (1.0 / PIECE))
    er = lax.broadcasted_iota(i32, (N_EXPERTS, N_EXPERTS), 0)
    ec = lax.broadcasted_iota(i32, (N_EXPERTS, N_EXPERTS), 1)
    lower = jnp.where(ec < er, 1.0, 0.0).astype(bf16)
    pieces_b = jnp.broadcast_to(pieces, (N_EXPERTS, 128)).astype(bf16)
    start = PIECE * jnp.dot(lower, pieces_b, preferred_element_type=f32)[:, 0:1]
    base = before + start
    slot = jnp.concatenate(
        [jnp.sum(jnp.where(h, base, 0.0), axis=0, keepdims=True) for h in hots], axis=0).astype(i32)
    route_ref[...] = jnp.concatenate([slot.astype(f32)] + gates, axis=0)

    row_io = lax.broadcasted_iota(i32, (S_BLK, n), 0).astype(bf16)
    slot_f = slot.astype(f32)
    one, zero = jnp.ones((S_BLK, n), bf16), jnp.zeros((S_BLK, n), bf16)
    for row0 in range(0, S_MAX, SORT_BLK):
        rows = min(SORT_BLK, S_MAX - row0)
        parts = []
        for sub in range(row0, row0 + rows, S_BLK):
            local = slot_f - float(sub)
            local = jnp.where((local >= 0.0) & (local < float(S_BLK)), local, -1.0).astype(bf16)
            hit = row_io == local[0:1, :]
            for k in range(1, TOP_K):
                hit = hit | (row_io == local[k:k + 1, :])
            parts.append(jnp.where(hit, one, zero))
        sel = parts[0] if len(parts) == 1 else jnp.concatenate(parts, axis=0)
        srt_ref[row0:row0 + rows, :] = jnp.dot(sel, xb, preferred_element_type=f32).astype(bf16)


def _mixer_kernel(xp_in_ref, xs_in_ref, sc_ref, sp_ref, g1_ref, win_ref, cw_ref, pw_ref, ps_ref,
                  wout_ref, g2_ref, rw2_ref, rb_ref, tri_ref,
                  x1_ref, srt_ref, route_ref, cnt_ref,
                  npc_ref, npp_ref, nsc_ref, nsv_ref,
                  ubuf, vbuf, xcur, zcur):
    i = pl.program_id(0)

    @pl.when(i < N_PROMPT_TILES)
    def _():
        s = i % SEQ_TILES

        @pl.when(s == 0)
        def _():
            ubuf[0:HIST, :] = jnp.zeros((HIST, D_CONV), f32)
            vbuf[0:HIST, :] = jnp.zeros((HIST, D_POOL), f32)

        x = xp_in_ref[...]
        xcur[...] = x
        xn = _rms(x, g1_ref[...]).astype(bf16)
        part = lambda j: jnp.dot(xn, win_ref[:, j * D_CONV:(j + 1) * D_CONV],
                                 preferred_element_type=f32)
        v = part(3)
        vbuf[HIST:HIST + TS, :] = v
        u = part(1) * part(2)
        ubuf[HIST:HIST + TS, :] = u
        bg = part(0)

        uh = ubuf[...]
        y = (pltpu.roll(uh, 2, axis=0) * cw_ref[0:1, :]
             + pltpu.roll(uh, 1, axis=0) * cw_ref[1:2, :])[HIST:, :] + u * cw_ref[2:3, :]
        zcur[:, :D_CONV] = bg * y

        pos = s * TS + lax.broadcasted_iota(i32, (TS, 1), 0)
        ps = []
        for g, w in enumerate(POOL_WINDOWS):
            sl = slice(g * POOL_GROUP, (g + 1) * POOL_GROUP)
            acc = vbuf[:, sl]
            for step in range(g + 1):
                acc = acc + pltpu.roll(acc, 1 << step, axis=0)
            cnt = jnp.minimum(w, pos + 1).astype(f32)
            ps.append(acc[HIST:, :] * (1.0 / cnt) - v[:, sl])
        zcur[:, D_CONV:] = _pool_project(jnp.concatenate(ps, axis=1), pw_ref, ps_ref)

        ubuf[HIST - 8:HIST, :] = ubuf[TS + HIST - 8:TS + HIST, :]
        vbuf[0:HIST, :] = vbuf[TS:TS + HIST, :]

        @pl.when(s == SEQ_TILES - 1)
        def _():
            npc_ref[0] = ubuf[HIST - 8:HIST, :]
            npp_ref[0] = vbuf[0:HIST, :]

    @pl.when(i == N_PROMPT_TILES)
    def _():
        x = xs_in_ref[...]
        xcur[...] = x
        xn = _rms(x, g1_ref[...]).astype(bf16)
        proj = jnp.dot(xn, win_ref[...], preferred_element_type=f32)
        nb = DEC_BATCH
        rows = lambda a, t: a[t * nb:(t + 1) * nb]
        bg = proj[:, :D_CONV]
        u = proj[:, D_CONV:2 * D_CONV] * proj[:, 2 * D_CONV:3 * D_CONV]
        v = proj[:, 3 * D_CONV:]
        up = [sc_ref[j] for j in range(CONV_W - 1)] + [rows(u, t) for t in range(DEC_SEQ)]
        vp = [sp_ref[j] for j in range(POOL_BUF)] + [rows(v, t) for t in range(DEC_SEQ)]
        zc, ps = [], []
        for t in range(DEC_SEQ):
            y = up[t] * cw_ref[0:1, :]
            for k in range(1, CONV_W):
                y = y + up[t + k] * cw_ref[k:k + 1, :]
            zc.append(rows(bg, t) * y)
            pg = []
            for g, w in enumerate(POOL_WINDOWS):
                sl = slice(g * POOL_GROUP, (g + 1) * POOL_GROUP)
                lo = t + POOL_BUF - w + 1
                acc = vp[lo][:, sl]
                for j in range(lo + 1, t + POOL_BUF + 1):
                    acc = acc + vp[j][:, sl]
                cnt = float(min(w, PAST_LEN + t + 1))
                pg.append(acc * (1.0 / cnt) - vp[t + POOL_BUF][:, sl])
            ps.append(jnp.concatenate(pg, axis=1))
        zcur[:, :D_CONV] = jnp.concatenate(zc, axis=0)
        zcur[:, D_CONV:] = _pool_project(jnp.concatenate(ps, axis=0), pw_ref, ps_ref)
        for j in range(CONV_W - 1):
            nsc_ref[j] = up[DEC_SEQ + j]
        for t in range(DEC_SEQ):
            nsv_ref[t] = rows(v, t)

    _route_and_sort(xcur[...], zcur[...], wout_ref, g2_ref, rw2_ref, rb_ref, tri_ref,
                    x1_ref, srt_ref, route_ref, cnt_ref)


def _mixer(xp, xs, sc_t, sp_t, g1, win, cw, pw, ps, wout, g2, rw2, rb, tri):
    const = lambda shape: pl.BlockSpec(shape, lambda i: (0,) * len(shape),
                                       pipeline_mode=pl.Buffered(1))
    ptile = lambda i: jnp.minimum(i, N_PROMPT_TILES - 1)
    pbatch = lambda i: jnp.minimum(i, N_PROMPT_TILES - 1) // SEQ_TILES
    out_shape = (
        jax.ShapeDtypeStruct((N_TOK, D_MODEL), f32),
        jax.ShapeDtypeStruct((N_SORT_ROWS, D_MODEL), bf16),
        jax.ShapeDtypeStruct((2 * TOP_K, N_TOK), f32),
        jax.ShapeDtypeStruct((N_TILES * N_EXPERTS, 1), f32),
        jax.ShapeDtypeStruct((BATCH, 8, D_CONV), f32),
        jax.ShapeDtypeStruct((BATCH, HIST, D_POOL), f32),
        jax.ShapeDtypeStruct((CONV_W - 1, DEC_BATCH, D_CONV), f32),
        jax.ShapeDtypeStruct((DEC_SEQ, DEC_BATCH, D_POOL), f32),
    )
    out_specs = (
        pl.BlockSpec((TS, D_MODEL), lambda i: (i, 0)),
        pl.BlockSpec((S_MAX, D_MODEL), lambda i: (i, 0)),
        pl.BlockSpec((2 * TOP_K, TS), lambda i: (0, i)),
        pl.BlockSpec((N_EXPERTS, 1), lambda i: (i, 0)),
        pl.BlockSpec((1, 8, D_CONV), lambda i: (pbatch(i), 0, 0)),
        pl.BlockSpec((1, HIST, D_POOL), lambda i: (pbatch(i), 0, 0)),
        pl.BlockSpec((CONV_W - 1, DEC_BATCH, D_CONV), lambda i: (0, 0, 0)),
        pl.BlockSpec((DEC_SEQ, DEC_BATCH, D_POOL), lambda i: (0, 0, 0)),
    )
    in_specs = [
        pl.BlockSpec((TS, D_MODEL), lambda i: (ptile(i), 0)),
        const((N_SAMPLE, D_MODEL)),
        const((CONV_W - 1, DEC_BATCH, D_CONV)),
        const((POOL_BUF, DEC_BATCH, D_POOL)),
        const((1, D_MODEL)),
        const((D_MODEL, D_IN)),
        const((CONV_W, D_CONV)),
        const((len(POOL_WINDOWS), POOL_GROUP, POOL_GROUP)),
        const((1, D_POOL)),
        const((D_MODEL, D_MODEL)),
        const((1, D_MODEL)),
        const((2 * N_EXPERTS, D_MODEL)),
        const((N_EXPERTS, 1)),
        const((TS, TS)),
    ]
    return pl.pallas_call(
        _mixer_kernel,
        grid=(N_TILES,),
        in_specs=in_specs,
        out_specs=out_specs,
        out_shape=out_shape,
        scratch_shapes=[
            pltpu.VMEM((TS + HIST, D_CONV), f32),
            pltpu.VMEM((TS + HIST, D_POOL), f32),
            pltpu.VMEM((TS, D_MODEL), f32),
            pltpu.VMEM((TS, D_MODEL), f32),
        ],
        compiler_params=pltpu.CompilerParams(
            dimension_semantics=("arbitrary",), vmem_limit_bytes=VMEM_LIMIT),
        name="mixer",
    )(xp, xs, sc_t, sp_t, g1, win, cw, pw, ps, wout, g2, rw2, rb, tri)


def _expert_kernel(te_ref, nu_ref, first_ref, ord_ref, nxt_ref, unit_ref, len_ref, pt_ref, tot_ref,
                   b1_ref, b2_ref, srt_hbm, w1_hbm, w2_hbm, ys_hbm,
                   xbuf, ybuf, zpiece, w1f, w2f, w1b, w2b, sem_w, sem_x, sem_y, sem_z):
    i = pl.program_id(0)
    n_used = nu_ref[0]
    live = i < n_used

    def weight_copies(e, slot):
        return (pltpu.make_async_copy(w1_hbm.at[e], w1f.at[slot], sem_w.at[slot]),
                pltpu.make_async_copy(w2_hbm.at[e], w2f.at[slot], sem_w.at[2 + slot]))

    def is_padding(entry, tile, p):
        if p < UNIT_PIECES:
            return entry < 0
        return (entry < 0) | (p >= len_ref[tile] * UNIT_PIECES)

    def start_x(tile, b):
        for p in range(PIECES):
            src = pt_ref[unit_ref[tile] * UNIT_PIECES + p]
            src = pl.multiple_of(jnp.where(is_padding(src, tile, p), ZERO_ROW, src), PIECE)
            pltpu.make_async_copy(srt_hbm.at[pl.ds(src, PIECE)],
                                  xbuf.at[b, pl.ds(p * PIECE, PIECE)], sem_x.at[b]).start()

    def start_y(tile, b, all_dump=False):
        for p in range(PIECES):
            dst = pt_ref[unit_ref[tile] * UNIT_PIECES + p]
            dump = is_padding(dst, tile, p) | all_dump
            dst = pl.multiple_of(jnp.where(dump, DUMP_ROW + b * TM + p * PIECE, dst), PIECE)
            pltpu.make_async_copy(ybuf.at[b, pl.ds(p * PIECE, PIECE)],
                                  ys_hbm.at[pl.ds(dst, PIECE)], sem_y.at[b]).start(priority=p % 2)

    def wait_x(b):
        pltpu.make_async_copy(srt_hbm.at[pl.ds(0, TM)], xbuf.at[b], sem_x.at[b]).wait()

    def wait_y(b):
        pltpu.make_async_copy(ybuf.at[b], ys_hbm.at[pl.ds(0, TM)], sem_y.at[b]).wait()

    @pl.when(i == 0)
    def _():
        zpiece[...] = jnp.zeros_like(zpiece)

        def fill_tile(t, c):
            def dst(p):
                row = pl.multiple_of(t * S_MAX + tot_ref[t] + p * PIECE, PIECE)
                return ys_hbm.at[pl.ds(row, PIECE)]

            def start(p, c2):
                pltpu.make_async_copy(zpiece, dst(p), sem_z.at[0]).start()
                return c2
            return lax.fori_loop(0, (S_MAX - tot_ref[t]) // PIECE, start, c)
        lax.fori_loop(0, N_TILES, fill_tile, 0)
        start_x(0, 0)
        start_x(jnp.minimum(1, n_used - 1), 1)
        ybuf[...] = jnp.zeros_like(ybuf)
        for b in range(2):
            pltpu.make_async_copy(ybuf.at[b], ys_hbm.at[pl.ds(DUMP_ROW + b * TM, TM)],
                                  sem_y.at[b]).start()

    @pl.when(live)
    def _():
        ring = i % 3
        expert = te_ref[i]
        wait_x(ring)

        @pl.when(first_ref[i] == 1)
        def _():
            slot = ord_ref[i] % 2

            @pl.when(i == 0)
            def _():
                for cp in weight_copies(te_ref[0], 0):
                    cp.start(priority=1)
            for cp in weight_copies(expert, slot):
                cp.wait()

            @pl.when(nxt_ref[i] >= 0)
            def _():
                for cp in weight_copies(nxt_ref[i], 1 - slot):
                    cp.start(priority=1)
            w1b[...] = w1f[slot].astype(bf16)
            w2b[...] = w2f[slot].astype(bf16)

        wait_y(ring)

        def mlp(rows):
            x = xbuf[ring, 0:rows, :]
            gu = jnp.dot(x, w1b[...], preferred_element_type=f32) + b1_ref[expert]
            gate = jnp.minimum(gu[:, :D_FF], SWIGLU_LIMIT)
            lin = jnp.clip(gu[:, D_FF:], -SWIGLU_LIMIT, SWIGLU_LIMIT)
            glu = gate * jax.nn.sigmoid(SWIGLU_ALPHA * gate)
            h = (glu * (lin + 1.0)).astype(bf16)
            start_x(jnp.minimum(i + 2, n_used - 1), (i + 2) % 3)
            start_y(jnp.maximum(i - 1, 0), (i + 2) % 3, all_dump=i == 0)
            out = jnp.dot(h, w2b[...], preferred_element_type=f32) + b2_ref[expert]
            ybuf[ring, 0:rows, :] = out.astype(bf16)

        for n_units in range(1, TILE_UNITS + 1):
            @pl.when(len_ref[i] == n_units)
            def _(n_units=n_units):
                mlp(n_units * UNIT)

        @pl.when(i == n_used - 1)
        def _():
            start_y(i, ring)
            for b in range(3):
                wait_y(b)
            wait_x((i + 1) % 3)
            wait_x((i + 2) % 3)

            def drain_tile(t, c):
                def wait(p, c2):
                    pltpu.make_async_copy(zpiece, ys_hbm.at[pl.ds(0, PIECE)], sem_z.at[0]).wait()
                    return c2
                return lax.fori_loop(0, (S_MAX - tot_ref[t]) // PIECE, wait, c)
            lax.fori_loop(0, N_TILES, drain_tile, 0)


def _experts(tile_expert, n_used, first, order, nxt, unit0, length, piece_row, tot, srt, w1, b1, w2, b2):
    whole = lambda shape: pl.BlockSpec(shape, lambda i, *_: (0,) * len(shape))
    grid_spec = pltpu.PrefetchScalarGridSpec(
        num_scalar_prefetch=9,
        grid=(N_ROW_TILES,),
        in_specs=[
            whole((N_EXPERTS, 1, 2 * D_FF)),
            whole((N_EXPERTS, 1, D_MODEL)),
            pl.BlockSpec(memory_space=pl.ANY),
            pl.BlockSpec(memory_space=pl.ANY),
            pl.BlockSpec(memory_space=pl.ANY),
        ],
        out_specs=pl.BlockSpec(memory_space=pl.ANY),
        scratch_shapes=[
            pltpu.VMEM((3, TM, D_MODEL), bf16), pltpu.VMEM((3, TM, D_MODEL), bf16),
            pltpu.VMEM((PIECE, D_MODEL), bf16),
            pltpu.VMEM((2, D_MODEL, 2 * D_FF), f32), pltpu.VMEM((2, D_FF, D_MODEL), f32),
            pltpu.VMEM((D_MODEL, 2 * D_FF), bf16), pltpu.VMEM((D_FF, D_MODEL), bf16),
            pltpu.SemaphoreType.DMA((4,)), pltpu.SemaphoreType.DMA((3,)),
            pltpu.SemaphoreType.DMA((3,)), pltpu.SemaphoreType.DMA((1,)),
        ],
    )
    return pl.pallas_call(
        _expert_kernel,
        grid_spec=grid_spec,
        out_shape=jax.ShapeDtypeStruct((N_SORT_ROWS + 3 * TM, D_MODEL), bf16),
        compiler_params=pltpu.CompilerParams(
            dimension_semantics=("arbitrary",), vmem_limit_bytes=VMEM_LIMIT),
        name="experts",
    )(tile_expert, n_used, first, order, nxt, unit0, length, piece_row, tot, b1, b2, srt, w1, w2)


def _combine_tile(route_ref, x1_ref, ys_ref, out_ref, *, gf_ref, wgt):
    slot = route_ref[0:TOP_K, :]
    g = route_ref[TOP_K:2 * TOP_K, :].astype(bf16)
    row_io = lax.broadcasted_iota(i32, (S_BLK, TS), 0).astype(bf16)
    for b in range(S_MAX // S_BLK):
        local = slot - float(b * S_BLK)
        local = jnp.where((local >= 0.0) & (local < float(S_BLK)), local, -1.0).astype(bf16)
        w = jnp.zeros((S_BLK, TS), bf16)
        for k in range(TOP_K):
            w = jnp.where(row_io == local[k:k + 1, :], g[k:k + 1, :], w)
        wgt[b * S_BLK:(b + 1) * S_BLK, :] = w

    y = x1_ref[...] + lax.dot_general(wgt[...], ys_ref[...], (((0,), (0,)), ((), ())),
                                      preferred_element_type=f32)
    out_ref[...] = _rms(y, gf_ref[...])


def _combine_kernel(route_hbm, x1_hbm, gf_ref, ys_hbm, yp_hbm, ysm_hbm, wgt):
    body = functools.partial(_combine_tile, gf_ref=gf_ref, wgt=wgt)

    def specs(first_tile):
        deep = dict(pipeline_mode=pl.Buffered(3))
        return dict(
            in_specs=[
                pl.BlockSpec((2 * TOP_K, TS), lambda i: (0, first_tile + i)),
                pl.BlockSpec((TS, D_MODEL), lambda i: (first_tile + i, 0), **deep),
                pl.BlockSpec((S_MAX, D_MODEL), lambda i: (first_tile + i, 0), **deep),
            ],
            out_specs=[pl.BlockSpec((TS, D_MODEL), lambda i: (i, 0))])

    pltpu.emit_pipeline(body, grid=(N_PROMPT_TILES,), **specs(0))(
        route_hbm, x1_hbm, ys_hbm, yp_hbm)
    pltpu.emit_pipeline(body, grid=(1,), **specs(N_PROMPT_TILES))(
        route_hbm, x1_hbm, ys_hbm, ysm_hbm)


def _combine(route_t, x1, gf, ys):
    return pl.pallas_call(
        _combine_kernel,
        in_specs=[
            pl.BlockSpec(memory_space=pl.ANY),
            pl.BlockSpec(memory_space=pl.ANY),
            pl.BlockSpec(memory_space=pltpu.VMEM),
            pl.BlockSpec(memory_space=pl.ANY),
        ],
        out_specs=(
            pl.BlockSpec(memory_space=pl.ANY),
            pl.BlockSpec(memory_space=pl.ANY),
        ),
        out_shape=(
            jax.ShapeDtypeStruct((N_PROMPT, D_MODEL), f32),
            jax.ShapeDtypeStruct((N_SAMPLE, D_MODEL), f32),
        ),
        scratch_shapes=[pltpu.VMEM((S_MAX, TS), bf16)],
        compiler_params=pltpu.CompilerParams(vmem_limit_bytes=VMEM_LIMIT),
        name="combine",
    )(route_t, x1, gf, ys)


def _routing_tables(cnt):
    n = cnt.reshape(N_TILES, N_EXPERTS).astype(i32)
    c = (n + (PIECE - 1)) // PIECE * PIECE
    lo = jnp.cumsum(c, axis=1) - c
    tot = jnp.sum(c, axis=1)
    group = jnp.sum(c, axis=0)
    padded = (group + (UNIT - 1)) // UNIT * UNIT
    gend = jnp.cumsum(padded)
    gstart = gend - padded
    cs = gstart[None, :] + jnp.cumsum(c, axis=0) - c
    units = padded // UNIT
    tiles = (units + (TILE_UNITS - 1)) // TILE_UNITS
    tend = jnp.cumsum(tiles)
    n_used = tend[-1]
    tile_id = jnp.arange(N_ROW_TILES, dtype=i32)
    tile_expert = jnp.sum((tile_id[:, None] >= tend[None, :]).astype(i32), axis=1)
    tile_expert = jnp.minimum(tile_expert, jnp.take(tile_expert, n_used - 1))
    e_ar = jnp.arange(N_EXPERTS, dtype=i32)
    mine = tile_expert[:, None] == e_ar[None, :]
    pick = lambda v: jnp.sum(jnp.where(mine, v[None, :], 0), axis=1)
    in_group = tile_id - pick(tend - tiles)
    unit0 = pick(gstart // UNIT) + TILE_UNITS * in_group
    length = jnp.clip(pick(units) - TILE_UNITS * in_group, 1, TILE_UNITS)
    first = (tile_id < n_used) & ((tile_id == 0) | (tile_expert != jnp.roll(tile_expert, 1)))
    order = jnp.maximum(jnp.cumsum(first.astype(i32)) - 1, 0)
    later = (e_ar[None, :] > e_ar[:, None]) & (padded[None, :] > 0)
    nxt_e = jnp.min(jnp.where(later, e_ar[None, :], N_EXPERTS), axis=1)
    nxt_e = jnp.where(nxt_e == N_EXPERTS, -1, nxt_e)
    nxt = pick(nxt_e)
    unit_row = jnp.arange(N_UNITS + TILE_UNITS - 1, dtype=i32) * UNIT
    unit_e = jnp.minimum(jnp.sum((unit_row[:, None] >= gend[None, :]).astype(i32), axis=1),
                         N_EXPERTS - 1)
    of_unit = (unit_e[:, None] == e_ar[None, :])[:, None, :]
    per_unit = lambda tab: jnp.sum(jnp.where(of_unit, tab[None], 0), axis=2)
    cs_u, c_u = per_unit(cs), per_unit(c)
    src_u = per_unit(jnp.arange(N_TILES, dtype=i32)[:, None] * S_MAX + lo)
    piece_r = unit_row[:, None] + jnp.arange(UNIT_PIECES, dtype=i32)[None, :] * PIECE
    chunk = jnp.sum((piece_r[:, :, None] >= cs_u[:, None, :]).astype(i32), axis=2) - 1
    is_chunk = chunk[:, :, None] == jnp.arange(N_TILES, dtype=i32)[None, None, :]
    of_chunk = lambda tab_u: jnp.sum(jnp.where(is_chunk, tab_u[:, None, :], 0), axis=2)
    off = piece_r - of_chunk(cs_u)
    piece_row = jnp.where(off < of_chunk(c_u), of_chunk(src_u) + off, -1).reshape(-1)
    return dict(tot=tot, n_used=n_used[None], tile_expert=tile_expert, first=first.astype(i32),
                order=order, nxt=nxt, unit0=unit0, length=length, piece_row=piece_row)


def kernel(x_prompt, x_sample, state_conv, state_pool, norm_mix_g, w_in, conv_w, pool_w, pool_scale,
           w_out, norm_ffn_g, router_w, router_b, exp_w1, exp_b1, exp_w2, exp_b2, final_norm_g):
    l = 0
    xp = x_prompt.reshape(N_PROMPT, D_MODEL)
    xs = jnp.transpose(x_sample, (1, 0, 2)).reshape(N_SAMPLE, D_MODEL)
    sc_t = jnp.transpose(state_conv[l], (1, 0, 2))
    sp_t = jnp.transpose(state_pool[l], (1, 0, 2))
    rw_t = router_w[l].T
    rwh = rw_t.astype(bf16)
    rw2 = jnp.concatenate([rwh, (rw_t - rwh.astype(f32)).astype(bf16)], axis=0)
    tok = jnp.arange(TS, dtype=i32)
    tri = (tok[:, None] < tok[None, :]).astype(bf16)

    (x1, srt, route_t, cnt, npc, npp, nsc, nsv) = _mixer(
        xp, xs, sc_t, sp_t, norm_mix_g[l][None, :], w_in[l].astype(bf16), conv_w[l],
        pool_w[l].astype(bf16), pool_scale[l][None, :], w_out[l].astype(bf16),
        norm_ffn_g[l][None, :], rw2, router_b[l][:, None], tri)

    t = _routing_tables(cnt)
    ys = _experts(t["tile_expert"], t["n_used"], t["first"], t["order"], t["nxt"], t["unit0"],
                  t["length"], t["piece_row"], t["tot"], srt, exp_w1[l], exp_b1[l][:, None, :],
                  exp_w2[l], exp_b2[l][:, None, :])
    y_p, y_s = _combine(route_t, x1, final_norm_g[None, :], ys)

    y_prompt = y_p.reshape(BATCH, SEQ, D_MODEL)
    y_sample = jnp.transpose(y_s.reshape(DEC_SEQ, DEC_BATCH, D_MODEL), (1, 0, 2))
    new_conv_prompt = npc[None, :, 8 - (CONV_W - 1):, :]
    new_pool_prompt = npp[None, :, HIST - POOL_BUF:, :]
    new_conv_sample = jnp.transpose(nsc, (1, 0, 2))[None]
    new_pool_sample = jnp.concatenate(
        [state_pool[l][:, DEC_SEQ:, :], jnp.transpose(nsv, (1, 0, 2))], axis=1)[None]
    return (y_prompt, y_sample, new_conv_prompt, new_pool_prompt, new_conv_sample, new_pool_sample)
```
